```python
import math
import jax
import jax.numpy as jnp
from jax import lax
import numpy as np

D_MODEL = 1024
BATCH = 32
SEQ = 256
DEPTH = 4
DEC_BATCH = 2
DEC_SEQ = 2048
PAST_LEN = 512

GRID_W = 64
N_MIXERS = 3
D_INNER = 2 * D_MODEL
N_S5_LAYERS = (DEPTH + 2) // 3
N_POOL_LAYERS = (DEPTH + 1) // 3
N_MLA_LAYERS = DEPTH // 3
S5_GROUP = 16
S5_GROUPS = D_INNER // S5_GROUP
S5_STATE = 64
POOL_WINDOWS = (2, 4, 8, 16)
POOL_GROUPS = len(POOL_WINDOWS)
POOL_GROUP_W = D_INNER // POOL_GROUPS
MLA_HEADS = 16
MLA_NOPE = 128
MLA_ROPE = 64
MLA_V = 128
MLA_Q_RANK = 256
MLA_KV_RANK = 128
ROPE_THETA = 10000.0
ATTN_BLOCK = 128
NORM_EPS = 1e-6

kernel_name = 'hybrid_s5_pool_mla_diffusion_step'


def _f32(t):
    return t.astype(jnp.float32)


def _rmsnorm(x, g):
    xf = _f32(x)
    y = xf * lax.rsqrt(jnp.mean(xf * xf, axis=-1, keepdims=True) + NORM_EPS)
    return (y * _f32(g)).astype(x.dtype)


def _ada(cond, w, b):
    m = jax.nn.silu(cond) @ w + b
    return jnp.split(m, 3, axis=-1)


def _lin_rec(left, right):
    a_l, b_l = left
    a_r, b_r = right
    return a_r * a_l, a_r * b_l + b_r


def _s5_mixer(h, h0_re, h0_im, w_in, lam_re, lam_im, log_step, b_re, b_im, c_re, c_im,
              d_skip, glu_w, glu_b, w_out):
    bsz, n_tok, _ = h.shape
    u, z = jnp.split(h @ w_in, 2, axis=-1)
    uf = _f32(u)
    ug = uf.reshape(bsz, n_tok, S5_GROUPS, S5_GROUP).astype(jnp.complex64)
    y = _f32(d_skip) * uf
    fin_re, fin_im = [], []
    for dirn in range(2):
        lam = lax.complex(_f32(lam_re[dirn]), _f32(lam_im[dirn]))
        step = jnp.exp(_f32(log_step[dirn]))[:, None]
        lam_bar = jnp.exp(lam * step)
        b_bar = ((lam_bar - 1.0) / lam)[..., None] * lax.complex(_f32(b_re[dirn]), _f32(b_im[dirn]))
        c_mat = lax.complex(_f32(c_re[dirn]), _f32(c_im[dirn]))
        bu = jnp.einsum('blgc,gpc->blgp', ug, b_bar)
        h0 = lax.complex(_f32(h0_re[:, dirn]), _f32(h0_im[:, dirn]))
        edge = 0 if dirn == 0 else n_tok - 1
        bu = bu.at[:, edge].add(lam_bar * h0)
        a = jnp.broadcast_to(lam_bar, (1, n_tok) + lam_bar.shape)
        _, states = lax.associative_scan(_lin_rec, (a, bu), axis=1, reverse=(dirn == 1))
        y = y + jnp.einsum('blgp,gcp->blgc', states, c_mat).real.reshape(bsz, n_tok, D_INNER)
        final = states[:, n_tok - 1 - edge]
        fin_re.append(final.real)
        fin_im.append(final.imag)
    y = jax.nn.gelu(y)
    y = y * jax.nn.sigmoid(y @ _f32(glu_w) + _f32(glu_b))
    out = (y.astype(h.dtype) * jax.nn.silu(z)) @ w_out
    return out, jnp.stack(fin_re, axis=1), jnp.stack(fin_im, axis=1)


def _pool_mixer(h, w_in, pool_w, pool_scale, w_out):
    bsz, n_tok, _ = h.shape
    u, z = jnp.split(h @ w_in, 2, axis=-1)
    ug = _f32(u).reshape(bsz, n_tok, POOL_GROUPS, POOL_GROUP_W)
    cs = jnp.concatenate([jnp.zeros_like(ug[:, :1]), jnp.cumsum(ug, axis=1)], axis=1)
    t = np.arange(n_tok)
    pooled = []
    for g, win in enumerate(POOL_WINDOWS):
        lo = win // 2
        start = np.clip(t - lo, 0, n_tok)
        end = np.clip(t - lo + win, 0, n_tok)
        cnt = (end - start).astype(np.float32)[:, None]
        csg = cs[:, :, g]
        pooled.append((csg[:, end] - csg[:, start]) / cnt - ug[:, :, g])
    p = jnp.stack(pooled, axis=2)
    m = jnp.einsum('blgc,gcd->blgd', p, _f32(pool_w)).reshape(bsz, n_tok, D_INNER)
    m = m * _f32(pool_scale)
    return (m.astype(h.dtype) * jax.nn.silu(z)) @ w_out


def _rope_2d(x, n_tok):
    rows = n_tok // GRID_W
    tok = jnp.arange(rows * GRID_W)
    row = (tok // GRID_W).astype(jnp.float32)
    col = (tok % GRID_W).astype(jnp.float32)
    axis_dim = MLA_ROPE // 2
    half = axis_dim // 2
    inv = ROPE_THETA ** (-jnp.arange(half, dtype=jnp.float32) / half)
    bshape = (n_tok,) + (1,) * (x.ndim - 3) + (half,)
    xf = _f32(x)
    out = []
    for i, pos in enumerate((row, col)):
        ang = (pos[:, None] * inv).reshape(bshape)
        cos, sin = jnp.cos(ang), jnp.sin(ang)
        seg = xf[..., i * axis_dim:(i + 1) * axis_dim]
        x1, x2 = seg[..., :half], seg[..., half:]
        out += [x1 * cos - x2 * sin, x1 * sin + x2 * cos]
    return jnp.concatenate(out, axis=-1).astype(x.dtype)


def _mla_project(h, w_in, q_norm, wq_b, kv_norm):
    bsz, n_tok, _ = h.shape
    splits = [MLA_Q_RANK, MLA_Q_RANK + MLA_KV_RANK, MLA_Q_RANK + MLA_KV_RANK + MLA_ROPE]
    q_a, ckv, kpe, z = jnp.split(h @ w_in, splits, axis=-1)
    q = (_rmsnorm(q_a, q_norm) @ wq_b).reshape(bsz, n_tok, MLA_HEADS, MLA_NOPE + MLA_ROPE)
    return q[..., :MLA_NOPE], q[..., MLA_NOPE:], _rmsnorm(ckv, kv_norm), kpe, z


def _mla_expand(ckv_n, wkv_b):
    bsz, n_tok, _ = ckv_n.shape
    kv = (ckv_n @ wkv_b).reshape(bsz, n_tok, MLA_HEADS, MLA_NOPE + MLA_V)
    return kv[..., :MLA_NOPE], kv[..., MLA_NOPE:]


def _mla_attend(q_nope, q_pe, k_nope, k_pe, v):
    bsz, n_q, n_h, _ = q_nope.shape
    qb = math.gcd(n_q, ATTN_BLOCK)
    nb = n_q // qb
    scale = (MLA_NOPE + MLA_ROPE) ** -0.5

    def to_blocks(t):
        return t.reshape((bsz, nb, qb) + t.shape[2:]).swapaxes(0, 1)

    def block(qs):
        qn, qp = qs
        s = jnp.einsum('bqhd,bkhd->bhqk', qn, k_nope) + jnp.einsum('bqhr,bkr->bhqk', qp, k_pe)
        p = jax.nn.softmax(_f32(s) * scale, axis=-1)
        return jnp.einsum('bhqk,bkhd->bqhd', p.astype(v.dtype), v)

    o = lax.map(block, (to_blocks(q_nope), to_blocks(q_pe)))
    return o.swapaxes(0, 1).reshape(bsz, n_q, n_h * MLA_V)


def _mla_context(h, w_in, q_norm, wq_b, kv_norm, wkv_b, w_out):
    q_nope, q_pe, ckv_n, kpe, z = _mla_project(h, w_in, q_norm, wq_b, kv_norm)
    k_nope, v = _mla_expand(ckv_n, wkv_b)
    o = _mla_attend(q_nope, q_pe, k_nope, kpe, v)
    return (o * jax.nn.silu(z)) @ w_out, ckv_n, kpe


def _mla_latent(h, ctx_ckv, ctx_kpe, w_in, q_norm, wq_b, kv_norm, wkv_b, w_out):
    n_tok = h.shape[1]
    q_nope, q_pe, ckv_n, kpe, z = _mla_project(h, w_in, q_norm, wq_b, kv_norm)
    q_pe = _rope_2d(q_pe, n_tok)
    kpe = _rope_2d(kpe, n_tok)
    k_nope, v = _mla_expand(jnp.concatenate([ctx_ckv.astype(ckv_n.dtype), ckv_n], axis=1), wkv_b)
    k_pe = jnp.concatenate([ctx_kpe.astype(kpe.dtype), kpe], axis=1)
    o = _mla_attend(q_nope, q_pe, k_nope, k_pe, v)
    return (o * jax.nn.silu(z)) @ w_out


def setup_inputs(seed: int = 0) -> dict:
    key = jax.random.key(seed)
    ks = iter(jax.random.split(key, 48))

    def nrm(shape, scale=1.0):
        return scale * jax.random.normal(next(ks), shape, jnp.float32)

    def gain(shape):
        return 1.0 + nrm(shape, 0.02)

    G, P, W, D = S5_GROUPS, S5_STATE, D_INNER, D_MODEL
    s5_lam_im = jnp.pi * jnp.arange(P, dtype=jnp.float32) + nrm((N_S5_LAYERS, 2, G, P), 0.01)
    s5_log_step = jax.random.uniform(next(ks), (N_S5_LAYERS, 2, G), jnp.float32,
                                     math.log(1e-3), math.log(1e-1))
    mla_in = MLA_Q_RANK + MLA_KV_RANK + MLA_ROPE + W
    return {
        'x_prompt': nrm((BATCH, SEQ, D)),
        'x_sample': nrm((DEC_BATCH, DEC_SEQ, D)),
        'state_s5_re': nrm((DEC_BATCH, N_S5_LAYERS, 2, G, P), 0.1),
        'state_s5_im': nrm((DEC_BATCH, N_S5_LAYERS, 2, G, P), 0.1),
        'cache_ckv': nrm((DEC_BATCH, N_MLA_LAYERS, PAST_LEN, MLA_KV_RANK)),
        'cache_kpe': nrm((DEC_BATCH, N_MLA_LAYERS, PAST_LEN, MLA_ROPE)),
        'c': nrm((DEC_BATCH, D)),
        'c_ctx': nrm((D,)),
        'norm_g': gain((DEPTH, D)),
        'ada_w': nrm((DEPTH, D, 3 * D), 0.5 * D ** -0.5),
        'ada_b': nrm((DEPTH, 3 * D), 0.02),
        'final_norm_g': gain((D,)),
        's5_w_in': nrm((N_S5_LAYERS, D, 2 * W), D ** -0.5),
        's5_lam_re': -0.5 + nrm((N_S5_LAYERS, 2, G, P), 0.01),
        's5_lam_im': s5_lam_im,
        's5_log_step': s5_log_step,
        's5_b_re': nrm((N_S5_LAYERS, 2, G, P, S5_GROUP), (2 * S5_GROUP) ** -0.5),
        's5_b_im': nrm((N_S5_LAYERS, 2, G, P, S5_GROUP), (2 * S5_GROUP) ** -0.5),
        's5_c_re': nrm((N_S5_LAYERS, 2, G, S5_GROUP, P), (2 * P) ** -0.5),
        's5_c_im': nrm((N_S5_LAYERS, 2, G, S5_GROUP, P), (2 * P) ** -0.5),
        's5_d': nrm((N_S5_LAYERS, W)),
        's5_glu_w': nrm((N_S5_LAYERS, W, W), W ** -0.5),
        's5_glu_b': nrm((N_S5_LAYERS, W), 0.02),
        's5_w_out': nrm((N_S5_LAYERS, W, D), W ** -0.5),
        'pool_w_in': nrm((N_POOL_LAYERS, D, 2 * W), D ** -0.5),
        'pool_w': nrm((N_POOL_LAYERS, POOL_GROUPS, POOL_GROUP_W, POOL_GROUP_W), POOL_GROUP_W ** -0.5),
        'pool_scale': 1.0 + nrm((N_POOL_LAYERS, W), 0.1),
        'pool_w_out': nrm((N_POOL_LAYERS, W, D), W ** -0.5),
        'mla_w_in': nrm((N_MLA_LAYERS, D, mla_in), D ** -0.5),
        'mla_q_norm': gain((N_MLA_LAYERS, MLA_Q_RANK)),
        'mla_wq_b': nrm((N_MLA_LAYERS, MLA_Q_RANK, MLA_HEADS * (MLA_NOPE + MLA_ROPE)), MLA_Q_RANK ** -0.5),
        'mla_kv_norm': gain((N_MLA_LAYERS, MLA_KV_RANK)),
        'mla_wkv_b': nrm((N_MLA_LAYERS, MLA_KV_RANK, MLA_HEADS * (MLA_NOPE + MLA_V)), MLA_KV_RANK ** -0.5),
        'mla_w_out': nrm((N_MLA_LAYERS, MLA_HEADS * MLA_V, D), (MLA_HEADS * MLA_V) ** -0.5),
    }


def reference(x_prompt, x_sample, state_s5_re, state_s5_im, cache_ckv, cache_kpe, c, c_ctx,
              norm_g, ada_w, ada_b, final_norm_g,
              s5_w_in, s5_lam_re, s5_lam_im, s5_log_step, s5_b_re, s5_b_im, s5_c_re, s5_c_im,
              s5_d, s5_glu_w, s5_glu_b, s5_w_out,
              pool_w_in, pool_w, pool_scale, pool_w_out,
              mla_w_in, mla_q_norm, mla_wq_b, mla_kv_norm, mla_wkv_b, mla_w_out):
    xp, xs = x_prompt, x_sample
    zero_state = jnp.zeros((xp.shape[0], 2, S5_GROUPS, S5_STATE), jnp.float32)
    new_re, new_im, new_ckv, new_kpe = [], [], [], []
    for layer in range(DEPTH):
        kind = layer % N_MIXERS
        j = layer // N_MIXERS
        sh_p, sc_p, g_p = _ada(c_ctx, ada_w[layer], ada_b[layer])
        sh_s, sc_s, g_s = [t[:, None] for t in _ada(c, ada_w[layer], ada_b[layer])]
        hp = _rmsnorm(xp, norm_g[layer]) * (1 + sc_p) + sh_p
        hs = _rmsnorm(xs, norm_g[layer]) * (1 + sc_s) + sh_s
        if kind == 0:
            prm = (s5_w_in[j], s5_lam_re[j], s5_lam_im[j], s5_log_step[j], s5_b_re[j], s5_b_im[j],
                   s5_c_re[j], s5_c_im[j], s5_d[j], s5_glu_w[j], s5_glu_b[j], s5_w_out[j])
            yp, fin_re, fin_im = _s5_mixer(hp, zero_state, zero_state, *prm)
            ys, _, _ = _s5_mixer(hs, state_s5_re[:, j], state_s5_im[:, j], *prm)
            new_re.append(fin_re)
            new_im.append(fin_im)
        elif kind == 1:
            prm = (pool_w_in[j], pool_w[j], pool_scale[j], pool_w_out[j])
            yp = _pool_mixer(hp, *prm)
            ys = _pool_mixer(hs, *prm)
        else:
            prm = (mla_w_in[j], mla_q_norm[j], mla_wq_b[j], mla_kv_norm[j], mla_wkv_b[j], mla_w_out[j])
            yp, ckv_n, kpe = _mla_context(hp, *prm)
            ys = _mla_latent(hs, cache_ckv[:, j], cache_kpe[:, j], *prm)
            new_ckv.append(ckv_n)
            new_kpe.append(kpe)
        xp = xp + g_p * yp
        xs = xs + g_s * ys
    y_prompt = _rmsnorm(xp, final_norm_g)
    y_sample = _rmsnorm(xs, final_norm_g)
    new_s5_re = jnp.stack(new_re, axis=1)
    new_s5_im = jnp.stack(new_im, axis=1)
    new_ckv_s = jnp.stack(new_ckv, axis=1)
    new_kpe_s = jnp.stack(new_kpe, axis=1)
    return (y_prompt, y_sample, new_s5_re, new_s5_im, new_ckv_s, new_kpe_s)
```

```python
import functools
import math

import jax
import jax.numpy as jnp
import numpy as np
from jax import lax
from jax.experimental import pallas as pl
from jax.experimental.pallas import tpu as pltpu

S5_GROUP = 16
S5_CHUNK = 16
POOL_WINDOWS = (2, 4, 8, 16)
MLA_HEADS = 16
MLA_NOPE = 128
MLA_ROPE = 64
MLA_V = 128
GRID_W = 64
ROPE_THETA = 10000.0
NORM_EPS = 1e-6
N_MIXERS = 3

LANES = 128
SUBLANES = 8
VMEM_LIMIT_BYTES = 56 * 1024 * 1024

F32 = jnp.float32
BF16 = jnp.bfloat16


def _cparams(*sem):
    return pltpu.CompilerParams(dimension_semantics=sem, vmem_limit_bytes=VMEM_LIMIT_BYTES)


def _sigmoid(x):
    return 1.0 / (1.0 + jnp.exp(-x))


def _silu(x):
    return x * _sigmoid(x)


def _gelu_tanh(x):
    c = math.sqrt(2.0 / math.pi)
    return 0.5 * x * (1.0 + jnp.tanh(c * (x + 0.044715 * (x * x * x))))


def _ada_kernel(c_ref, w_ref, b_ref, o_ref):
    a = _silu(c_ref[...])
    o_ref[...] = jnp.dot(a, w_ref[...], preferred_element_type=F32,
                         precision=lax.Precision.HIGHEST) + b_ref[...]


def _ada_call(conds, ada_w, ada_b):
    depth, d, d3 = ada_w.shape
    c8 = conds.shape[0]
    tn = 512
    return pl.pallas_call(
        _ada_kernel,
        out_shape=jax.ShapeDtypeStruct((depth, c8, d3), F32),
        grid=(depth, d3 // tn),
        in_specs=[
            pl.BlockSpec((c8, d), lambda l, n: (0, 0)),
            pl.BlockSpec((None, d, tn), lambda l, n: (l, 0, n)),
            pl.BlockSpec((None, 1, tn), lambda l, n: (l, 0, n)),
        ],
        out_specs=pl.BlockSpec((None, c8, tn), lambda l, n: (l, 0, n)),
        compiler_params=_cparams("arbitrary", "arbitrary"),
        name="ada_mod",
    )(conds, ada_w, ada_b.reshape(depth, 1, d3))


def _cond_of_block(i, n_prompt_blocks, blocks_per_sample):
    return jnp.where(i < n_prompt_blocks, 0, 1 + (i - n_prompt_blocks) // blocks_per_sample)


def _modulated(x, mod_ref, g_ref, d):
    ms = jnp.mean(x * x, axis=-1, keepdims=True)
    y = x * lax.rsqrt(ms + NORM_EPS) * g_ref[...]
    shift = mod_ref[:, 0:d]
    scale = mod_ref[:, d:2 * d]
    return (y * (1.0 + scale) + shift).astype(BF16)


def _inproj_kernel(x_ref, mod_ref, g_ref, *rest, d, n_chunk):
    n_out = len(rest) // 2
    w_refs, o_refs = rest[:n_out], rest[n_out:]
    h = _modulated(x_ref[...], mod_ref, g_ref, d)
    for w_ref, o_ref in zip(w_refs, o_refs):
        n = w_ref.shape[1]
        for c in range(0, n, n_chunk):
            e = min(c + n_chunk, n)
            o_ref[:, c:e] = jnp.dot(h, w_ref[:, c:e], preferred_element_type=F32).astype(o_ref.dtype)


def _inproj_call(x, mods_l, norm_g, weights, out_dtypes, *, n_prompt, sample_len, bm=512):
    n_tok, d = x.shape
    npb, bps = n_prompt // bm, sample_len // bm
    cond = functools.partial(_cond_of_block, n_prompt_blocks=npb, blocks_per_sample=bps)
    in_specs = [
        pl.BlockSpec((bm, d), lambda i: (i, 0)),
        pl.BlockSpec((None, 1, 3 * d), lambda i: (cond(i), 0, 0)),
        pl.BlockSpec((1, d), lambda i: (0, 0)),
    ] + [pl.BlockSpec(w.shape, lambda i: (0, 0)) for w in weights]
    out_specs = [pl.BlockSpec((bm, w.shape[1]), lambda i: (i, 0)) for w in weights]
    out_shape = [jax.ShapeDtypeStruct((n_tok, w.shape[1]), dt) for w, dt in zip(weights, out_dtypes)]
    return pl.pallas_call(
        functools.partial(_inproj_kernel, d=d, n_chunk=512),
        out_shape=out_shape,
        grid=(n_tok // bm,),
        in_specs=in_specs,
        out_specs=out_specs,
        compiler_params=_cparams("arbitrary"),
        name="norm_mod_inproj",
    )(x, mods_l, norm_g.reshape(1, d), *weights)


def _outproj_kernel(a_ref, x_ref, mod_ref, w_ref, fg_ref, o_ref, *, d, final_norm):
    y = jnp.dot(a_ref[...], w_ref[...], preferred_element_type=F32)
    gate = mod_ref[:, 2 * d:3 * d]
    xn = x_ref[...] + gate * y
    if final_norm:
        ms = jnp.mean(xn * xn, axis=-1, keepdims=True)
        xn = xn * lax.rsqrt(ms + NORM_EPS) * fg_ref[...]
    o_ref[...] = xn


def _outproj_call(act, x, mods_l, w_out, final_g, *, n_prompt, sample_len, final_norm, bm=512):
    n_tok, d = x.shape
    k = act.shape[1]
    npb, bps = n_prompt // bm, sample_len // bm
    cond = functools.partial(_cond_of_block, n_prompt_blocks=npb, blocks_per_sample=bps)
    return pl.pallas_call(
        functools.partial(_outproj_kernel, d=d, final_norm=final_norm),
        out_shape=jax.ShapeDtypeStruct((n_tok, d), F32),
        grid=(n_tok // bm,),
        in_specs=[
            pl.BlockSpec((bm, k), lambda i: (i, 0)),
            pl.BlockSpec((bm, d), lambda i: (i, 0)),
            pl.BlockSpec((None, 1, 3 * d), lambda i: (cond(i), 0, 0)),
            pl.BlockSpec((k, d), lambda i: (0, 0)),
            pl.BlockSpec((1, d), lambda i: (0, 0)),
        ],
        out_specs=pl.BlockSpec((bm, d), lambda i: (i, 0)),
        compiler_params=_cparams("arbitrary"),
        name="outproj_residual",
    )(act, x, mods_l, w_out, final_g.reshape(1, d))


def _s5_lane_perm(n_groups):
    lane = np.arange(2 * LANES)
    half, blk, c = lane // LANES, (lane % LANES) // S5_GROUP, lane % S5_GROUP
    g8 = (np.arange(n_groups) % 8)[:, None]
    t = 8 * half[None, :] + (blk[None, :] - g8) % 8
    return t * S5_GROUP + c[None, :]


def _s5_prep(lam_re, lam_im, log_step, b_re, b_im, c_re, c_im):
    hp = lax.Precision.HIGHEST
    t_chunk = S5_CHUNK
    n_groups, n_state = lam_re.shape[1], lam_re.shape[2]
    lam = lax.complex(lam_re.astype(F32), lam_im.astype(F32))
    step = jnp.exp(log_step.astype(F32))[..., None]
    lam_bar = jnp.exp(lam * step)
    b_bar = ((lam_bar - 1.0) / lam)[..., None] * lax.complex(b_re.astype(F32), b_im.astype(F32))
    c_mat = lax.complex(c_re.astype(F32), c_im.astype(F32))
    ks = jnp.arange(t_chunk + 1, dtype=F32)[:, None, None, None]
    pw = jnp.exp(ks * (lam * step)[None])

    cb = jnp.einsum('dgop,ldgp,dgpi->ldgoi', c_mat, pw[:t_chunk], b_bar, precision=hp).real
    tau = np.arange(t_chunk)
    lag_f = tau[None, :] - tau[:, None]
    kf = jnp.where((lag_f >= 0)[:, :, None, None, None], cb[np.clip(lag_f, 0, None), 0], 0.0)
    kb = jnp.where((lag_f <= 0)[:, :, None, None, None], cb[np.clip(-lag_f, 0, None), 1], 0.0)
    ktot = (kf + kb).transpose(2, 0, 4, 1, 3)
    ktot = ktot.reshape(n_groups, t_chunk * S5_GROUP, t_chunk * S5_GROUP)

    pin_f = pw[t_chunk - 1 - tau, 0][..., None] * b_bar[0][None]
    pin_b = pw[tau, 1][..., None] * b_bar[1][None]

    def rows_sc(m):
        return m.transpose(1, 0, 3, 2).reshape(n_groups, t_chunk * S5_GROUP, n_state)

    w1 = jnp.concatenate([ktot, rows_sc(pin_f.real), rows_sc(pin_b.real),
                          rows_sc(pin_f.imag), rows_sc(pin_b.imag)], axis=-1)

    po_f = c_mat[0][None] * pw[tau + 1, 0][:, :, None, :]
    po_b = c_mat[1][None] * pw[t_chunk - tau, 1][:, :, None, :]

    def rows_p(m):
        return m.transpose(1, 3, 0, 2).reshape(n_groups, n_state, t_chunk * S5_GROUP)

    pout = jnp.concatenate([rows_p(po_f.real), rows_p(po_b.real),
                            rows_p(-po_f.imag), rows_p(-po_b.imag)], axis=1)

    perm = jnp.asarray(_s5_lane_perm(n_groups))
    w1 = jnp.take_along_axis(w1, perm[:, :, None], axis=1)
    w1 = jnp.concatenate([jnp.take_along_axis(w1[:, :, :2 * LANES], perm[:, None, :], axis=2),
                          w1[:, :, 2 * LANES:]], axis=-1)
    pout = jnp.take_along_axis(pout, perm[:, None, :], axis=2)

    l16 = pw[t_chunk]
    lam_rows = jnp.stack([jnp.concatenate([l16[0].real, l16[1].real], axis=-1),
                          jnp.concatenate([l16[0].imag, l16[1].imag], axis=-1)], axis=1)
    lam_rows = jnp.concatenate([lam_rows, jnp.zeros((n_groups, SUBLANES - 2, 2 * n_state), F32)], axis=1)
    return w1.astype(BF16), pout.astype(BF16), lam_rows


def _lane_block_masks():
    blk = lax.broadcasted_iota(jnp.int32, (1, LANES), 1) // S5_GROUP
    return [blk == b for b in range(8)]


def _s5_to_chunks_kernel(u_ref, x_ref, *, n_groups):
    width = n_groups * S5_GROUP
    masks = _lane_block_masks()
    for o in range(width // LANES):
        rolled = []
        for t in range(S5_CHUNK):
            v = u_ref[:, t * width + o * LANES:t * width + (o + 1) * LANES]
            s = (t % 8) * S5_GROUP
            rolled.append(pltpu.roll(v, s, 1) if s else v)
        for g8 in range(8):
            for half in range(2):
                acc = rolled[8 * half + (0 - g8) % 8]
                for blk in range(1, 8):
                    acc = jnp.where(masks[blk], rolled[8 * half + (blk - g8) % 8], acc)
                x_ref[o * 8 + g8, :, half * LANES:(half + 1) * LANES] = acc.astype(BF16)


def _s5_from_chunks_kernel(y_ref, u_ref, d_ref, o_ref, *, n_groups):
    width = n_groups * S5_GROUP
    masks = _lane_block_masks()
    for o in range(width // LANES):
        d_vec = d_ref[:, o * LANES:(o + 1) * LANES]
        for t in range(S5_CHUNK):
            half, t8 = t // 8, t % 8
            acc = y_ref[o * 8 + (0 - t8) % 8, :, half * LANES:(half + 1) * LANES]
            for blk in range(1, 8):
                src = y_ref[o * 8 + (blk - t8) % 8, :, half * LANES:(half + 1) * LANES]
                acc = jnp.where(masks[blk], src, acc)
            s = ((8 - t8) % 8) * S5_GROUP
            nat = pltpu.roll(acc, s, 1) if s else acc
            sl = slice(t * width + o * LANES, t * width + (o + 1) * LANES)
            o_ref[:, sl] = _gelu_tanh(nat + d_vec * u_ref[:, sl])


def _s5_chunk_kernel(x_ref, w1_ref, po_ref, lam_ref, h0r_ref, h0i_ref,
                     y_ref, fr_ref, fi_ref, r_scr, st_scr, *, gb, n_seq, n_chunks, chunk_major):
    lane = lax.broadcasted_iota(jnp.int32, (1, LANES), 1)
    fwd_lanes = lane < (LANES // 2)
    for g in range(gb):
        r_scr[g] = jnp.dot(x_ref[g], w1_ref[g], preferred_element_type=F32)
    for g in range(gb):
        ar = lam_ref[g, 0:1, :]
        ai = lam_ref[g, 1:2, :]
        if chunk_major:
            s_re = h0r_ref[g]
            s_im = h0i_ref[g]
            for i in range(n_chunks):
                k = n_chunks - 1 - i
                rf = slice(i * n_seq, (i + 1) * n_seq)
                rb = slice(k * n_seq, (k + 1) * n_seq)
                st_scr[g, rf, 0:64] = s_re[:, 0:64]
                st_scr[g, rb, 64:128] = s_re[:, 64:128]
                st_scr[g, rf, 128:192] = s_im[:, 0:64]
                st_scr[g, rb, 192:256] = s_im[:, 64:128]
                v_re = jnp.where(fwd_lanes, r_scr[g, rf, 256:384], r_scr[g, rb, 256:384])
                v_im = jnp.where(fwd_lanes, r_scr[g, rf, 384:512], r_scr[g, rb, 384:512])
                s_re, s_im = ar * s_re - ai * s_im + v_re, ar * s_im + ai * s_re + v_im
            fr_ref[g] = s_re
            fi_ref[g] = s_im
        else:
            row = lax.broadcasted_iota(jnp.int32, (SUBLANES, LANES), 0)
            n_tiles = n_chunks // SUBLANES
            fr_ref[g] = jnp.zeros(fr_ref.shape[1:], F32)
            fi_ref[g] = jnp.zeros(fi_ref.shape[1:], F32)
            for b in range(n_seq):
                base = b * n_chunks

                def tile_step(v, carry, base=base, g=g, ar=ar, ai=ai):
                    f_re, f_im, b_re, b_im = carry
                    rf = pl.ds(pl.multiple_of(base + v * SUBLANES, SUBLANES), SUBLANES)
                    rb = pl.ds(pl.multiple_of(base + (n_tiles - 1 - v) * SUBLANES, SUBLANES), SUBLANES)
                    vf_re = r_scr[g, rf, 256:384]
                    vf_im = r_scr[g, rf, 384:512]
                    vb_re = r_scr[g, rb, 256:384]
                    vb_im = r_scr[g, rb, 384:512]
                    ef_re = jnp.zeros((SUBLANES, LANES), F32)
                    ef_im, eb_re, eb_im = ef_re, ef_re, ef_re
                    for r in range(SUBLANES):
                        f_re = pltpu.roll(f_re, 1, 0)
                        f_im = pltpu.roll(f_im, 1, 0)
                        b_re = pltpu.roll(b_re, SUBLANES - 1, 0)
                        b_im = pltpu.roll(b_im, SUBLANES - 1, 0)
                        mf = row == r
                        mb = row == (SUBLANES - 1 - r)
                        ef_re = jnp.where(mf, f_re, ef_re)
                        ef_im = jnp.where(mf, f_im, ef_im)
                        eb_re = jnp.where(mb, b_re, eb_re)
                        eb_im = jnp.where(mb, b_im, eb_im)
                        f_re, f_im = ar * f_re - ai * f_im + vf_re, ar * f_im + ai * f_re + vf_im
                        b_re, b_im = ar * b_re - ai * b_im + vb_re, ar * b_im + ai * b_re + vb_im
                    st_scr[g, rf, 0:64] = ef_re[:, 0:64]
                    st_scr[g, rf, 128:192] = ef_im[:, 0:64]
                    st_scr[g, rb, 64:128] = eb_re[:, 64:128]
                    st_scr[g, rb, 192:256] = eb_im[:, 64:128]
                    return f_re, f_im, b_re, b_im

                h_re = jnp.broadcast_to(h0r_ref[g, b:b + 1, :], (SUBLANES, LANES))
                h_im = jnp.broadcast_to(h0i_ref[g, b:b + 1, :], (SUBLANES, LANES))
                f_re, f_im, b_re, b_im = lax.fori_loop(0, n_tiles, tile_step, (h_re, h_im, h_re, h_im))
                fr_ref[g, b:b + 1, :] = jnp.where(fwd_lanes, f_re[SUBLANES - 1:SUBLANES, :], b_re[0:1, :])
                fi_ref[g, b:b + 1, :] = jnp.where(fwd_lanes, f_im[SUBLANES - 1:SUBLANES, :], b_im[0:1, :])
    for g in range(gb):
        y_ref[g] = r_scr[g, :, 0:256] + jnp.dot(st_scr[g].astype(BF16), po_ref[g],
                                                preferred_element_type=F32)


def _s5_chunk_call(xc, w1, pout, lam_rows, h0_re, h0_im, *, n_seq, n_chunks, chunk_major, gb=4):
    n_groups, rows, _ = xc.shape
    s8 = h0_re.shape[1]
    kern = functools.partial(_s5_chunk_kernel, gb=gb, n_seq=n_seq, n_chunks=n_chunks,
                             chunk_major=chunk_major)
    g3 = lambda i: (i, 0, 0)
    return pl.pallas_call(
        kern,
        out_shape=[jax.ShapeDtypeStruct((n_groups, rows, 2 * LANES), F32),
                   jax.ShapeDtypeStruct((n_groups, s8, LANES), F32),
                   jax.ShapeDtypeStruct((n_groups, s8, LANES), F32)],
        grid=(n_groups // gb,),
        in_specs=[
            pl.BlockSpec((gb, rows, 2 * LANES), g3),
            pl.BlockSpec((gb, 2 * LANES, 4 * LANES), g3),
            pl.BlockSpec((gb, 2 * LANES, 2 * LANES), g3),
            pl.BlockSpec((gb, SUBLANES, LANES), g3),
            pl.BlockSpec((gb, s8, LANES), g3),
            pl.BlockSpec((gb, s8, LANES), g3),
        ],
        out_specs=[pl.BlockSpec((gb, rows, 2 * LANES), g3),
                   pl.BlockSpec((gb, s8, LANES), g3),
                   pl.BlockSpec((gb, s8, LANES), g3)],
        scratch_shapes=[pltpu.VMEM((gb, rows, 4 * LANES), F32),
                        pltpu.VMEM((gb, rows, 2 * LANES), F32)],
        compiler_params=_cparams("arbitrary"),
        name="s5_chunk_scan",
    )(xc, w1, pout, lam_rows, h0_re, h0_im)


def _s5_to_chunks_call(u, *, row0, n_seq, seq_len, chunk_major, rows_per_step=32):
    n_tok, width = u.shape
    n_groups = width // S5_GROUP
    n_chunks = seq_len // S5_CHUNK
    cw = S5_CHUNK * width
    kern = functools.partial(_s5_to_chunks_kernel, n_groups=n_groups)
    if chunk_major:
        assert row0 == 0 and n_tok % seq_len == 0
        out = pl.pallas_call(
            kern,
            out_shape=jax.ShapeDtypeStruct((n_groups, n_chunks, n_seq, 2 * LANES), BF16),
            grid=(n_chunks,),
            in_specs=[pl.BlockSpec((n_seq, cw), lambda j: (0, j))],
            out_specs=pl.BlockSpec((n_groups, None, n_seq, 2 * LANES), lambda j: (0, j, 0, 0)),
            compiler_params=_cparams("arbitrary"),
            name="s5_to_chunks_cm",
        )(u.reshape(n_tok // seq_len, n_chunks * cw))
        return out.reshape(n_groups, n_chunks * n_seq, 2 * LANES)
    r = rows_per_step
    rows = n_seq * n_chunks
    blk0 = row0 // S5_CHUNK // r
    assert rows % r == 0 and row0 % (S5_CHUNK * r) == 0
    return pl.pallas_call(
        kern,
        out_shape=jax.ShapeDtypeStruct((n_groups, rows, 2 * LANES), BF16),
        grid=(rows // r,),
        in_specs=[pl.BlockSpec((r, cw), lambda i: (i + blk0, 0))],
        out_specs=pl.BlockSpec((n_groups, r, 2 * LANES), lambda i: (0, i, 0)),
        compiler_params=_cparams("arbitrary"),
        name="s5_to_chunks_sm",
    )(u.reshape(n_tok // S5_CHUNK, cw))


def _s5_from_chunks_call(yc, u, d_skip, prev, *, row0, n_seq, seq_len, chunk_major, rows_per_step=32):
    n_tok, width = u.shape
    n_groups = width // S5_GROUP
    n_chunks = seq_len // S5_CHUNK
    cw = S5_CHUNK * width
    kern = functools.partial(_s5_from_chunks_kernel, n_groups=n_groups)
    d2 = d_skip.reshape(1, width).astype(F32)
    if chunk_major:
        assert row0 == 0 and n_tok % seq_len == 0
        view = (n_tok // seq_len, n_chunks * cw)
        grid = (n_chunks,)
        yc_in = yc.reshape(n_groups, n_chunks, n_seq, 2 * LANES)
        y_spec = pl.BlockSpec((n_groups, None, n_seq, 2 * LANES), lambda j: (0, j, 0, 0))
        u_spec = pl.BlockSpec((n_seq, cw), lambda j: (0, j))
        d_spec = pl.BlockSpec((1, width), lambda j: (0, 0))
        name = "s5_from_chunks_cm"
    else:
        r = rows_per_step
        rows = n_seq * n_chunks
        blk0 = row0 // S5_CHUNK // r
        assert rows % r == 0 and row0 % (S5_CHUNK * r) == 0
        view = (n_tok // S5_CHUNK, cw)
        grid = (rows // r,)
        yc_in = yc
        y_spec = pl.BlockSpec((n_groups, r, 2 * LANES), lambda i: (0, i, 0))
        u_spec = pl.BlockSpec((r, cw), lambda i: (i + blk0, 0))
        d_spec = pl.BlockSpec((1, width), lambda i: (0, 0))
        name = "s5_from_chunks_sm"
    args = [yc_in, u.reshape(view), d2]
    in_specs = [y_spec, u_spec, d_spec]
    aliases = {}
    if prev is not None:
        def kern_alias(y_ref, u_ref, d_ref, prev_ref, o_ref, _k=kern):
            del prev_ref
            _k(y_ref, u_ref, d_ref, o_ref)
        body = kern_alias
        args.append(prev.reshape(view))
        in_specs.append(pl.BlockSpec(memory_space=pl.ANY))
        aliases = {3: 0}
    else:
        body = kern
    out = pl.pallas_call(
        body,
        out_shape=jax.ShapeDtypeStruct(view, F32),
        grid=grid,
        in_specs=in_specs,
        out_specs=u_spec,
        input_output_aliases=aliases,
        compiler_params=_cparams("arbitrary"),
        name=name,
    )(*args)
    return out.reshape(n_tok, width)


def _glu_kernel(y_ref, z_ref, w_ref, b_ref, o_ref, *, n_chunk):
    yb = y_ref[...].astype(BF16)
    n = w_ref.shape[1]
    for c in range(0, n, n_chunk):
        sl = slice(c, c + n_chunk)
        gate = _sigmoid(jnp.dot(yb, w_ref[:, sl], preferred_element_type=F32) + b_ref[:, sl])
        o_ref[:, sl] = (y_ref[:, sl] * gate * _silu(z_ref[:, sl].astype(F32))).astype(o_ref.dtype)


def _glu_call(y, z, glu_w, glu_b, *, bm=512):
    n_tok, width = y.shape
    return pl.pallas_call(
        functools.partial(_glu_kernel, n_chunk=512),
        out_shape=jax.ShapeDtypeStruct((n_tok, width), BF16),
        grid=(n_tok // bm,),
        in_specs=[
            pl.BlockSpec((bm, width), lambda i: (i, 0)),
            pl.BlockSpec((bm, width), lambda i: (i, 0)),
            pl.BlockSpec((width, width), lambda i: (0, 0)),
            pl.BlockSpec((1, width), lambda i: (0, 0)),
        ],
        out_specs=pl.BlockSpec((bm, width), lambda i: (i, 0)),
        compiler_params=_cparams("arbitrary"),
        name="s5_glu_gate",
    )(y, z, glu_w, glu_b.reshape(1, width).astype(F32))


def _s5_mix(u, z, prm, st_re, st_im, *, n_prompt_seq, prompt_len, n_sample_seq, sample_len, bm):
    lam_re, lam_im, log_step, b_re, b_im, c_re, c_im, d_skip, glu_w, glu_b = prm
    n_groups, n_state = lam_re.shape[1], lam_re.shape[2]
    n_prompt = n_prompt_seq * prompt_len
    w1, pout, lam_rows = _s5_prep(lam_re, lam_im, log_step, b_re, b_im, c_re, c_im)

    def state_rows(s, n_rows):
        t = jnp.concatenate([s[:, 0], s[:, 1]], axis=-1).transpose(1, 0, 2).astype(F32)
        return jnp.pad(t, ((0, 0), (0, n_rows - t.shape[1]), (0, 0)))

    zeros_p = jnp.zeros((n_groups, n_prompt_seq, 2 * n_state), F32)
    xc_p = _s5_to_chunks_call(u, row0=0, n_seq=n_prompt_seq, seq_len=prompt_len, chunk_major=True)
    yc_p, fr, fi = _s5_chunk_call(xc_p, w1, pout, lam_rows, zeros_p, zeros_p, n_seq=n_prompt_seq,
                                  n_chunks=prompt_len // S5_CHUNK, chunk_major=True)
    xc_s = _s5_to_chunks_call(u, row0=n_prompt, n_seq=n_sample_seq, seq_len=sample_len, chunk_major=False)
    yc_s, _, _ = _s5_chunk_call(xc_s, w1, pout, lam_rows, state_rows(st_re, SUBLANES),
                                state_rows(st_im, SUBLANES), n_seq=n_sample_seq,
                                n_chunks=sample_len // S5_CHUNK, chunk_major=False)
    y = _s5_from_chunks_call(yc_p, u, d_skip, None, row0=0, n_seq=n_prompt_seq, seq_len=prompt_len,
                             chunk_major=True)
    y = _s5_from_chunks_call(yc_s, u, d_skip, y, row0=n_prompt, n_seq=n_sample_seq, seq_len=sample_len,
                             chunk_major=False)
    act = _glu_call(y, z, glu_w, glu_b, bm=bm)

    def unpack(f):
        return jnp.stack([f[:, :, :n_state], f[:, :, n_state:]], axis=0).transpose(2, 0, 1, 3)

    return act, unpack(fr), unpack(fi)


def _pool_kernel(u_ref, z_ref, w_ref, s_ref, o_ref, *, n_prompt_blocks, prompt_len, sample_len):
    rows = u_ref.shape[0]
    seq_len = jnp.where(pl.program_id(0) < n_prompt_blocks, prompt_len, sample_len)
    t = lax.broadcasted_iota(jnp.int32, (rows, 1), 0) & (seq_len - 1)

    def later(x, k):
        return jnp.where(t + k < seq_len, pltpu.roll(x, rows - k, 0), 0.0)

    def earlier(x, k):
        return jnp.where(t >= k, pltpu.roll(x, k, 0), 0.0)

    def body(win):
        lo = win // 2
        u = u_ref[...]
        fwd = u
        bwd = earlier(u, 1)
        s = 1
        while s < lo:
            fwd = fwd + later(fwd, s)
            bwd = bwd + earlier(bwd, s)
            s *= 2
        cnt = jnp.minimum(t - lo + win, seq_len) - jnp.maximum(t - lo, 0)
        p = (fwd + bwd) / cnt.astype(F32) - u
        m = jnp.dot(p.astype(BF16), w_ref[...], preferred_element_type=F32) * s_ref[...]
        o_ref[...] = (m * _silu(z_ref[...].astype(F32))).astype(o_ref.dtype)

    for gi, win in enumerate(POOL_WINDOWS):
        pl.when(pl.program_id(1) == gi)(functools.partial(body, win))


def _pool_call(u, z, pool_w, pool_scale, *, n_prompt, prompt_len, sample_len, rows=2048):
    n_tok, width = u.shape
    n_groups = len(POOL_WINDOWS)
    cg = width // n_groups
    assert prompt_len & (prompt_len - 1) == 0 and sample_len & (sample_len - 1) == 0
    assert rows % prompt_len == 0 and rows % sample_len == 0 and n_prompt % rows == 0
    kern = functools.partial(_pool_kernel, n_prompt_blocks=n_prompt // rows, prompt_len=prompt_len,
                             sample_len=sample_len)
    return pl.pallas_call(
        kern,
        out_shape=jax.ShapeDtypeStruct((n_tok, width), BF16),
        grid=(n_tok // rows, n_groups),
        in_specs=[
            pl.BlockSpec((rows, cg), lambda i, g: (i, g)),
            pl.BlockSpec((rows, cg), lambda i, g: (i, g)),
            pl.BlockSpec((None, cg, cg), lambda i, g: (g, 0, 0)),
            pl.BlockSpec((1, cg), lambda i, g: (0, g)),
        ],
        out_specs=pl.BlockSpec((rows, cg), lambda i, g: (i, g)),
        compiler_params=_cparams("arbitrary", "arbitrary"),
        name="pool_mix",
    )(u, z, pool_w, pool_scale.reshape(1, width).astype(F32))


MLA_QW = 2 * LANES

_ROT_SRC = np.concatenate([np.arange(16, 32), np.arange(0, 16), np.arange(48, 64), np.arange(32, 48)])
_ROT_SIGN = np.concatenate([-np.ones(16), np.ones(16), -np.ones(16), np.ones(16)]).astype(np.float32)


def _rope_tables(n_prompt, n_sample_seq, sample_len):
    half = MLA_ROPE // 4
    tok = jnp.arange(sample_len)
    row = (tok // GRID_W).astype(F32)
    col = (tok % GRID_W).astype(F32)
    inv = ROPE_THETA ** (-jnp.arange(half, dtype=F32) / half)
    a_row, a_col = row[:, None] * inv, col[:, None] * inv
    cos = jnp.concatenate([jnp.cos(a_row), jnp.cos(a_row), jnp.cos(a_col), jnp.cos(a_col)], axis=-1)
    sin = jnp.concatenate([jnp.sin(a_row), jnp.sin(a_row), jnp.sin(a_col), jnp.sin(a_col)], axis=-1)
    pad = jnp.zeros((sample_len, LANES - MLA_ROPE), F32)
    cos_s = jnp.tile(jnp.concatenate([cos, pad], axis=-1), (n_sample_seq, 1))
    sin_s = jnp.tile(jnp.concatenate([sin, pad], axis=-1), (n_sample_seq, 1))
    cos_p = jnp.concatenate([jnp.ones((n_prompt, MLA_ROPE), F32), jnp.zeros((n_prompt, LANES - MLA_ROPE), F32)], -1)
    return jnp.concatenate([cos_p, cos_s]), jnp.concatenate([jnp.zeros((n_prompt, LANES), F32), sin_s])


def _rms(x, g):
    return x * lax.rsqrt(jnp.mean(x * x, axis=-1, keepdims=True) + NORM_EPS) * g


def _mla_post_kernel(sm_ref, cos_ref, sin_ref, qn_ref, kn_ref, wa_ref, wb_ref,
                     q_ref, ckv_ref, kpe_ref, *, q_rank, kv_rank, heads_per_dot):
    cosp, sinp = cos_ref[...], sin_ref[...]
    qn = _rms(sm_ref[:, 0:q_rank], qn_ref[...]).astype(BF16)
    for h0 in range(0, MLA_HEADS, heads_per_dot):
        a = jnp.dot(qn, wa_ref[:, h0 * MLA_QW:(h0 + heads_per_dot) * MLA_QW], preferred_element_type=F32)
        b = jnp.dot(qn, wb_ref[:, h0 * LANES:(h0 + heads_per_dot) * LANES], preferred_element_type=F32)
        for j in range(heads_per_dot):
            h = h0 + j
            q_ref[:, h * MLA_QW:h * MLA_QW + LANES] = a[:, j * MLA_QW:j * MLA_QW + LANES].astype(BF16)
            pe = a[:, j * MLA_QW + LANES:(j + 1) * MLA_QW] * cosp + b[:, j * LANES:(j + 1) * LANES] * sinp
            q_ref[:, h * MLA_QW + LANES:(h + 1) * MLA_QW] = pe.astype(BF16)
    c0 = q_rank
    ckv_ref[...] = _rms(sm_ref[:, c0:c0 + kv_rank], kn_ref[...])
    k0 = c0 + kv_rank
    kpe_ref[...] = (sm_ref[:, k0:k0 + LANES] * cosp + sm_ref[:, k0 + LANES:k0 + 2 * LANES] * sinp).astype(BF16)


def _mla_post_call(small, cos_t, sin_t, q_norm, kv_norm, wq_a, wq_b, *, bm=512):
    n_tok, ws = small.shape
    q_rank, kv_rank = q_norm.shape[-1], kv_norm.shape[-1]
    row = lambda i: (i, 0)
    fix = lambda i: (0, 0)
    kern = functools.partial(_mla_post_kernel, q_rank=q_rank, kv_rank=kv_rank, heads_per_dot=4)
    return pl.pallas_call(
        kern,
        out_shape=[jax.ShapeDtypeStruct((n_tok, MLA_HEADS * MLA_QW), BF16),
                   jax.ShapeDtypeStruct((n_tok, kv_rank), F32),
                   jax.ShapeDtypeStruct((n_tok, LANES), BF16)],
        grid=(n_tok // bm,),
        in_specs=[
            pl.BlockSpec((bm, ws), row),
            pl.BlockSpec((bm, LANES), row),
            pl.BlockSpec((bm, LANES), row),
            pl.BlockSpec((1, q_rank), fix),
            pl.BlockSpec((1, kv_rank), fix),
            pl.BlockSpec(wq_a.shape, fix),
            pl.BlockSpec(wq_b.shape, fix),
        ],
        out_specs=[pl.BlockSpec((bm, MLA_HEADS * MLA_QW), row),
                   pl.BlockSpec((bm, kv_rank), row),
                   pl.BlockSpec((bm, LANES), row)],
        compiler_params=_cparams("arbitrary"),
        name="mla_q_rope",
    )(small, cos_t, sin_t, q_norm.reshape(1, q_rank).astype(F32), kv_norm.reshape(1, kv_rank).astype(F32),
      wq_a, wq_b)


def _kv_expand_kernel(c_ref, w_ref, o_ref, *, n_chunk):
    c = c_ref[...].astype(BF16)
    n = w_ref.shape[1]
    for s in range(0, n, n_chunk):
        o_ref[:, s:s + n_chunk] = jnp.dot(c, w_ref[:, s:s + n_chunk],
                                          preferred_element_type=F32).astype(o_ref.dtype)


def _kv_expand_call(ckv, wkv_b, *, bm=512):
    rows, kr = ckv.shape
    n = wkv_b.shape[1]
    return pl.pallas_call(
        functools.partial(_kv_expand_kernel, n_chunk=1024),
        out_shape=jax.ShapeDtypeStruct((rows, n), BF16),
        grid=(rows // bm,),
        in_specs=[pl.BlockSpec((bm, kr), lambda i: (i, 0)), pl.BlockSpec((kr, n), lambda i: (0, 0))],
        out_specs=pl.BlockSpec((bm, n), lambda i: (i, 0)),
        compiler_params=_cparams("arbitrary"),
        name="mla_kv_expand",
    )(ckv, wkv_b)


def _attn_kernel(q_ref, kv_ref, kpe_ref, z_ref, *rest, hg, scale):
    o_ref = rest[-1]
    kpe = kpe_ref[...]
    for j in range(hg):
        q = q_ref[:, j * MLA_QW:(j + 1) * MLA_QW]
        kcat = jnp.concatenate([kv_ref[:, j * 2 * LANES:j * 2 * LANES + LANES], kpe], axis=1)
        s = lax.dot_general(q, kcat, (((1,), (1,)), ((), ())), preferred_element_type=F32) * scale
        e = jnp.exp(s - jnp.max(s, axis=-1, keepdims=True))
        l = jnp.sum(e, axis=-1, keepdims=True)
        v = kv_ref[:, j * 2 * LANES + LANES:(j + 1) * 2 * LANES]
        o = jnp.dot(e.astype(BF16), v, preferred_element_type=F32) / l
        zs = slice(j * MLA_V, (j + 1) * MLA_V)
        o_ref[:, zs] = (o * _silu(z_ref[:, zs].astype(F32))).astype(o_ref.dtype)


def _attn_call(q, kv, kpe, z, prev, *, q_row0, n_seq, q_len, k_len, hg, qb):
    n_tok = q.shape[0]
    width = MLA_HEADS * MLA_V
    nqb = q_len // qb
    qb0 = q_row0 // qb
    assert q_row0 % qb == 0 and q_len % qb == 0
    scale = float((MLA_NOPE + MLA_ROPE) ** -0.5)
    qrow = lambda b, g, i: (qb0 + b * nqb + i, g)
    in_specs = [
        pl.BlockSpec((qb, hg * MLA_QW), qrow),
        pl.BlockSpec((k_len, hg * 2 * LANES), lambda b, g, i: (b, g)),
        pl.BlockSpec((k_len, LANES), lambda b, g, i: (b, 0)),
        pl.BlockSpec((qb, hg * MLA_V), qrow),
    ]
    args = [q, kv, kpe, z]
    aliases = {}
    if prev is not None:
        in_specs.append(pl.BlockSpec(memory_space=pl.ANY))
        args.append(prev)
        aliases = {4: 0}
    return pl.pallas_call(
        functools.partial(_attn_kernel, hg=hg, scale=scale),
        out_shape=jax.ShapeDtypeStruct((n_tok, width), BF16),
        grid=(n_seq, MLA_HEADS // hg, nqb),
        in_specs=in_specs,
        out_specs=pl.BlockSpec((qb, hg * MLA_V), qrow),
        input_output_aliases=aliases,
        compiler_params=_cparams("arbitrary", "arbitrary", "arbitrary"),
        name="mla_attention",
    )(*args)


def _mla_weights(w_in, wq_b):
    q_rank = wq_b.shape[0]
    kv_rank = w_in.shape[1] - q_rank - MLA_ROPE - MLA_HEADS * MLA_V
    d = w_in.shape[0]
    c_kpe = q_rank + kv_rank
    zpad = jnp.zeros((d, LANES - MLA_ROPE), w_in.dtype)
    kpe_w = w_in[:, c_kpe:c_kpe + MLA_ROPE]
    w_small = jnp.concatenate([w_in[:, :c_kpe], kpe_w, zpad,
                               kpe_w[:, _ROT_SRC] * _ROT_SIGN, zpad], axis=1)
    w_z = w_in[:, c_kpe + MLA_ROPE:]
    hd = MLA_NOPE + MLA_ROPE
    wq3 = wq_b.reshape(q_rank, MLA_HEADS, hd)
    pe = wq3[:, :, MLA_NOPE:]
    z3 = jnp.zeros((q_rank, MLA_HEADS, LANES - MLA_ROPE), wq_b.dtype)
    wq_a = jnp.concatenate([wq3, z3], axis=-1).reshape(q_rank, MLA_HEADS * MLA_QW)
    wq_r = jnp.concatenate([pe[:, :, _ROT_SRC] * _ROT_SIGN, z3], axis=-1).reshape(q_rank, MLA_HEADS * LANES)
    return w_small.astype(BF16), w_z.astype(BF16), wq_a.astype(BF16), wq_r.astype(BF16)


def kernel(x_prompt, x_sample, state_s5_re, state_s5_im, cache_ckv, cache_kpe, c, c_ctx, norm_g, ada_w, ada_b, final_norm_g, s5_w_in, s5_lam_re, s5_lam_im, s5_log_step, s5_b_re, s5_b_im, s5_c_re, s5_c_im, s5_d, s5_glu_w, s5_glu_b, s5_w_out, pool_w_in, pool_w, pool_scale, pool_w_out, mla_w_in, mla_q_norm, mla_wq_b, mla_kv_norm, mla_wkv_b, mla_w_out):
    n_pseq, p_len, d = x_prompt.shape
    n_sseq, s_len, _ = x_sample.shape
    depth = norm_g.shape[0]
    n_prompt = n_pseq * p_len
    bm = 512
    geo = dict(n_prompt=n_prompt, sample_len=s_len, bm=bm)

    x = jnp.concatenate([x_prompt.reshape(n_prompt, d), x_sample.reshape(n_sseq * s_len, d)], axis=0)
    conds = jnp.concatenate([c_ctx[None, :], c, jnp.zeros((SUBLANES - 1 - n_sseq, d), F32)], axis=0)
    mods = _ada_call(conds.astype(F32), ada_w, ada_b)
    mods = mods.reshape(depth, SUBLANES, 1, 3 * d)

    new_re, new_im, new_ckv, new_kpe = [], [], [], []
    for layer in range(depth):
        kind, j = layer % N_MIXERS, layer // N_MIXERS
        last = layer == depth - 1
        ml = mods[layer]
        if kind == 0:
            width = s5_w_in.shape[2] // 2
            w = s5_w_in[j].astype(BF16)
            u, z = _inproj_call(x, ml, norm_g[layer], [w[:, :width], w[:, width:]], [F32, BF16], **geo)
            prm = (s5_lam_re[j], s5_lam_im[j], s5_log_step[j], s5_b_re[j], s5_b_im[j], s5_c_re[j],
                   s5_c_im[j], s5_d[j], s5_glu_w[j].astype(BF16), s5_glu_b[j])
            act, f_re, f_im = _s5_mix(u, z, prm, state_s5_re[:, j], state_s5_im[:, j], n_prompt_seq=n_pseq,
                                      prompt_len=p_len, n_sample_seq=n_sseq, sample_len=s_len, bm=bm)
            new_re.append(f_re)
            new_im.append(f_im)
            w_out = s5_w_out[j]
        elif kind == 1:
            width = pool_w_in.shape[2] // 2
            w = pool_w_in[j].astype(BF16)
            u, z = _inproj_call(x, ml, norm_g[layer], [w[:, :width], w[:, width:]], [F32, BF16], **geo)
            act = _pool_call(u, z, pool_w[j].astype(BF16), pool_scale[j], n_prompt=n_prompt,
                             prompt_len=p_len, sample_len=s_len)
            w_out = pool_w_out[j]
        else:
            q_rank, kv_rank = mla_q_norm.shape[-1], mla_kv_norm.shape[-1]
            w_small, w_z, wq_a, wq_r = _mla_weights(mla_w_in[j], mla_wq_b[j])
            small, z = _inproj_call(x, ml, norm_g[layer], [w_small, w_z], [F32, BF16], **geo)
            cos_t, sin_t = _rope_tables(n_prompt, n_sseq, s_len)
            q, ckv_n, kpe_k = _mla_post_call(small, cos_t, sin_t, mla_q_norm[j], mla_kv_norm[j], wq_a, wq_r,
                                             bm=bm)
            wkv = mla_wkv_b[j].astype(BF16)
            past = cache_ckv.shape[2]
            k_len = past + s_len
            ckv_s = jnp.concatenate([cache_ckv[:, j].astype(F32), ckv_n[n_prompt:].reshape(n_sseq, s_len, kv_rank)],
                                    axis=1).reshape(n_sseq * k_len, kv_rank)
            kpe_cache = jnp.concatenate([cache_kpe[:, j].astype(BF16),
                                         jnp.zeros((n_sseq, past, LANES - MLA_ROPE), BF16)], axis=-1)
            kpe_s = jnp.concatenate([kpe_cache, kpe_k[n_prompt:].reshape(n_sseq, s_len, LANES)],
                                    axis=1).reshape(n_sseq * k_len, LANES)
            kv_p = _kv_expand_call(ckv_n[:n_prompt], wkv, bm=bm)
            kv_s = _kv_expand_call(ckv_s, wkv, bm=bm)
            act = _attn_call(q, kv_p, kpe_k[:n_prompt], z, None, q_row0=0, n_seq=n_pseq, q_len=p_len,
                             k_len=p_len, hg=MLA_HEADS, qb=p_len)
            act = _attn_call(q, kv_s, kpe_s, z, act, q_row0=n_prompt, n_seq=n_sseq, q_len=s_len,
                             k_len=k_len, hg=4, qb=256)
            new_ckv.append(ckv_n[:n_prompt].reshape(n_pseq, p_len, kv_rank))
            c_kpe = q_rank + kv_rank
            new_kpe.append(small[:n_prompt, c_kpe:c_kpe + MLA_ROPE].reshape(n_pseq, p_len, MLA_ROPE))
            w_out = mla_w_out[j]
        x = _outproj_call(act, x, ml, w_out.astype(BF16), final_norm_g, final_norm=last, **geo)

    y_prompt = x[:n_prompt].reshape(n_pseq, p_len, d)
    y_sample = x[n_prompt:].reshape(n_sseq, s_len, d)
    return (y_prompt, y_sample, jnp.stack(new_re, axis=1), jnp.stack(new_im, axis=1),
            jnp.stack(new_ckv, axis=1), jnp.stack(new_kpe, axis=1))
```

```python
import functools
import math

import jax
import jax.numpy as jnp
import numpy as np
from jax import lax
from jax.experimental import pallas as pl
from jax.experimental.pallas import tpu as pltpu

S5_GROUP = 16
S5_CHUNK = 16
POOL_WINDOWS = (2, 4, 8, 16)
MLA_HEADS = 16
MLA_NOPE = 128
MLA_ROPE = 64
MLA_V = 128
GRID_W = 64
ROPE_THETA = 10000.0
NORM_EPS = 1e-6
N_MIXERS = 3

LANES = 128
SUBLANES = 8
VMEM_LIMIT_BYTES = 56 * 1024 * 1024

F32 = jnp.float32
BF16 = jnp.bfloat16
HIGHEST = lax.Precision.HIGHEST


def _cparams(*sem):
    return pltpu.CompilerParams(dimension_semantics=sem, vmem_limit_bytes=VMEM_LIMIT_BYTES)


def _sigmoid(x):
    return 1.0 / (1.0 + jnp.exp(-x))


def _silu(x):
    return x * _sigmoid(x)


def _gelu_tanh(x):
    c = math.sqrt(2.0 / math.pi)
    return 0.5 * x * (1.0 + jnp.tanh(c * (x + 0.044715 * (x * x * x))))


def _ada_kernel(c_ref, w_ref, b_ref, o_ref):
    a = _silu(c_ref[...])
    o_ref[...] = jnp.dot(a, w_ref[...], preferred_element_type=F32, precision=HIGHEST) + b_ref[...]


def _ada_call(conds, ada_w, ada_b):
    depth, d, d3 = ada_w.shape
    c8 = conds.shape[0]
    tn = 512
    return pl.pallas_call(
        _ada_kernel,
        out_shape=jax.ShapeDtypeStruct((depth, c8, d3), F32),
        grid=(depth, d3 // tn),
        in_specs=[
            pl.BlockSpec((c8, d), lambda l, n: (0, 0)),
            pl.BlockSpec((None, d, tn), lambda l, n: (l, 0, n)),
            pl.BlockSpec((None, 1, tn), lambda l, n: (l, 0, n)),
        ],
        out_specs=pl.BlockSpec((None, c8, tn), lambda l, n: (l, 0, n)),
        compiler_params=_cparams("arbitrary", "arbitrary"),
        name="ada_mod",
    )(conds, ada_w, ada_b.reshape(depth, 1, d3))


def _cond_of_block(i, n_prompt_blocks, blocks_per_sample):
    return jnp.where(i < n_prompt_blocks, 0, 1 + (i - n_prompt_blocks) // blocks_per_sample)


def _modulated(x, mod_ref, g_ref, d):
    ms = jnp.mean(x * x, axis=-1, keepdims=True)
    y = x * lax.rsqrt(ms + NORM_EPS) * g_ref[...]
    shift = mod_ref[:, 0:d]
    scale = mod_ref[:, d:2 * d]
    return (y * (1.0 + scale) + shift).astype(BF16)


def _inproj_kernel(x_ref, mod_ref, g_ref, *rest, d, n_chunk):
    n_out = len(rest) // 2
    w_refs, o_refs = rest[:n_out], rest[n_out:]
    h = _modulated(x_ref[...], mod_ref, g_ref, d)
    for w_ref, o_ref in zip(w_refs, o_refs):
        n = w_ref.shape[1]
        for c in range(0, n, n_chunk):
            e = min(c + n_chunk, n)
            r = jnp.dot(h, w_ref[:, c:e], preferred_element_type=F32).astype(o_ref.dtype)
            if len(o_ref.shape) == 3:
                for lb in range((e - c) // LANES):
                    o_ref[c // LANES + lb] = r[:, lb * LANES:(lb + 1) * LANES]
            else:
                o_ref[:, c:e] = r


def _inproj_call(x, mods_l, norm_g, weights, out_dtypes, *, n_prompt, sample_len, bm=512,
                 lane_blocked=()):
    n_tok, d = x.shape
    npb, bps = n_prompt // bm, sample_len // bm
    cond = functools.partial(_cond_of_block, n_prompt_blocks=npb, blocks_per_sample=bps)
    in_specs = [
        pl.BlockSpec((bm, d), lambda i: (i, 0)),
        pl.BlockSpec((None, 1, 3 * d), lambda i: (cond(i), 0, 0)),
        pl.BlockSpec((1, d), lambda i: (0, 0)),
    ] + [pl.BlockSpec(w.shape, lambda i: (0, 0)) for w in weights]
    out_specs, out_shape = [], []
    for k, (w, dt) in enumerate(zip(weights, out_dtypes)):
        n = w.shape[1]
        if k in lane_blocked:
            out_specs.append(pl.BlockSpec((n // LANES, bm, LANES), lambda i: (0, i, 0)))
            out_shape.append(jax.ShapeDtypeStruct((n // LANES, n_tok, LANES), dt))
        else:
            out_specs.append(pl.BlockSpec((bm, n), lambda i: (i, 0)))
            out_shape.append(jax.ShapeDtypeStruct((n_tok, n), dt))
    return pl.pallas_call(
        functools.partial(_inproj_kernel, d=d, n_chunk=512),
        out_shape=out_shape,
        grid=(n_tok // bm,),
        in_specs=in_specs,
        out_specs=out_specs,
        compiler_params=_cparams("arbitrary"),
        name="norm_mod_inproj",
    )(x, mods_l, norm_g.reshape(1, d), *weights)


def _outproj_kernel(a_ref, x_ref, mod_ref, w_ref, fg_ref, o_ref, *, d, final_norm):
    y = jnp.dot(a_ref[...], w_ref[...], preferred_element_type=F32)
    gate = mod_ref[:, 2 * d:3 * d]
    xn = x_ref[...] + gate * y
    if final_norm:
        ms = jnp.mean(xn * xn, axis=-1, keepdims=True)
        xn = xn * lax.rsqrt(ms + NORM_EPS) * fg_ref[...]
    o_ref[...] = xn


def _outproj_call(act, x, mods_l, w_out, final_g, *, n_prompt, sample_len, final_norm, bm=512):
    n_tok, d = x.shape
    k = act.shape[1]
    npb, bps = n_prompt // bm, sample_len // bm
    cond = functools.partial(_cond_of_block, n_prompt_blocks=npb, blocks_per_sample=bps)
    return pl.pallas_call(
        functools.partial(_outproj_kernel, d=d, final_norm=final_norm),
        out_shape=jax.ShapeDtypeStruct((n_tok, d), F32),
        grid=(n_tok // bm,),
        in_specs=[
            pl.BlockSpec((bm, k), lambda i: (i, 0)),
            pl.BlockSpec((bm, d), lambda i: (i, 0)),
            pl.BlockSpec((None, 1, 3 * d), lambda i: (cond(i), 0, 0)),
            pl.BlockSpec((k, d), lambda i: (0, 0)),
            pl.BlockSpec((1, d), lambda i: (0, 0)),
        ],
        out_specs=pl.BlockSpec((bm, d), lambda i: (i, 0)),
        compiler_params=_cparams("arbitrary"),
        name="outproj_residual",
    )(act, x, mods_l, w_out, final_g.reshape(1, d))


def _s5_time_of_lane_block():
    pos = np.arange(S5_CHUNK)
    half, blk = pos // 8, pos % 8
    g8 = np.arange(8)[:, None]
    return 8 * half[None, :] + (blk[None, :] - g8) % 8


def _s5_prep(lam_re, lam_im, log_step, b_re, b_im, c_re, c_im):
    t_chunk = S5_CHUNK
    n_groups, n_state = lam_re.shape[1], lam_re.shape[2]
    n_oct = n_groups // 8
    lam = lax.complex(lam_re.astype(F32), lam_im.astype(F32))
    step = jnp.exp(log_step.astype(F32))[..., None]
    lam_bar = jnp.exp(lam * step)
    b_bar = ((lam_bar - 1.0) / lam)[..., None] * lax.complex(b_re.astype(F32), b_im.astype(F32))
    c_mat = lax.complex(c_re.astype(F32), c_im.astype(F32))
    ks = jnp.arange(t_chunk + 1, dtype=F32)[:, None, None, None]
    pw = jnp.exp(ks * (lam * step)[None])

    cb = jnp.einsum('dgop,ldgp,dgpi->ldgoi', c_mat, pw[:t_chunk], b_bar, precision=HIGHEST).real
    cb = cb.reshape(t_chunk, 2, n_oct, 8, S5_GROUP, S5_GROUP)

    tl = _s5_time_of_lane_block()
    lag = tl[:, None, :] - tl[:, :, None]
    lags = np.arange(t_chunk)[None, :, None, None]
    oh_f = (lag[:, None] == lags).astype(np.float32)
    oh_b = (-lag[:, None] == lags).astype(np.float32)
    ktot = (jnp.einsum('klxy,lakoi->akxiyo', oh_f, cb[:, 0], precision=HIGHEST)
            + jnp.einsum('klxy,lakoi->akxiyo', oh_b, cb[:, 1], precision=HIGHEST))
    ktot = ktot.reshape(n_groups, 2 * LANES, 2 * LANES)

    pw_ri = jnp.stack([pw.real, pw.imag]).reshape(2, t_chunk + 1, 2, n_oct, 8, n_state)
    m_idx = np.arange(t_chunk + 1)[None, None, :]

    def power_table(exponent, direction):
        sel = (exponent[:, :, None] == m_idx).astype(np.float32)
        return jnp.einsum('kxm,rmakp->rakxp', sel, pw_ri[:, :, direction], precision=HIGHEST)

    def by_group(m):
        return m.reshape((n_oct, 8) + m.shape[1:])

    def state_update(tab, bb):
        br, bi = by_group(bb.real.transpose(0, 2, 1))[:, :, None], by_group(bb.imag.transpose(0, 2, 1))[:, :, None]
        tr, ti = tab[0][:, :, :, None, :], tab[1][:, :, :, None, :]
        shape = (n_groups, 2 * LANES, n_state)
        return (tr * br - ti * bi).reshape(shape), (tr * bi + ti * br).reshape(shape)

    def state_output(tab, cc):
        cr, ci = by_group(cc.real.transpose(0, 2, 1))[:, :, :, None, :], by_group(cc.imag.transpose(0, 2, 1))[:, :, :, None, :]
        tr, ti = tab[0].transpose(0, 1, 3, 2)[..., None], tab[1].transpose(0, 1, 3, 2)[..., None]
        shape = (n_groups, n_state, 2 * LANES)
        return (cr * tr - ci * ti).reshape(shape), (cr * ti + ci * tr).reshape(shape)

    pin_f = state_update(power_table(t_chunk - 1 - tl, 0), b_bar[0])
    pin_b = state_update(power_table(tl, 1), b_bar[1])
    w1 = jnp.concatenate([ktot, pin_f[0], pin_b[0], pin_f[1], pin_b[1]], axis=-1)

    po_f = state_output(power_table(tl + 1, 0), c_mat[0])
    po_b = state_output(power_table(t_chunk - tl, 1), c_mat[1])
    zero = jnp.zeros_like(po_f[0])
    pout = jnp.concatenate([po_f[0], zero, zero, po_b[0], -po_f[1], zero, zero, -po_b[1]], axis=1)

    l16 = pw[t_chunk]
    lam_rows = jnp.stack([jnp.concatenate([l16[0].real, l16[1].real], axis=-1),
                          jnp.concatenate([l16[0].imag, l16[1].imag], axis=-1)], axis=1)
    lam_rows = jnp.concatenate([lam_rows, jnp.zeros((n_groups, SUBLANES - 2, 2 * n_state), F32)], axis=1)
    return w1.astype(BF16), pout.astype(BF16), lam_rows


def _lane_block_masks():
    blk = lax.broadcasted_iota(jnp.int32, (1, LANES), 1) // S5_GROUP
    return [blk == b for b in range(8)]


def _s5_to_chunks_kernel(u_ref, x_ref, *, rows):
    masks = _lane_block_masks()
    for o in range(u_ref.shape[0]):
        rolled = []
        for t in range(S5_CHUNK):
            v = u_ref[o, pl.ds(t, rows, stride=S5_CHUNK), :]
            s = (t % 8) * S5_GROUP
            rolled.append(pltpu.roll(v, s, 1) if s else v)
        for g8 in range(8):
            for half in range(2):
                acc = rolled[8 * half + (0 - g8) % 8]
                for blk in range(1, 8):
                    acc = jnp.where(masks[blk], rolled[8 * half + (blk - g8) % 8], acc)
                x_ref[o * 8 + g8, :, half * LANES:(half + 1) * LANES] = acc.astype(BF16)


def _s5_to_chunks_call(u3, *, rows=32):
    n_blk, n_tok, _ = u3.shape
    n_groups = n_blk * 8
    n_rows = n_tok // S5_CHUNK
    return pl.pallas_call(
        functools.partial(_s5_to_chunks_kernel, rows=rows),
        out_shape=jax.ShapeDtypeStruct((n_groups, n_rows, 2 * LANES), BF16),
        grid=(n_rows // rows,),
        in_specs=[pl.BlockSpec((n_blk, rows * S5_CHUNK, LANES), lambda i: (0, i, 0))],
        out_specs=pl.BlockSpec((n_groups, rows, 2 * LANES), lambda i: (0, i, 0)),
        compiler_params=_cparams("arbitrary"),
        name="s5_to_chunks",
    )(u3)


def _s5_from_chunks_kernel(y_ref, u_ref, d_ref, o_ref, *, rows):
    masks = _lane_block_masks()
    for o in range(u_ref.shape[0]):
        d_vec = d_ref[:, o * LANES:(o + 1) * LANES]
        for t in range(S5_CHUNK):
            half, t8 = t // 8, t % 8
            acc = y_ref[o * 8 + (0 - t8) % 8, :, half * LANES:(half + 1) * LANES]
            for blk in range(1, 8):
                src = y_ref[o * 8 + (blk - t8) % 8, :, half * LANES:(half + 1) * LANES]
                acc = jnp.where(masks[blk], src, acc)
            s = ((8 - t8) % 8) * S5_GROUP
            nat = pltpu.roll(acc, s, 1) if s else acc
            tok = pl.ds(t, rows, stride=S5_CHUNK)
            o_ref[o, tok, :] = _gelu_tanh(nat + d_vec * u_ref[o, tok, :])


def _s5_from_chunks_call(yc, u3, d_skip, *, rows=32):
    n_blk, n_tok, _ = u3.shape
    n_groups, n_rows, _ = yc.shape
    tok_spec = pl.BlockSpec((n_blk, rows * S5_CHUNK, LANES), lambda i: (0, i, 0))
    return pl.pallas_call(
        functools.partial(_s5_from_chunks_kernel, rows=rows),
        out_shape=jax.ShapeDtypeStruct(u3.shape, F32),
        grid=(n_rows // rows,),
        in_specs=[pl.BlockSpec((n_groups, rows, 2 * LANES), lambda i: (0, i, 0)),
                  tok_spec,
                  pl.BlockSpec((1, n_blk * LANES), lambda i: (0, 0))],
        out_specs=tok_spec,
        compiler_params=_cparams("arbitrary"),
        name="s5_from_chunks",
    )(yc, u3, d_skip.reshape(1, n_blk * LANES).astype(F32))


def _s5_chunk_kernel(x_ref, w1_ref, po_ref, lam_ref, h0r_ref, h0i_ref,
                     y_ref, fr_ref, fi_ref, r_scr, st_scr, *, gb, segments):
    lane = lax.broadcasted_iota(jnp.int32, (1, LANES), 1)
    fwd_lanes = lane < (LANES // 2)
    for g in range(gb):
        r = jnp.dot(x_ref[g], w1_ref[g], preferred_element_type=F32)
        for cb in range(4):
            r_scr[g, cb] = r[:, cb * LANES:(cb + 1) * LANES]
    for row0, n_seq, n_chunks, from_input, to_output in segments:

        def step(i, carry, row0=row0, n_seq=n_seq, n_chunks=n_chunks):
            rows_f = pl.ds(row0 + i, n_seq, stride=n_chunks)
            rows_b = pl.ds(row0 + (n_chunks - 1) - i, n_seq, stride=n_chunks)
            out = []
            for g in range(gb):
                s_re, s_im = carry[g]
                st_scr[g, 0, rows_f, :] = s_re
                st_scr[g, 1, rows_b, :] = s_re
                st_scr[g, 2, rows_f, :] = s_im
                st_scr[g, 3, rows_b, :] = s_im
                v_re = jnp.where(fwd_lanes, r_scr[g, 2, rows_f, :], r_scr[g, 2, rows_b, :])
                v_im = jnp.where(fwd_lanes, r_scr[g, 3, rows_f, :], r_scr[g, 3, rows_b, :])
                ar = lam_ref[g, 0:1, :]
                ai = lam_ref[g, 1:2, :]
                out.append((ar * s_re - ai * s_im + v_re, ar * s_im + ai * s_re + v_im))
            return tuple(out)

        if from_input:
            init = tuple((h0r_ref[g, 0:n_seq, :], h0i_ref[g, 0:n_seq, :]) for g in range(gb))
        else:
            init = tuple((jnp.zeros((n_seq, LANES), F32),) * 2 for _ in range(gb))
        fin = lax.fori_loop(0, n_chunks, step, init, unroll=True if n_chunks <= 16 else 4)
        if to_output:
            for g in range(gb):
                fr_ref[g] = fin[g][0]
                fi_ref[g] = fin[g][1]
    for g in range(gb):
        st = jnp.concatenate([st_scr[g, cb] for cb in range(4)], axis=1).astype(BF16)
        y_ref[g] = (jnp.concatenate([r_scr[g, 0], r_scr[g, 1]], axis=1)
                    + jnp.dot(st, po_ref[g], preferred_element_type=F32))


def _s5_chunk_call(xc, w1, pout, lam_rows, h0_re, h0_im, *, segments, n_final, gb=4):
    n_groups, rows, _ = xc.shape
    s8 = h0_re.shape[1]
    kern = functools.partial(_s5_chunk_kernel, gb=gb, segments=segments)
    g3 = lambda i: (i, 0, 0)
    return pl.pallas_call(
        kern,
        out_shape=[jax.ShapeDtypeStruct((n_groups, rows, 2 * LANES), F32),
                   jax.ShapeDtypeStruct((n_groups, n_final, LANES), F32),
                   jax.ShapeDtypeStruct((n_groups, n_final, LANES), F32)],
        grid=(n_groups // gb,),
        in_specs=[
            pl.BlockSpec((gb, rows, 2 * LANES), g3),
            pl.BlockSpec((gb, 2 * LANES, 4 * LANES), g3),
            pl.BlockSpec((gb, 4 * LANES, 2 * LANES), g3),
            pl.BlockSpec((gb, SUBLANES, LANES), g3),
            pl.BlockSpec((gb, s8, LANES), g3),
            pl.BlockSpec((gb, s8, LANES), g3),
        ],
        out_specs=[pl.BlockSpec((gb, rows, 2 * LANES), g3),
                   pl.BlockSpec((gb, n_final, LANES), g3),
                   pl.BlockSpec((gb, n_final, LANES), g3)],
        scratch_shapes=[pltpu.VMEM((gb, 4, rows, LANES), F32),
                        pltpu.VMEM((gb, 4, rows, LANES), F32)],
        compiler_params=_cparams("arbitrary"),
        name="s5_chunk_scan",
    )(xc, w1, pout, lam_rows, h0_re, h0_im)


def _glu_kernel(y_ref, z_ref, w_ref, b_ref, o_ref, *, n_chunk):
    n_blk = y_ref.shape[0]
    yb = jnp.concatenate([y_ref[o].astype(BF16) for o in range(n_blk)], axis=1)
    per = n_chunk // LANES
    for c in range(0, n_blk, per):
        sl = slice(c * LANES, (c + per) * LANES)
        gate = _sigmoid(jnp.dot(yb, w_ref[:, sl], preferred_element_type=F32) + b_ref[:, sl])
        y = jnp.concatenate([y_ref[c + k] for k in range(per)], axis=1)
        o_ref[:, sl] = (y * gate * _silu(z_ref[:, sl].astype(F32))).astype(o_ref.dtype)


def _glu_call(y3, z, glu_w, glu_b, *, bm=512):
    n_blk, n_tok, _ = y3.shape
    width = n_blk * LANES
    return pl.pallas_call(
        functools.partial(_glu_kernel, n_chunk=min(512, width)),
        out_shape=jax.ShapeDtypeStruct((n_tok, width), BF16),
        grid=(n_tok // bm,),
        in_specs=[
            pl.BlockSpec((n_blk, bm, LANES), lambda i: (0, i, 0)),
            pl.BlockSpec((bm, width), lambda i: (i, 0)),
            pl.BlockSpec((width, width), lambda i: (0, 0)),
            pl.BlockSpec((1, width), lambda i: (0, 0)),
        ],
        out_specs=pl.BlockSpec((bm, width), lambda i: (i, 0)),
        compiler_params=_cparams("arbitrary"),
        name="s5_glu_gate",
    )(y3, z, glu_w, glu_b.reshape(1, width).astype(F32))


def _s5_mix(u3, z, prm, st_re, st_im, *, n_prompt_seq, prompt_len, n_sample_seq, sample_len, bm):
    lam_re, lam_im, log_step, b_re, b_im, c_re, c_im, d_skip, glu_w, glu_b = prm
    n_state = lam_re.shape[2]
    pc, sc = prompt_len // S5_CHUNK, sample_len // S5_CHUNK
    w1, pout, lam_rows = _s5_prep(lam_re, lam_im, log_step, b_re, b_im, c_re, c_im)

    def state_rows(s):
        t = jnp.concatenate([s[:, 0], s[:, 1]], axis=-1).transpose(1, 0, 2).astype(F32)
        return jnp.pad(t, ((0, 0), (0, SUBLANES - t.shape[1]), (0, 0)))

    segments = ((0, n_prompt_seq, pc, False, True), (n_prompt_seq * pc, n_sample_seq, sc, True, False))
    xc = _s5_to_chunks_call(u3)
    yc, fr, fi = _s5_chunk_call(xc, w1, pout, lam_rows, state_rows(st_re), state_rows(st_im),
                                segments=segments, n_final=n_prompt_seq)
    y3 = _s5_from_chunks_call(yc, u3, d_skip)
    act = _glu_call(y3, z, glu_w, glu_b, bm=bm)

    def unpack(f):
        return jnp.stack([f[:, :, :n_state], f[:, :, n_state:]], axis=0).transpose(2, 0, 1, 3)

    return act, unpack(fr), unpack(fi)


def _pool_kernel(u_ref, z_ref, w_ref, s_ref, o_ref, *, n_prompt_blocks, prompt_len, sample_len):
    rows = u_ref.shape[0]
    seq_len = jnp.where(pl.program_id(0) < n_prompt_blocks, prompt_len, sample_len)
    t = lax.broadcasted_iota(jnp.int32, (rows, 1), 0) & (seq_len - 1)

    def later(x, k):
        return jnp.where(t + k < seq_len, pltpu.roll(x, rows - k, 0), 0.0)

    def earlier(x, k):
        return jnp.where(t >= k, pltpu.roll(x, k, 0), 0.0)

    def body(win):
        lo = win // 2
        u = u_ref[...]
        fwd = u
        bwd = earlier(u, 1)
        s = 1
        while s < lo:
            fwd = fwd + later(fwd, s)
            bwd = bwd + earlier(bwd, s)
            s *= 2
        cnt = jnp.minimum(t - lo + win, seq_len) - jnp.maximum(t - lo, 0)
        p = (fwd + bwd) / cnt.astype(F32) - u
        m = jnp.dot(p.astype(BF16), w_ref[...], preferred_element_type=F32) * s_ref[...]
        o_ref[...] = (m * _silu(z_ref[...].astype(F32))).astype(o_ref.dtype)

    for gi, win in enumerate(POOL_WINDOWS):
        pl.when(pl.program_id(1) == gi)(functools.partial(body, win))


def _pool_call(u, z, pool_w, pool_scale, *, n_prompt, prompt_len, sample_len, rows=2048):
    n_tok, width = u.shape
    n_groups = len(POOL_WINDOWS)
    cg = width // n_groups
    assert prompt_len & (prompt_len - 1) == 0 and sample_len & (sample_len - 1) == 0
    assert rows % prompt_len == 0 and rows % sample_len == 0 and n_prompt % rows == 0
    kern = functools.partial(_pool_kernel, n_prompt_blocks=n_prompt // rows, prompt_len=prompt_len,
                             sample_len=sample_len)
    return pl.pallas_call(
        kern,
        out_shape=jax.ShapeDtypeStruct((n_tok, width), BF16),
        grid=(n_tok // rows, n_groups),
        in_specs=[
            pl.BlockSpec((rows, cg), lambda i, g: (i, g)),
            pl.BlockSpec((rows, cg), lambda i, g: (i, g)),
            pl.BlockSpec((None, cg, cg), lambda i, g: (g, 0, 0)),
            pl.BlockSpec((1, cg), lambda i, g: (0, g)),
        ],
        out_specs=pl.BlockSpec((rows, cg), lambda i, g: (i, g)),
        compiler_params=_cparams("arbitrary", "arbitrary"),
        name="pool_mix",
    )(u, z, pool_w, pool_scale.reshape(1, width).astype(F32))


MLA_QW = 2 * LANES

_ROT_SRC = np.concatenate([np.arange(16, 32), np.arange(0, 16), np.arange(48, 64), np.arange(32, 48)])
_ROT_SIGN = np.concatenate([-np.ones(16), np.ones(16), -np.ones(16), np.ones(16)]).astype(np.float32)


def _rope_tables(n_prompt, n_sample_seq, sample_len):
    half = MLA_ROPE // 4
    tok = jnp.arange(sample_len)
    row = (tok // GRID_W).astype(F32)
    col = (tok % GRID_W).astype(F32)
    inv = ROPE_THETA ** (-jnp.arange(half, dtype=F32) / half)
    a_row, a_col = row[:, None] * inv, col[:, None] * inv
    cos = jnp.concatenate([jnp.cos(a_row), jnp.cos(a_row), jnp.cos(a_col), jnp.cos(a_col)], axis=-1)
    sin = jnp.concatenate([jnp.sin(a_row), jnp.sin(a_row), jnp.sin(a_col), jnp.sin(a_col)], axis=-1)
    pad = jnp.zeros((sample_len, LANES - MLA_ROPE), F32)
    cos_s = jnp.tile(jnp.concatenate([cos, pad], axis=-1), (n_sample_seq, 1))
    sin_s = jnp.tile(jnp.concatenate([sin, pad], axis=-1), (n_sample_seq, 1))
    cos_p = jnp.concatenate([jnp.ones((n_prompt, MLA_ROPE), F32), jnp.zeros((n_prompt, LANES - MLA_ROPE), F32)], -1)
    return jnp.concatenate([cos_p, cos_s]), jnp.concatenate([jnp.zeros((n_prompt, LANES), F32), sin_s])


def _rms(x, g):
    return x * lax.rsqrt(jnp.mean(x * x, axis=-1, keepdims=True) + NORM_EPS) * g


def _mla_post_kernel(sm_ref, cos_ref, sin_ref, qn_ref, kn_ref, wa_ref, wb_ref,
                     q_ref, ckv_ref, kpe_ref, *, q_rank, kv_rank, heads_per_dot):
    cosp, sinp = cos_ref[...], sin_ref[...]
    qn = _rms(sm_ref[:, 0:q_rank], qn_ref[...]).astype(BF16)
    for h0 in range(0, MLA_HEADS, heads_per_dot):
        a = jnp.dot(qn, wa_ref[:, h0 * MLA_QW:(h0 + heads_per_dot) * MLA_QW], preferred_element_type=F32)
        b = jnp.dot(qn, wb_ref[:, h0 * LANES:(h0 + heads_per_dot) * LANES], preferred_element_type=F32)
        for j in range(heads_per_dot):
            h = h0 + j
            q_ref[:, h * MLA_QW:h * MLA_QW + LANES] = a[:, j * MLA_QW:j * MLA_QW + LANES].astype(BF16)
            pe = a[:, j * MLA_QW + LANES:(j + 1) * MLA_QW] * cosp + b[:, j * LANES:(j + 1) * LANES] * sinp
            q_ref[:, h * MLA_QW + LANES:(h + 1) * MLA_QW] = pe.astype(BF16)
    c0 = q_rank
    ckv_ref[...] = _rms(sm_ref[:, c0:c0 + kv_rank], kn_ref[...])
    k0 = c0 + kv_rank
    kpe_ref[...] = (sm_ref[:, k0:k0 + LANES] * cosp + sm_ref[:, k0 + LANES:k0 + 2 * LANES] * sinp).astype(BF16)


def _mla_post_call(small, cos_t, sin_t, q_norm, kv_norm, wq_a, wq_b, *, bm=512):
    n_tok, ws = small.shape
    q_rank, kv_rank = q_norm.shape[-1], kv_norm.shape[-1]
    row = lambda i: (i, 0)
    fix = lambda i: (0, 0)
    kern = functools.partial(_mla_post_kernel, q_rank=q_rank, kv_rank=kv_rank, heads_per_dot=4)
    return pl.pallas_call(
        kern,
        out_shape=[jax.ShapeDtypeStruct((n_tok, MLA_HEADS * MLA_QW), BF16),
                   jax.ShapeDtypeStruct((n_tok, kv_rank), F32),
                   jax.ShapeDtypeStruct((n_tok, LANES), BF16)],
        grid=(n_tok // bm,),
        in_specs=[
            pl.BlockSpec((bm, ws), row),
            pl.BlockSpec((bm, LANES), row),
            pl.BlockSpec((bm, LANES), row),
            pl.BlockSpec((1, q_rank), fix),
            pl.BlockSpec((1, kv_rank), fix),
            pl.BlockSpec(wq_a.shape, fix),
            pl.BlockSpec(wq_b.shape, fix),
        ],
        out_specs=[pl.BlockSpec((bm, MLA_HEADS * MLA_QW), row),
                   pl.BlockSpec((bm, kv_rank), row),
                   pl.BlockSpec((bm, LANES), row)],
        compiler_params=_cparams("arbitrary"),
        name="mla_q_rope",
    )(small, cos_t, sin_t, q_norm.reshape(1, q_rank).astype(F32), kv_norm.reshape(1, kv_rank).astype(F32),
      wq_a, wq_b)


def _kv_expand_kernel(c_ref, w_ref, o_ref, *, n_chunk):
    c = c_ref[...].astype(BF16)
    n = w_ref.shape[1]
    for s in range(0, n, n_chunk):
        o_ref[:, s:s + n_chunk] = jnp.dot(c, w_ref[:, s:s + n_chunk],
                                          preferred_element_type=F32).astype(o_ref.dtype)


def _kv_expand_call(ckv, wkv_b, *, bm=512):
    rows, kr = ckv.shape
    n = wkv_b.shape[1]
    return pl.pallas_call(
        functools.partial(_kv_expand_kernel, n_chunk=1024),
        out_shape=jax.ShapeDtypeStruct((rows, n), BF16),
        grid=(rows // bm,),
        in_specs=[pl.BlockSpec((bm, kr), lambda i: (i, 0)), pl.BlockSpec((kr, n), lambda i: (0, 0))],
        out_specs=pl.BlockSpec((bm, n), lambda i: (i, 0)),
        compiler_params=_cparams("arbitrary"),
        name="mla_kv_expand",
    )(ckv, wkv_b)


def _attn_kernel(q_ref, kv_ref, kpe_ref, z_ref, *rest, hg, scale):
    o_ref = rest[-1]
    kpe = kpe_ref[...]
    for j in range(hg):
        q = q_ref[:, j * MLA_QW:(j + 1) * MLA_QW]
        kcat = jnp.concatenate([kv_ref[:, j * 2 * LANES:j * 2 * LANES + LANES], kpe], axis=1)
        s = lax.dot_general(q, kcat, (((1,), (1,)), ((), ())), preferred_element_type=F32) * scale
        e = jnp.exp(s - jnp.max(s, axis=-1, keepdims=True))
        l = jnp.sum(e, axis=-1, keepdims=True)
        v = kv_ref[:, j * 2 * LANES + LANES:(j + 1) * 2 * LANES]
        o = jnp.dot(e.astype(BF16), v, preferred_element_type=F32) / l
        zs = slice(j * MLA_V, (j + 1) * MLA_V)
        o_ref[:, zs] = (o * _silu(z_ref[:, zs].astype(F32))).astype(o_ref.dtype)


def _attn_call(q, kv, kpe, z, prev, *, q_row0, n_seq, q_len, k_len, hg, qb):
    n_tok = q.shape[0]
    width = MLA_HEADS * MLA_V
    nqb = q_len // qb
    qb0 = q_row0 // qb
    assert q_row0 % qb == 0 and q_len % qb == 0
    scale = float((MLA_NOPE + MLA_ROPE) ** -0.5)
    qrow = lambda b, g, i: (qb0 + b * nqb + i, g)
    in_specs = [
        pl.BlockSpec((qb, hg * MLA_QW), qrow),
        pl.BlockSpec((k_len, hg * 2 * LANES), lambda b, g, i: (b, g)),
        pl.BlockSpec((k_len, LANES), lambda b, g, i: (b, 0)),
        pl.BlockSpec((qb, hg * MLA_V), qrow),
    ]
    args = [q, kv, kpe, z]
    aliases = {}
    if prev is not None:
        in_specs.append(pl.BlockSpec(memory_space=pl.ANY))
        args.append(prev)
        aliases = {4: 0}
    return pl.pallas_call(
        functools.partial(_attn_kernel, hg=hg, scale=scale),
        out_shape=jax.ShapeDtypeStruct((n_tok, width), BF16),
        grid=(n_seq, MLA_HEADS // hg, nqb),
        in_specs=in_specs,
        out_specs=pl.BlockSpec((qb, hg * MLA_V), qrow),
        input_output_aliases=aliases,
        compiler_params=_cparams("arbitrary", "arbitrary", "arbitrary"),
        name="mla_attention",
    )(*args)


def _mla_weights(w_in, wq_b):
    q_rank = wq_b.shape[0]
    kv_rank = w_in.shape[1] - q_rank - MLA_ROPE - MLA_HEADS * MLA_V
    d = w_in.shape[0]
    c_kpe = q_rank + kv_rank
    zpad = jnp.zeros((d, LANES - MLA_ROPE), w_in.dtype)
    kpe_w = w_in[:, c_kpe:c_kpe + MLA_ROPE]
    w_small = jnp.concatenate([w_in[:, :c_kpe], kpe_w, zpad,
                               kpe_w[:, _ROT_SRC] * _ROT_SIGN, zpad], axis=1)
    w_z = w_in[:, c_kpe + MLA_ROPE:]
    hd = MLA_NOPE + MLA_ROPE
    wq3 = wq_b.reshape(q_rank, MLA_HEADS, hd)
    pe = wq3[:, :, MLA_NOPE:]
    z3 = jnp.zeros((q_rank, MLA_HEADS, LANES - MLA_ROPE), wq_b.dtype)
    wq_a = jnp.concatenate([wq3, z3], axis=-1).reshape(q_rank, MLA_HEADS * MLA_QW)
    wq_r = jnp.concatenate([pe[:, :, _ROT_SRC] * _ROT_SIGN, z3], axis=-1).reshape(q_rank, MLA_HEADS * LANES)
    return w_small.astype(BF16), w_z.astype(BF16), wq_a.astype(BF16), wq_r.astype(BF16)


def kernel(x_prompt, x_sample, state_s5_re, state_s5_im, cache_ckv, cache_kpe, c, c_ctx, norm_g, ada_w, ada_b, final_norm_g, s5_w_in, s5_lam_re, s5_lam_im, s5_log_step, s5_b_re, s5_b_im, s5_c_re, s5_c_im, s5_d, s5_glu_w, s5_glu_b, s5_w_out, pool_w_in, pool_w, pool_scale, pool_w_out, mla_w_in, mla_q_norm, mla_wq_b, mla_kv_norm, mla_wkv_b, mla_w_out):
    n_pseq, p_len, d = x_prompt.shape
    n_sseq, s_len, _ = x_sample.shape
    depth = norm_g.shape[0]
    n_prompt = n_pseq * p_len
    bm = 512
    geo = dict(n_prompt=n_prompt, sample_len=s_len, bm=bm)

    x = jnp.concatenate([x_prompt.reshape(n_prompt, d), x_sample.reshape(n_sseq * s_len, d)], axis=0)
    conds = jnp.concatenate([c_ctx[None, :], c, jnp.zeros((SUBLANES - 1 - n_sseq, d), F32)], axis=0)
    mods = _ada_call(conds.astype(F32), ada_w, ada_b)
    mods = mods.reshape(depth, SUBLANES, 1, 3 * d)

    new_re, new_im, new_ckv, new_kpe = [], [], [], []
    for layer in range(depth):
        kind, j = layer % N_MIXERS, layer // N_MIXERS
        last = layer == depth - 1
        ml = mods[layer]
        if kind == 0:
            width = s5_w_in.shape[2] // 2
            w = s5_w_in[j].astype(BF16)
            u3, z = _inproj_call(x, ml, norm_g[layer], [w[:, :width], w[:, width:]], [F32, BF16],
                                 lane_blocked=(0,), **geo)
            prm = (s5_lam_re[j], s5_lam_im[j], s5_log_step[j], s5_b_re[j], s5_b_im[j], s5_c_re[j],
                   s5_c_im[j], s5_d[j], s5_glu_w[j].astype(BF16), s5_glu_b[j])
            act, f_re, f_im = _s5_mix(u3, z, prm, state_s5_re[:, j], state_s5_im[:, j], n_prompt_seq=n_pseq,
                                      prompt_len=p_len, n_sample_seq=n_sseq, sample_len=s_len, bm=bm)
            new_re.append(f_re)
            new_im.append(f_im)
            w_out = s5_w_out[j]
        elif kind == 1:
            width = pool_w_in.shape[2] // 2
            w = pool_w_in[j].astype(BF16)
            u, z = _inproj_call(x, ml, norm_g[layer], [w[:, :width], w[:, width:]], [F32, BF16], **geo)
            act = _pool_call(u, z, pool_w[j].astype(BF16), pool_scale[j], n_prompt=n_prompt,
                             prompt_len=p_len, sample_len=s_len)
            w_out = pool_w_out[j]
        else:
            q_rank, kv_rank = mla_q_norm.shape[-1], mla_kv_norm.shape[-1]
            w_small, w_z, wq_a, wq_r = _mla_weights(mla_w_in[j], mla_wq_b[j])
            small, z = _inproj_call(x, ml, norm_g[layer], [w_small, w_z], [F32, BF16], **geo)
            cos_t, sin_t = _rope_tables(n_prompt, n_sseq, s_len)
            q, ckv_n, kpe_k = _mla_post_call(small, cos_t, sin_t, mla_q_norm[j], mla_kv_norm[j], wq_a, wq_r,
                                             bm=bm)
            wkv = mla_wkv_b[j].astype(BF16)
            past = cache_ckv.shape[2]
            k_len = past + s_len
            ckv_s = jnp.concatenate([cache_ckv[:, j].astype(F32), ckv_n[n_prompt:].reshape(n_sseq, s_len, kv_rank)],
                                    axis=1).reshape(n_sseq * k_len, kv_rank)
            kpe_cache = jnp.concatenate([cache_kpe[:, j].astype(BF16),
                                         jnp.zeros((n_sseq, past, LANES - MLA_ROPE), BF16)], axis=-1)
            kpe_s = jnp.concatenate([kpe_cache, kpe_k[n_prompt:].reshape(n_sseq, s_len, LANES)],
                                    axis=1).reshape(n_sseq * k_len, LANES)
            kv_p = _kv_expand_call(ckv_n[:n_prompt], wkv, bm=bm)
            kv_s = _kv_expand_call(ckv_s, wkv, bm=bm)
            act = _attn_call(q, kv_p, kpe_k[:n_prompt], z, None, q_row0=0, n_seq=n_pseq, q_len=p_len,
                             k_len=p_len, hg=MLA_HEADS, qb=p_len)
            act = _attn_call(q, kv_s, kpe_s, z, act, q_row0=n_prompt, n_seq=n_sseq, q_len=s_len,
                             k_len=k_len, hg=4, qb=256)
            new_ckv.append(ckv_n[:n_prompt].reshape(n_pseq, p_len, kv_rank))
            c_kpe = q_rank + kv_rank
            new_kpe.append(small[:n_prompt, c_kpe:c_kpe + MLA_ROPE].reshape(n_pseq, p_len, MLA_ROPE))
            w_out = mla_w_out[j]
        x = _outproj_call(act, x, ml, w_out.astype(BF16), final_norm_g, final_norm=last, **geo)

    y_prompt = x[:n_prompt].reshape(n_pseq, p_len, d)
    y_sample = x[n_prompt:].reshape(n_sseq, s_len, d)
    return (y_prompt, y_sample, jnp.stack(new_re, axis=1), jnp.stack(new_im, axis=1),
            jnp.stack(new_ckv, axis=1), jnp.stack(new_kpe, axis=1))
```

```python
import functools
import math

import jax
import jax.numpy as jnp
import numpy as np
from jax import lax
from jax.experimental import pallas as pl
from jax.experimental.pallas import tpu as pltpu

S5_GROUP = 16
S5_CHUNK = 16
POOL_WINDOWS = (2, 4, 8, 16)
MLA_HEADS = 16
MLA_NOPE = 128
MLA_ROPE = 64
MLA_V = 128
GRID_W = 64
ROPE_THETA = 10000.0
NORM_EPS = 1e-6
N_MIXERS = 3

LANES = 128
SUBLANES = 8
VMEM_LIMIT_BYTES = 56 * 1024 * 1024

F32 = jnp.float32
BF16 = jnp.bfloat16
HIGHEST = lax.Precision.HIGHEST


def _cparams(*sem):
    return pltpu.CompilerParams(dimension_semantics=sem, vmem_limit_bytes=VMEM_LIMIT_BYTES)


def _sigmoid(x):
    return 1.0 / (1.0 + jnp.exp(-x))


def _silu(x):
    return x * _sigmoid(x)


def _gelu_tanh(x):
    c = math.sqrt(2.0 / math.pi)
    return 0.5 * x * (1.0 + jnp.tanh(c * (x + 0.044715 * (x * x * x))))


def _ada_kernel(c_ref, w_ref, b_ref, o_ref):
    a = _silu(c_ref[...])
    o_ref[...] = jnp.dot(a, w_ref[...], preferred_element_type=F32, precision=HIGHEST) + b_ref[...]


def _ada_call(conds, ada_w, ada_b):
    depth, d, d3 = ada_w.shape
    c8 = conds.shape[0]
    tn = 512
    return pl.pallas_call(
        _ada_kernel,
        out_shape=jax.ShapeDtypeStruct((depth, c8, d3), F32),
        grid=(depth, d3 // tn),
        in_specs=[
            pl.BlockSpec((c8, d), lambda l, n: (0, 0)),
            pl.BlockSpec((None, d, tn), lambda l, n: (l, 0, n)),
            pl.BlockSpec((None, 1, tn), lambda l, n: (l, 0, n)),
        ],
        out_specs=pl.BlockSpec((None, c8, tn), lambda l, n: (l, 0, n)),
        compiler_params=_cparams("arbitrary", "arbitrary"),
        name="ada_mod",
    )(conds, ada_w, ada_b.reshape(depth, 1, d3))


def _cond_of_block(i, n_prompt_blocks, blocks_per_sample):
    return jnp.where(i < n_prompt_blocks, 0, 1 + (i - n_prompt_blocks) // blocks_per_sample)


def _modulated(x, mod_ref, g_ref, d):
    ms = jnp.mean(x * x, axis=-1, keepdims=True)
    y = x * lax.rsqrt(ms + NORM_EPS) * g_ref[...]
    shift = mod_ref[:, 0:d]
    scale = mod_ref[:, d:2 * d]
    return (y * (1.0 + scale) + shift).astype(BF16)


def _inproj_kernel(x_ref, mod_ref, g_ref, *rest, d, n_chunk):
    n_out = len(rest) // 2
    w_refs, o_refs = rest[:n_out], rest[n_out:]
    h = _modulated(x_ref[...], mod_ref, g_ref, d)
    for w_ref, o_ref in zip(w_refs, o_refs):
        n = w_ref.shape[1]
        for c in range(0, n, n_chunk):
            e = min(c + n_chunk, n)
            r = jnp.dot(h, w_ref[:, c:e], preferred_element_type=F32).astype(o_ref.dtype)
            if len(o_ref.shape) == 3:
                for lb in range((e - c) // LANES):
                    o_ref[c // LANES + lb] = r[:, lb * LANES:(lb + 1) * LANES]
            else:
                o_ref[:, c:e] = r


def _inproj_call(x, mods_l, norm_g, weights, out_dtypes, *, n_prompt, sample_len, bm=512,
                 lane_blocked=()):
    n_tok, d = x.shape
    npb, bps = n_prompt // bm, sample_len // bm
    cond = functools.partial(_cond_of_block, n_prompt_blocks=npb, blocks_per_sample=bps)
    weights = [w if isinstance(w, tuple) else (w, 0, w.shape[1]) for w in weights]
    in_specs = [
        pl.BlockSpec((bm, d), lambda i: (i, 0)),
        pl.BlockSpec((None, 1, 3 * d), lambda i: (cond(i), 0, 0)),
        pl.BlockSpec((1, d), lambda i: (0, 0)),
    ] + [pl.BlockSpec((d, n), functools.partial(lambda i, blk: (0, blk), blk=blk)) for _, blk, n in weights]
    out_specs, out_shape = [], []
    for k, ((_, _, n), dt) in enumerate(zip(weights, out_dtypes)):
        if k in lane_blocked:
            out_specs.append(pl.BlockSpec((n // LANES, bm, LANES), lambda i: (0, i, 0)))
            out_shape.append(jax.ShapeDtypeStruct((n // LANES, n_tok, LANES), dt))
        else:
            out_specs.append(pl.BlockSpec((bm, n), lambda i: (i, 0)))
            out_shape.append(jax.ShapeDtypeStruct((n_tok, n), dt))
    return pl.pallas_call(
        functools.partial(_inproj_kernel, d=d, n_chunk=512),
        out_shape=out_shape,
        grid=(n_tok // bm,),
        in_specs=in_specs,
        out_specs=out_specs,
        compiler_params=_cparams("arbitrary"),
        name="norm_mod_inproj",
    )(x, mods_l, norm_g.reshape(1, d), *[w for w, _, _ in weights])


def _outproj_kernel(a_ref, x_ref, mod_ref, w_ref, fg_ref, o_ref, *, d, final_norm):
    y = jnp.dot(a_ref[...], w_ref[...], preferred_element_type=F32)
    gate = mod_ref[:, 2 * d:3 * d]
    xn = x_ref[...] + gate * y
    if final_norm:
        ms = jnp.mean(xn * xn, axis=-1, keepdims=True)
        xn = xn * lax.rsqrt(ms + NORM_EPS) * fg_ref[...]
    o_ref[...] = xn


def _outproj_call(act, x, mods_l, w_out, final_g, *, n_prompt, sample_len, final_norm, bm=512):
    n_tok, d = x.shape
    k = act.shape[1]
    npb, bps = n_prompt // bm, sample_len // bm
    cond = functools.partial(_cond_of_block, n_prompt_blocks=npb, blocks_per_sample=bps)
    return pl.pallas_call(
        functools.partial(_outproj_kernel, d=d, final_norm=final_norm),
        out_shape=jax.ShapeDtypeStruct((n_tok, d), F32),
        grid=(n_tok // bm,),
        in_specs=[
            pl.BlockSpec((bm, k), lambda i: (i, 0)),
            pl.BlockSpec((bm, d), lambda i: (i, 0)),
            pl.BlockSpec((None, 1, 3 * d), lambda i: (cond(i), 0, 0)),
            pl.BlockSpec((k, d), lambda i: (0, 0)),
            pl.BlockSpec((1, d), lambda i: (0, 0)),
        ],
        out_specs=pl.BlockSpec((bm, d), lambda i: (i, 0)),
        compiler_params=_cparams("arbitrary"),
        name="outproj_residual",
    )(act, x, mods_l, w_out, final_g.reshape(1, d))


def _s5_time_of_lane_block():
    pos = np.arange(S5_CHUNK)
    half, blk = pos // 8, pos % 8
    g8 = np.arange(8)[:, None]
    return 8 * half[None, :] + (blk[None, :] - g8) % 8


def _s5_tables(lam_re, lam_im, log_step, b_re, b_im, c_re, c_im):
    t_chunk = S5_CHUNK
    n_groups, n_state = lam_re.shape[1], lam_re.shape[2]
    n_oct = n_groups // 8
    lam = lax.complex(lam_re.astype(F32), lam_im.astype(F32))
    step = jnp.exp(log_step.astype(F32))[..., None]
    lam_bar = jnp.exp(lam * step)
    b_bar = ((lam_bar - 1.0) / lam)[..., None] * lax.complex(b_re.astype(F32), b_im.astype(F32))
    c_mat = lax.complex(c_re.astype(F32), c_im.astype(F32))
    ks = jnp.arange(t_chunk + 1, dtype=F32)[:, None, None, None]
    pw = jnp.exp(ks * (lam * step)[None])

    cb = jnp.einsum('dgop,ldgp,dgpi->ldgoi', c_mat, pw[:t_chunk], b_bar, precision=HIGHEST).real
    rf = cb[:, 0].transpose(1, 3, 0, 2).reshape(n_groups, S5_GROUP, 2 * LANES)
    rb = cb[::-1, 1].transpose(1, 3, 0, 2).reshape(n_groups, S5_GROUP, 2 * LANES)

    tl = _s5_time_of_lane_block()
    pw_ri = jnp.stack([pw.real, pw.imag]).reshape(2, t_chunk + 1, 2, n_oct, 8, n_state)
    m_idx = np.arange(t_chunk + 1)[None, None, :]

    def power_table(exponent, direction):
        sel = (exponent[:, :, None] == m_idx).astype(np.float32)
        return jnp.einsum('kxm,rmakp->rakxp', sel, pw_ri[:, :, direction], precision=HIGHEST)

    def by_group(m):
        return m.reshape((n_oct, 8) + m.shape[1:])

    def state_update(tab, bb):
        br, bi = by_group(bb.real.transpose(0, 2, 1))[:, :, None], by_group(bb.imag.transpose(0, 2, 1))[:, :, None]
        tr, ti = tab[0][:, :, :, None, :], tab[1][:, :, :, None, :]
        shape = (n_groups, 2 * LANES, n_state)
        return (tr * br - ti * bi).reshape(shape), (tr * bi + ti * br).reshape(shape)

    def state_output(tab, cc):
        cr, ci = by_group(cc.real.transpose(0, 2, 1))[:, :, :, None, :], by_group(cc.imag.transpose(0, 2, 1))[:, :, :, None, :]
        tr, ti = tab[0].transpose(0, 1, 3, 2)[..., None], tab[1].transpose(0, 1, 3, 2)[..., None]
        shape = (n_groups, n_state, 2 * LANES)
        return (cr * tr - ci * ti).reshape(shape), (cr * ti + ci * tr).reshape(shape)

    pin_f = state_update(power_table(t_chunk - 1 - tl, 0), b_bar[0])
    pin_b = state_update(power_table(tl, 1), b_bar[1])
    pin = jnp.concatenate([pin_f[0], pin_b[0], pin_f[1], pin_b[1]], axis=-1)

    po_f = state_output(power_table(tl + 1, 0), c_mat[0])
    po_b = state_output(power_table(t_chunk - tl, 1), c_mat[1])
    zero = jnp.zeros_like(po_f[0])
    pout = jnp.concatenate([po_f[0], zero, zero, po_b[0], -po_f[1], zero, zero, -po_b[1]], axis=1)

    l16 = pw[t_chunk]
    lam_rows = jnp.stack([jnp.concatenate([l16[0].real, l16[1].real], axis=-1),
                          jnp.concatenate([l16[0].imag, l16[1].imag], axis=-1)], axis=0)
    return rf, rb, pin.astype(BF16), pout.astype(BF16), lam_rows


def _s5_kmat_kernel(rf_ref, rb_ref, k_ref):
    lane = lax.broadcasted_iota(jnp.int32, (S5_GROUP, 2 * LANES), 1)
    for g8 in range(8):
        rf, rb = rf_ref[g8], rb_ref[g8]
        for sigma in range(S5_CHUNK):
            sf, sb = sigma * S5_GROUP, (sigma + 1) * S5_GROUP
            fwd = jnp.where(lane >= sf, pltpu.roll(rf, sf, 1) if sf else rf, 0.0)
            bwd = jnp.where(lane < sb, pltpu.roll(rb, sb, 1) if sb < 2 * LANES else rb, 0.0)
            blk = fwd + bwd
            pos = (sigma // 8) * 8 + (sigma % 8 + g8) % 8
            rows = slice(pos * S5_GROUP, (pos + 1) * S5_GROUP)
            for half in range(2):
                part = blk[:, half * LANES:(half + 1) * LANES]
                if g8:
                    part = pltpu.roll(part, g8 * S5_GROUP, 1)
                k_ref[g8, rows, half * LANES:(half + 1) * LANES] = part.astype(BF16)


def _s5_kmat_call(rf, rb):
    n_groups = rf.shape[0]
    spec = pl.BlockSpec((8, S5_GROUP, 2 * LANES), lambda i: (i, 0, 0))
    return pl.pallas_call(
        _s5_kmat_kernel,
        out_shape=jax.ShapeDtypeStruct((n_groups, 2 * LANES, 2 * LANES), BF16),
        grid=(n_groups // 8,),
        in_specs=[spec, spec],
        out_specs=pl.BlockSpec((8, 2 * LANES, 2 * LANES), lambda i: (i, 0, 0)),
        compiler_params=_cparams("arbitrary"),
        name="s5_kmat",
    )(rf, rb)


def _lane_block_masks():
    blk = lax.broadcasted_iota(jnp.int32, (1, LANES), 1) // S5_GROUP
    return [blk == b for b in range(8)]


def _s5_to_chunks_kernel(u_ref, x_ref, *, rows):
    masks = _lane_block_masks()
    for o in range(u_ref.shape[0]):
        rolled = []
        for t in range(S5_CHUNK):
            v = u_ref[o, pl.ds(t, rows, stride=S5_CHUNK), :]
            s = (t % 8) * S5_GROUP
            rolled.append(pltpu.roll(v, s, 1) if s else v)
        for g8 in range(8):
            for half in range(2):
                acc = rolled[8 * half + (0 - g8) % 8]
                for blk in range(1, 8):
                    acc = jnp.where(masks[blk], rolled[8 * half + (blk - g8) % 8], acc)
                x_ref[o * 8 + g8, :, half * LANES:(half + 1) * LANES] = acc.astype(BF16)


def _s5_to_chunks_call(u3, *, rows=32):
    n_blk, n_tok, _ = u3.shape
    n_groups = n_blk * 8
    n_rows = n_tok // S5_CHUNK
    return pl.pallas_call(
        functools.partial(_s5_to_chunks_kernel, rows=rows),
        out_shape=jax.ShapeDtypeStruct((n_groups, n_rows, 2 * LANES), BF16),
        grid=(n_rows // rows,),
        in_specs=[pl.BlockSpec((n_blk, rows * S5_CHUNK, LANES), lambda i: (0, i, 0))],
        out_specs=pl.BlockSpec((n_groups, rows, 2 * LANES), lambda i: (0, i, 0)),
        compiler_params=_cparams("arbitrary"),
        name="s5_to_chunks",
    )(u3)


def _s5_from_chunks_kernel(y_ref, u_ref, d_ref, o_ref, *, rows):
    masks = _lane_block_masks()
    for o in range(u_ref.shape[0]):
        d_vec = d_ref[:, o * LANES:(o + 1) * LANES]
        for t in range(S5_CHUNK):
            half, t8 = t // 8, t % 8
            acc = y_ref[o * 8 + (0 - t8) % 8, :, half * LANES:(half + 1) * LANES]
            for blk in range(1, 8):
                src = y_ref[o * 8 + (blk - t8) % 8, :, half * LANES:(half + 1) * LANES]
                acc = jnp.where(masks[blk], src, acc)
            s = ((8 - t8) % 8) * S5_GROUP
            nat = pltpu.roll(acc, s, 1) if s else acc
            tok = pl.ds(t, rows, stride=S5_CHUNK)
            o_ref[o, tok, :] = _gelu_tanh(nat + d_vec * u_ref[o, tok, :])


def _s5_from_chunks_call(yc, u3, d_skip, *, rows=32):
    n_blk, n_tok, _ = u3.shape
    n_groups, n_rows, _ = yc.shape
    tok_spec = pl.BlockSpec((n_blk, rows * S5_CHUNK, LANES), lambda i: (0, i, 0))
    return pl.pallas_call(
        functools.partial(_s5_from_chunks_kernel, rows=rows),
        out_shape=jax.ShapeDtypeStruct(u3.shape, F32),
        grid=(n_rows // rows,),
        in_specs=[pl.BlockSpec((n_groups, rows, 2 * LANES), lambda i: (0, i, 0)),
                  tok_spec,
                  pl.BlockSpec((1, n_blk * LANES), lambda i: (0, 0))],
        out_specs=tok_spec,
        compiler_params=_cparams("arbitrary"),
        name="s5_from_chunks",
    )(yc, u3, d_skip.reshape(1, n_blk * LANES).astype(F32))


def _s5_chunk_kernel(x_ref, kt_ref, pin_ref, po_ref, lam_ref, h0r_ref, h0i_ref,
                     y_ref, fr_ref, fi_ref, r_scr, st_scr, *, segments, seq_block):
    gb = SUBLANES
    rows = x_ref.shape[1]
    lane = lax.broadcasted_iota(jnp.int32, (1, LANES), 1)
    fwd_lanes = lane < (LANES // 2)
    for g in range(gb):
        x = x_ref[g]
        y_ref[g] = jnp.dot(x, kt_ref[g], preferred_element_type=F32)
        r = jnp.dot(x, pin_ref[g], preferred_element_type=F32)
        of_group = pl.ds(g, rows, stride=gb)
        r_scr[0, of_group, :] = r[:, 0:LANES]
        r_scr[1, of_group, :] = r[:, LANES:2 * LANES]
    ar, ai = lam_ref[0], lam_ref[1]
    for row0, n_seq, n_chunks, from_input, to_output in segments:
        for b0 in range(0, n_seq, seq_block):
            nb = min(seq_block, n_seq - b0)

            def step(i, carry, row0=row0, n_chunks=n_chunks, b0=b0, nb=nb):
                out = []
                for k in range(nb):
                    base = row0 + (b0 + k) * n_chunks
                    at_f = pl.ds(pl.multiple_of((base + i) * gb, gb), gb)
                    at_b = pl.ds(pl.multiple_of((base + (n_chunks - 1) - i) * gb, gb), gb)
                    s_re, s_im = carry[k]
                    st_scr[0, at_f, :] = s_re
                    st_scr[1, at_b, :] = s_re
                    st_scr[2, at_f, :] = s_im
                    st_scr[3, at_b, :] = s_im
                    v_re = jnp.where(fwd_lanes, r_scr[0, at_f, :], r_scr[0, at_b, :])
                    v_im = jnp.where(fwd_lanes, r_scr[1, at_f, :], r_scr[1, at_b, :])
                    out.append((ar * s_re - ai * s_im + v_re, ar * s_im + ai * s_re + v_im))
                return tuple(out)

            if from_input:
                init = tuple((h0r_ref[b0 + k], h0i_ref[b0 + k]) for k in range(nb))
            else:
                init = tuple((jnp.zeros((gb, LANES), F32),) * 2 for _ in range(nb))
            fin = lax.fori_loop(0, n_chunks, step, init, unroll=2)
            if to_output:
                for k in range(nb):
                    fr_ref[b0 + k] = fin[k][0]
                    fi_ref[b0 + k] = fin[k][1]
    for g in range(gb):
        of_group = pl.ds(g, rows, stride=gb)
        st = jnp.concatenate([st_scr[cb, of_group, :] for cb in range(4)], axis=1).astype(BF16)
        y_ref[g] = y_ref[g] + jnp.dot(st, po_ref[g], preferred_element_type=F32)


def _s5_chunk_call(xc, kt, pin, pout, lam_rows, h0_re, h0_im, *, layer, segments, n_final):
    n_groups, rows, _ = xc.shape
    gb = SUBLANES
    s_in = h0_re.shape[0]
    kern = functools.partial(_s5_chunk_kernel, segments=segments, seq_block=8)
    g3 = lambda i: (i, 0, 0)
    blk0 = layer * (n_groups // gb)
    p3 = lambda i: (i + blk0, 0, 0)
    mid = lambda i: (0, i, 0)
    return pl.pallas_call(
        kern,
        out_shape=[jax.ShapeDtypeStruct((n_groups, rows, 2 * LANES), F32),
                   jax.ShapeDtypeStruct((n_final, n_groups, LANES), F32),
                   jax.ShapeDtypeStruct((n_final, n_groups, LANES), F32)],
        grid=(n_groups // gb,),
        in_specs=[
            pl.BlockSpec((gb, rows, 2 * LANES), g3),
            pl.BlockSpec((gb, 2 * LANES, 2 * LANES), p3),
            pl.BlockSpec((gb, 2 * LANES, 2 * LANES), p3),
            pl.BlockSpec((gb, 4 * LANES, 2 * LANES), p3),
            pl.BlockSpec((2, gb, LANES), lambda i: (0, i + blk0, 0)),
            pl.BlockSpec((s_in, gb, LANES), mid),
            pl.BlockSpec((s_in, gb, LANES), mid),
        ],
        out_specs=[pl.BlockSpec((gb, rows, 2 * LANES), g3),
                   pl.BlockSpec((n_final, gb, LANES), mid),
                   pl.BlockSpec((n_final, gb, LANES), mid)],
        scratch_shapes=[pltpu.VMEM((2, rows * gb, LANES), F32),
                        pltpu.VMEM((4, rows * gb, LANES), F32)],
        compiler_params=_cparams("arbitrary"),
        name="s5_chunk_scan",
    )(xc, kt, pin, pout, lam_rows, h0_re, h0_im)


def _glu_kernel(y_ref, z_ref, w_ref, b_ref, o_ref, *, n_chunk):
    n_blk = y_ref.shape[0]
    yb = jnp.concatenate([y_ref[o].astype(BF16) for o in range(n_blk)], axis=1)
    per = n_chunk // LANES
    for c in range(0, n_blk, per):
        sl = slice(c * LANES, (c + per) * LANES)
        gate = _sigmoid(jnp.dot(yb, w_ref[:, sl], preferred_element_type=F32) + b_ref[:, sl])
        y = jnp.concatenate([y_ref[c + k] for k in range(per)], axis=1)
        o_ref[:, sl] = (y * gate * _silu(z_ref[:, sl].astype(F32))).astype(o_ref.dtype)


def _glu_call(y3, z, glu_w, glu_b, *, bm=512):
    n_blk, n_tok, _ = y3.shape
    width = n_blk * LANES
    return pl.pallas_call(
        functools.partial(_glu_kernel, n_chunk=min(512, width)),
        out_shape=jax.ShapeDtypeStruct((n_tok, width), BF16),
        grid=(n_tok // bm,),
        in_specs=[
            pl.BlockSpec((n_blk, bm, LANES), lambda i: (0, i, 0)),
            pl.BlockSpec((bm, width), lambda i: (i, 0)),
            pl.BlockSpec((width, width), lambda i: (0, 0)),
            pl.BlockSpec((1, width), lambda i: (0, 0)),
        ],
        out_specs=pl.BlockSpec((bm, width), lambda i: (i, 0)),
        compiler_params=_cparams("arbitrary"),
        name="s5_glu_gate",
    )(y3, z, glu_w, glu_b.reshape(1, width).astype(F32))


def _s5_prep_all(lam_re, lam_im, log_step, b_re, b_im, c_re, c_im):
    tabs = jax.vmap(_s5_tables)(lam_re, lam_im, log_step, b_re, b_im, c_re, c_im)
    rf, rb, pin, pout = [t.reshape((-1,) + t.shape[2:]) for t in tabs[:4]]
    lam_rows = tabs[4].transpose(1, 0, 2, 3).reshape(2, -1, LANES)
    return _s5_kmat_call(rf, rb), pin, pout, lam_rows


def _s5_mix(u3, z, layer, mats, d_skip, glu_w, glu_b, st_re, st_im, *, n_prompt_seq, prompt_len,
            n_sample_seq, sample_len, bm):
    kt, pin, pout, lam_rows = mats
    n_state = LANES // 2
    pc, sc = prompt_len // S5_CHUNK, sample_len // S5_CHUNK

    def state_rows(s):
        return jnp.concatenate([s[:, 0], s[:, 1]], axis=-1).astype(F32)

    segments = ((0, n_prompt_seq, pc, False, True), (n_prompt_seq * pc, n_sample_seq, sc, True, False))
    xc = _s5_to_chunks_call(u3)
    yc, fr, fi = _s5_chunk_call(xc, kt, pin, pout, lam_rows, state_rows(st_re), state_rows(st_im),
                                layer=layer, segments=segments, n_final=n_prompt_seq)
    y3 = _s5_from_chunks_call(yc, u3, d_skip)
    act = _glu_call(y3, z, glu_w, glu_b, bm=bm)

    def unpack(f):
        return jnp.stack([f[:, :, :n_state], f[:, :, n_state:]], axis=1)

    return act, unpack(fr), unpack(fi)


def _pool_kernel(u_ref, z_ref, w_ref, s_ref, o_ref, *, n_prompt_blocks, prompt_len, sample_len):
    rows = u_ref.shape[0]
    seq_len = jnp.where(pl.program_id(0) < n_prompt_blocks, prompt_len, sample_len)
    t = lax.broadcasted_iota(jnp.int32, (rows, 1), 0) & (seq_len - 1)

    def later(x, k):
        return jnp.where(t + k < seq_len, pltpu.roll(x, rows - k, 0), 0.0)

    def earlier(x, k):
        return jnp.where(t >= k, pltpu.roll(x, k, 0), 0.0)

    def body(win):
        lo = win // 2
        u = u_ref[...]
        fwd = u
        bwd = earlier(u, 1)
        s = 1
        while s < lo:
            fwd = fwd + later(fwd, s)
            bwd = bwd + earlier(bwd, s)
            s *= 2
        cnt = jnp.minimum(t - lo + win, seq_len) - jnp.maximum(t - lo, 0)
        p = (fwd + bwd) / cnt.astype(F32) - u
        m = jnp.dot(p.astype(BF16), w_ref[...], preferred_element_type=F32) * s_ref[...]
        o_ref[...] = (m * _silu(z_ref[...].astype(F32))).astype(o_ref.dtype)

    for gi, win in enumerate(POOL_WINDOWS):
        pl.when(pl.program_id(1) == gi)(functools.partial(body, win))


def _pool_call(u, z, pool_w, pool_scale, *, n_prompt, prompt_len, sample_len, rows=2048):
    n_tok, width = u.shape
    n_groups = len(POOL_WINDOWS)
    cg = width // n_groups
    assert prompt_len & (prompt_len - 1) == 0 and sample_len & (sample_len - 1) == 0
    assert rows % prompt_len == 0 and rows % sample_len == 0 and n_prompt % rows == 0
    kern = functools.partial(_pool_kernel, n_prompt_blocks=n_prompt // rows, prompt_len=prompt_len,
                             sample_len=sample_len)
    return pl.pallas_call(
        kern,
        out_shape=jax.ShapeDtypeStruct((n_tok, width), BF16),
        grid=(n_tok // rows, n_groups),
        in_specs=[
            pl.BlockSpec((rows, cg), lambda i, g: (i, g)),
            pl.BlockSpec((rows, cg), lambda i, g: (i, g)),
            pl.BlockSpec((None, cg, cg), lambda i, g: (g, 0, 0)),
            pl.BlockSpec((1, cg), lambda i, g: (0, g)),
        ],
        out_specs=pl.BlockSpec((rows, cg), lambda i, g: (i, g)),
        compiler_params=_cparams("arbitrary", "arbitrary"),
        name="pool_mix",
    )(u, z, pool_w, pool_scale.reshape(1, width).astype(F32))


MLA_QW = 2 * LANES

_ROT_SRC = np.concatenate([np.arange(16, 32), np.arange(0, 16), np.arange(48, 64), np.arange(32, 48)])
_ROT_SIGN = np.concatenate([-np.ones(16), np.ones(16), -np.ones(16), np.ones(16)]).astype(np.float32)


def _rope_tables(n_prompt, n_sample_seq, sample_len):
    half = MLA_ROPE // 4
    tok = jnp.arange(sample_len)
    row = (tok // GRID_W).astype(F32)
    col = (tok % GRID_W).astype(F32)
    inv = ROPE_THETA ** (-jnp.arange(half, dtype=F32) / half)
    a_row, a_col = row[:, None] * inv, col[:, None] * inv
    cos = jnp.concatenate([jnp.cos(a_row), jnp.cos(a_row), jnp.cos(a_col), jnp.cos(a_col)], axis=-1)
    sin = jnp.concatenate([jnp.sin(a_row), jnp.sin(a_row), jnp.sin(a_col), jnp.sin(a_col)], axis=-1)
    pad = jnp.zeros((sample_len, LANES - MLA_ROPE), F32)
    cos_s = jnp.tile(jnp.concatenate([cos, pad], axis=-1), (n_sample_seq, 1))
    sin_s = jnp.tile(jnp.concatenate([sin, pad], axis=-1), (n_sample_seq, 1))
    cos_p = jnp.concatenate([jnp.ones((n_prompt, MLA_ROPE), F32), jnp.zeros((n_prompt, LANES - MLA_ROPE), F32)], -1)
    return jnp.concatenate([cos_p, cos_s]), jnp.concatenate([jnp.zeros((n_prompt, LANES), F32), sin_s])


def _rms(x, g):
    return x * lax.rsqrt(jnp.mean(x * x, axis=-1, keepdims=True) + NORM_EPS) * g


def _mla_post_kernel(sm_ref, cos_ref, sin_ref, qn_ref, kn_ref, wa_ref, wb_ref,
                     q_ref, ckv_ref, kpe_ref, *, q_rank, kv_rank, heads_per_dot):
    cosp, sinp = cos_ref[...], sin_ref[...]
    qn = _rms(sm_ref[:, 0:q_rank], qn_ref[...]).astype(BF16)
    for h0 in range(0, MLA_HEADS, heads_per_dot):
        a = jnp.dot(qn, wa_ref[:, h0 * MLA_QW:(h0 + heads_per_dot) * MLA_QW], preferred_element_type=F32)
        b = jnp.dot(qn, wb_ref[:, h0 * LANES:(h0 + heads_per_dot) * LANES], preferred_element_type=F32)
        for j in range(heads_per_dot):
            h = h0 + j
            q_ref[:, h * MLA_QW:h * MLA_QW + LANES] = a[:, j * MLA_QW:j * MLA_QW + LANES].astype(BF16)
            pe = a[:, j * MLA_QW + LANES:(j + 1) * MLA_QW] * cosp + b[:, j * LANES:(j + 1) * LANES] * sinp
            q_ref[:, h * MLA_QW + LANES:(h + 1) * MLA_QW] = pe.astype(BF16)
    c0 = q_rank
    ckv_ref[...] = _rms(sm_ref[:, c0:c0 + kv_rank], kn_ref[...])
    k0 = c0 + kv_rank
    kpe_ref[...] = (sm_ref[:, k0:k0 + LANES] * cosp + sm_ref[:, k0 + LANES:k0 + 2 * LANES] * sinp).astype(BF16)


def _mla_post_call(small, cos_t, sin_t, q_norm, kv_norm, wq_a, wq_b, *, bm=512):
    n_tok, ws = small.shape
    q_rank, kv_rank = q_norm.shape[-1], kv_norm.shape[-1]
    row = lambda i: (i, 0)
    fix = lambda i: (0, 0)
    kern = functools.partial(_mla_post_kernel, q_rank=q_rank, kv_rank=kv_rank, heads_per_dot=4)
    return pl.pallas_call(
        kern,
        out_shape=[jax.ShapeDtypeStruct((n_tok, MLA_HEADS * MLA_QW), BF16),
                   jax.ShapeDtypeStruct((n_tok, kv_rank), F32),
                   jax.ShapeDtypeStruct((n_tok, LANES), BF16)],
        grid=(n_tok // bm,),
        in_specs=[
            pl.BlockSpec((bm, ws), row),
            pl.BlockSpec((bm, LANES), row),
            pl.BlockSpec((bm, LANES), row),
            pl.BlockSpec((1, q_rank), fix),
            pl.BlockSpec((1, kv_rank), fix),
            pl.BlockSpec(wq_a.shape, fix),
            pl.BlockSpec(wq_b.shape, fix),
        ],
        out_specs=[pl.BlockSpec((bm, MLA_HEADS * MLA_QW), row),
                   pl.BlockSpec((bm, kv_rank), row),
                   pl.BlockSpec((bm, LANES), row)],
        compiler_params=_cparams("arbitrary"),
        name="mla_q_rope",
    )(small, cos_t, sin_t, q_norm.reshape(1, q_rank).astype(F32), kv_norm.reshape(1, kv_rank).astype(F32),
      wq_a, wq_b)


def _kv_expand_kernel(c_ref, w_ref, o_ref, *, n_chunk):
    c = c_ref[...].astype(BF16)
    n = w_ref.shape[1]
    for s in range(0, n, n_chunk):
        o_ref[:, s:s + n_chunk] = jnp.dot(c, w_ref[:, s:s + n_chunk],
                                          preferred_element_type=F32).astype(o_ref.dtype)


def _kv_expand_call(ckv, wkv_b, *, bm=512):
    rows, kr = ckv.shape
    n = wkv_b.shape[1]
    return pl.pallas_call(
        functools.partial(_kv_expand_kernel, n_chunk=1024),
        out_shape=jax.ShapeDtypeStruct((rows, n), BF16),
        grid=(rows // bm,),
        in_specs=[pl.BlockSpec((bm, kr), lambda i: (i, 0)), pl.BlockSpec((kr, n), lambda i: (0, 0))],
        out_specs=pl.BlockSpec((bm, n), lambda i: (i, 0)),
        compiler_params=_cparams("arbitrary"),
        name="mla_kv_expand",
    )(ckv, wkv_b)


def _attn_kernel(q_ref, kv_ref, kpe_ref, z_ref, *rest, hg, scale, kc):
    o_ref, kcat_scr = rest[-2], rest[-1]
    k_len = kv_ref.shape[0]
    qb = q_ref.shape[0]
    c2 = scale * math.log2(math.e)

    @pl.when(pl.program_id(2) == 0)
    def _():
        for j in range(hg):
            kcat_scr[j, :, 0:LANES] = kv_ref[:, j * 2 * LANES:j * 2 * LANES + LANES]
            kcat_scr[j, :, LANES:2 * LANES] = kpe_ref[...]

    for j in range(hg):
        q = q_ref[:, j * MLA_QW:(j + 1) * MLA_QW]
        m = jnp.full((qb, 1), -jnp.inf, F32)
        l = jnp.zeros((qb, 1), F32)
        acc = jnp.zeros((qb, MLA_V), F32)
        for c0 in range(0, k_len, kc):
            rows = slice(c0, min(c0 + kc, k_len))
            s = lax.dot_general(q, kcat_scr[j, rows, :], (((1,), (1,)), ((), ())),
                                preferred_element_type=F32)
            m_new = jnp.maximum(m, jnp.max(s, axis=-1, keepdims=True))
            alpha = jnp.exp2((m - m_new) * c2)
            e = jnp.exp2((s - m_new) * c2)
            l = alpha * l + jnp.sum(e, axis=-1, keepdims=True)
            v = kv_ref[rows, j * 2 * LANES + LANES:(j + 1) * 2 * LANES]
            acc = alpha * acc + jnp.dot(e.astype(BF16), v, preferred_element_type=F32)
            m = m_new
        zs = slice(j * MLA_V, (j + 1) * MLA_V)
        o_ref[:, zs] = (acc / l * _silu(z_ref[:, zs].astype(F32))).astype(o_ref.dtype)


def _attn_call(q, kv, kpe, z, prev, *, q_row0, n_seq, q_len, k_len, hg, qb):
    n_tok = q.shape[0]
    width = MLA_HEADS * MLA_V
    nqb = q_len // qb
    qb0 = q_row0 // qb
    assert q_row0 % qb == 0 and q_len % qb == 0
    scale = float((MLA_NOPE + MLA_ROPE) ** -0.5)
    qrow = lambda b, g, i: (qb0 + b * nqb + i, g)
    in_specs = [
        pl.BlockSpec((qb, hg * MLA_QW), qrow),
        pl.BlockSpec((k_len, hg * 2 * LANES), lambda b, g, i: (b, g)),
        pl.BlockSpec((k_len, LANES), lambda b, g, i: (b, 0)),
        pl.BlockSpec((qb, hg * MLA_V), qrow),
    ]
    args = [q, kv, kpe, z]
    aliases = {}
    if prev is not None:
        in_specs.append(pl.BlockSpec(memory_space=pl.ANY))
        args.append(prev)
        aliases = {4: 0}
    return pl.pallas_call(
        functools.partial(_attn_kernel, hg=hg, scale=scale, kc=512),
        out_shape=jax.ShapeDtypeStruct((n_tok, width), BF16),
        grid=(n_seq, MLA_HEADS // hg, nqb),
        in_specs=in_specs,
        out_specs=pl.BlockSpec((qb, hg * MLA_V), qrow),
        scratch_shapes=[pltpu.VMEM((hg, k_len, MLA_QW), BF16)],
        input_output_aliases=aliases,
        compiler_params=_cparams("arbitrary", "arbitrary", "arbitrary"),
        name="mla_attention",
    )(*args)


def _mla_weights(w_in, wq_b):
    q_rank = wq_b.shape[0]
    kv_rank = w_in.shape[1] - q_rank - MLA_ROPE - MLA_HEADS * MLA_V
    d = w_in.shape[0]
    c_kpe = q_rank + kv_rank
    zpad = jnp.zeros((d, LANES - MLA_ROPE), w_in.dtype)
    kpe_w = w_in[:, c_kpe:c_kpe + MLA_ROPE]
    w_small = jnp.concatenate([w_in[:, :c_kpe], kpe_w, zpad,
                               kpe_w[:, _ROT_SRC] * _ROT_SIGN, zpad], axis=1)
    w_z = w_in[:, c_kpe + MLA_ROPE:]
    hd = MLA_NOPE + MLA_ROPE
    wq3 = wq_b.reshape(q_rank, MLA_HEADS, hd)
    pe = wq3[:, :, MLA_NOPE:]
    z3 = jnp.zeros((q_rank, MLA_HEADS, LANES - MLA_ROPE), wq_b.dtype)
    wq_a = jnp.concatenate([wq3, z3], axis=-1).reshape(q_rank, MLA_HEADS * MLA_QW)
    wq_r = jnp.concatenate([pe[:, :, _ROT_SRC] * _ROT_SIGN, z3], axis=-1).reshape(q_rank, MLA_HEADS * LANES)
    return w_small.astype(BF16), w_z.astype(BF16), wq_a.astype(BF16), wq_r.astype(BF16)


def kernel(x_prompt, x_sample, state_s5_re, state_s5_im, cache_ckv, cache_kpe, c, c_ctx, norm_g, ada_w, ada_b, final_norm_g, s5_w_in, s5_lam_re, s5_lam_im, s5_log_step, s5_b_re, s5_b_im, s5_c_re, s5_c_im, s5_d, s5_glu_w, s5_glu_b, s5_w_out, pool_w_in, pool_w, pool_scale, pool_w_out, mla_w_in, mla_q_norm, mla_wq_b, mla_kv_norm, mla_wkv_b, mla_w_out):
    n_pseq, p_len, d = x_prompt.shape
    n_sseq, s_len, _ = x_sample.shape
    depth = norm_g.shape[0]
    n_prompt = n_pseq * p_len
    bm = 512
    geo = dict(n_prompt=n_prompt, sample_len=s_len, bm=bm)

    x = jnp.concatenate([x_prompt.reshape(n_prompt, d), x_sample.reshape(n_sseq * s_len, d)], axis=0)
    conds = jnp.concatenate([c_ctx[None, :], c, jnp.zeros((SUBLANES - 1 - n_sseq, d), F32)], axis=0)
    mods = _ada_call(conds.astype(F32), ada_w, ada_b)
    mods = mods.reshape(depth, SUBLANES, 1, 3 * d)
    s5_mats = _s5_prep_all(s5_lam_re, s5_lam_im, s5_log_step, s5_b_re, s5_b_im, s5_c_re, s5_c_im)

    new_re, new_im, new_ckv, new_kpe = [], [], [], []
    for layer in range(depth):
        kind, j = layer % N_MIXERS, layer // N_MIXERS
        last = layer == depth - 1
        ml = mods[layer]
        if kind == 0:
            width = s5_w_in.shape[2] // 2
            w = s5_w_in[j].astype(BF16)
            u3, z = _inproj_call(x, ml, norm_g[layer], [(w, 0, width), (w, 1, width)], [F32, BF16],
                                 lane_blocked=(0,), **geo)
            act, f_re, f_im = _s5_mix(u3, z, j, s5_mats, s5_d[j], s5_glu_w[j].astype(BF16), s5_glu_b[j],
                                      state_s5_re[:, j], state_s5_im[:, j], n_prompt_seq=n_pseq,
                                      prompt_len=p_len, n_sample_seq=n_sseq, sample_len=s_len, bm=bm)
            new_re.append(f_re)
            new_im.append(f_im)
            w_out = s5_w_out[j]
        elif kind == 1:
            width = pool_w_in.shape[2] // 2
            w = pool_w_in[j].astype(BF16)
            u, z = _inproj_call(x, ml, norm_g[layer], [(w, 0, width), (w, 1, width)], [F32, BF16], **geo)
            act = _pool_call(u, z, pool_w[j].astype(BF16), pool_scale[j], n_prompt=n_prompt,
                             prompt_len=p_len, sample_len=s_len)
            w_out = pool_w_out[j]
        else:
            q_rank, kv_rank = mla_q_norm.shape[-1], mla_kv_norm.shape[-1]
            w_small, w_z, wq_a, wq_r = _mla_weights(mla_w_in[j], mla_wq_b[j])
            small, z = _inproj_call(x, ml, norm_g[layer], [w_small, w_z], [F32, BF16], **geo)
            cos_t, sin_t = _rope_tables(n_prompt, n_sseq, s_len)
            q, ckv_n, kpe_k = _mla_post_call(small, cos_t, sin_t, mla_q_norm[j], mla_kv_norm[j], wq_a, wq_r,
                                             bm=bm)
            wkv = mla_wkv_b[j].astype(BF16)
            past = cache_ckv.shape[2]
            k_len = past + s_len
            ckv_s = jnp.concatenate([cache_ckv[:, j].astype(F32), ckv_n[n_prompt:].reshape(n_sseq, s_len, kv_rank)],
                                    axis=1).reshape(n_sseq * k_len, kv_rank)
            kpe_cache = jnp.concatenate([cache_kpe[:, j].astype(BF16),
                                         jnp.zeros((n_sseq, past, LANES - MLA_ROPE), BF16)], axis=-1)
            kpe_s = jnp.concatenate([kpe_cache, kpe_k[n_prompt:].reshape(n_sseq, s_len, LANES)],
                                    axis=1).reshape(n_sseq * k_len, LANES)
            kv_p = _kv_expand_call(ckv_n[:n_prompt], wkv, bm=bm)
            kv_s = _kv_expand_call(ckv_s, wkv, bm=bm)
            act = _attn_call(q, kv_p, kpe_k[:n_prompt], z, None, q_row0=0, n_seq=n_pseq, q_len=p_len,
                             k_len=p_len, hg=MLA_HEADS, qb=p_len)
            act = _attn_call(q, kv_s, kpe_s, z, act, q_row0=n_prompt, n_seq=n_sseq, q_len=s_len,
                             k_len=k_len, hg=4, qb=256)
            new_ckv.append(ckv_n[:n_prompt].reshape(n_pseq, p_len, kv_rank))
            c_kpe = q_rank + kv_rank
            new_kpe.append(small[:n_prompt, c_kpe:c_kpe + MLA_ROPE].reshape(n_pseq, p_len, MLA_ROPE))
            w_out = mla_w_out[j]
        x = _outproj_call(act, x, ml, w_out.astype(BF16), final_norm_g, final_norm=last, **geo)

    y_prompt = x[:n_prompt].reshape(n_pseq, p_len, d)
    y_sample = x[n_prompt:].reshape(n_sseq, s_len, d)
    return (y_prompt, y_sample, jnp.stack(new_re, axis=1), jnp.stack(new_im, axis=1),
            jnp.stack(new_ckv, axis=1), jnp.stack(new_kpe, axis=1))
```

```python
import functools
import math

import jax
import jax.numpy as jnp
import numpy as np
from jax import lax
from jax.experimental import pallas as pl
from jax.experimental.pallas import tpu as pltpu

S5_GROUP = 16
S5_CHUNK = 16
POOL_WINDOWS = (2, 4, 8, 16)
MLA_HEADS = 16
MLA_NOPE = 128
MLA_ROPE = 64
MLA_V = 128
GRID_W = 64
ROPE_THETA = 10000.0
NORM_EPS = 1e-6
N_MIXERS = 3

LANES = 128
SUBLANES = 8
VMEM_LIMIT_BYTES = 56 * 1024 * 1024

F32 = jnp.float32
BF16 = jnp.bfloat16
HIGHEST = lax.Precision.HIGHEST


def _cparams(*sem):
    return pltpu.CompilerParams(dimension_semantics=sem, vmem_limit_bytes=VMEM_LIMIT_BYTES)


def _sigmoid(x):
    return 1.0 / (1.0 + jnp.exp(-x))


def _silu(x):
    return x * _sigmoid(x)


def _gelu_tanh(x):
    c = math.sqrt(2.0 / math.pi)
    return 0.5 * x * (1.0 + jnp.tanh(c * (x + 0.044715 * (x * x * x))))


def _ada_kernel(c_ref, w_ref, b_ref, o_ref):
    a = _silu(c_ref[...])
    o_ref[...] = jnp.dot(a, w_ref[...], preferred_element_type=F32, precision=HIGHEST) + b_ref[...]


def _ada_call(conds, ada_w, ada_b):
    depth, d, d3 = ada_w.shape
    c8 = conds.shape[0]
    tn = 512
    return pl.pallas_call(
        _ada_kernel,
        out_shape=jax.ShapeDtypeStruct((depth, c8, d3), F32),
        grid=(depth, d3 // tn),
        in_specs=[
            pl.BlockSpec((c8, d), lambda l, n: (0, 0)),
            pl.BlockSpec((None, d, tn), lambda l, n: (l, 0, n)),
            pl.BlockSpec((None, 1, tn), lambda l, n: (l, 0, n)),
        ],
        out_specs=pl.BlockSpec((None, c8, tn), lambda l, n: (l, 0, n)),
        compiler_params=_cparams("arbitrary", "arbitrary"),
        name="ada_mod",
    )(conds, ada_w, ada_b.reshape(depth, 1, d3))


def _cond_of_block(i, n_prompt_blocks, blocks_per_sample):
    return jnp.where(i < n_prompt_blocks, 0, 1 + (i - n_prompt_blocks) // blocks_per_sample)


def _modulated(x, mod_ref, g_ref, d):
    ms = jnp.mean(x * x, axis=-1, keepdims=True)
    y = x * lax.rsqrt(ms + NORM_EPS) * g_ref[...]
    shift = mod_ref[:, 0:d]
    scale = mod_ref[:, d:2 * d]
    return (y * (1.0 + scale) + shift).astype(BF16)


def _inproj_kernel(x_ref, mod_ref, g_ref, *rest, d, n_chunk):
    n_out = len(rest) // 2
    w_refs, o_refs = rest[:n_out], rest[n_out:]
    h = _modulated(x_ref[...], mod_ref, g_ref, d)
    for w_ref, o_ref in zip(w_refs, o_refs):
        n = w_ref.shape[1]
        for c in range(0, n, n_chunk):
            e = min(c + n_chunk, n)
            r = jnp.dot(h, w_ref[:, c:e], preferred_element_type=F32).astype(o_ref.dtype)
            if len(o_ref.shape) == 3:
                for lb in range((e - c) // LANES):
                    o_ref[c // LANES + lb] = r[:, lb * LANES:(lb + 1) * LANES]
            else:
                o_ref[:, c:e] = r


def _inproj_call(x, mods_l, norm_g, weights, out_dtypes, *, n_prompt, sample_len, bm=512,
                 lane_blocked=()):
    n_tok, d = x.shape
    npb, bps = n_prompt // bm, sample_len // bm
    cond = functools.partial(_cond_of_block, n_prompt_blocks=npb, blocks_per_sample=bps)
    weights = [w if isinstance(w, tuple) else (w, 0, w.shape[1]) for w in weights]
    in_specs = [
        pl.BlockSpec((bm, d), lambda i: (i, 0)),
        pl.BlockSpec((None, 1, 3 * d), lambda i: (cond(i), 0, 0)),
        pl.BlockSpec((1, d), lambda i: (0, 0)),
    ] + [pl.BlockSpec((d, n), functools.partial(lambda i, blk: (0, blk), blk=blk)) for _, blk, n in weights]
    out_specs, out_shape = [], []
    for k, ((_, _, n), dt) in enumerate(zip(weights, out_dtypes)):
        if k in lane_blocked:
            out_specs.append(pl.BlockSpec((n // LANES, bm, LANES), lambda i: (0, i, 0)))
            out_shape.append(jax.ShapeDtypeStruct((n // LANES, n_tok, LANES), dt))
        else:
            out_specs.append(pl.BlockSpec((bm, n), lambda i: (i, 0)))
            out_shape.append(jax.ShapeDtypeStruct((n_tok, n), dt))
    return pl.pallas_call(
        functools.partial(_inproj_kernel, d=d, n_chunk=512),
        out_shape=out_shape,
        grid=(n_tok // bm,),
        in_specs=in_specs,
        out_specs=out_specs,
        compiler_params=_cparams("arbitrary"),
        name="norm_mod_inproj",
    )(x, mods_l, norm_g.reshape(1, d), *[w for w, _, _ in weights])


def _outproj_kernel(a_ref, x_ref, mod_ref, w_ref, fg_ref, o_ref, *, d, final_norm):
    y = jnp.dot(a_ref[...], w_ref[...], preferred_element_type=F32)
    gate = mod_ref[:, 2 * d:3 * d]
    xn = x_ref[...] + gate * y
    if final_norm:
        ms = jnp.mean(xn * xn, axis=-1, keepdims=True)
        xn = xn * lax.rsqrt(ms + NORM_EPS) * fg_ref[...]
    o_ref[...] = xn


def _outproj_call(act, x, mods_l, w_out, final_g, *, n_prompt, sample_len, final_norm, bm=512):
    n_tok, d = x.shape
    k = act.shape[1]
    npb, bps = n_prompt // bm, sample_len // bm
    cond = functools.partial(_cond_of_block, n_prompt_blocks=npb, blocks_per_sample=bps)
    return pl.pallas_call(
        functools.partial(_outproj_kernel, d=d, final_norm=final_norm),
        out_shape=jax.ShapeDtypeStruct((n_tok, d), F32),
        grid=(n_tok // bm,),
        in_specs=[
            pl.BlockSpec((bm, k), lambda i: (i, 0)),
            pl.BlockSpec((bm, d), lambda i: (i, 0)),
            pl.BlockSpec((None, 1, 3 * d), lambda i: (cond(i), 0, 0)),
            pl.BlockSpec((k, d), lambda i: (0, 0)),
            pl.BlockSpec((1, d), lambda i: (0, 0)),
        ],
        out_specs=pl.BlockSpec((bm, d), lambda i: (i, 0)),
        compiler_params=_cparams("arbitrary"),
        name="outproj_residual",
    )(act, x, mods_l, w_out, final_g.reshape(1, d))


def _s5_time_of_lane_block():
    pos = np.arange(S5_CHUNK)
    half, blk = pos // 8, pos % 8
    g8 = np.arange(8)[:, None]
    return 8 * half[None, :] + (blk[None, :] - g8) % 8


def _s5_tables(lam_re, lam_im, log_step, b_re, b_im, c_re, c_im):
    t_chunk = S5_CHUNK
    n_groups, n_state = lam_re.shape[1], lam_re.shape[2]
    n_oct = n_groups // 8
    lam = lax.complex(lam_re.astype(F32), lam_im.astype(F32))
    step = jnp.exp(log_step.astype(F32))[..., None]
    lam_bar = jnp.exp(lam * step)
    b_bar = ((lam_bar - 1.0) / lam)[..., None] * lax.complex(b_re.astype(F32), b_im.astype(F32))
    c_mat = lax.complex(c_re.astype(F32), c_im.astype(F32))
    ks = jnp.arange(t_chunk + 1, dtype=F32)[:, None, None, None]
    pw = jnp.exp(ks * (lam * step)[None])

    cb = jnp.einsum('dgop,ldgp,dgpi->ldgoi', c_mat, pw[:t_chunk], b_bar, precision=HIGHEST).real
    vcol = jnp.concatenate([cb[:0:-1, 0], (cb[0, 0] + cb[0, 1])[None], cb[1:, 1]], axis=0)
    vcol = vcol.transpose(1, 0, 3, 2).reshape(n_groups, (2 * t_chunk - 1) * S5_GROUP, S5_GROUP)
    vrep = jnp.tile(vcol, (1, 1, LANES // S5_GROUP))

    tl = _s5_time_of_lane_block()
    pw_ri = jnp.stack([pw.real, pw.imag]).reshape(2, t_chunk + 1, 2, n_oct, 8, n_state)
    m_idx = np.arange(t_chunk + 1)[None, None, :]

    def power_table(exponent, direction):
        sel = (exponent[:, :, None] == m_idx).astype(np.float32)
        tab = jnp.einsum('kxm,rmakp->rakxp', sel, pw_ri[:, :, direction], precision=HIGHEST)
        return tab.reshape(2, n_groups, t_chunk, n_state)

    def both(fwd, bwd):
        m = jnp.concatenate([fwd, bwd], axis=-1)
        return jnp.stack([m.real, m.imag])

    tin = jnp.concatenate([power_table(t_chunk - 1 - tl, 0), power_table(tl, 1)], axis=-1)
    tout = jnp.concatenate([power_table(tl + 1, 0), power_table(t_chunk - tl, 1)], axis=-1)
    bt = both(b_bar[0].transpose(0, 2, 1), b_bar[1].transpose(0, 2, 1))
    ct = both(c_mat[0], c_mat[1])
    lam_rows = both(pw[t_chunk, 0][:, None], pw[t_chunk, 1][:, None])[:, :, 0]
    return vrep, tin, tout, bt, ct, lam_rows


def _s5_kmat_kernel(v_ref, tin_ref, tout_ref, bt_ref, ct_ref, k_ref, pin_ref, pot_ref):
    masks = _lane_block_masks()
    for g8 in range(8):
        br, bi = bt_ref[0, g8], bt_ref[1, g8]
        cr, ci = ct_ref[0, g8], ct_ref[1, g8]
        for pos in range(S5_CHUNK):
            rows = slice(pos * S5_GROUP, (pos + 1) * S5_GROUP)
            tr, ti = tin_ref[0, g8, pos:pos + 1, :], tin_ref[1, g8, pos:pos + 1, :]
            pin_ref[g8, rows, 0:LANES] = (tr * br - ti * bi).astype(BF16)
            pin_ref[g8, rows, LANES:2 * LANES] = (tr * bi + ti * br).astype(BF16)
            tr, ti = tout_ref[0, g8, pos:pos + 1, :], tout_ref[1, g8, pos:pos + 1, :]
            pot_ref[g8, rows, 0:LANES] = (tr * cr - ti * ci).astype(BF16)
            pot_ref[g8, rows, LANES:2 * LANES] = (-(tr * ci + ti * cr)).astype(BF16)
        for pos in range(S5_CHUNK):
            sigma = 8 * (pos // 8) + (pos % 8 - g8) % 8
            rows = slice(pos * S5_GROUP, (pos + 1) * S5_GROUP)
            for half in range(2):
                acc = None
                for blk in range(8):
                    tau = 8 * half + (blk - g8) % 8
                    m = S5_CHUNK - 1 - tau + sigma
                    src = v_ref[g8, m * S5_GROUP:(m + 1) * S5_GROUP, :]
                    acc = src if acc is None else jnp.where(masks[blk], src, acc)
                k_ref[g8, rows, half * LANES:(half + 1) * LANES] = acc.astype(BF16)


def _s5_kmat_call(vrep, tin, tout, bt, ct):
    n_groups = vrep.shape[0]
    lag_spec = pl.BlockSpec((8,) + vrep.shape[1:], lambda i: (i, 0, 0))
    tab_spec = pl.BlockSpec((2, 8, S5_GROUP, LANES), lambda i: (0, i, 0, 0))
    mat = jax.ShapeDtypeStruct((n_groups, 2 * LANES, 2 * LANES), BF16)
    mat_spec = pl.BlockSpec((8, 2 * LANES, 2 * LANES), lambda i: (i, 0, 0))
    return pl.pallas_call(
        _s5_kmat_kernel,
        out_shape=[mat, mat, mat],
        grid=(n_groups // 8,),
        in_specs=[lag_spec, tab_spec, tab_spec, tab_spec, tab_spec],
        out_specs=[mat_spec, mat_spec, mat_spec],
        compiler_params=_cparams("arbitrary"),
        name="s5_kmat",
    )(vrep, tin, tout, bt, ct)


def _lane_block_masks():
    blk = lax.broadcasted_iota(jnp.int32, (1, LANES), 1) // S5_GROUP
    return [blk == b for b in range(8)]


def _s5_to_chunks_kernel(u_ref, x_ref, *, rows):
    masks = _lane_block_masks()
    for o in range(u_ref.shape[0]):
        rolled = []
        for t in range(S5_CHUNK):
            v = u_ref[o, pl.ds(t, rows, stride=S5_CHUNK), :]
            s = (t % 8) * S5_GROUP
            rolled.append(pltpu.roll(v, s, 1) if s else v)
        for g8 in range(8):
            for half in range(2):
                acc = rolled[8 * half + (0 - g8) % 8]
                for blk in range(1, 8):
                    acc = jnp.where(masks[blk], rolled[8 * half + (blk - g8) % 8], acc)
                x_ref[o * 8 + g8, :, half * LANES:(half + 1) * LANES] = acc.astype(BF16)


def _s5_to_chunks_call(u3, *, rows=32):
    n_blk, n_tok, _ = u3.shape
    n_groups = n_blk * 8
    n_rows = n_tok // S5_CHUNK
    return pl.pallas_call(
        functools.partial(_s5_to_chunks_kernel, rows=rows),
        out_shape=jax.ShapeDtypeStruct((n_groups, n_rows, 2 * LANES), BF16),
        grid=(n_rows // rows,),
        in_specs=[pl.BlockSpec((n_blk, rows * S5_CHUNK, LANES), lambda i: (0, i, 0))],
        out_specs=pl.BlockSpec((n_groups, rows, 2 * LANES), lambda i: (0, i, 0)),
        compiler_params=_cparams("arbitrary"),
        name="s5_to_chunks",
    )(u3)


def _s5_from_chunks_kernel(y_ref, u_ref, d_ref, o_ref, *, rows):
    masks = _lane_block_masks()
    for o in range(u_ref.shape[0]):
        d_vec = d_ref[:, o * LANES:(o + 1) * LANES]
        for t in range(S5_CHUNK):
            half, t8 = t // 8, t % 8
            acc = y_ref[o * 8 + (0 - t8) % 8, :, half * LANES:(half + 1) * LANES]
            for blk in range(1, 8):
                src = y_ref[o * 8 + (blk - t8) % 8, :, half * LANES:(half + 1) * LANES]
                acc = jnp.where(masks[blk], src, acc)
            s = ((8 - t8) % 8) * S5_GROUP
            nat = pltpu.roll(acc, s, 1) if s else acc
            tok = pl.ds(t, rows, stride=S5_CHUNK)
            o_ref[o, tok, :] = _gelu_tanh(nat + d_vec * u_ref[o, tok, :])


def _s5_from_chunks_call(yc, u3, d_skip, *, rows=32):
    n_blk, n_tok, _ = u3.shape
    n_groups, n_rows, _ = yc.shape
    tok_spec = pl.BlockSpec((n_blk, rows * S5_CHUNK, LANES), lambda i: (0, i, 0))
    return pl.pallas_call(
        functools.partial(_s5_from_chunks_kernel, rows=rows),
        out_shape=jax.ShapeDtypeStruct(u3.shape, F32),
        grid=(n_rows // rows,),
        in_specs=[pl.BlockSpec((n_groups, rows, 2 * LANES), lambda i: (0, i, 0)),
                  tok_spec,
                  pl.BlockSpec((1, n_blk * LANES), lambda i: (0, 0))],
        out_specs=tok_spec,
        compiler_params=_cparams("arbitrary"),
        name="s5_from_chunks",
    )(yc, u3, d_skip.reshape(1, n_blk * LANES).astype(F32))


def _s5_chunk_kernel(x_ref, kt_ref, pin_ref, po_ref, lam_ref, h0r_ref, h0i_ref,
                     y_ref, fr_ref, fi_ref, r_scr, st_scr, *, segments, seq_block):
    gb = SUBLANES
    rows = x_ref.shape[1]
    lane = lax.broadcasted_iota(jnp.int32, (1, LANES), 1)
    fwd_lanes = lane < (LANES // 2)
    for g in range(gb):
        x = x_ref[g]
        y_ref[g] = jnp.dot(x, kt_ref[g], preferred_element_type=F32)
        r = jnp.dot(x, pin_ref[g], preferred_element_type=F32)
        of_group = pl.ds(g, rows, stride=gb)
        r_scr[0, of_group, :] = r[:, 0:LANES]
        r_scr[1, of_group, :] = r[:, LANES:2 * LANES]
    ar, ai = lam_ref[0], lam_ref[1]
    for row0, n_seq, n_chunks, from_input, to_output in segments:
        for b0 in range(0, n_seq, seq_block):
            nb = min(seq_block, n_seq - b0)

            def step(i, carry, row0=row0, n_chunks=n_chunks, b0=b0, nb=nb):
                out = []
                for k in range(nb):
                    base = row0 + (b0 + k) * n_chunks
                    at_f = pl.ds(pl.multiple_of((base + i) * gb, gb), gb)
                    at_b = pl.ds(pl.multiple_of((base + (n_chunks - 1) - i) * gb, gb), gb)
                    s_re, s_im = carry[k]
                    half = LANES // 2
                    st_scr[0, at_f, 0:half] = s_re[:, 0:half]
                    st_scr[0, at_b, half:LANES] = s_re[:, half:LANES]
                    st_scr[1, at_f, 0:half] = s_im[:, 0:half]
                    st_scr[1, at_b, half:LANES] = s_im[:, half:LANES]
                    v_re = jnp.where(fwd_lanes, r_scr[0, at_f, :], r_scr[0, at_b, :])
                    v_im = jnp.where(fwd_lanes, r_scr[1, at_f, :], r_scr[1, at_b, :])
                    out.append((ar * s_re - ai * s_im + v_re, ar * s_im + ai * s_re + v_im))
                return tuple(out)

            if from_input:
                init = tuple((h0r_ref[b0 + k], h0i_ref[b0 + k]) for k in range(nb))
            else:
                init = tuple((jnp.zeros((gb, LANES), F32),) * 2 for _ in range(nb))
            fin = lax.fori_loop(0, n_chunks, step, init, unroll=2)
            if to_output:
                for k in range(nb):
                    fr_ref[b0 + k] = fin[k][0]
                    fi_ref[b0 + k] = fin[k][1]
    for g in range(gb):
        of_group = pl.ds(g, rows, stride=gb)
        st = jnp.concatenate([st_scr[cb, of_group, :] for cb in range(2)], axis=1).astype(BF16)
        y_ref[g] = y_ref[g] + lax.dot_general(st, po_ref[g], (((1,), (1,)), ((), ())),
                                              preferred_element_type=F32)


def _s5_chunk_call(xc, kt, pin, pout, lam_rows, h0_re, h0_im, *, layer, segments, n_final):
    n_groups, rows, _ = xc.shape
    gb = SUBLANES
    s_in = h0_re.shape[0]
    kern = functools.partial(_s5_chunk_kernel, segments=segments, seq_block=8)
    g3 = lambda i: (i, 0, 0)
    blk0 = layer * (n_groups // gb)
    p3 = lambda i: (i + blk0, 0, 0)
    mid = lambda i: (0, i, 0)
    return pl.pallas_call(
        kern,
        out_shape=[jax.ShapeDtypeStruct((n_groups, rows, 2 * LANES), F32),
                   jax.ShapeDtypeStruct((n_final, n_groups, LANES), F32),
                   jax.ShapeDtypeStruct((n_final, n_groups, LANES), F32)],
        grid=(n_groups // gb,),
        in_specs=[
            pl.BlockSpec((gb, rows, 2 * LANES), g3),
            pl.BlockSpec((gb, 2 * LANES, 2 * LANES), p3),
            pl.BlockSpec((gb, 2 * LANES, 2 * LANES), p3),
            pl.BlockSpec((gb, 2 * LANES, 2 * LANES), p3),
            pl.BlockSpec((2, gb, LANES), lambda i: (0, i + blk0, 0)),
            pl.BlockSpec((s_in, gb, LANES), mid),
            pl.BlockSpec((s_in, gb, LANES), mid),
        ],
        out_specs=[pl.BlockSpec((gb, rows, 2 * LANES), g3),
                   pl.BlockSpec((n_final, gb, LANES), mid),
                   pl.BlockSpec((n_final, gb, LANES), mid)],
        scratch_shapes=[pltpu.VMEM((2, rows * gb, LANES), F32),
                        pltpu.VMEM((2, rows * gb, LANES), F32)],
        compiler_params=_cparams("arbitrary"),
        name="s5_chunk_scan",
    )(xc, kt, pin, pout, lam_rows, h0_re, h0_im)


def _glu_kernel(y_ref, z_ref, w_ref, b_ref, o_ref, *, n_chunk):
    n_blk = y_ref.shape[0]
    yb = jnp.concatenate([y_ref[o].astype(BF16) for o in range(n_blk)], axis=1)
    per = n_chunk // LANES
    for c in range(0, n_blk, per):
        sl = slice(c * LANES, (c + per) * LANES)
        gate = _sigmoid(jnp.dot(yb, w_ref[:, sl], preferred_element_type=F32) + b_ref[:, sl])
        y = jnp.concatenate([y_ref[c + k] for k in range(per)], axis=1)
        o_ref[:, sl] = (y * gate * _silu(z_ref[:, sl].astype(F32))).astype(o_ref.dtype)


def _glu_call(y3, z, glu_w, glu_b, *, bm=512):
    n_blk, n_tok, _ = y3.shape
    width = n_blk * LANES
    return pl.pallas_call(
        functools.partial(_glu_kernel, n_chunk=min(512, width)),
        out_shape=jax.ShapeDtypeStruct((n_tok, width), BF16),
        grid=(n_tok // bm,),
        in_specs=[
            pl.BlockSpec((n_blk, bm, LANES), lambda i: (0, i, 0)),
            pl.BlockSpec((bm, width), lambda i: (i, 0)),
            pl.BlockSpec((width, width), lambda i: (0, 0)),
            pl.BlockSpec((1, width), lambda i: (0, 0)),
        ],
        out_specs=pl.BlockSpec((bm, width), lambda i: (i, 0)),
        compiler_params=_cparams("arbitrary"),
        name="s5_glu_gate",
    )(y3, z, glu_w, glu_b.reshape(1, width).astype(F32))


def _s5_prep_all(lam_re, lam_im, log_step, b_re, b_im, c_re, c_im):
    tabs = jax.vmap(_s5_tables)(lam_re, lam_im, log_step, b_re, b_im, c_re, c_im)
    vrep = tabs[0].reshape((-1,) + tabs[0].shape[2:])
    tin, tout, bt, ct, lam_rows = [jnp.moveaxis(t, 0, 1).reshape((2, -1) + t.shape[3:]) for t in tabs[1:]]
    kt, pin, pot = _s5_kmat_call(vrep, tin, tout, bt, ct)
    return kt, pin, pot, lam_rows


def _s5_mix(u3, z, layer, mats, d_skip, glu_w, glu_b, st_re, st_im, *, n_prompt_seq, prompt_len,
            n_sample_seq, sample_len, bm):
    kt, pin, pout, lam_rows = mats
    n_state = LANES // 2
    pc, sc = prompt_len // S5_CHUNK, sample_len // S5_CHUNK

    def state_rows(s):
        return jnp.concatenate([s[:, 0], s[:, 1]], axis=-1).astype(F32)

    segments = ((0, n_prompt_seq, pc, False, True), (n_prompt_seq * pc, n_sample_seq, sc, True, False))
    xc = _s5_to_chunks_call(u3)
    yc, fr, fi = _s5_chunk_call(xc, kt, pin, pout, lam_rows, state_rows(st_re), state_rows(st_im),
                                layer=layer, segments=segments, n_final=n_prompt_seq)
    y3 = _s5_from_chunks_call(yc, u3, d_skip)
    act = _glu_call(y3, z, glu_w, glu_b, bm=bm)

    def unpack(f):
        return jnp.stack([f[:, :, :n_state], f[:, :, n_state:]], axis=1)

    return act, unpack(fr), unpack(fi)


def _pool_kernel(u_ref, z_ref, w_ref, s_ref, o_ref, *, n_prompt_blocks, prompt_len, sample_len):
    rows = u_ref.shape[0]
    seq_len = jnp.where(pl.program_id(0) < n_prompt_blocks, prompt_len, sample_len)
    t = lax.broadcasted_iota(jnp.int32, (rows, 1), 0) & (seq_len - 1)

    def later(x, k):
        return jnp.where(t + k < seq_len, pltpu.roll(x, rows - k, 0), 0.0)

    def earlier(x, k):
        return jnp.where(t >= k, pltpu.roll(x, k, 0), 0.0)

    def body(win):
        lo = win // 2
        u = u_ref[...]
        fwd = u
        bwd = earlier(u, 1)
        s = 1
        while s < lo:
            fwd = fwd + later(fwd, s)
            bwd = bwd + earlier(bwd, s)
            s *= 2
        cnt = jnp.minimum(t - lo + win, seq_len) - jnp.maximum(t - lo, 0)
        p = (fwd + bwd) / cnt.astype(F32) - u
        m = jnp.dot(p.astype(BF16), w_ref[...], preferred_element_type=F32) * s_ref[...]
        o_ref[...] = (m * _silu(z_ref[...].astype(F32))).astype(o_ref.dtype)

    for gi, win in enumerate(POOL_WINDOWS):
        pl.when(pl.program_id(1) == gi)(functools.partial(body, win))


def _pool_call(u, z, pool_w, pool_scale, *, n_prompt, prompt_len, sample_len, rows=2048):
    n_tok, width = u.shape
    n_groups = len(POOL_WINDOWS)
    cg = width // n_groups
    assert prompt_len & (prompt_len - 1) == 0 and sample_len & (sample_len - 1) == 0
    assert rows % prompt_len == 0 and rows % sample_len == 0 and n_prompt % rows == 0
    kern = functools.partial(_pool_kernel, n_prompt_blocks=n_prompt // rows, prompt_len=prompt_len,
                             sample_len=sample_len)
    return pl.pallas_call(
        kern,
        out_shape=jax.ShapeDtypeStruct((n_tok, width), BF16),
        grid=(n_tok // rows, n_groups),
        in_specs=[
            pl.BlockSpec((rows, cg), lambda i, g: (i, g)),
            pl.BlockSpec((rows, cg), lambda i, g: (i, g)),
            pl.BlockSpec((None, cg, cg), lambda i, g: (g, 0, 0)),
            pl.BlockSpec((1, cg), lambda i, g: (0, g)),
        ],
        out_specs=pl.BlockSpec((rows, cg), lambda i, g: (i, g)),
        compiler_params=_cparams("arbitrary", "arbitrary"),
        name="pool_mix",
    )(u, z, pool_w, pool_scale.reshape(1, width).astype(F32))


MLA_QW = 2 * LANES

_ROT_SRC = np.concatenate([np.arange(16, 32), np.arange(0, 16), np.arange(48, 64), np.arange(32, 48)])
_ROT_SIGN = np.concatenate([-np.ones(16), np.ones(16), -np.ones(16), np.ones(16)]).astype(np.float32)


def _rope_tables(n_prompt, n_sample_seq, sample_len):
    half = MLA_ROPE // 4
    tok = jnp.arange(sample_len)
    row = (tok // GRID_W).astype(F32)
    col = (tok % GRID_W).astype(F32)
    inv = ROPE_THETA ** (-jnp.arange(half, dtype=F32) / half)
    a_row, a_col = row[:, None] * inv, col[:, None] * inv
    cos = jnp.concatenate([jnp.cos(a_row), jnp.cos(a_row), jnp.cos(a_col), jnp.cos(a_col)], axis=-1)
    sin = jnp.concatenate([jnp.sin(a_row), jnp.sin(a_row), jnp.sin(a_col), jnp.sin(a_col)], axis=-1)
    pad = jnp.zeros((sample_len, LANES - MLA_ROPE), F32)
    cos_s = jnp.tile(jnp.concatenate([cos, pad], axis=-1), (n_sample_seq, 1))
    sin_s = jnp.tile(jnp.concatenate([sin, pad], axis=-1), (n_sample_seq, 1))
    cos_p = jnp.concatenate([jnp.ones((n_prompt, MLA_ROPE), F32), jnp.zeros((n_prompt, LANES - MLA_ROPE), F32)], -1)
    return jnp.concatenate([cos_p, cos_s]), jnp.concatenate([jnp.zeros((n_prompt, LANES), F32), sin_s])


def _rms(x, g):
    return x * lax.rsqrt(jnp.mean(x * x, axis=-1, keepdims=True) + NORM_EPS) * g


def _mla_post_kernel(sm_ref, cos_ref, sin_ref, qn_ref, kn_ref, wa_ref, wb_ref,
                     q_ref, ckv_ref, kpe_ref, *, q_rank, kv_rank, heads_per_dot):
    cosp, sinp = cos_ref[...], sin_ref[...]
    qn = _rms(sm_ref[:, 0:q_rank], qn_ref[...]).astype(BF16)
    for h0 in range(0, MLA_HEADS, heads_per_dot):
        a = jnp.dot(qn, wa_ref[:, h0 * MLA_QW:(h0 + heads_per_dot) * MLA_QW], preferred_element_type=F32)
        b = jnp.dot(qn, wb_ref[:, h0 * LANES:(h0 + heads_per_dot) * LANES], preferred_element_type=F32)
        for j in range(heads_per_dot):
            h = h0 + j
            q_ref[:, h * MLA_QW:h * MLA_QW + LANES] = a[:, j * MLA_QW:j * MLA_QW + LANES].astype(BF16)
            pe = a[:, j * MLA_QW + LANES:(j + 1) * MLA_QW] * cosp + b[:, j * LANES:(j + 1) * LANES] * sinp
            q_ref[:, h * MLA_QW + LANES:(h + 1) * MLA_QW] = pe.astype(BF16)
    c0 = q_rank
    ckv_ref[...] = _rms(sm_ref[:, c0:c0 + kv_rank], kn_ref[...])
    k0 = c0 + kv_rank
    kpe_ref[...] = (sm_ref[:, k0:k0 + LANES] * cosp + sm_ref[:, k0 + LANES:k0 + 2 * LANES] * sinp).astype(BF16)


def _mla_post_call(small, cos_t, sin_t, q_norm, kv_norm, wq_a, wq_b, *, bm=512):
    n_tok, ws = small.shape
    q_rank, kv_rank = q_norm.shape[-1], kv_norm.shape[-1]
    row = lambda i: (i, 0)
    fix = lambda i: (0, 0)
    kern = functools.partial(_mla_post_kernel, q_rank=q_rank, kv_rank=kv_rank, heads_per_dot=4)
    return pl.pallas_call(
        kern,
        out_shape=[jax.ShapeDtypeStruct((n_tok, MLA_HEADS * MLA_QW), BF16),
                   jax.ShapeDtypeStruct((n_tok, kv_rank), F32),
                   jax.ShapeDtypeStruct((n_tok, LANES), BF16)],
        grid=(n_tok // bm,),
        in_specs=[
            pl.BlockSpec((bm, ws), row),
            pl.BlockSpec((bm, LANES), row),
            pl.BlockSpec((bm, LANES), row),
            pl.BlockSpec((1, q_rank), fix),
            pl.BlockSpec((1, kv_rank), fix),
            pl.BlockSpec(wq_a.shape, fix),
            pl.BlockSpec(wq_b.shape, fix),
        ],
        out_specs=[pl.BlockSpec((bm, MLA_HEADS * MLA_QW), row),
                   pl.BlockSpec((bm, kv_rank), row),
                   pl.BlockSpec((bm, LANES), row)],
        compiler_params=_cparams("arbitrary"),
        name="mla_q_rope",
    )(small, cos_t, sin_t, q_norm.reshape(1, q_rank).astype(F32), kv_norm.reshape(1, kv_rank).astype(F32),
      wq_a, wq_b)


def _kv_expand_kernel(c_ref, w_ref, o_ref, *, n_chunk):
    c = c_ref[...].astype(BF16)
    n = w_ref.shape[1]
    for s in range(0, n, n_chunk):
        o_ref[:, s:s + n_chunk] = jnp.dot(c, w_ref[:, s:s + n_chunk],
                                          preferred_element_type=F32).astype(o_ref.dtype)


def _kv_expand_call(ckv, wkv_b, *, bm=512):
    rows, kr = ckv.shape
    n = wkv_b.shape[1]
    return pl.pallas_call(
        functools.partial(_kv_expand_kernel, n_chunk=1024),
        out_shape=jax.ShapeDtypeStruct((rows, n), BF16),
        grid=(rows // bm,),
        in_specs=[pl.BlockSpec((bm, kr), lambda i: (i, 0)), pl.BlockSpec((kr, n), lambda i: (0, 0))],
        out_specs=pl.BlockSpec((bm, n), lambda i: (i, 0)),
        compiler_params=_cparams("arbitrary"),
        name="mla_kv_expand",
    )(ckv, wkv_b)


def _attn_kernel(q_ref, kv_ref, kpe_ref, z_ref, *rest, hg, scale, kc):
    o_ref, kcat_scr, vext_scr = rest[-3], rest[-2], rest[-1]
    c2 = scale * math.log2(math.e)
    del kc

    @pl.when(pl.program_id(2) == 0)
    def _():
        ones = jnp.ones((kv_ref.shape[0], LANES), BF16)
        for j in range(hg):
            kcat_scr[j, :, 0:LANES] = kv_ref[:, j * 2 * LANES:j * 2 * LANES + LANES]
            kcat_scr[j, :, LANES:2 * LANES] = kpe_ref[...]
            vext_scr[j, :, 0:LANES] = kv_ref[:, j * 2 * LANES + LANES:(j + 1) * 2 * LANES]
            vext_scr[j, :, LANES:2 * LANES] = ones

    for j in range(hg):
        q = q_ref[:, j * MLA_QW:(j + 1) * MLA_QW]
        s = lax.dot_general(q, kcat_scr[j], (((1,), (1,)), ((), ())), preferred_element_type=F32)
        e = jnp.exp2((s - jnp.max(s, axis=-1, keepdims=True)) * c2)
        pv = jnp.dot(e.astype(BF16), vext_scr[j], preferred_element_type=F32)
        zs = slice(j * MLA_V, (j + 1) * MLA_V)
        o = pv[:, 0:MLA_V] / pv[:, MLA_V:2 * MLA_V]
        o_ref[:, zs] = (o * _silu(z_ref[:, zs].astype(F32))).astype(o_ref.dtype)


def _attn_call(q, kv, kpe, z, prev, *, q_row0, n_seq, q_len, k_len, hg, qb):
    n_tok = q.shape[0]
    width = MLA_HEADS * MLA_V
    nqb = q_len // qb
    qb0 = q_row0 // qb
    assert q_row0 % qb == 0 and q_len % qb == 0
    scale = float((MLA_NOPE + MLA_ROPE) ** -0.5)
    qrow = lambda b, g, i: (qb0 + b * nqb + i, g)
    in_specs = [
        pl.BlockSpec((qb, hg * MLA_QW), qrow),
        pl.BlockSpec((k_len, hg * 2 * LANES), lambda b, g, i: (b, g)),
        pl.BlockSpec((k_len, LANES), lambda b, g, i: (b, 0)),
        pl.BlockSpec((qb, hg * MLA_V), qrow),
    ]
    args = [q, kv, kpe, z]
    aliases = {}
    if prev is not None:
        in_specs.append(pl.BlockSpec(memory_space=pl.ANY))
        args.append(prev)
        aliases = {4: 0}
    return pl.pallas_call(
        functools.partial(_attn_kernel, hg=hg, scale=scale, kc=512),
        out_shape=jax.ShapeDtypeStruct((n_tok, width), BF16),
        grid=(n_seq, MLA_HEADS // hg, nqb),
        in_specs=in_specs,
        out_specs=pl.BlockSpec((qb, hg * MLA_V), qrow),
        scratch_shapes=[pltpu.VMEM((hg, k_len, MLA_QW), BF16),
                        pltpu.VMEM((hg, k_len, 2 * MLA_V), BF16)],
        input_output_aliases=aliases,
        compiler_params=_cparams("arbitrary", "arbitrary", "arbitrary"),
        name="mla_attention",
    )(*args)


def _mla_weights(w_in, wq_b):
    q_rank = wq_b.shape[0]
    kv_rank = w_in.shape[1] - q_rank - MLA_ROPE - MLA_HEADS * MLA_V
    d = w_in.shape[0]
    c_kpe = q_rank + kv_rank
    zpad = jnp.zeros((d, LANES - MLA_ROPE), w_in.dtype)
    kpe_w = w_in[:, c_kpe:c_kpe + MLA_ROPE]
    w_small = jnp.concatenate([w_in[:, :c_kpe], kpe_w, zpad,
                               kpe_w[:, _ROT_SRC] * _ROT_SIGN, zpad], axis=1)
    w_z = w_in[:, c_kpe + MLA_ROPE:]
    hd = MLA_NOPE + MLA_ROPE
    wq3 = wq_b.reshape(q_rank, MLA_HEADS, hd)
    pe = wq3[:, :, MLA_NOPE:]
    z3 = jnp.zeros((q_rank, MLA_HEADS, LANES - MLA_ROPE), wq_b.dtype)
    wq_a = jnp.concatenate([wq3, z3], axis=-1).reshape(q_rank, MLA_HEADS * MLA_QW)
    wq_r = jnp.concatenate([pe[:, :, _ROT_SRC] * _ROT_SIGN, z3], axis=-1).reshape(q_rank, MLA_HEADS * LANES)
    return w_small.astype(BF16), w_z.astype(BF16), wq_a.astype(BF16), wq_r.astype(BF16)


def kernel(x_prompt, x_sample, state_s5_re, state_s5_im, cache_ckv, cache_kpe, c, c_ctx, norm_g, ada_w, ada_b, final_norm_g, s5_w_in, s5_lam_re, s5_lam_im, s5_log_step, s5_b_re, s5_b_im, s5_c_re, s5_c_im, s5_d, s5_glu_w, s5_glu_b, s5_w_out, pool_w_in, pool_w, pool_scale, pool_w_out, mla_w_in, mla_q_norm, mla_wq_b, mla_kv_norm, mla_wkv_b, mla_w_out):
    n_pseq, p_len, d = x_prompt.shape
    n_sseq, s_len, _ = x_sample.shape
    depth = norm_g.shape[0]
    n_prompt = n_pseq * p_len
    bm = 512
    geo = dict(n_prompt=n_prompt, sample_len=s_len, bm=bm)

    x = jnp.concatenate([x_prompt.reshape(n_prompt, d), x_sample.reshape(n_sseq * s_len, d)], axis=0)
    conds = jnp.concatenate([c_ctx[None, :], c, jnp.zeros((SUBLANES - 1 - n_sseq, d), F32)], axis=0)
    mods = _ada_call(conds.astype(F32), ada_w, ada_b)
    mods = mods.reshape(depth, SUBLANES, 1, 3 * d)
    s5_mats = _s5_prep_all(s5_lam_re, s5_lam_im, s5_log_step, s5_b_re, s5_b_im, s5_c_re, s5_c_im)

    new_re, new_im, new_ckv, new_kpe = [], [], [], []
    for layer in range(depth):
        kind, j = layer % N_MIXERS, layer // N_MIXERS
        last = layer == depth - 1
        ml = mods[layer]
        if kind == 0:
            width = s5_w_in.shape[2] // 2
            w = s5_w_in[j].astype(BF16)
            u3, z = _inproj_call(x, ml, norm_g[layer], [(w, 0, width), (w, 1, width)], [F32, BF16],
                                 lane_blocked=(0,), **geo)
            act, f_re, f_im = _s5_mix(u3, z, j, s5_mats, s5_d[j], s5_glu_w[j].astype(BF16), s5_glu_b[j],
                                      state_s5_re[:, j], state_s5_im[:, j], n_prompt_seq=n_pseq,
                                      prompt_len=p_len, n_sample_seq=n_sseq, sample_len=s_len, bm=bm)
            new_re.append(f_re)
            new_im.append(f_im)
            w_out = s5_w_out[j]
        elif kind == 1:
            width = pool_w_in.shape[2] // 2
            w = pool_w_in[j].astype(BF16)
            u, z = _inproj_call(x, ml, norm_g[layer], [(w, 0, width), (w, 1, width)], [F32, BF16], **geo)
            act = _pool_call(u, z, pool_w[j].astype(BF16), pool_scale[j], n_prompt=n_prompt,
                             prompt_len=p_len, sample_len=s_len)
            w_out = pool_w_out[j]
        else:
            q_rank, kv_rank = mla_q_norm.shape[-1], mla_kv_norm.shape[-1]
            w_small, w_z, wq_a, wq_r = _mla_weights(mla_w_in[j], mla_wq_b[j])
            small, z = _inproj_call(x, ml, norm_g[layer], [w_small, w_z], [F32, BF16], **geo)
            cos_t, sin_t = _rope_tables(n_prompt, n_sseq, s_len)
            q, ckv_n, kpe_k = _mla_post_call(small, cos_t, sin_t, mla_q_norm[j], mla_kv_norm[j], wq_a, wq_r,
                                             bm=bm)
            wkv = mla_wkv_b[j].astype(BF16)
            past = cache_ckv.shape[2]
            k_len = past + s_len
            ckv_s = jnp.concatenate([cache_ckv[:, j].astype(F32), ckv_n[n_prompt:].reshape(n_sseq, s_len, kv_rank)],
                                    axis=1).reshape(n_sseq * k_len, kv_rank)
            kpe_cache = jnp.concatenate([cache_kpe[:, j].astype(BF16),
                                         jnp.zeros((n_sseq, past, LANES - MLA_ROPE), BF16)], axis=-1)
            kpe_s = jnp.concatenate([kpe_cache, kpe_k[n_prompt:].reshape(n_sseq, s_len, LANES)],
                                    axis=1).reshape(n_sseq * k_len, LANES)
            kv_p = _kv_expand_call(ckv_n[:n_prompt], wkv, bm=bm)
            kv_s = _kv_expand_call(ckv_s, wkv, bm=bm)
            act = _attn_call(q, kv_p, kpe_k[:n_prompt], z, None, q_row0=0, n_seq=n_pseq, q_len=p_len,
                             k_len=p_len, hg=MLA_HEADS, qb=p_len)
            act = _attn_call(q, kv_s, kpe_s, z, act, q_row0=n_prompt, n_seq=n_sseq, q_len=s_len,
                             k_len=k_len, hg=4, qb=256)
            new_ckv.append(ckv_n[:n_prompt].reshape(n_pseq, p_len, kv_rank))
            c_kpe = q_rank + kv_rank
            new_kpe.append(small[:n_prompt, c_kpe:c_kpe + MLA_ROPE].reshape(n_pseq, p_len, MLA_ROPE))
            w_out = mla_w_out[j]
        x = _outproj_call(act, x, ml, w_out.astype(BF16), final_norm_g, final_norm=last, **geo)

    y_prompt = x[:n_prompt].reshape(n_pseq, p_len, d)
    y_sample = x[n_prompt:].reshape(n_sseq, s_len, d)
    return (y_prompt, y_sample, jnp.stack(new_re, axis=1), jnp.stack(new_im, axis=1),
            jnp.stack(new_ckv, axis=1), jnp.stack(new_kpe, axis=1))
```

```python
import functools
import math

import jax
import jax.numpy as jnp
import numpy as np
from jax import lax
from jax.experimental import pallas as pl
from jax.experimental.pallas import tpu as pltpu

S5_GROUP = 16
S5_CHUNK = 16
POOL_WINDOWS = (2, 4, 8, 16)
MLA_HEADS = 16
MLA_NOPE = 128
MLA_ROPE = 64
MLA_V = 128
GRID_W = 64
ROPE_THETA = 10000.0
NORM_EPS = 1e-6
N_MIXERS = 3

LANES = 128
SUBLANES = 8
VMEM_LIMIT_BYTES = 56 * 1024 * 1024

F32 = jnp.float32
BF16 = jnp.bfloat16
HIGHEST = lax.Precision.HIGHEST


def _cparams(*sem):
    return pltpu.CompilerParams(dimension_semantics=sem, vmem_limit_bytes=VMEM_LIMIT_BYTES)


def _sigmoid(x):
    return 1.0 / (1.0 + jnp.exp(-x))


def _silu(x):
    return x * _sigmoid(x)


def _gelu_tanh(x):
    c = math.sqrt(2.0 / math.pi)
    return 0.5 * x * (1.0 + jnp.tanh(c * (x + 0.044715 * (x * x * x))))


def _ada_kernel(c_ref, w_ref, b_ref, o_ref):
    a = _silu(c_ref[...])
    o_ref[...] = jnp.dot(a, w_ref[...], preferred_element_type=F32, precision=HIGHEST) + b_ref[...]


def _ada_call(conds, ada_w, ada_b):
    depth, d, d3 = ada_w.shape
    c8 = conds.shape[0]
    tn = 512
    return pl.pallas_call(
        _ada_kernel,
        out_shape=jax.ShapeDtypeStruct((depth, c8, d3), F32),
        grid=(depth, d3 // tn),
        in_specs=[
            pl.BlockSpec((c8, d), lambda l, n: (0, 0)),
            pl.BlockSpec((None, d, tn), lambda l, n: (l, 0, n)),
            pl.BlockSpec((None, 1, tn), lambda l, n: (l, 0, n)),
        ],
        out_specs=pl.BlockSpec((None, c8, tn), lambda l, n: (l, 0, n)),
        compiler_params=_cparams("arbitrary", "arbitrary"),
        name="ada_mod",
    )(conds, ada_w, ada_b.reshape(depth, 1, d3))


def _cond_of_block(i, n_prompt_blocks, blocks_per_sample):
    return jnp.where(i < n_prompt_blocks, 0, 1 + (i - n_prompt_blocks) // blocks_per_sample)


def _modulated(x, mod_ref, g_ref, d):
    ms = jnp.mean(x * x, axis=-1, keepdims=True)
    y = x * lax.rsqrt(ms + NORM_EPS) * g_ref[...]
    shift = mod_ref[:, 0:d]
    scale = mod_ref[:, d:2 * d]
    return (y * (1.0 + scale) + shift).astype(BF16)


def _inproj_kernel(x_ref, mod_ref, g_ref, *rest, d, n_chunk, chunk_rows):
    if chunk_rows:
        rest, xc_ref = rest[:-1], rest[-1]
    n_out = len(rest) // 2
    w_refs, o_refs = rest[:n_out], rest[n_out:]
    h = _modulated(x_ref[...], mod_ref, g_ref, d)
    for w_ref, o_ref in zip(w_refs, o_refs):
        n = w_ref.shape[1]
        for c in range(0, n, n_chunk):
            e = min(c + n_chunk, n)
            r = jnp.dot(h, w_ref[:, c:e], preferred_element_type=F32).astype(o_ref.dtype)
            if len(o_ref.shape) == 3:
                for lb in range((e - c) // LANES):
                    o_ref[c // LANES + lb] = r[:, lb * LANES:(lb + 1) * LANES]
            else:
                o_ref[:, c:e] = r
    if chunk_rows:
        _s5_to_chunks_kernel(o_refs[0], xc_ref, rows=chunk_rows)


def _inproj_call(x, mods_l, norm_g, weights, out_dtypes, *, n_prompt, sample_len, bm=512,
                 lane_blocked=(), s5_chunks=False):
    n_tok, d = x.shape
    npb, bps = n_prompt // bm, sample_len // bm
    cond = functools.partial(_cond_of_block, n_prompt_blocks=npb, blocks_per_sample=bps)
    weights = [w if isinstance(w, tuple) else (w, 0, w.shape[1]) for w in weights]
    in_specs = [
        pl.BlockSpec((bm, d), lambda i: (i, 0)),
        pl.BlockSpec((None, 1, 3 * d), lambda i: (cond(i), 0, 0)),
        pl.BlockSpec((1, d), lambda i: (0, 0)),
    ] + [pl.BlockSpec((d, n), functools.partial(lambda i, blk: (0, blk), blk=blk)) for _, blk, n in weights]
    out_specs, out_shape = [], []
    for k, ((_, _, n), dt) in enumerate(zip(weights, out_dtypes)):
        if k in lane_blocked:
            out_specs.append(pl.BlockSpec((n // LANES, bm, LANES), lambda i: (0, i, 0)))
            out_shape.append(jax.ShapeDtypeStruct((n // LANES, n_tok, LANES), dt))
        else:
            out_specs.append(pl.BlockSpec((bm, n), lambda i: (i, 0)))
            out_shape.append(jax.ShapeDtypeStruct((n_tok, n), dt))
    chunk_rows = bm // S5_CHUNK if s5_chunks else 0
    if s5_chunks:
        assert 0 in lane_blocked
        n_groups = weights[0][2] // S5_GROUP
        out_specs.append(pl.BlockSpec((n_groups, chunk_rows, 2 * LANES), lambda i: (0, i, 0)))
        out_shape.append(jax.ShapeDtypeStruct((n_groups, n_tok // S5_CHUNK, 2 * LANES), BF16))
    return pl.pallas_call(
        functools.partial(_inproj_kernel, d=d, n_chunk=512, chunk_rows=chunk_rows),
        out_shape=out_shape,
        grid=(n_tok // bm,),
        in_specs=in_specs,
        out_specs=out_specs,
        compiler_params=_cparams("arbitrary"),
        name="norm_mod_inproj",
    )(x, mods_l, norm_g.reshape(1, d), *[w for w, _, _ in weights])


def _outproj_kernel(*refs, d, final_norm, n_prompt_blocks):
    x_ref, mod_ref, w_ref, fg_ref, o_ref = refs[-5:]

    def finish(a_ref):
        y = jnp.dot(a_ref[...], w_ref[...], preferred_element_type=F32)
        gate = mod_ref[:, 2 * d:3 * d]
        xn = x_ref[...] + gate * y
        if final_norm:
            ms = jnp.mean(xn * xn, axis=-1, keepdims=True)
            xn = xn * lax.rsqrt(ms + NORM_EPS) * fg_ref[...]
        o_ref[...] = xn

    if len(refs) == 6:
        finish(refs[0])
    else:
        pl.when(pl.program_id(0) < n_prompt_blocks)(functools.partial(finish, refs[0]))
        pl.when(pl.program_id(0) >= n_prompt_blocks)(functools.partial(finish, refs[1]))


def _outproj_call(act, x, mods_l, w_out, final_g, *, n_prompt, sample_len, final_norm, bm=512):
    n_tok, d = x.shape
    npb, bps = n_prompt // bm, sample_len // bm
    cond = functools.partial(_cond_of_block, n_prompt_blocks=npb, blocks_per_sample=bps)
    if isinstance(act, tuple):
        k = act[0].shape[1]
        acts = list(act)
        act_specs = [pl.BlockSpec((bm, k), lambda i: (jnp.minimum(i, npb - 1), 0)),
                     pl.BlockSpec((bm, k), lambda i: (jnp.maximum(i - npb, 0), 0))]
    else:
        k = act.shape[1]
        acts = [act]
        act_specs = [pl.BlockSpec((bm, k), lambda i: (i, 0))]
    return pl.pallas_call(
        functools.partial(_outproj_kernel, d=d, final_norm=final_norm, n_prompt_blocks=npb),
        out_shape=jax.ShapeDtypeStruct((n_tok, d), F32),
        grid=(n_tok // bm,),
        in_specs=act_specs + [
            pl.BlockSpec((bm, d), lambda i: (i, 0)),
            pl.BlockSpec((None, 1, 3 * d), lambda i: (cond(i), 0, 0)),
            pl.BlockSpec((k, d), lambda i: (0, 0)),
            pl.BlockSpec((1, d), lambda i: (0, 0)),
        ],
        out_specs=pl.BlockSpec((bm, d), lambda i: (i, 0)),
        compiler_params=_cparams("arbitrary"),
        name="outproj_residual",
    )(*acts, x, mods_l, w_out, final_g.reshape(1, d))


def _s5_time_of_lane_block():
    pos = np.arange(S5_CHUNK)
    half, blk = pos // 8, pos % 8
    g8 = np.arange(8)[:, None]
    return 8 * half[None, :] + (blk[None, :] - g8) % 8


def _s5_tables(lam_re, lam_im, log_step, b_re, b_im, c_re, c_im):
    t_chunk = S5_CHUNK
    n_groups, n_state = lam_re.shape[1], lam_re.shape[2]
    n_oct = n_groups // 8
    lam = lax.complex(lam_re.astype(F32), lam_im.astype(F32))
    step = jnp.exp(log_step.astype(F32))[..., None]
    lam_bar = jnp.exp(lam * step)
    b_bar = ((lam_bar - 1.0) / lam)[..., None] * lax.complex(b_re.astype(F32), b_im.astype(F32))
    c_mat = lax.complex(c_re.astype(F32), c_im.astype(F32))
    ks = jnp.arange(t_chunk + 1, dtype=F32)[:, None, None, None]
    pw = jnp.exp(ks * (lam * step)[None])

    cb = jnp.einsum('dgop,ldgp,dgpi->ldgoi', c_mat, pw[:t_chunk], b_bar, precision=HIGHEST).real
    vcol = jnp.concatenate([cb[:0:-1, 0], (cb[0, 0] + cb[0, 1])[None], cb[1:, 1]], axis=0)
    vcol = vcol.transpose(1, 0, 3, 2).reshape(n_groups, (2 * t_chunk - 1) * S5_GROUP, S5_GROUP)
    vrep = jnp.tile(vcol, (1, 1, LANES // S5_GROUP))

    tl = _s5_time_of_lane_block()
    pw_ri = jnp.stack([pw.real, pw.imag]).reshape(2, t_chunk + 1, 2, n_oct, 8, n_state)
    m_idx = np.arange(t_chunk + 1)[None, None, :]

    def power_table(exponent, direction):
        sel = (exponent[:, :, None] == m_idx).astype(np.float32)
        tab = jnp.einsum('kxm,rmakp->rakxp', sel, pw_ri[:, :, direction], precision=HIGHEST)
        return tab.reshape(2, n_groups, t_chunk, n_state)

    def both(fwd, bwd):
        m = jnp.concatenate([fwd, bwd], axis=-1)
        return jnp.stack([m.real, m.imag])

    tin = jnp.concatenate([power_table(t_chunk - 1 - tl, 0), power_table(tl, 1)], axis=-1)
    tout = jnp.concatenate([power_table(tl + 1, 0), power_table(t_chunk - tl, 1)], axis=-1)
    bt = both(b_bar[0].transpose(0, 2, 1), b_bar[1].transpose(0, 2, 1))
    ct = both(c_mat[0], c_mat[1])
    lam_rows = both(pw[t_chunk, 0][:, None], pw[t_chunk, 1][:, None])[:, :, 0]
    return vrep, tin, tout, bt, ct, lam_rows


def _s5_kmat_kernel(v_ref, tin_ref, tout_ref, bt_ref, ct_ref, k_ref, pin_ref, pot_ref):
    masks = _lane_block_masks()
    for g8 in range(8):
        br, bi = bt_ref[0, g8], bt_ref[1, g8]
        cr, ci = ct_ref[0, g8], ct_ref[1, g8]
        for pos in range(S5_CHUNK):
            rows = slice(pos * S5_GROUP, (pos + 1) * S5_GROUP)
            tr, ti = tin_ref[0, g8, pos:pos + 1, :], tin_ref[1, g8, pos:pos + 1, :]
            pin_ref[g8, rows, 0:LANES] = (tr * br - ti * bi).astype(BF16)
            pin_ref[g8, rows, LANES:2 * LANES] = (tr * bi + ti * br).astype(BF16)
            tr, ti = tout_ref[0, g8, pos:pos + 1, :], tout_ref[1, g8, pos:pos + 1, :]
            pot_ref[g8, rows, 0:LANES] = (tr * cr - ti * ci).astype(BF16)
            pot_ref[g8, rows, LANES:2 * LANES] = (-(tr * ci + ti * cr)).astype(BF16)
        for pos in range(S5_CHUNK):
            sigma = 8 * (pos // 8) + (pos % 8 - g8) % 8
            rows = slice(pos * S5_GROUP, (pos + 1) * S5_GROUP)
            for half in range(2):
                acc = None
                for blk in range(8):
                    tau = 8 * half + (blk - g8) % 8
                    m = S5_CHUNK - 1 - tau + sigma
                    src = v_ref[g8, m * S5_GROUP:(m + 1) * S5_GROUP, :]
                    acc = src if acc is None else jnp.where(masks[blk], src, acc)
                k_ref[g8, rows, half * LANES:(half + 1) * LANES] = acc.astype(BF16)


def _s5_kmat_call(vrep, tin, tout, bt, ct):
    n_groups = vrep.shape[0]
    lag_spec = pl.BlockSpec((8,) + vrep.shape[1:], lambda i: (i, 0, 0))
    tab_spec = pl.BlockSpec((2, 8, S5_GROUP, LANES), lambda i: (0, i, 0, 0))
    mat = jax.ShapeDtypeStruct((n_groups, 2 * LANES, 2 * LANES), BF16)
    mat_spec = pl.BlockSpec((8, 2 * LANES, 2 * LANES), lambda i: (i, 0, 0))
    return pl.pallas_call(
        _s5_kmat_kernel,
        out_shape=[mat, mat, mat],
        grid=(n_groups // 8,),
        in_specs=[lag_spec, tab_spec, tab_spec, tab_spec, tab_spec],
        out_specs=[mat_spec, mat_spec, mat_spec],
        compiler_params=_cparams("arbitrary"),
        name="s5_kmat",
    )(vrep, tin, tout, bt, ct)


def _lane_block_masks():
    blk = lax.broadcasted_iota(jnp.int32, (1, LANES), 1) // S5_GROUP
    return [blk == b for b in range(8)]


def _s5_to_chunks_kernel(u_ref, x_ref, *, rows):
    masks = _lane_block_masks()
    for o in range(u_ref.shape[0]):
        rolled = []
        for t in range(S5_CHUNK):
            v = u_ref[o, pl.ds(t, rows, stride=S5_CHUNK), :]
            s = (t % 8) * S5_GROUP
            rolled.append(pltpu.roll(v, s, 1) if s else v)
        for g8 in range(8):
            for half in range(2):
                acc = rolled[8 * half + (0 - g8) % 8]
                for blk in range(1, 8):
                    acc = jnp.where(masks[blk], rolled[8 * half + (blk - g8) % 8], acc)
                x_ref[o * 8 + g8, :, half * LANES:(half + 1) * LANES] = acc.astype(BF16)


def _s5_from_chunks_kernel(y_ref, u_ref, d_ref, o_ref, *, rows):
    masks = _lane_block_masks()
    for o in range(u_ref.shape[0]):
        d_vec = d_ref[:, o * LANES:(o + 1) * LANES]
        for t in range(S5_CHUNK):
            half, t8 = t // 8, t % 8
            acc = y_ref[o * 8 + (0 - t8) % 8, :, half * LANES:(half + 1) * LANES]
            for blk in range(1, 8):
                src = y_ref[o * 8 + (blk - t8) % 8, :, half * LANES:(half + 1) * LANES]
                acc = jnp.where(masks[blk], src, acc)
            s = ((8 - t8) % 8) * S5_GROUP
            nat = pltpu.roll(acc, s, 1) if s else acc
            tok = pl.ds(t, rows, stride=S5_CHUNK)
            o_ref[o, tok, :] = _gelu_tanh(nat + d_vec * u_ref[o, tok, :])


def _s5_from_chunks_call(yc, u3, d_skip, *, rows=32):
    n_blk, n_tok, _ = u3.shape
    n_groups, n_rows, _ = yc.shape
    tok_spec = pl.BlockSpec((n_blk, rows * S5_CHUNK, LANES), lambda i: (0, i, 0))
    return pl.pallas_call(
        functools.partial(_s5_from_chunks_kernel, rows=rows),
        out_shape=jax.ShapeDtypeStruct(u3.shape, F32),
        grid=(n_rows // rows,),
        in_specs=[pl.BlockSpec((n_groups, rows, 2 * LANES), lambda i: (0, i, 0)),
                  tok_spec,
                  pl.BlockSpec((1, n_blk * LANES), lambda i: (0, 0))],
        out_specs=tok_spec,
        compiler_params=_cparams("arbitrary"),
        name="s5_from_chunks",
    )(yc, u3, d_skip.reshape(1, n_blk * LANES).astype(F32))


def _s5_chunk_kernel(x_ref, kt_ref, pin_ref, po_ref, lam_ref, h0r_ref, h0i_ref,
                     y_ref, fr_ref, fi_ref, r_scr, st_scr, *, segments, seq_block):
    gb = SUBLANES
    rows = x_ref.shape[1]
    lane = lax.broadcasted_iota(jnp.int32, (1, LANES), 1)
    fwd_lanes = lane < (LANES // 2)
    for g in range(gb):
        x = x_ref[g]
        y_ref[g] = jnp.dot(x, kt_ref[g], preferred_element_type=F32)
        r = jnp.dot(x, pin_ref[g], preferred_element_type=F32)
        of_group = pl.ds(g, rows, stride=gb)
        r_scr[0, of_group, :] = r[:, 0:LANES]
        r_scr[1, of_group, :] = r[:, LANES:2 * LANES]
    ar, ai = lam_ref[0], lam_ref[1]
    for row0, n_seq, n_chunks, from_input, to_output in segments:
        for b0 in range(0, n_seq, seq_block):
            nb = min(seq_block, n_seq - b0)

            def step(i, carry, row0=row0, n_chunks=n_chunks, b0=b0, nb=nb):
                out = []
                for k in range(nb):
                    base = row0 + (b0 + k) * n_chunks
                    at_f = pl.ds(pl.multiple_of((base + i) * gb, gb), gb)
                    at_b = pl.ds(pl.multiple_of((base + (n_chunks - 1) - i) * gb, gb), gb)
                    s_re, s_im = carry[k]
                    half = LANES // 2
                    st_scr[0, at_f, 0:half] = s_re[:, 0:half]
                    st_scr[0, at_b, half:LANES] = s_re[:, half:LANES]
                    st_scr[1, at_f, 0:half] = s_im[:, 0:half]
                    st_scr[1, at_b, half:LANES] = s_im[:, half:LANES]
                    v_re = jnp.where(fwd_lanes, r_scr[0, at_f, :], r_scr[0, at_b, :])
                    v_im = jnp.where(fwd_lanes, r_scr[1, at_f, :], r_scr[1, at_b, :])
                    out.append((ar * s_re - ai * s_im + v_re, ar * s_im + ai * s_re + v_im))
                return tuple(out)

            if from_input:
                init = tuple((h0r_ref[b0 + k], h0i_ref[b0 + k]) for k in range(nb))
            else:
                init = tuple((jnp.zeros((gb, LANES), F32),) * 2 for _ in range(nb))
            fin = lax.fori_loop(0, n_chunks, step, init, unroll=2)
            if to_output:
                for k in range(nb):
                    fr_ref[b0 + k] = fin[k][0]
                    fi_ref[b0 + k] = fin[k][1]
    for g in range(gb):
        of_group = pl.ds(g, rows, stride=gb)
        st = jnp.concatenate([st_scr[cb, of_group, :] for cb in range(2)], axis=1).astype(BF16)
        y_ref[g] = y_ref[g] + lax.dot_general(st, po_ref[g], (((1,), (1,)), ((), ())),
                                              preferred_element_type=F32)


def _s5_chunk_call(xc, kt, pin, pout, lam_rows, h0_re, h0_im, *, layer, segments, n_final):
    n_groups, rows, _ = xc.shape
    gb = SUBLANES
    s_in = h0_re.shape[0]
    kern = functools.partial(_s5_chunk_kernel, segments=segments, seq_block=8)
    g3 = lambda i: (i, 0, 0)
    blk0 = layer * (n_groups // gb)
    p3 = lambda i: (i + blk0, 0, 0)
    mid = lambda i: (0, i, 0)
    return pl.pallas_call(
        kern,
        out_shape=[jax.ShapeDtypeStruct((n_groups, rows, 2 * LANES), F32),
                   jax.ShapeDtypeStruct((n_final, n_groups, LANES), F32),
                   jax.ShapeDtypeStruct((n_final, n_groups, LANES), F32)],
        grid=(n_groups // gb,),
        in_specs=[
            pl.BlockSpec((gb, rows, 2 * LANES), g3),
            pl.BlockSpec((gb, 2 * LANES, 2 * LANES), p3),
            pl.BlockSpec((gb, 2 * LANES, 2 * LANES), p3),
            pl.BlockSpec((gb, 2 * LANES, 2 * LANES), p3),
            pl.BlockSpec((2, gb, LANES), lambda i: (0, i + blk0, 0)),
            pl.BlockSpec((s_in, gb, LANES), mid),
            pl.BlockSpec((s_in, gb, LANES), mid),
        ],
        out_specs=[pl.BlockSpec((gb, rows, 2 * LANES), g3),
                   pl.BlockSpec((n_final, gb, LANES), mid),
                   pl.BlockSpec((n_final, gb, LANES), mid)],
        scratch_shapes=[pltpu.VMEM((2, rows * gb, LANES), F32),
                        pltpu.VMEM((2, rows * gb, LANES), F32)],
        compiler_params=_cparams("arbitrary"),
        name="s5_chunk_scan",
    )(xc, kt, pin, pout, lam_rows, h0_re, h0_im)


def _glu_kernel(y_ref, z_ref, w_ref, b_ref, o_ref, *, n_chunk):
    n_blk = y_ref.shape[0]
    yb = jnp.concatenate([y_ref[o].astype(BF16) for o in range(n_blk)], axis=1)
    per = n_chunk // LANES
    for c in range(0, n_blk, per):
        sl = slice(c * LANES, (c + per) * LANES)
        gate = _sigmoid(jnp.dot(yb, w_ref[:, sl], preferred_element_type=F32) + b_ref[:, sl])
        y = jnp.concatenate([y_ref[c + k] for k in range(per)], axis=1)
        o_ref[:, sl] = (y * gate * _silu(z_ref[:, sl].astype(F32))).astype(o_ref.dtype)


def _glu_call(y3, z, glu_w, glu_b, *, bm=512):
    n_blk, n_tok, _ = y3.shape
    width = n_blk * LANES
    return pl.pallas_call(
        functools.partial(_glu_kernel, n_chunk=min(512, width)),
        out_shape=jax.ShapeDtypeStruct((n_tok, width), BF16),
        grid=(n_tok // bm,),
        in_specs=[
            pl.BlockSpec((n_blk, bm, LANES), lambda i: (0, i, 0)),
            pl.BlockSpec((bm, width), lambda i: (i, 0)),
            pl.BlockSpec((width, width), lambda i: (0, 0)),
            pl.BlockSpec((1, width), lambda i: (0, 0)),
        ],
        out_specs=pl.BlockSpec((bm, width), lambda i: (i, 0)),
        compiler_params=_cparams("arbitrary"),
        name="s5_glu_gate",
    )(y3, z, glu_w, glu_b.reshape(1, width).astype(F32))


def _s5_prep_all(lam_re, lam_im, log_step, b_re, b_im, c_re, c_im):
    tabs = jax.vmap(_s5_tables)(lam_re, lam_im, log_step, b_re, b_im, c_re, c_im)
    vrep = tabs[0].reshape((-1,) + tabs[0].shape[2:])
    tin, tout, bt, ct, lam_rows = [jnp.moveaxis(t, 0, 1).reshape((2, -1) + t.shape[3:]) for t in tabs[1:]]
    kt, pin, pot = _s5_kmat_call(vrep, tin, tout, bt, ct)
    return kt, pin, pot, lam_rows


def _s5_mix(u3, xc, z, layer, mats, d_skip, glu_w, glu_b, st_re, st_im, *, n_prompt_seq, prompt_len,
            n_sample_seq, sample_len, bm):
    kt, pin, pout, lam_rows = mats
    n_state = LANES // 2
    pc, sc = prompt_len // S5_CHUNK, sample_len // S5_CHUNK

    def state_rows(s):
        return jnp.concatenate([s[:, 0], s[:, 1]], axis=-1).astype(F32)

    segments = ((0, n_prompt_seq, pc, False, True), (n_prompt_seq * pc, n_sample_seq, sc, True, False))
    yc, fr, fi = _s5_chunk_call(xc, kt, pin, pout, lam_rows, state_rows(st_re), state_rows(st_im),
                                layer=layer, segments=segments, n_final=n_prompt_seq)
    y3 = _s5_from_chunks_call(yc, u3, d_skip)
    act = _glu_call(y3, z, glu_w, glu_b, bm=bm)

    def unpack(f):
        return jnp.stack([f[:, :, :n_state], f[:, :, n_state:]], axis=1)

    return act, unpack(fr), unpack(fi)


def _pool_kernel(u_ref, z_ref, w_ref, s_ref, o_ref, *, n_prompt_blocks, prompt_len, sample_len):
    rows = u_ref.shape[0]
    seq_len = jnp.where(pl.program_id(0) < n_prompt_blocks, prompt_len, sample_len)
    t = lax.broadcasted_iota(jnp.int32, (rows, 1), 0) & (seq_len - 1)

    def later(x, k):
        return jnp.where(t + k < seq_len, pltpu.roll(x, rows - k, 0), 0.0)

    def earlier(x, k):
        return jnp.where(t >= k, pltpu.roll(x, k, 0), 0.0)

    def body(win):
        lo = win // 2
        u = u_ref[...]
        fwd = u
        bwd = earlier(u, 1)
        s = 1
        while s < lo:
            fwd = fwd + later(fwd, s)
            bwd = bwd + earlier(bwd, s)
            s *= 2
        cnt = jnp.minimum(t - lo + win, seq_len) - jnp.maximum(t - lo, 0)
        p = (fwd + bwd) / cnt.astype(F32) - u
        m = jnp.dot(p.astype(BF16), w_ref[...], preferred_element_type=F32) * s_ref[...]
        o_ref[...] = (m * _silu(z_ref[...].astype(F32))).astype(o_ref.dtype)

    for gi, win in enumerate(POOL_WINDOWS):
        pl.when(pl.program_id(1) == gi)(functools.partial(body, win))


def _pool_call(u, z, pool_w, pool_scale, *, n_prompt, prompt_len, sample_len, rows=2048):
    n_tok, width = u.shape
    n_groups = len(POOL_WINDOWS)
    cg = width // n_groups
    assert prompt_len & (prompt_len - 1) == 0 and sample_len & (sample_len - 1) == 0
    assert rows % prompt_len == 0 and rows % sample_len == 0 and n_prompt % rows == 0
    kern = functools.partial(_pool_kernel, n_prompt_blocks=n_prompt // rows, prompt_len=prompt_len,
                             sample_len=sample_len)
    return pl.pallas_call(
        kern,
        out_shape=jax.ShapeDtypeStruct((n_tok, width), BF16),
        grid=(n_tok // rows, n_groups),
        in_specs=[
            pl.BlockSpec((rows, cg), lambda i, g: (i, g)),
            pl.BlockSpec((rows, cg), lambda i, g: (i, g)),
            pl.BlockSpec((None, cg, cg), lambda i, g: (g, 0, 0)),
            pl.BlockSpec((1, cg), lambda i, g: (0, g)),
        ],
        out_specs=pl.BlockSpec((rows, cg), lambda i, g: (i, g)),
        compiler_params=_cparams("arbitrary", "arbitrary"),
        name="pool_mix",
    )(u, z, pool_w, pool_scale.reshape(1, width).astype(F32))


MLA_QW = 2 * LANES

_ROT_SRC = np.concatenate([np.arange(16, 32), np.arange(0, 16), np.arange(48, 64), np.arange(32, 48)])
_ROT_SIGN = np.concatenate([-np.ones(16), np.ones(16), -np.ones(16), np.ones(16)]).astype(np.float32)


def _rope_tables(n_prompt, n_sample_seq, sample_len):
    half = MLA_ROPE // 4
    tok = jnp.arange(sample_len)
    row = (tok // GRID_W).astype(F32)
    col = (tok % GRID_W).astype(F32)
    inv = ROPE_THETA ** (-jnp.arange(half, dtype=F32) / half)
    a_row, a_col = row[:, None] * inv, col[:, None] * inv
    cos = jnp.concatenate([jnp.cos(a_row), jnp.cos(a_row), jnp.cos(a_col), jnp.cos(a_col)], axis=-1)
    sin = jnp.concatenate([jnp.sin(a_row), jnp.sin(a_row), jnp.sin(a_col), jnp.sin(a_col)], axis=-1)
    pad = jnp.zeros((sample_len, LANES - MLA_ROPE), F32)
    cos_s = jnp.tile(jnp.concatenate([cos, pad], axis=-1), (n_sample_seq, 1))
    sin_s = jnp.tile(jnp.concatenate([sin, pad], axis=-1), (n_sample_seq, 1))
    cos_p = jnp.concatenate([jnp.ones((n_prompt, MLA_ROPE), F32), jnp.zeros((n_prompt, LANES - MLA_ROPE), F32)], -1)
    return jnp.concatenate([cos_p, cos_s]), jnp.concatenate([jnp.zeros((n_prompt, LANES), F32), sin_s])


def _rms(x, g):
    return x * lax.rsqrt(jnp.mean(x * x, axis=-1, keepdims=True) + NORM_EPS) * g


def _mla_post_kernel(sm_ref, cos_ref, sin_ref, qn_ref, kn_ref, wa_ref, wb_ref,
                     q_ref, ckv_ref, kpe_ref, *, q_rank, kv_rank, heads_per_dot):
    cosp, sinp = cos_ref[...], sin_ref[...]
    qn = _rms(sm_ref[:, 0:q_rank], qn_ref[...]).astype(BF16)
    for h0 in range(0, MLA_HEADS, heads_per_dot):
        a = jnp.dot(qn, wa_ref[:, h0 * MLA_QW:(h0 + heads_per_dot) * MLA_QW], preferred_element_type=F32)
        b = jnp.dot(qn, wb_ref[:, h0 * LANES:(h0 + heads_per_dot) * LANES], preferred_element_type=F32)
        for j in range(heads_per_dot):
            h = h0 + j
            q_ref[:, h * MLA_QW:h * MLA_QW + LANES] = a[:, j * MLA_QW:j * MLA_QW + LANES].astype(BF16)
            pe = a[:, j * MLA_QW + LANES:(j + 1) * MLA_QW] * cosp + b[:, j * LANES:(j + 1) * LANES] * sinp
            q_ref[:, h * MLA_QW + LANES:(h + 1) * MLA_QW] = pe.astype(BF16)
    c0 = q_rank
    ckv_ref[...] = _rms(sm_ref[:, c0:c0 + kv_rank], kn_ref[...])
    k0 = c0 + kv_rank
    kpe_ref[...] = (sm_ref[:, k0:k0 + LANES] * cosp + sm_ref[:, k0 + LANES:k0 + 2 * LANES] * sinp).astype(BF16)


def _mla_post_call(small, cos_t, sin_t, q_norm, kv_norm, wq_a, wq_b, *, bm=512):
    n_tok, ws = small.shape
    q_rank, kv_rank = q_norm.shape[-1], kv_norm.shape[-1]
    row = lambda i: (i, 0)
    fix = lambda i: (0, 0)
    kern = functools.partial(_mla_post_kernel, q_rank=q_rank, kv_rank=kv_rank, heads_per_dot=4)
    return pl.pallas_call(
        kern,
        out_shape=[jax.ShapeDtypeStruct((n_tok, MLA_HEADS * MLA_QW), BF16),
                   jax.ShapeDtypeStruct((n_tok, kv_rank), F32),
                   jax.ShapeDtypeStruct((n_tok, LANES), BF16)],
        grid=(n_tok // bm,),
        in_specs=[
            pl.BlockSpec((bm, ws), row),
            pl.BlockSpec((bm, LANES), row),
            pl.BlockSpec((bm, LANES), row),
            pl.BlockSpec((1, q_rank), fix),
            pl.BlockSpec((1, kv_rank), fix),
            pl.BlockSpec(wq_a.shape, fix),
            pl.BlockSpec(wq_b.shape, fix),
        ],
        out_specs=[pl.BlockSpec((bm, MLA_HEADS * MLA_QW), row),
                   pl.BlockSpec((bm, kv_rank), row),
                   pl.BlockSpec((bm, LANES), row)],
        compiler_params=_cparams("arbitrary"),
        name="mla_q_rope",
    )(small, cos_t, sin_t, q_norm.reshape(1, q_rank).astype(F32), kv_norm.reshape(1, kv_rank).astype(F32),
      wq_a, wq_b)


def _kv_expand_kernel(c_ref, w_ref, o_ref, *, n_chunk):
    c = c_ref[...].astype(BF16)
    n = w_ref.shape[1]
    for s in range(0, n, n_chunk):
        o_ref[:, s:s + n_chunk] = jnp.dot(c, w_ref[:, s:s + n_chunk],
                                          preferred_element_type=F32).astype(o_ref.dtype)


def _kv_expand_call(ckv, wkv_b, *, bm=512):
    rows, kr = ckv.shape
    n = wkv_b.shape[1]
    return pl.pallas_call(
        functools.partial(_kv_expand_kernel, n_chunk=1024),
        out_shape=jax.ShapeDtypeStruct((rows, n), BF16),
        grid=(rows // bm,),
        in_specs=[pl.BlockSpec((bm, kr), lambda i: (i, 0)), pl.BlockSpec((kr, n), lambda i: (0, 0))],
        out_specs=pl.BlockSpec((bm, n), lambda i: (i, 0)),
        compiler_params=_cparams("arbitrary"),
        name="mla_kv_expand",
    )(ckv, wkv_b)


def _attn_kernel(q_ref, kv_ref, kpe_ref, z_ref, o_ref, kcat_scr, vext_scr, *, hg, scale):
    c2 = scale * math.log2(math.e)

    @pl.when(pl.program_id(2) == 0)
    def _():
        ones = jnp.ones((kv_ref.shape[0], LANES), BF16)
        for j in range(hg):
            kcat_scr[j, :, 0:LANES] = kv_ref[:, j * 2 * LANES:j * 2 * LANES + LANES]
            kcat_scr[j, :, LANES:2 * LANES] = kpe_ref[...]
            vext_scr[j, :, 0:LANES] = kv_ref[:, j * 2 * LANES + LANES:(j + 1) * 2 * LANES]
            vext_scr[j, :, LANES:2 * LANES] = ones

    for j in range(hg):
        q = q_ref[:, j * MLA_QW:(j + 1) * MLA_QW]
        s = lax.dot_general(q, kcat_scr[j], (((1,), (1,)), ((), ())), preferred_element_type=F32)
        e = jnp.exp2((s - jnp.max(s, axis=-1, keepdims=True)) * c2)
        pv = jnp.dot(e.astype(BF16), vext_scr[j], preferred_element_type=F32)
        zs = slice(j * MLA_V, (j + 1) * MLA_V)
        o = pv[:, 0:MLA_V] / pv[:, MLA_V:2 * MLA_V]
        o_ref[:, zs] = (o * _silu(z_ref[:, zs].astype(F32))).astype(o_ref.dtype)


def _attn_call(q, kv, kpe, z, *, q_row0, n_seq, q_len, k_len, hg, qb):
    width = MLA_HEADS * MLA_V
    nqb = q_len // qb
    qb0 = q_row0 // qb
    assert q_row0 % qb == 0 and q_len % qb == 0
    scale = float((MLA_NOPE + MLA_ROPE) ** -0.5)
    qrow = lambda b, g, i: (qb0 + b * nqb + i, g)
    return pl.pallas_call(
        functools.partial(_attn_kernel, hg=hg, scale=scale),
        out_shape=jax.ShapeDtypeStruct((n_seq * q_len, width), BF16),
        grid=(n_seq, MLA_HEADS // hg, nqb),
        in_specs=[
            pl.BlockSpec((qb, hg * MLA_QW), qrow),
            pl.BlockSpec((k_len, hg * 2 * LANES), lambda b, g, i: (b, g)),
            pl.BlockSpec((k_len, LANES), lambda b, g, i: (b, 0)),
            pl.BlockSpec((qb, hg * MLA_V), qrow),
        ],
        out_specs=pl.BlockSpec((qb, hg * MLA_V), lambda b, g, i: (b * nqb + i, g)),
        scratch_shapes=[pltpu.VMEM((hg, k_len, MLA_QW), BF16),
                        pltpu.VMEM((hg, k_len, 2 * MLA_V), BF16)],
        compiler_params=_cparams("arbitrary", "arbitrary", "arbitrary"),
        name="mla_attention",
    )(q, kv, kpe, z)


def _mla_weights(w_in, wq_b):
    q_rank = wq_b.shape[0]
    kv_rank = w_in.shape[1] - q_rank - MLA_ROPE - MLA_HEADS * MLA_V
    d = w_in.shape[0]
    c_kpe = q_rank + kv_rank
    zpad = jnp.zeros((d, LANES - MLA_ROPE), w_in.dtype)
    kpe_w = w_in[:, c_kpe:c_kpe + MLA_ROPE]
    w_small = jnp.concatenate([w_in[:, :c_kpe], kpe_w, zpad,
                               kpe_w[:, _ROT_SRC] * _ROT_SIGN, zpad], axis=1)
    w_z = w_in[:, c_kpe + MLA_ROPE:]
    hd = MLA_NOPE + MLA_ROPE
    wq3 = wq_b.reshape(q_rank, MLA_HEADS, hd)
    pe = wq3[:, :, MLA_NOPE:]
    z3 = jnp.zeros((q_rank, MLA_HEADS, LANES - MLA_ROPE), wq_b.dtype)
    wq_a = jnp.concatenate([wq3, z3], axis=-1).reshape(q_rank, MLA_HEADS * MLA_QW)
    wq_r = jnp.concatenate([pe[:, :, _ROT_SRC] * _ROT_SIGN, z3], axis=-1).reshape(q_rank, MLA_HEADS * LANES)
    return w_small.astype(BF16), w_z.astype(BF16), wq_a.astype(BF16), wq_r.astype(BF16)


def kernel(x_prompt, x_sample, state_s5_re, state_s5_im, cache_ckv, cache_kpe, c, c_ctx, norm_g, ada_w, ada_b, final_norm_g, s5_w_in, s5_lam_re, s5_lam_im, s5_log_step, s5_b_re, s5_b_im, s5_c_re, s5_c_im, s5_d, s5_glu_w, s5_glu_b, s5_w_out, pool_w_in, pool_w, pool_scale, pool_w_out, mla_w_in, mla_q_norm, mla_wq_b, mla_kv_norm, mla_wkv_b, mla_w_out):
    n_pseq, p_len, d = x_prompt.shape
    n_sseq, s_len, _ = x_sample.shape
    depth = norm_g.shape[0]
    n_prompt = n_pseq * p_len
    bm = 512
    geo = dict(n_prompt=n_prompt, sample_len=s_len, bm=bm)

    x = jnp.concatenate([x_prompt.reshape(n_prompt, d), x_sample.reshape(n_sseq * s_len, d)], axis=0)
    conds = jnp.concatenate([c_ctx[None, :], c, jnp.zeros((SUBLANES - 1 - n_sseq, d), F32)], axis=0)
    mods = _ada_call(conds.astype(F32), ada_w, ada_b)
    mods = mods.reshape(depth, SUBLANES, 1, 3 * d)
    s5_mats = _s5_prep_all(s5_lam_re, s5_lam_im, s5_log_step, s5_b_re, s5_b_im, s5_c_re, s5_c_im)

    new_re, new_im, new_ckv, new_kpe = [], [], [], []
    for layer in range(depth):
        kind, j = layer % N_MIXERS, layer // N_MIXERS
        last = layer == depth - 1
        ml = mods[layer]
        if kind == 0:
            width = s5_w_in.shape[2] // 2
            w = s5_w_in[j].astype(BF16)
            u3, z, xc = _inproj_call(x, ml, norm_g[layer], [(w, 0, width), (w, 1, width)], [F32, BF16],
                                     lane_blocked=(0,), s5_chunks=True, **geo)
            act, f_re, f_im = _s5_mix(u3, xc, z, j, s5_mats, s5_d[j], s5_glu_w[j].astype(BF16), s5_glu_b[j],
                                      state_s5_re[:, j], state_s5_im[:, j], n_prompt_seq=n_pseq,
                                      prompt_len=p_len, n_sample_seq=n_sseq, sample_len=s_len, bm=bm)
            new_re.append(f_re)
            new_im.append(f_im)
            w_out = s5_w_out[j]
        elif kind == 1:
            width = pool_w_in.shape[2] // 2
            w = pool_w_in[j].astype(BF16)
            u, z = _inproj_call(x, ml, norm_g[layer], [(w, 0, width), (w, 1, width)], [F32, BF16], **geo)
            act = _pool_call(u, z, pool_w[j].astype(BF16), pool_scale[j], n_prompt=n_prompt,
                             prompt_len=p_len, sample_len=s_len)
            w_out = pool_w_out[j]
        else:
            q_rank, kv_rank = mla_q_norm.shape[-1], mla_kv_norm.shape[-1]
            w_small, w_z, wq_a, wq_r = _mla_weights(mla_w_in[j], mla_wq_b[j])
            small, z = _inproj_call(x, ml, norm_g[layer], [w_small, w_z], [F32, BF16], **geo)
            cos_t, sin_t = _rope_tables(n_prompt, n_sseq, s_len)
            q, ckv_n, kpe_k = _mla_post_call(small, cos_t, sin_t, mla_q_norm[j], mla_kv_norm[j], wq_a, wq_r,
                                             bm=bm)
            wkv = mla_wkv_b[j].astype(BF16)
            past = cache_ckv.shape[2]
            k_len = past + s_len
            ckv_s = jnp.concatenate([cache_ckv[:, j].astype(F32), ckv_n[n_prompt:].reshape(n_sseq, s_len, kv_rank)],
                                    axis=1).reshape(n_sseq * k_len, kv_rank)
            kpe_cache = jnp.concatenate([cache_kpe[:, j].astype(BF16),
                                         jnp.zeros((n_sseq, past, LANES - MLA_ROPE), BF16)], axis=-1)
            kpe_s = jnp.concatenate([kpe_cache, kpe_k[n_prompt:].reshape(n_sseq, s_len, LANES)],
                                    axis=1).reshape(n_sseq * k_len, LANES)
            kv_p = _kv_expand_call(ckv_n[:n_prompt], wkv, bm=bm)
            kv_s = _kv_expand_call(ckv_s, wkv, bm=bm)
            act = (_attn_call(q, kv_p, kpe_k[:n_prompt], z, q_row0=0, n_seq=n_pseq, q_len=p_len,
                              k_len=p_len, hg=MLA_HEADS, qb=p_len),
                   _attn_call(q, kv_s, kpe_s, z, q_row0=n_prompt, n_seq=n_sseq, q_len=s_len,
                              k_len=k_len, hg=4, qb=256))
            new_ckv.append(ckv_n[:n_prompt].reshape(n_pseq, p_len, kv_rank))
            c_kpe = q_rank + kv_rank
            new_kpe.append(small[:n_prompt, c_kpe:c_kpe + MLA_ROPE].reshape(n_pseq, p_len, MLA_ROPE))
            w_out = mla_w_out[j]
        x = _outproj_call(act, x, ml, w_out.astype(BF16), final_norm_g, final_norm=last, **geo)

    y_prompt = x[:n_prompt].reshape(n_pseq, p_len, d)
    y_sample = x[n_prompt:].reshape(n_sseq, s_len, d)
    return (y_prompt, y_sample, jnp.stack(new_re, axis=1), jnp.stack(new_im, axis=1),
            jnp.stack(new_ckv, axis=1), jnp.stack(new_kpe, axis=1))
```

```python
import functools
import math

import jax
import jax.numpy as jnp
import numpy as np
from jax import lax
from jax.experimental import pallas as pl
from jax.experimental.pallas import tpu as pltpu

S5_GROUP = 16
S5_CHUNK = 16
POOL_WINDOWS = (2, 4, 8, 16)
MLA_HEADS = 16
MLA_NOPE = 128
MLA_ROPE = 64
MLA_V = 128
GRID_W = 64
ROPE_THETA = 10000.0
NORM_EPS = 1e-6
N_MIXERS = 3

LANES = 128
SUBLANES = 8
VMEM_LIMIT_BYTES = 56 * 1024 * 1024

F32 = jnp.float32
BF16 = jnp.bfloat16
HIGHEST = lax.Precision.HIGHEST


def _cparams(*sem):
    return pltpu.CompilerParams(dimension_semantics=sem, vmem_limit_bytes=VMEM_LIMIT_BYTES)


def _sigmoid(x):
    return 1.0 / (1.0 + jnp.exp(-x))


def _silu(x):
    return x * _sigmoid(x)


def _gelu_tanh(x):
    c = math.sqrt(2.0 / math.pi)
    hx = 0.5 * x
    return hx + hx * jnp.tanh(x * (c + (c * 0.044715) * (x * x)))


def _ada_kernel(c_ref, w_ref, b_ref, o_ref):
    a = _silu(c_ref[...])
    o_ref[...] = jnp.dot(a, w_ref[...], preferred_element_type=F32, precision=HIGHEST) + b_ref[...]


def _ada_call(conds, ada_w, ada_b):
    depth, d, d3 = ada_w.shape
    c8 = conds.shape[0]
    tn = 512
    return pl.pallas_call(
        _ada_kernel,
        out_shape=jax.ShapeDtypeStruct((depth, c8, d3), F32),
        grid=(depth, d3 // tn),
        in_specs=[
            pl.BlockSpec((c8, d), lambda l, n: (0, 0)),
            pl.BlockSpec((None, d, tn), lambda l, n: (l, 0, n)),
            pl.BlockSpec((None, 1, tn), lambda l, n: (l, 0, n)),
        ],
        out_specs=pl.BlockSpec((None, c8, tn), lambda l, n: (l, 0, n)),
        compiler_params=_cparams("arbitrary", "arbitrary"),
        name="ada_mod",
    )(conds, ada_w, ada_b.reshape(depth, 1, d3))


def _cond_of_block(i, n_prompt_blocks, blocks_per_sample):
    return jnp.where(i < n_prompt_blocks, 0, 1 + (i - n_prompt_blocks) // blocks_per_sample)


def _modulated(x, mod_ref, g_ref, d):
    ms = jnp.mean(x * x, axis=-1, keepdims=True)
    y = x * lax.rsqrt(ms + NORM_EPS) * g_ref[...]
    shift = mod_ref[:, 0:d]
    scale = mod_ref[:, d:2 * d]
    return (y * (1.0 + scale) + shift).astype(BF16)


def _inproj_kernel(x_ref, mod_ref, g_ref, *rest, d, n_chunk, chunk_rows):
    if chunk_rows:
        rest, xc_ref = rest[:-1], rest[-1]
    n_out = len(rest) // 2
    w_refs, o_refs = rest[:n_out], rest[n_out:]
    h = _modulated(x_ref[...], mod_ref, g_ref, d)
    for w_ref, o_ref in zip(w_refs, o_refs):
        n = w_ref.shape[1]
        for c in range(0, n, n_chunk):
            e = min(c + n_chunk, n)
            r = jnp.dot(h, w_ref[:, c:e], preferred_element_type=F32).astype(o_ref.dtype)
            if len(o_ref.shape) == 3:
                for lb in range((e - c) // LANES):
                    o_ref[c // LANES + lb] = r[:, lb * LANES:(lb + 1) * LANES]
            else:
                o_ref[:, c:e] = r
    if chunk_rows:
        _s5_to_chunks_kernel(o_refs[0], xc_ref, rows=chunk_rows)


def _inproj_call(x, mods_l, norm_g, weights, out_dtypes, *, n_prompt, sample_len, bm=512,
                 lane_blocked=(), s5_chunks=False):
    n_tok, d = x.shape
    npb, bps = n_prompt // bm, sample_len // bm
    cond = functools.partial(_cond_of_block, n_prompt_blocks=npb, blocks_per_sample=bps)
    weights = [w if isinstance(w, tuple) else (w, 0, w.shape[1]) for w in weights]
    in_specs = [
        pl.BlockSpec((bm, d), lambda i: (i, 0)),
        pl.BlockSpec((None, 1, 3 * d), lambda i: (cond(i), 0, 0)),
        pl.BlockSpec((1, d), lambda i: (0, 0)),
    ] + [pl.BlockSpec((d, n), functools.partial(lambda i, blk: (0, blk), blk=blk)) for _, blk, n in weights]
    out_specs, out_shape = [], []
    for k, ((_, _, n), dt) in enumerate(zip(weights, out_dtypes)):
        if k in lane_blocked:
            out_specs.append(pl.BlockSpec((n // LANES, bm, LANES), lambda i: (0, i, 0)))
            out_shape.append(jax.ShapeDtypeStruct((n // LANES, n_tok, LANES), dt))
        else:
            out_specs.append(pl.BlockSpec((bm, n), lambda i: (i, 0)))
            out_shape.append(jax.ShapeDtypeStruct((n_tok, n), dt))
    chunk_rows = bm // S5_CHUNK if s5_chunks else 0
    if s5_chunks:
        assert 0 in lane_blocked
        n_groups = weights[0][2] // S5_GROUP
        out_specs.append(pl.BlockSpec((n_groups, chunk_rows, 2 * LANES), lambda i: (0, i, 0)))
        out_shape.append(jax.ShapeDtypeStruct((n_groups, n_tok // S5_CHUNK, 2 * LANES), BF16))
    return pl.pallas_call(
        functools.partial(_inproj_kernel, d=d, n_chunk=512, chunk_rows=chunk_rows),
        out_shape=out_shape,
        grid=(n_tok // bm,),
        in_specs=in_specs,
        out_specs=out_specs,
        compiler_params=_cparams("arbitrary"),
        name="norm_mod_inproj",
    )(x, mods_l, norm_g.reshape(1, d), *[w for w, _, _ in weights])


def _outproj_kernel(*refs, d, final_norm, n_prompt_blocks):
    x_ref, mod_ref, w_ref, fg_ref, o_ref = refs[-5:]

    def finish(a_ref):
        y = jnp.dot(a_ref[...], w_ref[...], preferred_element_type=F32)
        gate = mod_ref[:, 2 * d:3 * d]
        xn = x_ref[...] + gate * y
        if final_norm:
            ms = jnp.mean(xn * xn, axis=-1, keepdims=True)
            xn = xn * lax.rsqrt(ms + NORM_EPS) * fg_ref[...]
        o_ref[...] = xn

    if len(refs) == 6:
        finish(refs[0])
    else:
        pl.when(pl.program_id(0) < n_prompt_blocks)(functools.partial(finish, refs[0]))
        pl.when(pl.program_id(0) >= n_prompt_blocks)(functools.partial(finish, refs[1]))


def _outproj_call(act, x, mods_l, w_out, final_g, *, n_prompt, sample_len, final_norm, bm=512):
    n_tok, d = x.shape
    npb, bps = n_prompt // bm, sample_len // bm
    cond = functools.partial(_cond_of_block, n_prompt_blocks=npb, blocks_per_sample=bps)
    if isinstance(act, tuple):
        k = act[0].shape[1]
        acts = list(act)
        act_specs = [pl.BlockSpec((bm, k), lambda i: (jnp.minimum(i, npb - 1), 0)),
                     pl.BlockSpec((bm, k), lambda i: (jnp.maximum(i - npb, 0), 0))]
    else:
        k = act.shape[1]
        acts = [act]
        act_specs = [pl.BlockSpec((bm, k), lambda i: (i, 0))]
    return pl.pallas_call(
        functools.partial(_outproj_kernel, d=d, final_norm=final_norm, n_prompt_blocks=npb),
        out_shape=jax.ShapeDtypeStruct((n_tok, d), F32),
        grid=(n_tok // bm,),
        in_specs=act_specs + [
            pl.BlockSpec((bm, d), lambda i: (i, 0)),
            pl.BlockSpec((None, 1, 3 * d), lambda i: (cond(i), 0, 0)),
            pl.BlockSpec((k, d), lambda i: (0, 0)),
            pl.BlockSpec((1, d), lambda i: (0, 0)),
        ],
        out_specs=pl.BlockSpec((bm, d), lambda i: (i, 0)),
        compiler_params=_cparams("arbitrary"),
        name="outproj_residual",
    )(*acts, x, mods_l, w_out, final_g.reshape(1, d))


def _s5_time_of_lane_block():
    pos = np.arange(S5_CHUNK)
    half, blk = pos // 8, pos % 8
    g8 = np.arange(8)[:, None]
    return 8 * half[None, :] + (blk[None, :] - g8) % 8


def _s5_tables(lam_re, lam_im, log_step, b_re, b_im, c_re, c_im):
    t_chunk = S5_CHUNK
    n_groups, n_state = lam_re.shape[1], lam_re.shape[2]
    n_oct = n_groups // 8
    lam = lax.complex(lam_re.astype(F32), lam_im.astype(F32))
    step = jnp.exp(log_step.astype(F32))[..., None]
    lam_bar = jnp.exp(lam * step)
    b_bar = ((lam_bar - 1.0) / lam)[..., None] * lax.complex(b_re.astype(F32), b_im.astype(F32))
    c_mat = lax.complex(c_re.astype(F32), c_im.astype(F32))
    ks = jnp.arange(t_chunk + 1, dtype=F32)[:, None, None, None]
    pw = jnp.exp(ks * (lam * step)[None])

    zeros = jnp.zeros((t_chunk - 1, n_groups, n_state), pw.dtype)
    lag_f = jnp.concatenate([zeros, pw[:t_chunk, 0], zeros[:1]], axis=0)
    lag_b = jnp.concatenate([pw[t_chunk - 1::-1, 1], zeros, zeros[:1]], axis=0)
    plag = jnp.concatenate([lag_f, lag_b], axis=-1).transpose(1, 0, 2)
    plag = jnp.stack([plag.real, plag.imag])

    tl = _s5_time_of_lane_block()
    pw_ri = jnp.stack([pw.real, pw.imag]).reshape(2, t_chunk + 1, 2, n_oct, 8, n_state)
    m_idx = np.arange(t_chunk + 1)[None, None, :]

    def power_table(exponent, direction):
        sel = (exponent[:, :, None] == m_idx).astype(np.float32)
        tab = jnp.einsum('kxm,rmakp->rakxp', sel, pw_ri[:, :, direction], precision=HIGHEST)
        return tab.reshape(2, n_groups, t_chunk, n_state)

    def both(fwd, bwd):
        m = jnp.concatenate([fwd, bwd], axis=-1)
        return jnp.stack([m.real, m.imag])

    tin = jnp.concatenate([power_table(t_chunk - 1 - tl, 0), power_table(tl, 1)], axis=-1)
    tout = jnp.concatenate([power_table(tl + 1, 0), power_table(t_chunk - tl, 1)], axis=-1)
    bt = both(b_bar[0].transpose(0, 2, 1), b_bar[1].transpose(0, 2, 1))
    ct = both(c_mat[0], c_mat[1])
    lam_rows = both(pw[t_chunk, 0][:, None], pw[t_chunk, 1][:, None])[:, :, 0]
    return plag, tin, tout, bt, ct, lam_rows


def _s5_kmat_kernel(plag_ref, tin_ref, tout_ref, bt_ref, ct_ref, k_ref, pin_ref, pot_ref, x_scr, v_scr):
    masks = _lane_block_masks()
    n_lag = 2 * S5_CHUNK - 1
    for g8 in range(8):
        br, bi = bt_ref[0, g8], bt_ref[1, g8]
        cr, ci = ct_ref[0, g8], ct_ref[1, g8]
        def split(a):
            hi = a.astype(BF16)
            return hi, (a - hi.astype(F32)).astype(BF16)

        for m in range(n_lag):
            rows = slice(m * S5_GROUP, (m + 1) * S5_GROUP)
            pr, pi = plag_ref[0, g8, m:m + 1, :], plag_ref[1, g8, m:m + 1, :]
            for c0, part in ((0, cr * pr - ci * pi), (LANES, -(cr * pi + ci * pr))):
                x_scr[0, rows, c0:c0 + LANES], x_scr[1, rows, c0:c0 + LANES] = split(part)
        b_hi, b_lo = split(jnp.concatenate([jnp.concatenate([br, bi], axis=1)] * (LANES // S5_GROUP), axis=0))
        nt = functools.partial(lax.dot_general, dimension_numbers=(((1,), (1,)), ((), ())),
                               preferred_element_type=F32)
        v_scr[...] = nt(x_scr[0], b_hi) + nt(x_scr[0], b_lo) + nt(x_scr[1], b_hi)
        for pos in range(S5_CHUNK):
            rows = slice(pos * S5_GROUP, (pos + 1) * S5_GROUP)
            tr, ti = tin_ref[0, g8, pos:pos + 1, :], tin_ref[1, g8, pos:pos + 1, :]
            pin_ref[g8, rows, 0:LANES] = (tr * br - ti * bi).astype(BF16)
            pin_ref[g8, rows, LANES:2 * LANES] = (tr * bi + ti * br).astype(BF16)
            tr, ti = tout_ref[0, g8, pos:pos + 1, :], tout_ref[1, g8, pos:pos + 1, :]
            pot_ref[g8, rows, 0:LANES] = (tr * cr - ti * ci).astype(BF16)
            pot_ref[g8, rows, LANES:2 * LANES] = (-(tr * ci + ti * cr)).astype(BF16)
        for pos in range(S5_CHUNK):
            tau = 8 * (pos // 8) + (pos % 8 - g8) % 8
            rows = slice(pos * S5_GROUP, (pos + 1) * S5_GROUP)
            for half in range(2):
                acc = None
                for blk in range(8):
                    sigma = 8 * half + (blk - g8) % 8
                    m = S5_CHUNK - 1 - sigma + tau
                    src = v_scr[m * S5_GROUP:(m + 1) * S5_GROUP, :]
                    acc = src if acc is None else jnp.where(masks[blk], src, acc)
                k_ref[g8, rows, half * LANES:(half + 1) * LANES] = acc.astype(BF16)


def _s5_kmat_call(plag, tin, tout, bt, ct):
    n_groups = plag.shape[1]
    n_lag_rows = (2 * S5_CHUNK - 1) * S5_GROUP
    lag_spec = pl.BlockSpec((2, 8) + plag.shape[2:], lambda i: (0, i, 0, 0))
    tab_spec = pl.BlockSpec((2, 8, S5_GROUP, LANES), lambda i: (0, i, 0, 0))
    mat = jax.ShapeDtypeStruct((n_groups, 2 * LANES, 2 * LANES), BF16)
    mat_spec = pl.BlockSpec((8, 2 * LANES, 2 * LANES), lambda i: (i, 0, 0))
    return pl.pallas_call(
        _s5_kmat_kernel,
        out_shape=[mat, mat, mat],
        grid=(n_groups // 8,),
        in_specs=[lag_spec, tab_spec, tab_spec, tab_spec, tab_spec],
        out_specs=[mat_spec, mat_spec, mat_spec],
        scratch_shapes=[pltpu.VMEM((2, n_lag_rows, 2 * LANES), BF16), pltpu.VMEM((n_lag_rows, LANES), F32)],
        compiler_params=_cparams("arbitrary"),
        name="s5_kmat",
    )(plag, tin, tout, bt, ct)


def _lane_block_masks():
    blk = lax.broadcasted_iota(jnp.int32, (1, LANES), 1) // S5_GROUP
    return [blk == b for b in range(8)]


def _diagonal_merge(src):
    blk = lax.broadcasted_iota(jnp.int32, (1, LANES), 1) // S5_GROUP
    q = list(src)
    for bit in (1, 2, 4):
        take = (blk & bit) != 0
        q = [jnp.where(take, q[(x + bit) % 8], q[x]) for x in range(8)]
    return [q[(-t) % 8] for t in range(8)]


def _s5_to_chunks_kernel(u_ref, x_ref, *, rows):
    for o in range(u_ref.shape[0]):
        for r0 in range(0, rows, SUBLANES):
            for half in range(2):
                rolled = []
                for t8 in range(8):
                    v = u_ref[o, pl.ds(r0 * S5_CHUNK + 8 * half + t8, SUBLANES, stride=S5_CHUNK), :]
                    rolled.append(pltpu.roll(v, t8 * S5_GROUP, 1) if t8 else v)
                for g8, merged in enumerate(_diagonal_merge(rolled)):
                    x_ref[o * 8 + g8, r0:r0 + SUBLANES, half * LANES:(half + 1) * LANES] = merged.astype(BF16)


def _s5_from_chunks_kernel(y_ref, u_ref, d_ref, o_ref, *, rows):
    for o in range(u_ref.shape[0]):
        for r0 in range(0, rows, SUBLANES):
            for half in range(2):
                src = [y_ref[o * 8 + g8, r0:r0 + SUBLANES, half * LANES:(half + 1) * LANES] for g8 in range(8)]
                for t8, merged in enumerate(_diagonal_merge(src)):
                    nat = pltpu.roll(merged, (8 - t8) * S5_GROUP, 1) if t8 else merged
                    o_ref[o, pl.ds(r0 * S5_CHUNK + 8 * half + t8, SUBLANES, stride=S5_CHUNK), :] = nat
        d_vec = d_ref[:, o * LANES:(o + 1) * LANES]
        o_ref[o] = _gelu_tanh(o_ref[o] + d_vec * u_ref[o])


def _s5_from_chunks_call(yc, u3, d_skip, *, rows=32):
    n_blk, n_tok, _ = u3.shape
    n_groups, n_rows, _ = yc.shape
    tok_spec = pl.BlockSpec((n_blk, rows * S5_CHUNK, LANES), lambda i: (0, i, 0))
    return pl.pallas_call(
        functools.partial(_s5_from_chunks_kernel, rows=rows),
        out_shape=jax.ShapeDtypeStruct(u3.shape, F32),
        grid=(n_rows // rows,),
        in_specs=[pl.BlockSpec((n_groups, rows, 2 * LANES), lambda i: (0, i, 0)),
                  tok_spec,
                  pl.BlockSpec((1, n_blk * LANES), lambda i: (0, 0))],
        out_specs=tok_spec,
        compiler_params=_cparams("arbitrary"),
        name="s5_from_chunks",
    )(yc, u3, d_skip.reshape(1, n_blk * LANES).astype(F32))


def _s5_chunk_kernel(x_ref, kt_ref, pin_ref, po_ref, lam_ref, h0r_ref, h0i_ref,
                     y_ref, fr_ref, fi_ref, r_scr, st_scr, *, segments, seq_block):
    gb = SUBLANES
    rows = x_ref.shape[1]
    lane = lax.broadcasted_iota(jnp.int32, (1, LANES), 1)
    fwd_lanes = lane < (LANES // 2)
    for g in range(gb):
        x = x_ref[g]
        y_ref[g] = lax.dot_general(x, kt_ref[g], (((1,), (1,)), ((), ())), preferred_element_type=F32)
        r = jnp.dot(x, pin_ref[g], preferred_element_type=F32)
        of_group = pl.ds(g, rows, stride=gb)
        r_scr[0, of_group, :] = r[:, 0:LANES]
        r_scr[1, of_group, :] = r[:, LANES:2 * LANES]
    ar, ai = lam_ref[0], lam_ref[1]
    for row0, n_seq, n_chunks, from_input, to_output in segments:
        for b0 in range(0, n_seq, seq_block):
            nb = min(seq_block, n_seq - b0)

            def step(i, carry, row0=row0, n_chunks=n_chunks, b0=b0, nb=nb):
                out = []
                for k in range(nb):
                    base = row0 + (b0 + k) * n_chunks
                    at_f = pl.ds(pl.multiple_of((base + i) * gb, gb), gb)
                    at_b = pl.ds(pl.multiple_of((base + (n_chunks - 1) - i) * gb, gb), gb)
                    s_re, s_im = carry[k]
                    half = LANES // 2
                    st_scr[0, at_f, 0:half] = s_re[:, 0:half]
                    st_scr[0, at_b, half:LANES] = s_re[:, half:LANES]
                    st_scr[1, at_f, 0:half] = s_im[:, 0:half]
                    st_scr[1, at_b, half:LANES] = s_im[:, half:LANES]
                    v_re = jnp.where(fwd_lanes, r_scr[0, at_f, :], r_scr[0, at_b, :])
                    v_im = jnp.where(fwd_lanes, r_scr[1, at_f, :], r_scr[1, at_b, :])
                    out.append((ar * s_re - ai * s_im + v_re, ar * s_im + ai * s_re + v_im))
                return tuple(out)

            if from_input:
                init = tuple((h0r_ref[b0 + k], h0i_ref[b0 + k]) for k in range(nb))
            else:
                init = tuple((jnp.zeros((gb, LANES), F32),) * 2 for _ in range(nb))
            fin = lax.fori_loop(0, n_chunks, step, init, unroll=2)
            if to_output:
                for k in range(nb):
                    fr_ref[b0 + k] = fin[k][0]
                    fi_ref[b0 + k] = fin[k][1]
    for g in range(gb):
        of_group = pl.ds(g, rows, stride=gb)
        st = jnp.concatenate([st_scr[cb, of_group, :] for cb in range(2)], axis=1).astype(BF16)
        y_ref[g] = y_ref[g] + lax.dot_general(st, po_ref[g], (((1,), (1,)), ((), ())),
                                              preferred_element_type=F32)


def _s5_chunk_call(xc, kt, pin, pout, lam_rows, h0_re, h0_im, *, layer, segments, n_final):
    n_groups, rows, _ = xc.shape
    gb = SUBLANES
    s_in = h0_re.shape[0]
    kern = functools.partial(_s5_chunk_kernel, segments=segments, seq_block=8)
    g3 = lambda i: (i, 0, 0)
    blk0 = layer * (n_groups // gb)
    p3 = lambda i: (i + blk0, 0, 0)
    mid = lambda i: (0, i, 0)
    return pl.pallas_call(
        kern,
        out_shape=[jax.ShapeDtypeStruct((n_groups, rows, 2 * LANES), F32),
                   jax.ShapeDtypeStruct((n_final, n_groups, LANES), F32),
                   jax.ShapeDtypeStruct((n_final, n_groups, LANES), F32)],
        grid=(n_groups // gb,),
        in_specs=[
            pl.BlockSpec((gb, rows, 2 * LANES), g3),
            pl.BlockSpec((gb, 2 * LANES, 2 * LANES), p3),
            pl.BlockSpec((gb, 2 * LANES, 2 * LANES), p3),
            pl.BlockSpec((gb, 2 * LANES, 2 * LANES), p3),
            pl.BlockSpec((2, gb, LANES), lambda i: (0, i + blk0, 0)),
            pl.BlockSpec((s_in, gb, LANES), mid),
            pl.BlockSpec((s_in, gb, LANES), mid),
        ],
        out_specs=[pl.BlockSpec((gb, rows, 2 * LANES), g3),
                   pl.BlockSpec((n_final, gb, LANES), mid),
                   pl.BlockSpec((n_final, gb, LANES), mid)],
        scratch_shapes=[pltpu.VMEM((2, rows * gb, LANES), F32),
                        pltpu.VMEM((2, rows * gb, LANES), F32)],
        compiler_params=_cparams("arbitrary"),
        name="s5_chunk_scan",
    )(xc, kt, pin, pout, lam_rows, h0_re, h0_im)


def _glu_kernel(y_ref, z_ref, w_ref, b_ref, o_ref, *, n_chunk):
    n_blk = y_ref.shape[0]
    yb = jnp.concatenate([y_ref[o].astype(BF16) for o in range(n_blk)], axis=1)
    per = n_chunk // LANES
    for c in range(0, n_blk, per):
        sl = slice(c * LANES, (c + per) * LANES)
        gate = _sigmoid(jnp.dot(yb, w_ref[:, sl], preferred_element_type=F32) + b_ref[:, sl])
        y = jnp.concatenate([y_ref[c + k] for k in range(per)], axis=1)
        o_ref[:, sl] = (y * gate * _silu(z_ref[:, sl].astype(F32))).astype(o_ref.dtype)


def _glu_call(y3, z, glu_w, glu_b, *, bm=512):
    n_blk, n_tok, _ = y3.shape
    width = n_blk * LANES
    return pl.pallas_call(
        functools.partial(_glu_kernel, n_chunk=min(512, width)),
        out_shape=jax.ShapeDtypeStruct((n_tok, width), BF16),
        grid=(n_tok // bm,),
        in_specs=[
            pl.BlockSpec((n_blk, bm, LANES), lambda i: (0, i, 0)),
            pl.BlockSpec((bm, width), lambda i: (i, 0)),
            pl.BlockSpec((width, width), lambda i: (0, 0)),
            pl.BlockSpec((1, width), lambda i: (0, 0)),
        ],
        out_specs=pl.BlockSpec((bm, width), lambda i: (i, 0)),
        compiler_params=_cparams("arbitrary"),
        name="s5_glu_gate",
    )(y3, z, glu_w, glu_b.reshape(1, width).astype(F32))


def _s5_prep_all(lam_re, lam_im, log_step, b_re, b_im, c_re, c_im):
    tabs = jax.vmap(_s5_tables)(lam_re, lam_im, log_step, b_re, b_im, c_re, c_im)
    plag, tin, tout, bt, ct, lam_rows = [jnp.moveaxis(t, 0, 1).reshape((2, -1) + t.shape[3:]) for t in tabs]
    kt, pin, pot = _s5_kmat_call(plag, tin, tout, bt, ct)
    return kt, pin, pot, lam_rows


def _s5_mix(u3, xc, z, layer, mats, d_skip, glu_w, glu_b, st_re, st_im, *, n_prompt_seq, prompt_len,
            n_sample_seq, sample_len, bm):
    kt, pin, pout, lam_rows = mats
    n_state = LANES // 2
    pc, sc = prompt_len // S5_CHUNK, sample_len // S5_CHUNK

    def state_rows(s):
        return jnp.concatenate([s[:, 0], s[:, 1]], axis=-1).astype(F32)

    segments = ((0, n_prompt_seq, pc, False, True), (n_prompt_seq * pc, n_sample_seq, sc, True, False))
    yc, fr, fi = _s5_chunk_call(xc, kt, pin, pout, lam_rows, state_rows(st_re), state_rows(st_im),
                                layer=layer, segments=segments, n_final=n_prompt_seq)
    y3 = _s5_from_chunks_call(yc, u3, d_skip)
    act = _glu_call(y3, z, glu_w, glu_b, bm=bm)

    def unpack(f):
        return jnp.stack([f[:, :, :n_state], f[:, :, n_state:]], axis=1)

    return act, unpack(fr), unpack(fi)


def _pool_kernel(u_ref, z_ref, w_ref, s_ref, o_ref, *, n_prompt_blocks, prompt_len, sample_len):
    rows = u_ref.shape[0]
    seq_len = jnp.where(pl.program_id(0) < n_prompt_blocks, prompt_len, sample_len)
    t = lax.broadcasted_iota(jnp.int32, (rows, 1), 0) & (seq_len - 1)

    def later(x, k):
        return jnp.where(t + k < seq_len, pltpu.roll(x, rows - k, 0), 0.0)

    def earlier(x, k):
        return jnp.where(t >= k, pltpu.roll(x, k, 0), 0.0)

    def body(win):
        lo = win // 2
        u = u_ref[...]
        fwd = u
        bwd = earlier(u, 1)
        s = 1
        while s < lo:
            fwd = fwd + later(fwd, s)
            bwd = bwd + earlier(bwd, s)
            s *= 2
        cnt = jnp.minimum(t - lo + win, seq_len) - jnp.maximum(t - lo, 0)
        p = (fwd + bwd) / cnt.astype(F32) - u
        m = jnp.dot(p.astype(BF16), w_ref[...], preferred_element_type=F32) * s_ref[...]
        o_ref[...] = (m * _silu(z_ref[...].astype(F32))).astype(o_ref.dtype)

    for gi, win in enumerate(POOL_WINDOWS):
        pl.when(pl.program_id(1) == gi)(functools.partial(body, win))


def _pool_call(u, z, pool_w, pool_scale, *, n_prompt, prompt_len, sample_len, rows=2048):
    n_tok, width = u.shape
    n_groups = len(POOL_WINDOWS)
    cg = width // n_groups
    assert prompt_len & (prompt_len - 1) == 0 and sample_len & (sample_len - 1) == 0
    assert rows % prompt_len == 0 and rows % sample_len == 0 and n_prompt % rows == 0
    kern = functools.partial(_pool_kernel, n_prompt_blocks=n_prompt // rows, prompt_len=prompt_len,
                             sample_len=sample_len)
    return pl.pallas_call(
        kern,
        out_shape=jax.ShapeDtypeStruct((n_tok, width), BF16),
        grid=(n_tok // rows, n_groups),
        in_specs=[
            pl.BlockSpec((rows, cg), lambda i, g: (i, g)),
            pl.BlockSpec((rows, cg), lambda i, g: (i, g)),
            pl.BlockSpec((None, cg, cg), lambda i, g: (g, 0, 0)),
            pl.BlockSpec((1, cg), lambda i, g: (0, g)),
        ],
        out_specs=pl.BlockSpec((rows, cg), lambda i, g: (i, g)),
        compiler_params=_cparams("arbitrary", "arbitrary"),
        name="pool_mix",
    )(u, z, pool_w, pool_scale.reshape(1, width).astype(F32))


MLA_QW = 2 * LANES

_ROT_SRC = np.concatenate([np.arange(16, 32), np.arange(0, 16), np.arange(48, 64), np.arange(32, 48)])
_ROT_SIGN = np.concatenate([-np.ones(16), np.ones(16), -np.ones(16), np.ones(16)]).astype(np.float32)


def _rope_tables(n_prompt, n_sample_seq, sample_len):
    half = MLA_ROPE // 4
    tok = jnp.arange(sample_len)
    row = (tok // GRID_W).astype(F32)
    col = (tok % GRID_W).astype(F32)
    inv = ROPE_THETA ** (-jnp.arange(half, dtype=F32) / half)
    a_row, a_col = row[:, None] * inv, col[:, None] * inv
    cos = jnp.concatenate([jnp.cos(a_row), jnp.cos(a_row), jnp.cos(a_col), jnp.cos(a_col)], axis=-1)
    sin = jnp.concatenate([jnp.sin(a_row), jnp.sin(a_row), jnp.sin(a_col), jnp.sin(a_col)], axis=-1)
    pad = jnp.zeros((sample_len, LANES - MLA_ROPE), F32)
    cos_s = jnp.tile(jnp.concatenate([cos, pad], axis=-1), (n_sample_seq, 1))
    sin_s = jnp.tile(jnp.concatenate([sin, pad], axis=-1), (n_sample_seq, 1))
    cos_p = jnp.concatenate([jnp.ones((n_prompt, MLA_ROPE), F32), jnp.zeros((n_prompt, LANES - MLA_ROPE), F32)], -1)
    return jnp.concatenate([cos_p, cos_s]), jnp.concatenate([jnp.zeros((n_prompt, LANES), F32), sin_s])


def _rms(x, g):
    return x * lax.rsqrt(jnp.mean(x * x, axis=-1, keepdims=True) + NORM_EPS) * g


def _mla_post_kernel(sm_ref, cos_ref, sin_ref, qn_ref, kn_ref, wa_ref, wb_ref,
                     q_ref, ckv_ref, kpe_ref, *, q_rank, kv_rank, heads_per_dot):
    cosp, sinp = cos_ref[...], sin_ref[...]
    qn = _rms(sm_ref[:, 0:q_rank], qn_ref[...]).astype(BF16)
    for h0 in range(0, MLA_HEADS, heads_per_dot):
        a = jnp.dot(qn, wa_ref[:, h0 * MLA_QW:(h0 + heads_per_dot) * MLA_QW], preferred_element_type=F32)
        b = jnp.dot(qn, wb_ref[:, h0 * LANES:(h0 + heads_per_dot) * LANES], preferred_element_type=F32)
        for j in range(heads_per_dot):
            h = h0 + j
            q_ref[:, h * MLA_QW:h * MLA_QW + LANES] = a[:, j * MLA_QW:j * MLA_QW + LANES].astype(BF16)
            pe = a[:, j * MLA_QW + LANES:(j + 1) * MLA_QW] * cosp + b[:, j * LANES:(j + 1) * LANES] * sinp
            q_ref[:, h * MLA_QW + LANES:(h + 1) * MLA_QW] = pe.astype(BF16)
    c0 = q_rank
    ckv_ref[...] = _rms(sm_ref[:, c0:c0 + kv_rank], kn_ref[...])
    k0 = c0 + kv_rank
    kpe_ref[...] = (sm_ref[:, k0:k0 + LANES] * cosp + sm_ref[:, k0 + LANES:k0 + 2 * LANES] * sinp).astype(BF16)


def _mla_post_call(small, cos_t, sin_t, q_norm, kv_norm, wq_a, wq_b, *, bm=512):
    n_tok, ws = small.shape
    q_rank, kv_rank = q_norm.shape[-1], kv_norm.shape[-1]
    row = lambda i: (i, 0)
    fix = lambda i: (0, 0)
    kern = functools.partial(_mla_post_kernel, q_rank=q_rank, kv_rank=kv_rank, heads_per_dot=4)
    return pl.pallas_call(
        kern,
        out_shape=[jax.ShapeDtypeStruct((n_tok, MLA_HEADS * MLA_QW), BF16),
                   jax.ShapeDtypeStruct((n_tok, kv_rank), F32),
                   jax.ShapeDtypeStruct((n_tok, LANES), BF16)],
        grid=(n_tok // bm,),
        in_specs=[
            pl.BlockSpec((bm, ws), row),
            pl.BlockSpec((bm, LANES), row),
            pl.BlockSpec((bm, LANES), row),
            pl.BlockSpec((1, q_rank), fix),
            pl.BlockSpec((1, kv_rank), fix),
            pl.BlockSpec(wq_a.shape, fix),
            pl.BlockSpec(wq_b.shape, fix),
        ],
        out_specs=[pl.BlockSpec((bm, MLA_HEADS * MLA_QW), row),
                   pl.BlockSpec((bm, kv_rank), row),
                   pl.BlockSpec((bm, LANES), row)],
        compiler_params=_cparams("arbitrary"),
        name="mla_q_rope",
    )(small, cos_t, sin_t, q_norm.reshape(1, q_rank).astype(F32), kv_norm.reshape(1, kv_rank).astype(F32),
      wq_a, wq_b)


def _kv_expand_kernel(c_ref, w_ref, o_ref, *, n_chunk):
    c = c_ref[...].astype(BF16)
    n = w_ref.shape[1]
    for s in range(0, n, n_chunk):
        o_ref[:, s:s + n_chunk] = jnp.dot(c, w_ref[:, s:s + n_chunk],
                                          preferred_element_type=F32).astype(o_ref.dtype)


def _kv_expand_call(ckv, wkv_b, *, bm=512):
    rows, kr = ckv.shape
    n = wkv_b.shape[1]
    return pl.pallas_call(
        functools.partial(_kv_expand_kernel, n_chunk=1024),
        out_shape=jax.ShapeDtypeStruct((rows, n), BF16),
        grid=(rows // bm,),
        in_specs=[pl.BlockSpec((bm, kr), lambda i: (i, 0)), pl.BlockSpec((kr, n), lambda i: (0, 0))],
        out_specs=pl.BlockSpec((bm, n), lambda i: (i, 0)),
        compiler_params=_cparams("arbitrary"),
        name="mla_kv_expand",
    )(ckv, wkv_b)


def _attn_kernel(q_ref, kv_ref, kpe_ref, z_ref, o_ref, kcat_scr, vext_scr, *, hg, scale):
    c2 = scale * math.log2(math.e)

    @pl.when(pl.program_id(2) == 0)
    def _():
        ones = jnp.ones((kv_ref.shape[0], LANES), BF16)
        for j in range(hg):
            kcat_scr[j, :, 0:LANES] = kv_ref[:, j * 2 * LANES:j * 2 * LANES + LANES]
            kcat_scr[j, :, LANES:2 * LANES] = kpe_ref[...]
            vext_scr[j, :, 0:LANES] = kv_ref[:, j * 2 * LANES + LANES:(j + 1) * 2 * LANES]
            vext_scr[j, :, LANES:2 * LANES] = ones

    for j in range(hg):
        q = q_ref[:, j * MLA_QW:(j + 1) * MLA_QW]
        s = lax.dot_general(q, kcat_scr[j], (((1,), (1,)), ((), ())), preferred_element_type=F32)
        e = jnp.exp2((s - jnp.max(s, axis=-1, keepdims=True)) * c2)
        pv = jnp.dot(e.astype(BF16), vext_scr[j], preferred_element_type=F32)
        zs = slice(j * MLA_V, (j + 1) * MLA_V)
        o = pv[:, 0:MLA_V] / pv[:, MLA_V:2 * MLA_V]
        o_ref[:, zs] = (o * _silu(z_ref[:, zs].astype(F32))).astype(o_ref.dtype)


def _attn_call(q, kv, kpe, z, *, q_row0, n_seq, q_len, k_len, hg, qb):
    width = MLA_HEADS * MLA_V
    nqb = q_len // qb
    qb0 = q_row0 // qb
    assert q_row0 % qb == 0 and q_len % qb == 0
    scale = float((MLA_NOPE + MLA_ROPE) ** -0.5)
    qrow = lambda b, g, i: (qb0 + b * nqb + i, g)
    return pl.pallas_call(
        functools.partial(_attn_kernel, hg=hg, scale=scale),
        out_shape=jax.ShapeDtypeStruct((n_seq * q_len, width), BF16),
        grid=(n_seq, MLA_HEADS // hg, nqb),
        in_specs=[
            pl.BlockSpec((qb, hg * MLA_QW), qrow),
            pl.BlockSpec((k_len, hg * 2 * LANES), lambda b, g, i: (b, g)),
            pl.BlockSpec((k_len, LANES), lambda b, g, i: (b, 0)),
            pl.BlockSpec((qb, hg * MLA_V), qrow),
        ],
        out_specs=pl.BlockSpec((qb, hg * MLA_V), lambda b, g, i: (b * nqb + i, g)),
        scratch_shapes=[pltpu.VMEM((hg, k_len, MLA_QW), BF16),
                        pltpu.VMEM((hg, k_len, 2 * MLA_V), BF16)],
        compiler_params=_cparams("arbitrary", "arbitrary", "arbitrary"),
        name="mla_attention",
    )(q, kv, kpe, z)


def _mla_weights(w_in, wq_b):
    q_rank = wq_b.shape[0]
    kv_rank = w_in.shape[1] - q_rank - MLA_ROPE - MLA_HEADS * MLA_V
    d = w_in.shape[0]
    c_kpe = q_rank + kv_rank
    zpad = jnp.zeros((d, LANES - MLA_ROPE), w_in.dtype)
    kpe_w = w_in[:, c_kpe:c_kpe + MLA_ROPE]
    w_small = jnp.concatenate([w_in[:, :c_kpe], kpe_w, zpad,
                               kpe_w[:, _ROT_SRC] * _ROT_SIGN, zpad], axis=1)
    w_z = w_in[:, c_kpe + MLA_ROPE:]
    hd = MLA_NOPE + MLA_ROPE
    wq3 = wq_b.reshape(q_rank, MLA_HEADS, hd)
    pe = wq3[:, :, MLA_NOPE:]
    z3 = jnp.zeros((q_rank, MLA_HEADS, LANES - MLA_ROPE), wq_b.dtype)
    wq_a = jnp.concatenate([wq3, z3], axis=-1).reshape(q_rank, MLA_HEADS * MLA_QW)
    wq_r = jnp.concatenate([pe[:, :, _ROT_SRC] * _ROT_SIGN, z3], axis=-1).reshape(q_rank, MLA_HEADS * LANES)
    return w_small.astype(BF16), w_z.astype(BF16), wq_a.astype(BF16), wq_r.astype(BF16)


def kernel(x_prompt, x_sample, state_s5_re, state_s5_im, cache_ckv, cache_kpe, c, c_ctx, norm_g, ada_w, ada_b, final_norm_g, s5_w_in, s5_lam_re, s5_lam_im, s5_log_step, s5_b_re, s5_b_im, s5_c_re, s5_c_im, s5_d, s5_glu_w, s5_glu_b, s5_w_out, pool_w_in, pool_w, pool_scale, pool_w_out, mla_w_in, mla_q_norm, mla_wq_b, mla_kv_norm, mla_wkv_b, mla_w_out):
    n_pseq, p_len, d = x_prompt.shape
    n_sseq, s_len, _ = x_sample.shape
    depth = norm_g.shape[0]
    n_prompt = n_pseq * p_len
    bm = 512
    geo = dict(n_prompt=n_prompt, sample_len=s_len, bm=bm)

    x = jnp.concatenate([x_prompt.reshape(n_prompt, d), x_sample.reshape(n_sseq * s_len, d)], axis=0)
    conds = jnp.concatenate([c_ctx[None, :], c, jnp.zeros((SUBLANES - 1 - n_sseq, d), F32)], axis=0)
    mods = _ada_call(conds.astype(F32), ada_w, ada_b)
    mods = mods.reshape(depth, SUBLANES, 1, 3 * d)
    s5_mats = _s5_prep_all(s5_lam_re, s5_lam_im, s5_log_step, s5_b_re, s5_b_im, s5_c_re, s5_c_im)

    new_re, new_im, new_ckv, new_kpe = [], [], [], []
    for layer in range(depth):
        kind, j = layer % N_MIXERS, layer // N_MIXERS
        last = layer == depth - 1
        ml = mods[layer]
        if kind == 0:
            width = s5_w_in.shape[2] // 2
            w = s5_w_in[j].astype(BF16)
            u3, z, xc = _inproj_call(x, ml, norm_g[layer], [(w, 0, width), (w, 1, width)], [F32, BF16],
                                     lane_blocked=(0,), s5_chunks=True, **geo)
            act, f_re, f_im = _s5_mix(u3, xc, z, j, s5_mats, s5_d[j], s5_glu_w[j].astype(BF16), s5_glu_b[j],
                                      state_s5_re[:, j], state_s5_im[:, j], n_prompt_seq=n_pseq,
                                      prompt_len=p_len, n_sample_seq=n_sseq, sample_len=s_len, bm=bm)
            new_re.append(f_re)
            new_im.append(f_im)
            w_out = s5_w_out[j]
        elif kind == 1:
            width = pool_w_in.shape[2] // 2
            w = pool_w_in[j].astype(BF16)
            u, z = _inproj_call(x, ml, norm_g[layer], [(w, 0, width), (w, 1, width)], [F32, BF16], **geo)
            act = _pool_call(u, z, pool_w[j].astype(BF16), pool_scale[j], n_prompt=n_prompt,
                             prompt_len=p_len, sample_len=s_len)
            w_out = pool_w_out[j]
        else:
            q_rank, kv_rank = mla_q_norm.shape[-1], mla_kv_norm.shape[-1]
            w_small, w_z, wq_a, wq_r = _mla_weights(mla_w_in[j], mla_wq_b[j])
            small, z = _inproj_call(x, ml, norm_g[layer], [w_small, w_z], [F32, BF16], **geo)
            cos_t, sin_t = _rope_tables(n_prompt, n_sseq, s_len)
            q, ckv_n, kpe_k = _mla_post_call(small, cos_t, sin_t, mla_q_norm[j], mla_kv_norm[j], wq_a, wq_r,
                                             bm=bm)
            wkv = mla_wkv_b[j].astype(BF16)
            past = cache_ckv.shape[2]
            k_len = past + s_len
            ckv_s = jnp.concatenate([cache_ckv[:, j].astype(F32), ckv_n[n_prompt:].reshape(n_sseq, s_len, kv_rank)],
                                    axis=1).reshape(n_sseq * k_len, kv_rank)
            kpe_cache = jnp.concatenate([cache_kpe[:, j].astype(BF16),
                                         jnp.zeros((n_sseq, past, LANES - MLA_ROPE), BF16)], axis=-1)
            kpe_s = jnp.concatenate([kpe_cache, kpe_k[n_prompt:].reshape(n_sseq, s_len, LANES)],
                                    axis=1).reshape(n_sseq * k_len, LANES)
            kv_p = _kv_expand_call(ckv_n[:n_prompt], wkv, bm=bm)
            kv_s = _kv_expand_call(ckv_s, wkv, bm=bm)
            act = (_attn_call(q, kv_p, kpe_k[:n_prompt], z, q_row0=0, n_seq=n_pseq, q_len=p_len,
                              k_len=p_len, hg=MLA_HEADS, qb=p_len),
                   _attn_call(q, kv_s, kpe_s, z, q_row0=n_prompt, n_seq=n_sseq, q_len=s_len,
                              k_len=k_len, hg=4, qb=256))
            new_ckv.append(ckv_n[:n_prompt].reshape(n_pseq, p_len, kv_rank))
            c_kpe = q_rank + kv_rank
            new_kpe.append(small[:n_prompt, c_kpe:c_kpe + MLA_ROPE].reshape(n_pseq, p_len, MLA_ROPE))
            w_out = mla_w_out[j]
        x = _outproj_call(act, x, ml, w_out.astype(BF16), final_norm_g, final_norm=last, **geo)

    y_prompt = x[:n_prompt].reshape(n_pseq, p_len, d)
    y_sample = x[n_prompt:].reshape(n_sseq, s_len, d)
    return (y_prompt, y_sample, jnp.stack(new_re, axis=1), jnp.stack(new_im, axis=1),
            jnp.stack(new_ckv, axis=1), jnp.stack(new_kpe, axis=1))
```

```python
import functools
import math

import jax
import jax.numpy as jnp
import numpy as np
from jax import lax
from jax.experimental import pallas as pl
from jax.experimental.pallas import tpu as pltpu

S5_GROUP = 16
S5_CHUNK = 16
POOL_WINDOWS = (2, 4, 8, 16)
MLA_HEADS = 16
MLA_NOPE = 128
MLA_ROPE = 64
MLA_V = 128
GRID_W = 64
ROPE_THETA = 10000.0
NORM_EPS = 1e-6
N_MIXERS = 3

LANES = 128
SUBLANES = 8
VMEM_LIMIT_BYTES = 56 * 1024 * 1024

F32 = jnp.float32
BF16 = jnp.bfloat16
HIGHEST = lax.Precision.HIGHEST


def _cparams(*sem):
    return pltpu.CompilerParams(dimension_semantics=sem, vmem_limit_bytes=VMEM_LIMIT_BYTES)


def _sigmoid(x):
    return 1.0 / (1.0 + jnp.exp(-x))


def _silu(x):
    return x * _sigmoid(x)


def _gelu_tanh(x):
    c = math.sqrt(2.0 / math.pi)
    hx = 0.5 * x
    return hx + hx * jnp.tanh(x * (c + (c * 0.044715) * (x * x)))


def _ada_kernel(c_ref, w_ref, b_ref, o_ref):
    a = _silu(c_ref[...])
    o_ref[...] = jnp.dot(a, w_ref[...], preferred_element_type=F32, precision=HIGHEST) + b_ref[...]


def _ada_call(conds, ada_w, ada_b):
    depth, d, d3 = ada_w.shape
    c8 = conds.shape[0]
    tn = 512
    return pl.pallas_call(
        _ada_kernel,
        out_shape=jax.ShapeDtypeStruct((depth, c8, d3), F32),
        grid=(depth, d3 // tn),
        in_specs=[
            pl.BlockSpec((c8, d), lambda l, n: (0, 0)),
            pl.BlockSpec((None, d, tn), lambda l, n: (l, 0, n)),
            pl.BlockSpec((None, 1, tn), lambda l, n: (l, 0, n)),
        ],
        out_specs=pl.BlockSpec((None, c8, tn), lambda l, n: (l, 0, n)),
        compiler_params=_cparams("arbitrary", "arbitrary"),
        name="ada_mod",
    )(conds, ada_w, ada_b.reshape(depth, 1, d3))


def _cond_of_block(i, n_prompt_blocks, blocks_per_sample):
    return jnp.where(i < n_prompt_blocks, 0, 1 + (i - n_prompt_blocks) // blocks_per_sample)


def _modulated(x, mod_ref, g_ref, d):
    ms = jnp.mean(x * x, axis=-1, keepdims=True)
    y = x * lax.rsqrt(ms + NORM_EPS) * g_ref[...]
    shift = mod_ref[:, 0:d]
    scale = mod_ref[:, d:2 * d]
    return (y * (1.0 + scale) + shift).astype(BF16)


def _inproj_kernel(x_ref, mod_ref, g_ref, *rest, d, n_chunk, chunk_rows):
    u_scr = None
    if chunk_rows:
        rest, xc_ref, u_scr = rest[:-2], rest[-2], rest[-1]
    n_out = len(rest) // 2
    w_refs, o_refs = rest[:n_out], rest[n_out:]
    h = _modulated(x_ref[...], mod_ref, g_ref, d)
    for k, (w_ref, o_ref) in enumerate(zip(w_refs, o_refs)):
        n = w_ref.shape[1]
        for c in range(0, n, n_chunk):
            e = min(c + n_chunk, n)
            r = jnp.dot(h, w_ref[:, c:e], preferred_element_type=F32)
            if len(o_ref.shape) == 3:
                for lb in range((e - c) // LANES):
                    part = r[:, lb * LANES:(lb + 1) * LANES]
                    o_ref[c // LANES + lb] = part.astype(o_ref.dtype)
                    if k == 0 and u_scr is not None:
                        u_scr[c // LANES + lb] = part
            else:
                o_ref[:, c:e] = r.astype(o_ref.dtype)
    if chunk_rows:
        _s5_to_chunks_kernel(u_scr, xc_ref, rows=chunk_rows)


def _inproj_call(x, mods_l, norm_g, weights, out_dtypes, *, n_prompt, sample_len, bm=512,
                 lane_blocked=(), s5_chunks=False):
    n_tok, d = x.shape
    npb, bps = n_prompt // bm, sample_len // bm
    cond = functools.partial(_cond_of_block, n_prompt_blocks=npb, blocks_per_sample=bps)
    weights = [w if isinstance(w, tuple) else (w, 0, w.shape[1]) for w in weights]
    in_specs = [
        pl.BlockSpec((bm, d), lambda i: (i, 0)),
        pl.BlockSpec((None, 1, 3 * d), lambda i: (cond(i), 0, 0)),
        pl.BlockSpec((1, d), lambda i: (0, 0)),
    ] + [pl.BlockSpec((d, n), functools.partial(lambda i, blk: (0, blk), blk=blk)) for _, blk, n in weights]
    out_specs, out_shape = [], []
    for k, ((_, _, n), dt) in enumerate(zip(weights, out_dtypes)):
        if k in lane_blocked:
            out_specs.append(pl.BlockSpec((n // LANES, bm, LANES), lambda i: (0, i, 0)))
            out_shape.append(jax.ShapeDtypeStruct((n // LANES, n_tok, LANES), dt))
        else:
            out_specs.append(pl.BlockSpec((bm, n), lambda i: (i, 0)))
            out_shape.append(jax.ShapeDtypeStruct((n_tok, n), dt))
    chunk_rows = bm // S5_CHUNK if s5_chunks else 0
    scratch = []
    if s5_chunks:
        assert 0 in lane_blocked
        n_groups = weights[0][2] // S5_GROUP
        out_specs.append(pl.BlockSpec((n_groups, chunk_rows, 2 * LANES), lambda i: (0, i, 0)))
        out_shape.append(jax.ShapeDtypeStruct((n_groups, n_tok // S5_CHUNK, 2 * LANES), BF16))
        scratch = [pltpu.VMEM((weights[0][2] // LANES, bm, LANES), F32)]
    return pl.pallas_call(
        functools.partial(_inproj_kernel, d=d, n_chunk=512, chunk_rows=chunk_rows),
        out_shape=out_shape,
        grid=(n_tok // bm,),
        in_specs=in_specs,
        out_specs=out_specs,
        scratch_shapes=scratch,
        compiler_params=_cparams("arbitrary"),
        name="norm_mod_inproj",
    )(x, mods_l, norm_g.reshape(1, d), *[w for w, _, _ in weights])


def _outproj_kernel(*refs, d, final_norm, n_prompt_blocks):
    x_ref, mod_ref, w_ref, fg_ref, o_ref = refs[-5:]

    def finish(a_ref):
        y = jnp.dot(a_ref[...], w_ref[...], preferred_element_type=F32)
        gate = mod_ref[:, 2 * d:3 * d]
        xn = x_ref[...] + gate * y
        if final_norm:
            ms = jnp.mean(xn * xn, axis=-1, keepdims=True)
            xn = xn * lax.rsqrt(ms + NORM_EPS) * fg_ref[...]
        o_ref[...] = xn

    if len(refs) == 6:
        finish(refs[0])
    else:
        pl.when(pl.program_id(0) < n_prompt_blocks)(functools.partial(finish, refs[0]))
        pl.when(pl.program_id(0) >= n_prompt_blocks)(functools.partial(finish, refs[1]))


def _outproj_call(act, x, mods_l, w_out, final_g, *, n_prompt, sample_len, final_norm, bm=512):
    n_tok, d = x.shape
    npb, bps = n_prompt // bm, sample_len // bm
    cond = functools.partial(_cond_of_block, n_prompt_blocks=npb, blocks_per_sample=bps)
    if isinstance(act, tuple):
        k = act[0].shape[1]
        acts = list(act)
        act_specs = [pl.BlockSpec((bm, k), lambda i: (jnp.minimum(i, npb - 1), 0)),
                     pl.BlockSpec((bm, k), lambda i: (jnp.maximum(i - npb, 0), 0))]
    else:
        k = act.shape[1]
        acts = [act]
        act_specs = [pl.BlockSpec((bm, k), lambda i: (i, 0))]
    return pl.pallas_call(
        functools.partial(_outproj_kernel, d=d, final_norm=final_norm, n_prompt_blocks=npb),
        out_shape=jax.ShapeDtypeStruct((n_tok, d), F32),
        grid=(n_tok // bm,),
        in_specs=act_specs + [
            pl.BlockSpec((bm, d), lambda i: (i, 0)),
            pl.BlockSpec((None, 1, 3 * d), lambda i: (cond(i), 0, 0)),
            pl.BlockSpec((k, d), lambda i: (0, 0)),
            pl.BlockSpec((1, d), lambda i: (0, 0)),
        ],
        out_specs=pl.BlockSpec((bm, d), lambda i: (i, 0)),
        compiler_params=_cparams("arbitrary"),
        name="outproj_residual",
    )(*acts, x, mods_l, w_out, final_g.reshape(1, d))


def _s5_time_of_lane_block():
    pos = np.arange(S5_CHUNK)
    half, blk = pos // 8, pos % 8
    g8 = np.arange(8)[:, None]
    return 8 * half[None, :] + (blk[None, :] - g8) % 8


def _s5_tables(lam_re, lam_im, log_step, b_re, b_im, c_re, c_im):
    t_chunk = S5_CHUNK
    n_groups, n_state = lam_re.shape[1], lam_re.shape[2]
    n_oct = n_groups // 8
    lam = lax.complex(lam_re.astype(F32), lam_im.astype(F32))
    step = jnp.exp(log_step.astype(F32))[..., None]
    lam_bar = jnp.exp(lam * step)
    b_bar = ((lam_bar - 1.0) / lam)[..., None] * lax.complex(b_re.astype(F32), b_im.astype(F32))
    c_mat = lax.complex(c_re.astype(F32), c_im.astype(F32))
    ks = jnp.arange(t_chunk + 1, dtype=F32)[:, None, None, None]
    pw = jnp.exp(ks * (lam * step)[None])

    zeros = jnp.zeros((t_chunk - 1, n_groups, n_state), pw.dtype)
    lag_f = jnp.concatenate([zeros, pw[:t_chunk, 0], zeros[:1]], axis=0)
    lag_b = jnp.concatenate([pw[t_chunk - 1::-1, 1], zeros, zeros[:1]], axis=0)
    plag = jnp.concatenate([lag_f, lag_b], axis=-1).transpose(1, 0, 2)
    plag = jnp.stack([plag.real, plag.imag])

    tl = _s5_time_of_lane_block()
    pw_ri = jnp.stack([pw.real, pw.imag]).reshape(2, t_chunk + 1, 2, n_oct, 8, n_state)
    m_idx = np.arange(t_chunk + 1)[None, None, :]

    def power_table(exponent, direction):
        sel = (exponent[:, :, None] == m_idx).astype(np.float32)
        tab = jnp.einsum('kxm,rmakp->rakxp', sel, pw_ri[:, :, direction], precision=HIGHEST)
        return tab.reshape(2, n_groups, t_chunk, n_state)

    def both(fwd, bwd):
        m = jnp.concatenate([fwd, bwd], axis=-1)
        return jnp.stack([m.real, m.imag])

    tin = jnp.concatenate([power_table(t_chunk - 1 - tl, 0), power_table(tl, 1)], axis=-1)
    tout = jnp.concatenate([power_table(tl + 1, 0), power_table(t_chunk - tl, 1)], axis=-1)
    bt = both(b_bar[0].transpose(0, 2, 1), b_bar[1].transpose(0, 2, 1))
    ct = both(c_mat[0], c_mat[1])
    lam_rows = both(pw[t_chunk, 0][:, None], pw[t_chunk, 1][:, None])[:, :, 0]
    return plag, tin, tout, bt, ct, lam_rows


def _s5_kmat_kernel(plag_ref, tin_ref, tout_ref, bt_ref, ct_ref, k_ref, pin_ref, pot_ref, x_scr, v_scr):
    masks = _lane_block_masks()
    n_lag = 2 * S5_CHUNK - 1
    for g8 in range(8):
        br, bi = bt_ref[0, g8], bt_ref[1, g8]
        cr, ci = ct_ref[0, g8], ct_ref[1, g8]
        def split(a):
            hi = a.astype(BF16)
            return hi, (a - hi.astype(F32)).astype(BF16)

        for m in range(n_lag):
            rows = slice(m * S5_GROUP, (m + 1) * S5_GROUP)
            pr, pi = plag_ref[0, g8, m:m + 1, :], plag_ref[1, g8, m:m + 1, :]
            for c0, part in ((0, cr * pr - ci * pi), (LANES, -(cr * pi + ci * pr))):
                x_scr[0, rows, c0:c0 + LANES], x_scr[1, rows, c0:c0 + LANES] = split(part)
        b_hi, b_lo = split(jnp.concatenate([jnp.concatenate([br, bi], axis=1)] * (LANES // S5_GROUP), axis=0))
        nt = functools.partial(lax.dot_general, dimension_numbers=(((1,), (1,)), ((), ())),
                               preferred_element_type=F32)
        v_scr[...] = nt(x_scr[0], b_hi) + nt(x_scr[0], b_lo) + nt(x_scr[1], b_hi)
        for pos in range(S5_CHUNK):
            rows = slice(pos * S5_GROUP, (pos + 1) * S5_GROUP)
            tr, ti = tin_ref[0, g8, pos:pos + 1, :], tin_ref[1, g8, pos:pos + 1, :]
            pin_ref[g8, rows, 0:LANES] = (tr * br - ti * bi).astype(BF16)
            pin_ref[g8, rows, LANES:2 * LANES] = (tr * bi + ti * br).astype(BF16)
            tr, ti = tout_ref[0, g8, pos:pos + 1, :], tout_ref[1, g8, pos:pos + 1, :]
            pot_ref[g8, rows, 0:LANES] = (tr * cr - ti * ci).astype(BF16)
            pot_ref[g8, rows, LANES:2 * LANES] = (-(tr * ci + ti * cr)).astype(BF16)
        for pos in range(S5_CHUNK):
            tau = 8 * (pos // 8) + (pos % 8 - g8) % 8
            rows = slice(pos * S5_GROUP, (pos + 1) * S5_GROUP)
            for half in range(2):
                acc = None
                for blk in range(8):
                    sigma = 8 * half + (blk - g8) % 8
                    m = S5_CHUNK - 1 - sigma + tau
                    src = v_scr[m * S5_GROUP:(m + 1) * S5_GROUP, :]
                    acc = src if acc is None else jnp.where(masks[blk], src, acc)
                k_ref[g8, rows, half * LANES:(half + 1) * LANES] = acc.astype(BF16)


def _s5_kmat_call(plag, tin, tout, bt, ct):
    n_groups = plag.shape[1]
    n_lag_rows = (2 * S5_CHUNK - 1) * S5_GROUP
    lag_spec = pl.BlockSpec((2, 8) + plag.shape[2:], lambda i: (0, i, 0, 0))
    tab_spec = pl.BlockSpec((2, 8, S5_GROUP, LANES), lambda i: (0, i, 0, 0))
    mat = jax.ShapeDtypeStruct((n_groups, 2 * LANES, 2 * LANES), BF16)
    mat_spec = pl.BlockSpec((8, 2 * LANES, 2 * LANES), lambda i: (i, 0, 0))
    return pl.pallas_call(
        _s5_kmat_kernel,
        out_shape=[mat, mat, mat],
        grid=(n_groups // 8,),
        in_specs=[lag_spec, tab_spec, tab_spec, tab_spec, tab_spec],
        out_specs=[mat_spec, mat_spec, mat_spec],
        scratch_shapes=[pltpu.VMEM((2, n_lag_rows, 2 * LANES), BF16), pltpu.VMEM((n_lag_rows, LANES), F32)],
        compiler_params=_cparams("arbitrary"),
        name="s5_kmat",
    )(plag, tin, tout, bt, ct)


def _lane_block_masks():
    blk = lax.broadcasted_iota(jnp.int32, (1, LANES), 1) // S5_GROUP
    return [blk == b for b in range(8)]


def _diagonal_merge(src):
    blk = lax.broadcasted_iota(jnp.int32, (1, LANES), 1) // S5_GROUP
    q = list(src)
    for bit in (1, 2, 4):
        take = (blk & bit) != 0
        q = [jnp.where(take, q[(x + bit) % 8], q[x]) for x in range(8)]
    return [q[(-t) % 8] for t in range(8)]


def _s5_to_chunks_kernel(u_ref, x_ref, *, rows):
    for o in range(u_ref.shape[0]):
        for r0 in range(0, rows, SUBLANES):
            for half in range(2):
                rolled = []
                for t8 in range(8):
                    v = u_ref[o, pl.ds(r0 * S5_CHUNK + 8 * half + t8, SUBLANES, stride=S5_CHUNK), :]
                    rolled.append(pltpu.roll(v, t8 * S5_GROUP, 1) if t8 else v)
                for g8, merged in enumerate(_diagonal_merge(rolled)):
                    x_ref[o * 8 + g8, r0:r0 + SUBLANES, half * LANES:(half + 1) * LANES] = merged.astype(BF16)


def _s5_from_chunks_kernel(y_ref, u_ref, d_ref, o_ref, nat_scr, *, rows):
    tile = 2 * SUBLANES
    for o in range(u_ref.shape[0]):
        for r0 in range(0, rows, tile):
            for half in range(2):
                src = [y_ref[o * 8 + g8, r0:r0 + tile, half * LANES:(half + 1) * LANES].astype(F32)
                       for g8 in range(8)]
                for t8, merged in enumerate(_diagonal_merge(src)):
                    nat = pltpu.roll(merged, (8 - t8) * S5_GROUP, 1) if t8 else merged
                    nat_scr[pl.ds(r0 * S5_CHUNK + 8 * half + t8, tile, stride=S5_CHUNK), :] = nat
        d_vec = d_ref[:, o * LANES:(o + 1) * LANES]
        o_ref[o] = _gelu_tanh(nat_scr[...] + d_vec * u_ref[o].astype(F32)).astype(o_ref.dtype)


def _s5_from_chunks_call(yc, u3, d_skip, *, rows=32):
    n_blk, n_tok, _ = u3.shape
    n_groups, n_rows, _ = yc.shape
    tok_spec = pl.BlockSpec((n_blk, rows * S5_CHUNK, LANES), lambda i: (0, i, 0))
    return pl.pallas_call(
        functools.partial(_s5_from_chunks_kernel, rows=rows),
        out_shape=jax.ShapeDtypeStruct(u3.shape, BF16),
        grid=(n_rows // rows,),
        in_specs=[pl.BlockSpec((n_groups, rows, 2 * LANES), lambda i: (0, i, 0)),
                  tok_spec,
                  pl.BlockSpec((1, n_blk * LANES), lambda i: (0, 0))],
        out_specs=tok_spec,
        scratch_shapes=[pltpu.VMEM((rows * S5_CHUNK, LANES), F32)],
        compiler_params=_cparams("arbitrary"),
        name="s5_from_chunks",
    )(yc, u3, d_skip.reshape(1, n_blk * LANES).astype(F32))


def _s5_chunk_kernel(x_ref, kt_ref, pin_ref, po_ref, lam_ref, h0r_ref, h0i_ref,
                     y_ref, fr_ref, fi_ref, r_scr, st_scr, yi_scr, *, segments, seq_block):
    gb = SUBLANES
    rows = x_ref.shape[1]
    lane = lax.broadcasted_iota(jnp.int32, (1, LANES), 1)
    fwd_lanes = lane < (LANES // 2)
    for g in range(gb):
        x = x_ref[g]
        yi_scr[g] = lax.dot_general(x, kt_ref[g], (((1,), (1,)), ((), ())), preferred_element_type=F32)
        r = jnp.dot(x, pin_ref[g], preferred_element_type=F32)
        of_group = pl.ds(g, rows, stride=gb)
        r_scr[0, of_group, :] = r[:, 0:LANES]
        r_scr[1, of_group, :] = r[:, LANES:2 * LANES]
    ar, ai = lam_ref[0], lam_ref[1]
    for row0, n_seq, n_chunks, from_input, to_output in segments:
        for b0 in range(0, n_seq, seq_block):
            nb = min(seq_block, n_seq - b0)

            def step(i, carry, row0=row0, n_chunks=n_chunks, b0=b0, nb=nb):
                out = []
                for k in range(nb):
                    base = row0 + (b0 + k) * n_chunks
                    at_f = pl.ds(pl.multiple_of((base + i) * gb, gb), gb)
                    at_b = pl.ds(pl.multiple_of((base + (n_chunks - 1) - i) * gb, gb), gb)
                    s_re, s_im = carry[k]
                    half = LANES // 2
                    st_scr[0, at_f, 0:half] = s_re[:, 0:half]
                    st_scr[0, at_b, half:LANES] = s_re[:, half:LANES]
                    st_scr[1, at_f, 0:half] = s_im[:, 0:half]
                    st_scr[1, at_b, half:LANES] = s_im[:, half:LANES]
                    v_re = jnp.where(fwd_lanes, r_scr[0, at_f, :], r_scr[0, at_b, :])
                    v_im = jnp.where(fwd_lanes, r_scr[1, at_f, :], r_scr[1, at_b, :])
                    out.append((ar * s_re - ai * s_im + v_re, ar * s_im + ai * s_re + v_im))
                return tuple(out)

            if from_input:
                init = tuple((h0r_ref[b0 + k], h0i_ref[b0 + k]) for k in range(nb))
            else:
                init = tuple((jnp.zeros((gb, LANES), F32),) * 2 for _ in range(nb))
            fin = lax.fori_loop(0, n_chunks, step, init, unroll=2)
            if to_output:
                for k in range(nb):
                    fr_ref[b0 + k] = fin[k][0]
                    fi_ref[b0 + k] = fin[k][1]
    for g in range(gb):
        of_group = pl.ds(g, rows, stride=gb)
        st = jnp.concatenate([st_scr[cb, of_group, :] for cb in range(2)], axis=1).astype(BF16)
        y_ref[g] = (yi_scr[g] + lax.dot_general(st, po_ref[g], (((1,), (1,)), ((), ())),
                                                preferred_element_type=F32)).astype(y_ref.dtype)


def _s5_chunk_call(xc, kt, pin, pout, lam_rows, h0_re, h0_im, *, layer, segments, n_final):
    n_groups, rows, _ = xc.shape
    gb = SUBLANES
    s_in = h0_re.shape[0]
    kern = functools.partial(_s5_chunk_kernel, segments=segments, seq_block=8)
    g3 = lambda i: (i, 0, 0)
    blk0 = layer * (n_groups // gb)
    p3 = lambda i: (i + blk0, 0, 0)
    mid = lambda i: (0, i, 0)
    return pl.pallas_call(
        kern,
        out_shape=[jax.ShapeDtypeStruct((n_groups, rows, 2 * LANES), BF16),
                   jax.ShapeDtypeStruct((n_final, n_groups, LANES), F32),
                   jax.ShapeDtypeStruct((n_final, n_groups, LANES), F32)],
        grid=(n_groups // gb,),
        in_specs=[
            pl.BlockSpec((gb, rows, 2 * LANES), g3),
            pl.BlockSpec((gb, 2 * LANES, 2 * LANES), p3),
            pl.BlockSpec((gb, 2 * LANES, 2 * LANES), p3),
            pl.BlockSpec((gb, 2 * LANES, 2 * LANES), p3),
            pl.BlockSpec((2, gb, LANES), lambda i: (0, i + blk0, 0)),
            pl.BlockSpec((s_in, gb, LANES), mid),
            pl.BlockSpec((s_in, gb, LANES), mid),
        ],
        out_specs=[pl.BlockSpec((gb, rows, 2 * LANES), g3),
                   pl.BlockSpec((n_final, gb, LANES), mid),
                   pl.BlockSpec((n_final, gb, LANES), mid)],
        scratch_shapes=[pltpu.VMEM((2, rows * gb, LANES), F32),
                        pltpu.VMEM((2, rows * gb, LANES), F32),
                        pltpu.VMEM((gb, rows, 2 * LANES), F32)],
        compiler_params=_cparams("arbitrary"),
        name="s5_chunk_scan",
    )(xc, kt, pin, pout, lam_rows, h0_re, h0_im)


def _glu_kernel(y_ref, z_ref, w_ref, b_ref, o_ref, *, n_chunk):
    n_blk = y_ref.shape[0]
    yb = jnp.concatenate([y_ref[o].astype(BF16) for o in range(n_blk)], axis=1)
    per = n_chunk // LANES
    for c in range(0, n_blk, per):
        sl = slice(c * LANES, (c + per) * LANES)
        gate = _sigmoid(jnp.dot(yb, w_ref[:, sl], preferred_element_type=F32) + b_ref[:, sl])
        y = jnp.concatenate([y_ref[c + k] for k in range(per)], axis=1).astype(F32)
        o_ref[:, sl] = (y * gate * _silu(z_ref[:, sl].astype(F32))).astype(o_ref.dtype)


def _glu_call(y3, z, glu_w, glu_b, *, bm=512):
    n_blk, n_tok, _ = y3.shape
    width = n_blk * LANES
    return pl.pallas_call(
        functools.partial(_glu_kernel, n_chunk=min(512, width)),
        out_shape=jax.ShapeDtypeStruct((n_tok, width), BF16),
        grid=(n_tok // bm,),
        in_specs=[
            pl.BlockSpec((n_blk, bm, LANES), lambda i: (0, i, 0)),
            pl.BlockSpec((bm, width), lambda i: (i, 0)),
            pl.BlockSpec((width, width), lambda i: (0, 0)),
            pl.BlockSpec((1, width), lambda i: (0, 0)),
        ],
        out_specs=pl.BlockSpec((bm, width), lambda i: (i, 0)),
        compiler_params=_cparams("arbitrary"),
        name="s5_glu_gate",
    )(y3, z, glu_w, glu_b.reshape(1, width).astype(F32))


def _s5_prep_all(lam_re, lam_im, log_step, b_re, b_im, c_re, c_im):
    tabs = jax.vmap(_s5_tables)(lam_re, lam_im, log_step, b_re, b_im, c_re, c_im)
    plag, tin, tout, bt, ct, lam_rows = [jnp.moveaxis(t, 0, 1).reshape((2, -1) + t.shape[3:]) for t in tabs]
    kt, pin, pot = _s5_kmat_call(plag, tin, tout, bt, ct)
    return kt, pin, pot, lam_rows


def _s5_mix(u3, xc, z, layer, mats, d_skip, glu_w, glu_b, st_re, st_im, *, n_prompt_seq, prompt_len,
            n_sample_seq, sample_len, bm):
    kt, pin, pout, lam_rows = mats
    n_state = LANES // 2
    pc, sc = prompt_len // S5_CHUNK, sample_len // S5_CHUNK

    def state_rows(s):
        return jnp.concatenate([s[:, 0], s[:, 1]], axis=-1).astype(F32)

    segments = ((0, n_prompt_seq, pc, False, True), (n_prompt_seq * pc, n_sample_seq, sc, True, False))
    yc, fr, fi = _s5_chunk_call(xc, kt, pin, pout, lam_rows, state_rows(st_re), state_rows(st_im),
                                layer=layer, segments=segments, n_final=n_prompt_seq)
    y3 = _s5_from_chunks_call(yc, u3, d_skip)
    act = _glu_call(y3, z, glu_w, glu_b, bm=bm)

    def unpack(f):
        return jnp.stack([f[:, :, :n_state], f[:, :, n_state:]], axis=1)

    return act, unpack(fr), unpack(fi)


def _pool_kernel(u_ref, z_ref, w_ref, s_ref, o_ref, *, n_prompt_blocks, prompt_len, sample_len):
    rows = u_ref.shape[0]
    seq_len = jnp.where(pl.program_id(0) < n_prompt_blocks, prompt_len, sample_len)
    t = lax.broadcasted_iota(jnp.int32, (rows, 1), 0) & (seq_len - 1)

    def later(x, k):
        return jnp.where(t + k < seq_len, pltpu.roll(x, rows - k, 0), 0.0)

    def earlier(x, k):
        return jnp.where(t >= k, pltpu.roll(x, k, 0), 0.0)

    def body(win):
        lo = win // 2
        u = u_ref[...]
        fwd = u
        bwd = earlier(u, 1)
        s = 1
        while s < lo:
            fwd = fwd + later(fwd, s)
            bwd = bwd + earlier(bwd, s)
            s *= 2
        cnt = jnp.minimum(t - lo + win, seq_len) - jnp.maximum(t - lo, 0)
        p = (fwd + bwd) / cnt.astype(F32) - u
        m = jnp.dot(p.astype(BF16), w_ref[...], preferred_element_type=F32) * s_ref[...]
        o_ref[...] = (m * _silu(z_ref[...].astype(F32))).astype(o_ref.dtype)

    for gi, win in enumerate(POOL_WINDOWS):
        pl.when(pl.program_id(1) == gi)(functools.partial(body, win))


def _pool_call(u, z, pool_w, pool_scale, *, n_prompt, prompt_len, sample_len, rows=2048):
    n_tok, width = u.shape
    n_groups = len(POOL_WINDOWS)
    cg = width // n_groups
    assert prompt_len & (prompt_len - 1) == 0 and sample_len & (sample_len - 1) == 0
    assert rows % prompt_len == 0 and rows % sample_len == 0 and n_prompt % rows == 0
    kern = functools.partial(_pool_kernel, n_prompt_blocks=n_prompt // rows, prompt_len=prompt_len,
                             sample_len=sample_len)
    return pl.pallas_call(
        kern,
        out_shape=jax.ShapeDtypeStruct((n_tok, width), BF16),
        grid=(n_tok // rows, n_groups),
        in_specs=[
            pl.BlockSpec((rows, cg), lambda i, g: (i, g)),
            pl.BlockSpec((rows, cg), lambda i, g: (i, g)),
            pl.BlockSpec((None, cg, cg), lambda i, g: (g, 0, 0)),
            pl.BlockSpec((1, cg), lambda i, g: (0, g)),
        ],
        out_specs=pl.BlockSpec((rows, cg), lambda i, g: (i, g)),
        compiler_params=_cparams("arbitrary", "arbitrary"),
        name="pool_mix",
    )(u, z, pool_w, pool_scale.reshape(1, width).astype(F32))


MLA_QW = 2 * LANES

_ROT_SRC = np.concatenate([np.arange(16, 32), np.arange(0, 16), np.arange(48, 64), np.arange(32, 48)])
_ROT_SIGN = np.concatenate([-np.ones(16), np.ones(16), -np.ones(16), np.ones(16)]).astype(np.float32)


def _rope_tables(n_prompt, n_sample_seq, sample_len):
    half = MLA_ROPE // 4
    tok = jnp.arange(sample_len)
    row = (tok // GRID_W).astype(F32)
    col = (tok % GRID_W).astype(F32)
    inv = ROPE_THETA ** (-jnp.arange(half, dtype=F32) / half)
    a_row, a_col = row[:, None] * inv, col[:, None] * inv
    cos = jnp.concatenate([jnp.cos(a_row), jnp.cos(a_row), jnp.cos(a_col), jnp.cos(a_col)], axis=-1)
    sin = jnp.concatenate([jnp.sin(a_row), jnp.sin(a_row), jnp.sin(a_col), jnp.sin(a_col)], axis=-1)
    pad = jnp.zeros((sample_len, LANES - MLA_ROPE), F32)
    cos_s = jnp.tile(jnp.concatenate([cos, pad], axis=-1), (n_sample_seq, 1))
    sin_s = jnp.tile(jnp.concatenate([sin, pad], axis=-1), (n_sample_seq, 1))
    cos_p = jnp.concatenate([jnp.ones((n_prompt, MLA_ROPE), F32), jnp.zeros((n_prompt, LANES - MLA_ROPE), F32)], -1)
    return jnp.concatenate([cos_p, cos_s]), jnp.concatenate([jnp.zeros((n_prompt, LANES), F32), sin_s])


def _rms(x, g):
    return x * lax.rsqrt(jnp.mean(x * x, axis=-1, keepdims=True) + NORM_EPS) * g


def _mla_post_kernel(sm_ref, cos_ref, sin_ref, qn_ref, kn_ref, wa_ref, wb_ref,
                     q_ref, ckv_ref, kpe_ref, *, q_rank, kv_rank, heads_per_dot):
    cosp, sinp = cos_ref[...], sin_ref[...]
    qn = _rms(sm_ref[:, 0:q_rank], qn_ref[...]).astype(BF16)
    for h0 in range(0, MLA_HEADS, heads_per_dot):
        a = jnp.dot(qn, wa_ref[:, h0 * MLA_QW:(h0 + heads_per_dot) * MLA_QW], preferred_element_type=F32)
        b = jnp.dot(qn, wb_ref[:, h0 * LANES:(h0 + heads_per_dot) * LANES], preferred_element_type=F32)
        for j in range(heads_per_dot):
            h = h0 + j
            q_ref[:, h * MLA_QW:h * MLA_QW + LANES] = a[:, j * MLA_QW:j * MLA_QW + LANES].astype(BF16)
            pe = a[:, j * MLA_QW + LANES:(j + 1) * MLA_QW] * cosp + b[:, j * LANES:(j + 1) * LANES] * sinp
            q_ref[:, h * MLA_QW + LANES:(h + 1) * MLA_QW] = pe.astype(BF16)
    c0 = q_rank
    ckv_ref[...] = _rms(sm_ref[:, c0:c0 + kv_rank], kn_ref[...])
    k0 = c0 + kv_rank
    kpe_ref[...] = (sm_ref[:, k0:k0 + LANES] * cosp + sm_ref[:, k0 + LANES:k0 + 2 * LANES] * sinp).astype(BF16)


def _mla_post_call(small, cos_t, sin_t, q_norm, kv_norm, wq_a, wq_b, *, bm=512):
    n_tok, ws = small.shape
    q_rank, kv_rank = q_norm.shape[-1], kv_norm.shape[-1]
    row = lambda i: (i, 0)
    fix = lambda i: (0, 0)
    kern = functools.partial(_mla_post_kernel, q_rank=q_rank, kv_rank=kv_rank, heads_per_dot=4)
    return pl.pallas_call(
        kern,
        out_shape=[jax.ShapeDtypeStruct((n_tok, MLA_HEADS * MLA_QW), BF16),
                   jax.ShapeDtypeStruct((n_tok, kv_rank), F32),
                   jax.ShapeDtypeStruct((n_tok, LANES), BF16)],
        grid=(n_tok // bm,),
        in_specs=[
            pl.BlockSpec((bm, ws), row),
            pl.BlockSpec((bm, LANES), row),
            pl.BlockSpec((bm, LANES), row),
            pl.BlockSpec((1, q_rank), fix),
            pl.BlockSpec((1, kv_rank), fix),
            pl.BlockSpec(wq_a.shape, fix),
            pl.BlockSpec(wq_b.shape, fix),
        ],
        out_specs=[pl.BlockSpec((bm, MLA_HEADS * MLA_QW), row),
                   pl.BlockSpec((bm, kv_rank), row),
                   pl.BlockSpec((bm, LANES), row)],
        compiler_params=_cparams("arbitrary"),
        name="mla_q_rope",
    )(small, cos_t, sin_t, q_norm.reshape(1, q_rank).astype(F32), kv_norm.reshape(1, kv_rank).astype(F32),
      wq_a, wq_b)


def _kv_expand_kernel(c_ref, w_ref, o_ref, *, n_chunk):
    c = c_ref[...].astype(BF16)
    n = w_ref.shape[1]
    for s in range(0, n, n_chunk):
        o_ref[:, s:s + n_chunk] = jnp.dot(c, w_ref[:, s:s + n_chunk],
                                          preferred_element_type=F32).astype(o_ref.dtype)


def _kv_expand_call(ckv, wkv_b, *, bm=512):
    rows, kr = ckv.shape
    n = wkv_b.shape[1]
    return pl.pallas_call(
        functools.partial(_kv_expand_kernel, n_chunk=1024),
        out_shape=jax.ShapeDtypeStruct((rows, n), BF16),
        grid=(rows // bm,),
        in_specs=[pl.BlockSpec((bm, kr), lambda i: (i, 0)), pl.BlockSpec((kr, n), lambda i: (0, 0))],
        out_specs=pl.BlockSpec((bm, n), lambda i: (i, 0)),
        compiler_params=_cparams("arbitrary"),
        name="mla_kv_expand",
    )(ckv, wkv_b)


def _attn_kernel(q_ref, kv_ref, kpe_ref, z_ref, o_ref, kcat_scr, vext_scr, *, hg, scale):
    c2 = scale * math.log2(math.e)

    @pl.when(pl.program_id(2) == 0)
    def _():
        ones = jnp.ones((kv_ref.shape[0], LANES), BF16)
        for j in range(hg):
            kcat_scr[j, :, 0:LANES] = kv_ref[:, j * 2 * LANES:j * 2 * LANES + LANES]
            kcat_scr[j, :, LANES:2 * LANES] = kpe_ref[...]
            vext_scr[j, :, 0:LANES] = kv_ref[:, j * 2 * LANES + LANES:(j + 1) * 2 * LANES]
            vext_scr[j, :, LANES:2 * LANES] = ones

    for j in range(hg):
        q = q_ref[:, j * MLA_QW:(j + 1) * MLA_QW]
        s = lax.dot_general(q, kcat_scr[j], (((1,), (1,)), ((), ())), preferred_element_type=F32)
        e = jnp.exp2((s - jnp.max(s, axis=-1, keepdims=True)) * c2)
        pv = jnp.dot(e.astype(BF16), vext_scr[j], preferred_element_type=F32)
        zs = slice(j * MLA_V, (j + 1) * MLA_V)
        o = pv[:, 0:MLA_V] / pv[:, MLA_V:2 * MLA_V]
        o_ref[:, zs] = (o * _silu(z_ref[:, zs].astype(F32))).astype(o_ref.dtype)


def _attn_call(q, kv, kpe, z, *, q_row0, n_seq, q_len, k_len, hg, qb):
    width = MLA_HEADS * MLA_V
    nqb = q_len // qb
    qb0 = q_row0 // qb
    assert q_row0 % qb == 0 and q_len % qb == 0
    scale = float((MLA_NOPE + MLA_ROPE) ** -0.5)
    qrow = lambda b, g, i: (qb0 + b * nqb + i, g)
    return pl.pallas_call(
        functools.partial(_attn_kernel, hg=hg, scale=scale),
        out_shape=jax.ShapeDtypeStruct((n_seq * q_len, width), BF16),
        grid=(n_seq, MLA_HEADS // hg, nqb),
        in_specs=[
            pl.BlockSpec((qb, hg * MLA_QW), qrow),
            pl.BlockSpec((k_len, hg * 2 * LANES), lambda b, g, i: (b, g)),
            pl.BlockSpec((k_len, LANES), lambda b, g, i: (b, 0)),
            pl.BlockSpec((qb, hg * MLA_V), qrow),
        ],
        out_specs=pl.BlockSpec((qb, hg * MLA_V), lambda b, g, i: (b * nqb + i, g)),
        scratch_shapes=[pltpu.VMEM((hg, k_len, MLA_QW), BF16),
                        pltpu.VMEM((hg, k_len, 2 * MLA_V), BF16)],
        compiler_params=_cparams("arbitrary", "arbitrary", "arbitrary"),
        name="mla_attention",
    )(q, kv, kpe, z)


def _mla_weights(w_in, wq_b):
    q_rank = wq_b.shape[0]
    kv_rank = w_in.shape[1] - q_rank - MLA_ROPE - MLA_HEADS * MLA_V
    d = w_in.shape[0]
    c_kpe = q_rank + kv_rank
    zpad = jnp.zeros((d, LANES - MLA_ROPE), w_in.dtype)
    kpe_w = w_in[:, c_kpe:c_kpe + MLA_ROPE]
    w_small = jnp.concatenate([w_in[:, :c_kpe], kpe_w, zpad,
                               kpe_w[:, _ROT_SRC] * _ROT_SIGN, zpad], axis=1)
    w_z = w_in[:, c_kpe + MLA_ROPE:]
    hd = MLA_NOPE + MLA_ROPE
    wq3 = wq_b.reshape(q_rank, MLA_HEADS, hd)
    pe = wq3[:, :, MLA_NOPE:]
    z3 = jnp.zeros((q_rank, MLA_HEADS, LANES - MLA_ROPE), wq_b.dtype)
    wq_a = jnp.concatenate([wq3, z3], axis=-1).reshape(q_rank, MLA_HEADS * MLA_QW)
    wq_r = jnp.concatenate([pe[:, :, _ROT_SRC] * _ROT_SIGN, z3], axis=-1).reshape(q_rank, MLA_HEADS * LANES)
    return w_small.astype(BF16), w_z.astype(BF16), wq_a.astype(BF16), wq_r.astype(BF16)


def kernel(x_prompt, x_sample, state_s5_re, state_s5_im, cache_ckv, cache_kpe, c, c_ctx, norm_g, ada_w, ada_b, final_norm_g, s5_w_in, s5_lam_re, s5_lam_im, s5_log_step, s5_b_re, s5_b_im, s5_c_re, s5_c_im, s5_d, s5_glu_w, s5_glu_b, s5_w_out, pool_w_in, pool_w, pool_scale, pool_w_out, mla_w_in, mla_q_norm, mla_wq_b, mla_kv_norm, mla_wkv_b, mla_w_out):
    n_pseq, p_len, d = x_prompt.shape
    n_sseq, s_len, _ = x_sample.shape
    depth = norm_g.shape[0]
    n_prompt = n_pseq * p_len
    bm = 512
    geo = dict(n_prompt=n_prompt, sample_len=s_len, bm=bm)

    x = jnp.concatenate([x_prompt.reshape(n_prompt, d), x_sample.reshape(n_sseq * s_len, d)], axis=0)
    conds = jnp.concatenate([c_ctx[None, :], c, jnp.zeros((SUBLANES - 1 - n_sseq, d), F32)], axis=0)
    mods = _ada_call(conds.astype(F32), ada_w, ada_b)
    mods = mods.reshape(depth, SUBLANES, 1, 3 * d)
    s5_mats = _s5_prep_all(s5_lam_re, s5_lam_im, s5_log_step, s5_b_re, s5_b_im, s5_c_re, s5_c_im)

    new_re, new_im, new_ckv, new_kpe = [], [], [], []
    for layer in range(depth):
        kind, j = layer % N_MIXERS, layer // N_MIXERS
        last = layer == depth - 1
        ml = mods[layer]
        if kind == 0:
            width = s5_w_in.shape[2] // 2
            w = s5_w_in[j].astype(BF16)
            u3, z, xc = _inproj_call(x, ml, norm_g[layer], [(w, 0, width), (w, 1, width)], [BF16, BF16],
                                     lane_blocked=(0,), s5_chunks=True, **geo)
            act, f_re, f_im = _s5_mix(u3, xc, z, j, s5_mats, s5_d[j], s5_glu_w[j].astype(BF16), s5_glu_b[j],
                                      state_s5_re[:, j], state_s5_im[:, j], n_prompt_seq=n_pseq,
                                      prompt_len=p_len, n_sample_seq=n_sseq, sample_len=s_len, bm=bm)
            new_re.append(f_re)
            new_im.append(f_im)
            w_out = s5_w_out[j]
        elif kind == 1:
            width = pool_w_in.shape[2] // 2
            w = pool_w_in[j].astype(BF16)
            u, z = _inproj_call(x, ml, norm_g[layer], [(w, 0, width), (w, 1, width)], [F32, BF16], **geo)
            act = _pool_call(u, z, pool_w[j].astype(BF16), pool_scale[j], n_prompt=n_prompt,
                             prompt_len=p_len, sample_len=s_len)
            w_out = pool_w_out[j]
        else:
            q_rank, kv_rank = mla_q_norm.shape[-1], mla_kv_norm.shape[-1]
            w_small, w_z, wq_a, wq_r = _mla_weights(mla_w_in[j], mla_wq_b[j])
            small, z = _inproj_call(x, ml, norm_g[layer], [w_small, w_z], [F32, BF16], **geo)
            cos_t, sin_t = _rope_tables(n_prompt, n_sseq, s_len)
            q, ckv_n, kpe_k = _mla_post_call(small, cos_t, sin_t, mla_q_norm[j], mla_kv_norm[j], wq_a, wq_r,
                                             bm=bm)
            wkv = mla_wkv_b[j].astype(BF16)
            past = cache_ckv.shape[2]
            k_len = past + s_len
            ckv_s = jnp.concatenate([cache_ckv[:, j].astype(F32), ckv_n[n_prompt:].reshape(n_sseq, s_len, kv_rank)],
                                    axis=1).reshape(n_sseq * k_len, kv_rank)
            kpe_cache = jnp.concatenate([cache_kpe[:, j].astype(BF16),
                                         jnp.zeros((n_sseq, past, LANES - MLA_ROPE), BF16)], axis=-1)
            kpe_s = jnp.concatenate([kpe_cache, kpe_k[n_prompt:].reshape(n_sseq, s_len, LANES)],
                                    axis=1).reshape(n_sseq * k_len, LANES)
            kv_p = _kv_expand_call(ckv_n[:n_prompt], wkv, bm=bm)
            kv_s = _kv_expand_call(ckv_s, wkv, bm=bm)
            act = (_attn_call(q, kv_p, kpe_k[:n_prompt], z, q_row0=0, n_seq=n_pseq, q_len=p_len,
                              k_len=p_len, hg=MLA_HEADS, qb=p_len),
                   _attn_call(q, kv_s, kpe_s, z, q_row0=n_prompt, n_seq=n_sseq, q_len=s_len,
                              k_len=k_len, hg=4, qb=256))
            new_ckv.append(ckv_n[:n_prompt].reshape(n_pseq, p_len, kv_rank))
            c_kpe = q_rank + kv_rank
            new_kpe.append(small[:n_prompt, c_kpe:c_kpe + MLA_ROPE].reshape(n_pseq, p_len, MLA_ROPE))
            w_out = mla_w_out[j]
        x = _outproj_call(act, x, ml, w_out.astype(BF16), final_norm_g, final_norm=last, **geo)

    y_prompt = x[:n_prompt].reshape(n_pseq, p_len, d)
    y_sample = x[n_prompt:].reshape(n_sseq, s_len, d)
    return (y_prompt, y_sample, jnp.stack(new_re, axis=1), jnp.stack(new_im, axis=1),
            jnp.stack(new_ckv, axis=1), jnp.stack(new_kpe, axis=1))
```

```python
import functools
import math

import jax
import jax.numpy as jnp
import numpy as np
from jax import lax
from jax.experimental import pallas as pl
from jax.experimental.pallas import tpu as pltpu

S5_GROUP = 16
S5_CHUNK = 16
POOL_WINDOWS = (2, 4, 8, 16)
MLA_HEADS = 16
MLA_NOPE = 128
MLA_ROPE = 64
MLA_V = 128
GRID_W = 64
ROPE_THETA = 10000.0
NORM_EPS = 1e-6
N_MIXERS = 3

LANES = 128
SUBLANES = 8
VMEM_LIMIT_BYTES = 56 * 1024 * 1024

F32 = jnp.float32
BF16 = jnp.bfloat16
HIGHEST = lax.Precision.HIGHEST


def _cparams(*sem):
    return pltpu.CompilerParams(dimension_semantics=sem, vmem_limit_bytes=VMEM_LIMIT_BYTES)


def _sigmoid(x):
    return 1.0 / (1.0 + jnp.exp(-x))


def _silu(x):
    return x * _sigmoid(x)


def _gelu_tanh(x):
    c = math.sqrt(2.0 / math.pi)
    hx = 0.5 * x
    return hx + hx * jnp.tanh(x * (c + (c * 0.044715) * (x * x)))


def _ada_kernel(c_ref, w_ref, b_ref, o_ref):
    a = _silu(c_ref[...])
    o_ref[...] = jnp.dot(a, w_ref[...], preferred_element_type=F32, precision=HIGHEST) + b_ref[...]


def _ada_call(conds, ada_w, ada_b):
    depth, d, d3 = ada_w.shape
    c8 = conds.shape[0]
    tn = 512
    return pl.pallas_call(
        _ada_kernel,
        out_shape=jax.ShapeDtypeStruct((depth, c8, d3), F32),
        grid=(depth, d3 // tn),
        in_specs=[
            pl.BlockSpec((c8, d), lambda l, n: (0, 0)),
            pl.BlockSpec((None, d, tn), lambda l, n: (l, 0, n)),
            pl.BlockSpec((None, 1, tn), lambda l, n: (l, 0, n)),
        ],
        out_specs=pl.BlockSpec((None, c8, tn), lambda l, n: (l, 0, n)),
        compiler_params=_cparams("arbitrary", "arbitrary"),
        name="ada_mod",
    )(conds, ada_w, ada_b.reshape(depth, 1, d3))


def _cond_of_block(i, n_prompt_blocks, blocks_per_sample):
    return jnp.where(i < n_prompt_blocks, 0, 1 + (i - n_prompt_blocks) // blocks_per_sample)


def _modulated(x, mod_ref, g_ref, d):
    ms = jnp.mean(x * x, axis=-1, keepdims=True)
    y = x * lax.rsqrt(ms + NORM_EPS) * g_ref[...]
    shift = mod_ref[:, 0:d]
    scale = mod_ref[:, d:2 * d]
    return (y * (1.0 + scale) + shift).astype(BF16)


def _inproj_kernel(*refs, d, n_chunk, chunk_rows, n_prompt_blocks, n_x):
    if n_x == 2:
        x = jnp.where(pl.program_id(0) < n_prompt_blocks, refs[0][...], refs[1][...])
    else:
        x = refs[0][...]
    mod_ref, g_ref = refs[n_x], refs[n_x + 1]
    rest = refs[n_x + 2:]
    u_scr = None
    if chunk_rows:
        rest, xc_ref, u_scr = rest[:-2], rest[-2], rest[-1]
    n_out = len(rest) // 2
    w_refs, o_refs = rest[:n_out], rest[n_out:]
    h = _modulated(x, mod_ref, g_ref, d)
    for k, (w_ref, o_ref) in enumerate(zip(w_refs, o_refs)):
        n = w_ref.shape[1]
        for c in range(0, n, n_chunk):
            e = min(c + n_chunk, n)
            r = jnp.dot(h, w_ref[:, c:e], preferred_element_type=F32)
            if len(o_ref.shape) == 3:
                for lb in range((e - c) // LANES):
                    part = r[:, lb * LANES:(lb + 1) * LANES]
                    o_ref[c // LANES + lb] = part.astype(o_ref.dtype)
                    if k == 0 and u_scr is not None:
                        u_scr[c // LANES + lb] = part
            else:
                o_ref[:, c:e] = r.astype(o_ref.dtype)
    if chunk_rows:
        _s5_to_chunks_kernel(u_scr, xc_ref, rows=chunk_rows)


def _inproj_call(x, mods_l, norm_g, weights, out_dtypes, *, n_prompt, sample_len, bm=512,
                 lane_blocked=(), s5_chunks=False):
    xs = list(x) if isinstance(x, tuple) else [x]
    n_tok = sum(a.shape[0] for a in xs)
    d = xs[0].shape[1]
    npb, bps = n_prompt // bm, sample_len // bm
    cond = functools.partial(_cond_of_block, n_prompt_blocks=npb, blocks_per_sample=bps)
    weights = [w if isinstance(w, tuple) else (w, 0, w.shape[1]) for w in weights]
    x_specs = _split_specs((bm, d), npb) if len(xs) == 2 else [pl.BlockSpec((bm, d), lambda i: (i, 0))]
    in_specs = x_specs + [
        pl.BlockSpec((None, 1, 3 * d), lambda i: (cond(i), 0, 0)),
        pl.BlockSpec((1, d), lambda i: (0, 0)),
    ] + [pl.BlockSpec((d, n), functools.partial(lambda i, blk: (0, blk), blk=blk)) for _, blk, n in weights]
    out_specs, out_shape = [], []
    for k, ((_, _, n), dt) in enumerate(zip(weights, out_dtypes)):
        if k in lane_blocked:
            out_specs.append(pl.BlockSpec((n // LANES, bm, LANES), lambda i: (0, i, 0)))
            out_shape.append(jax.ShapeDtypeStruct((n // LANES, n_tok, LANES), dt))
        else:
            out_specs.append(pl.BlockSpec((bm, n), lambda i: (i, 0)))
            out_shape.append(jax.ShapeDtypeStruct((n_tok, n), dt))
    chunk_rows = bm // S5_CHUNK if s5_chunks else 0
    scratch = []
    if s5_chunks:
        assert 0 in lane_blocked
        n_groups = weights[0][2] // S5_GROUP
        out_specs.append(pl.BlockSpec((n_groups, chunk_rows, 2 * LANES), lambda i: (0, i, 0)))
        out_shape.append(jax.ShapeDtypeStruct((n_groups, n_tok // S5_CHUNK, 2 * LANES), BF16))
        scratch = [pltpu.VMEM((weights[0][2] // LANES, bm, LANES), F32)]
    return pl.pallas_call(
        functools.partial(_inproj_kernel, d=d, n_chunk=512, chunk_rows=chunk_rows,
                          n_prompt_blocks=npb, n_x=len(xs)),
        out_shape=out_shape,
        grid=(n_tok // bm,),
        in_specs=in_specs,
        out_specs=out_specs,
        scratch_shapes=scratch,
        compiler_params=_cparams("arbitrary"),
        name="norm_mod_inproj",
    )(*xs, mods_l, norm_g.reshape(1, d), *[w for w, _, _ in weights])


def _outproj_kernel(*refs, d, final_norm, n_prompt_blocks, n_act, n_x):
    a_refs, x_refs = refs[:n_act], refs[n_act:n_act + n_x]
    mod_ref, w_ref, fg_ref = refs[n_act + n_x:n_act + n_x + 3]
    o_refs = refs[n_act + n_x + 3:]

    def finish(a_ref, x_ref, o_ref):
        y = jnp.dot(a_ref[...], w_ref[...], preferred_element_type=F32)
        gate = mod_ref[:, 2 * d:3 * d]
        xn = x_ref[...] + gate * y
        if final_norm:
            ms = jnp.mean(xn * xn, axis=-1, keepdims=True)
            xn = xn * lax.rsqrt(ms + NORM_EPS) * fg_ref[...]
        o_ref[...] = xn

    if max(n_act, n_x, len(o_refs)) == 1:
        finish(a_refs[0], x_refs[0], o_refs[0])
    else:
        is_prompt = pl.program_id(0) < n_prompt_blocks
        pl.when(is_prompt)(functools.partial(finish, a_refs[0], x_refs[0], o_refs[0]))
        pl.when(jnp.logical_not(is_prompt))(functools.partial(finish, a_refs[-1], x_refs[-1], o_refs[-1]))


def _split_specs(block, npb):
    return [pl.BlockSpec(block, lambda i: (jnp.minimum(i, npb - 1), 0)),
            pl.BlockSpec(block, lambda i: (jnp.maximum(i - npb, 0), 0))]


def _outproj_call(act, x, mods_l, w_out, final_g, *, n_prompt, sample_len, final_norm, bm=512,
                  split_out=False):
    acts = list(act) if isinstance(act, tuple) else [act]
    xs = list(x) if isinstance(x, tuple) else [x]
    n_tok = sum(a.shape[0] for a in xs)
    d, k = xs[0].shape[1], acts[0].shape[1]
    npb, bps = n_prompt // bm, sample_len // bm
    cond = functools.partial(_cond_of_block, n_prompt_blocks=npb, blocks_per_sample=bps)
    row = lambda i: (i, 0)
    act_specs = _split_specs((bm, k), npb) if len(acts) == 2 else [pl.BlockSpec((bm, k), row)]
    x_specs = _split_specs((bm, d), npb) if len(xs) == 2 else [pl.BlockSpec((bm, d), row)]
    if split_out:
        out_shape = [jax.ShapeDtypeStruct((n_prompt, d), F32), jax.ShapeDtypeStruct((n_tok - n_prompt, d), F32)]
        out_specs = _split_specs((bm, d), npb)
    else:
        out_shape = jax.ShapeDtypeStruct((n_tok, d), F32)
        out_specs = pl.BlockSpec((bm, d), row)
    return pl.pallas_call(
        functools.partial(_outproj_kernel, d=d, final_norm=final_norm, n_prompt_blocks=npb,
                          n_act=len(acts), n_x=len(xs)),
        out_shape=out_shape,
        grid=(n_tok // bm,),
        in_specs=act_specs + x_specs + [
            pl.BlockSpec((None, 1, 3 * d), lambda i: (cond(i), 0, 0)),
            pl.BlockSpec((k, d), lambda i: (0, 0)),
            pl.BlockSpec((1, d), lambda i: (0, 0)),
        ],
        out_specs=out_specs,
        compiler_params=_cparams("arbitrary"),
        name="outproj_residual",
    )(*acts, *xs, mods_l, w_out, final_g.reshape(1, d))


def _s5_time_of_lane_block():
    pos = np.arange(S5_CHUNK)
    half, blk = pos // 8, pos % 8
    g8 = np.arange(8)[:, None]
    return 8 * half[None, :] + (blk[None, :] - g8) % 8


def _s5_tables(lam_re, lam_im, log_step, b_re, b_im, c_re, c_im):
    t_chunk = S5_CHUNK
    n_groups, n_state = lam_re.shape[1], lam_re.shape[2]
    n_oct = n_groups // 8
    lam = lax.complex(lam_re.astype(F32), lam_im.astype(F32))
    step = jnp.exp(log_step.astype(F32))[..., None]
    lam_bar = jnp.exp(lam * step)
    b_bar = ((lam_bar - 1.0) / lam)[..., None] * lax.complex(b_re.astype(F32), b_im.astype(F32))
    c_mat = lax.complex(c_re.astype(F32), c_im.astype(F32))
    ks = jnp.arange(t_chunk + 1, dtype=F32)[:, None, None, None]
    pw = jnp.exp(ks * (lam * step)[None])

    zeros = jnp.zeros((t_chunk - 1, n_groups, n_state), pw.dtype)
    lag_f = jnp.concatenate([zeros, pw[:t_chunk, 0], zeros[:1]], axis=0)
    lag_b = jnp.concatenate([pw[t_chunk - 1::-1, 1], zeros, zeros[:1]], axis=0)
    plag = jnp.concatenate([lag_f, lag_b], axis=-1).transpose(1, 0, 2)
    plag = jnp.stack([plag.real, plag.imag])

    tl = _s5_time_of_lane_block()
    pw_ri = jnp.stack([pw.real, pw.imag]).reshape(2, t_chunk + 1, 2, n_oct, 8, n_state)
    m_idx = np.arange(t_chunk + 1)[None, None, :]

    def power_table(exponent, direction):
        sel = (exponent[:, :, None] == m_idx).astype(np.float32)
        tab = jnp.einsum('kxm,rmakp->rakxp', sel, pw_ri[:, :, direction], precision=HIGHEST)
        return tab.reshape(2, n_groups, t_chunk, n_state)

    def both(fwd, bwd):
        m = jnp.concatenate([fwd, bwd], axis=-1)
        return jnp.stack([m.real, m.imag])

    tin = jnp.concatenate([power_table(t_chunk - 1 - tl, 0), power_table(tl, 1)], axis=-1)
    tout = jnp.concatenate([power_table(tl + 1, 0), power_table(t_chunk - tl, 1)], axis=-1)
    bt = both(b_bar[0].transpose(0, 2, 1), b_bar[1].transpose(0, 2, 1))
    ct = both(c_mat[0], c_mat[1])
    lam_rows = both(pw[t_chunk, 0][:, None], pw[t_chunk, 1][:, None])[:, :, 0]
    return plag, tin, tout, bt, ct, lam_rows


def _s5_kmat_kernel(plag_ref, tin_ref, tout_ref, bt_ref, ct_ref, k_ref, pin_ref, pot_ref, x_scr, v_scr):
    masks = _lane_block_masks()
    n_lag = 2 * S5_CHUNK - 1
    for g8 in range(8):
        br, bi = bt_ref[0, g8], bt_ref[1, g8]
        cr, ci = ct_ref[0, g8], ct_ref[1, g8]
        def split(a):
            hi = a.astype(BF16)
            return hi, (a - hi.astype(F32)).astype(BF16)

        for m in range(n_lag):
            rows = slice(m * S5_GROUP, (m + 1) * S5_GROUP)
            pr, pi = plag_ref[0, g8, m:m + 1, :], plag_ref[1, g8, m:m + 1, :]
            for c0, part in ((0, cr * pr - ci * pi), (LANES, -(cr * pi + ci * pr))):
                x_scr[0, rows, c0:c0 + LANES], x_scr[1, rows, c0:c0 + LANES] = split(part)
        b_hi, b_lo = split(jnp.concatenate([jnp.concatenate([br, bi], axis=1)] * (LANES // S5_GROUP), axis=0))
        nt = functools.partial(lax.dot_general, dimension_numbers=(((1,), (1,)), ((), ())),
                               preferred_element_type=F32)
        v_scr[...] = nt(x_scr[0], b_hi) + nt(x_scr[0], b_lo) + nt(x_scr[1], b_hi)
        for pos in range(S5_CHUNK):
            rows = slice(pos * S5_GROUP, (pos + 1) * S5_GROUP)
            tr, ti = tin_ref[0, g8, pos:pos + 1, :], tin_ref[1, g8, pos:pos + 1, :]
            pin_ref[g8, rows, 0:LANES] = (tr * br - ti * bi).astype(BF16)
            pin_ref[g8, rows, LANES:2 * LANES] = (tr * bi + ti * br).astype(BF16)
            tr, ti = tout_ref[0, g8, pos:pos + 1, :], tout_ref[1, g8, pos:pos + 1, :]
            pot_ref[g8, rows, 0:LANES] = (tr * cr - ti * ci).astype(BF16)
            pot_ref[g8, rows, LANES:2 * LANES] = (-(tr * ci + ti * cr)).astype(BF16)
        for pos in range(S5_CHUNK):
            tau = 8 * (pos // 8) + (pos % 8 - g8) % 8
            rows = slice(pos * S5_GROUP, (pos + 1) * S5_GROUP)
            for half in range(2):
                acc = None
                for blk in range(8):
                    sigma = 8 * half + (blk - g8) % 8
                    m = S5_CHUNK - 1 - sigma + tau
                    src = v_scr[m * S5_GROUP:(m + 1) * S5_GROUP, :]
                    acc = src if acc is None else jnp.where(masks[blk], src, acc)
                k_ref[g8, rows, half * LANES:(half + 1) * LANES] = acc.astype(BF16)


def _s5_kmat_call(plag, tin, tout, bt, ct):
    n_groups = plag.shape[1]
    n_lag_rows = (2 * S5_CHUNK - 1) * S5_GROUP
    lag_spec = pl.BlockSpec((2, 8) + plag.shape[2:], lambda i: (0, i, 0, 0))
    tab_spec = pl.BlockSpec((2, 8, S5_GROUP, LANES), lambda i: (0, i, 0, 0))
    mat = jax.ShapeDtypeStruct((n_groups, 2 * LANES, 2 * LANES), BF16)
    mat_spec = pl.BlockSpec((8, 2 * LANES, 2 * LANES), lambda i: (i, 0, 0))
    return pl.pallas_call(
        _s5_kmat_kernel,
        out_shape=[mat, mat, mat],
        grid=(n_groups // 8,),
        in_specs=[lag_spec, tab_spec, tab_spec, tab_spec, tab_spec],
        out_specs=[mat_spec, mat_spec, mat_spec],
        scratch_shapes=[pltpu.VMEM((2, n_lag_rows, 2 * LANES), BF16), pltpu.VMEM((n_lag_rows, LANES), F32)],
        compiler_params=_cparams("arbitrary"),
        name="s5_kmat",
    )(plag, tin, tout, bt, ct)


def _lane_block_masks():
    blk = lax.broadcasted_iota(jnp.int32, (1, LANES), 1) // S5_GROUP
    return [blk == b for b in range(8)]


def _diagonal_merge(src):
    blk = lax.broadcasted_iota(jnp.int32, (1, LANES), 1) // S5_GROUP
    q = list(src)
    for bit in (1, 2, 4):
        take = (blk & bit) != 0
        q = [jnp.where(take, q[(x + bit) % 8], q[x]) for x in range(8)]
    return [q[(-t) % 8] for t in range(8)]


def _s5_to_chunks_kernel(u_ref, x_ref, *, rows):
    for o in range(u_ref.shape[0]):
        for r0 in range(0, rows, SUBLANES):
            for half in range(2):
                rolled = []
                for t8 in range(8):
                    v = u_ref[o, pl.ds(r0 * S5_CHUNK + 8 * half + t8, SUBLANES, stride=S5_CHUNK), :]
                    rolled.append(pltpu.roll(v, t8 * S5_GROUP, 1) if t8 else v)
                for g8, merged in enumerate(_diagonal_merge(rolled)):
                    x_ref[o * 8 + g8, r0:r0 + SUBLANES, half * LANES:(half + 1) * LANES] = merged.astype(BF16)


def _s5_from_chunks_kernel(y_ref, u_ref, d_ref, o_ref, nat_scr, *, rows):
    tile = 2 * SUBLANES
    for o in range(u_ref.shape[0]):
        for r0 in range(0, rows, tile):
            for half in range(2):
                src = [y_ref[o * 8 + g8, r0:r0 + tile, half * LANES:(half + 1) * LANES].astype(F32)
                       for g8 in range(8)]
                for t8, merged in enumerate(_diagonal_merge(src)):
                    nat = pltpu.roll(merged, (8 - t8) * S5_GROUP, 1) if t8 else merged
                    nat_scr[pl.ds(r0 * S5_CHUNK + 8 * half + t8, tile, stride=S5_CHUNK), :] = nat
        d_vec = d_ref[:, o * LANES:(o + 1) * LANES]
        o_ref[o] = _gelu_tanh(nat_scr[...] + d_vec * u_ref[o].astype(F32)).astype(o_ref.dtype)


def _s5_from_chunks_call(yc, u3, d_skip, *, rows=32):
    n_blk, n_tok, _ = u3.shape
    n_groups, n_rows, _ = yc.shape
    tok_spec = pl.BlockSpec((n_blk, rows * S5_CHUNK, LANES), lambda i: (0, i, 0))
    return pl.pallas_call(
        functools.partial(_s5_from_chunks_kernel, rows=rows),
        out_shape=jax.ShapeDtypeStruct(u3.shape, F32),
        grid=(n_rows // rows,),
        in_specs=[pl.BlockSpec((n_groups, rows, 2 * LANES), lambda i: (0, i, 0)),
                  tok_spec,
                  pl.BlockSpec((1, n_blk * LANES), lambda i: (0, 0))],
        out_specs=tok_spec,
        scratch_shapes=[pltpu.VMEM((rows * S5_CHUNK, LANES), F32)],
        compiler_params=_cparams("arbitrary"),
        name="s5_from_chunks",
    )(yc, u3, d_skip.reshape(1, n_blk * LANES).astype(F32))


def _s5_chunk_kernel(x_ref, kt_ref, pin_ref, po_ref, lam_ref, h0r_ref, h0i_ref,
                     y_ref, fr_ref, fi_ref, r_scr, st_scr, yi_scr, *, segments, seq_block):
    gb = SUBLANES
    rows = x_ref.shape[1]
    lane = lax.broadcasted_iota(jnp.int32, (1, LANES), 1)
    fwd_lanes = lane < (LANES // 2)
    for g in range(gb):
        x = x_ref[g]
        yi_scr[g] = lax.dot_general(x, kt_ref[g], (((1,), (1,)), ((), ())), preferred_element_type=F32)
        r = jnp.dot(x, pin_ref[g], preferred_element_type=F32)
        of_group = pl.ds(g, rows, stride=gb)
        r_scr[0, of_group, :] = r[:, 0:LANES]
        r_scr[1, of_group, :] = r[:, LANES:2 * LANES]
    ar, ai = lam_ref[0], lam_ref[1]
    for row0, n_seq, n_chunks, from_input, to_output in segments:
        for b0 in range(0, n_seq, seq_block):
            nb = min(seq_block, n_seq - b0)

            def step(i, carry, row0=row0, n_chunks=n_chunks, b0=b0, nb=nb):
                out = []
                for k in range(nb):
                    base = row0 + (b0 + k) * n_chunks
                    at_f = pl.ds(pl.multiple_of((base + i) * gb, gb), gb)
                    at_b = pl.ds(pl.multiple_of((base + (n_chunks - 1) - i) * gb, gb), gb)
                    s_re, s_im = carry[k]
                    half = LANES // 2
                    st_scr[0, at_f, 0:half] = s_re[:, 0:half]
                    st_scr[0, at_b, half:LANES] = s_re[:, half:LANES]
                    st_scr[1, at_f, 0:half] = s_im[:, 0:half]
                    st_scr[1, at_b, half:LANES] = s_im[:, half:LANES]
                    v_re = jnp.where(fwd_lanes, r_scr[0, at_f, :], r_scr[0, at_b, :])
                    v_im = jnp.where(fwd_lanes, r_scr[1, at_f, :], r_scr[1, at_b, :])
                    out.append((ar * s_re - ai * s_im + v_re, ar * s_im + ai * s_re + v_im))
                return tuple(out)

            if from_input:
                init = tuple((h0r_ref[b0 + k], h0i_ref[b0 + k]) for k in range(nb))
            else:
                init = tuple((jnp.zeros((gb, LANES), F32),) * 2 for _ in range(nb))
            fin = lax.fori_loop(0, n_chunks, step, init, unroll=2)
            if to_output:
                for k in range(nb):
                    fr_ref[b0 + k] = fin[k][0]
                    fi_ref[b0 + k] = fin[k][1]
    for g in range(gb):
        of_group = pl.ds(g, rows, stride=gb)
        st = jnp.concatenate([st_scr[cb, of_group, :] for cb in range(2)], axis=1).astype(BF16)
        y_ref[g] = (yi_scr[g] + lax.dot_general(st, po_ref[g], (((1,), (1,)), ((), ())),
                                                preferred_element_type=F32)).astype(y_ref.dtype)


def _s5_chunk_call(xc, kt, pin, pout, lam_rows, h0_re, h0_im, *, layer, segments, n_final):
    n_groups, rows, _ = xc.shape
    gb = SUBLANES
    s_in = h0_re.shape[0]
    kern = functools.partial(_s5_chunk_kernel, segments=segments, seq_block=8)
    g3 = lambda i: (i, 0, 0)
    blk0 = layer * (n_groups // gb)
    p3 = lambda i: (i + blk0, 0, 0)
    mid = lambda i: (0, i, 0)
    return pl.pallas_call(
        kern,
        out_shape=[jax.ShapeDtypeStruct((n_groups, rows, 2 * LANES), F32),
                   jax.ShapeDtypeStruct((n_final, n_groups, LANES), F32),
                   jax.ShapeDtypeStruct((n_final, n_groups, LANES), F32)],
        grid=(n_groups // gb,),
        in_specs=[
            pl.BlockSpec((gb, rows, 2 * LANES), g3),
            pl.BlockSpec((gb, 2 * LANES, 2 * LANES), p3),
            pl.BlockSpec((gb, 2 * LANES, 2 * LANES), p3),
            pl.BlockSpec((gb, 2 * LANES, 2 * LANES), p3),
            pl.BlockSpec((2, gb, LANES), lambda i: (0, i + blk0, 0)),
            pl.BlockSpec((s_in, gb, LANES), mid),
            pl.BlockSpec((s_in, gb, LANES), mid),
        ],
        out_specs=[pl.BlockSpec((gb, rows, 2 * LANES), g3),
                   pl.BlockSpec((n_final, gb, LANES), mid),
                   pl.BlockSpec((n_final, gb, LANES), mid)],
        scratch_shapes=[pltpu.VMEM((2, rows * gb, LANES), F32),
                        pltpu.VMEM((2, rows * gb, LANES), F32),
                        pltpu.VMEM((gb, rows, 2 * LANES), F32)],
        compiler_params=_cparams("arbitrary"),
        name="s5_chunk_scan",
    )(xc, kt, pin, pout, lam_rows, h0_re, h0_im)


def _glu_kernel(y_ref, z_ref, w_ref, b_ref, o_ref, *, n_chunk):
    n_blk = y_ref.shape[0]
    yb = jnp.concatenate([y_ref[o].astype(BF16) for o in range(n_blk)], axis=1)
    per = n_chunk // LANES
    for c in range(0, n_blk, per):
        sl = slice(c * LANES, (c + per) * LANES)
        gate = _sigmoid(jnp.dot(yb, w_ref[:, sl], preferred_element_type=F32) + b_ref[:, sl])
        y = jnp.concatenate([y_ref[c + k] for k in range(per)], axis=1).astype(F32)
        o_ref[:, sl] = (y * gate * _silu(z_ref[:, sl].astype(F32))).astype(o_ref.dtype)


def _glu_call(y3, z, glu_w, glu_b, *, bm=512):
    n_blk, n_tok, _ = y3.shape
    width = n_blk * LANES
    return pl.pallas_call(
        functools.partial(_glu_kernel, n_chunk=min(512, width)),
        out_shape=jax.ShapeDtypeStruct((n_tok, width), BF16),
        grid=(n_tok // bm,),
        in_specs=[
            pl.BlockSpec((n_blk, bm, LANES), lambda i: (0, i, 0)),
            pl.BlockSpec((bm, width), lambda i: (i, 0)),
            pl.BlockSpec((width, width), lambda i: (0, 0)),
            pl.BlockSpec((1, width), lambda i: (0, 0)),
        ],
        out_specs=pl.BlockSpec((bm, width), lambda i: (i, 0)),
        compiler_params=_cparams("arbitrary"),
        name="s5_glu_gate",
    )(y3, z, glu_w, glu_b.reshape(1, width).astype(F32))


def _s5_prep_all(lam_re, lam_im, log_step, b_re, b_im, c_re, c_im):
    tabs = jax.vmap(_s5_tables)(lam_re, lam_im, log_step, b_re, b_im, c_re, c_im)
    plag, tin, tout, bt, ct, lam_rows = [jnp.moveaxis(t, 0, 1).reshape((2, -1) + t.shape[3:]) for t in tabs]
    kt, pin, pot = _s5_kmat_call(plag, tin, tout, bt, ct)
    return kt, pin, pot, lam_rows


def _s5_mix(u3, xc, z, layer, mats, d_skip, glu_w, glu_b, st_re, st_im, *, n_prompt_seq, prompt_len,
            n_sample_seq, sample_len, bm):
    kt, pin, pout, lam_rows = mats
    n_state = LANES // 2
    pc, sc = prompt_len // S5_CHUNK, sample_len // S5_CHUNK

    def state_rows(s):
        return jnp.concatenate([s[:, 0], s[:, 1]], axis=-1).astype(F32)

    segments = ((0, n_prompt_seq, pc, False, True), (n_prompt_seq * pc, n_sample_seq, sc, True, False))
    yc, fr, fi = _s5_chunk_call(xc, kt, pin, pout, lam_rows, state_rows(st_re), state_rows(st_im),
                                layer=layer, segments=segments, n_final=n_prompt_seq)
    y3 = _s5_from_chunks_call(yc, u3, d_skip)
    act = _glu_call(y3, z, glu_w, glu_b, bm=2 * bm)

    def unpack(f):
        return jnp.stack([f[:, :, :n_state], f[:, :, n_state:]], axis=1)

    return act, unpack(fr), unpack(fi)


def _pool_kernel(u_ref, z_ref, w_ref, s_ref, o_ref, *, n_prompt_blocks, prompt_len, sample_len):
    rows = u_ref.shape[0]
    seq_len = jnp.where(pl.program_id(0) < n_prompt_blocks, prompt_len, sample_len)
    t = lax.broadcasted_iota(jnp.int32, (rows, 1), 0) & (seq_len - 1)

    def later(x, k):
        return jnp.where(t + k < seq_len, pltpu.roll(x, rows - k, 0), 0.0)

    def earlier(x, k):
        return jnp.where(t >= k, pltpu.roll(x, k, 0), 0.0)

    def body(win):
        lo = win // 2
        u = u_ref[...]
        fwd = u
        bwd = earlier(u, 1)
        s = 1
        while s < lo:
            fwd = fwd + later(fwd, s)
            bwd = bwd + earlier(bwd, s)
            s *= 2
        cnt = jnp.minimum(t - lo + win, seq_len) - jnp.maximum(t - lo, 0)
        p = (fwd + bwd) / cnt.astype(F32) - u
        m = jnp.dot(p.astype(BF16), w_ref[...], preferred_element_type=F32) * s_ref[...]
        o_ref[...] = (m * _silu(z_ref[...].astype(F32))).astype(o_ref.dtype)

    for gi, win in enumerate(POOL_WINDOWS):
        pl.when(pl.program_id(1) == gi)(functools.partial(body, win))


def _pool_call(u, z, pool_w, pool_scale, *, n_prompt, prompt_len, sample_len, rows=2048):
    n_tok, width = u.shape
    n_groups = len(POOL_WINDOWS)
    cg = width // n_groups
    assert prompt_len & (prompt_len - 1) == 0 and sample_len & (sample_len - 1) == 0
    assert rows % prompt_len == 0 and rows % sample_len == 0 and n_prompt % rows == 0
    kern = functools.partial(_pool_kernel, n_prompt_blocks=n_prompt // rows, prompt_len=prompt_len,
                             sample_len=sample_len)
    return pl.pallas_call(
        kern,
        out_shape=jax.ShapeDtypeStruct((n_tok, width), BF16),
        grid=(n_tok // rows, n_groups),
        in_specs=[
            pl.BlockSpec((rows, cg), lambda i, g: (i, g)),
            pl.BlockSpec((rows, cg), lambda i, g: (i, g)),
            pl.BlockSpec((None, cg, cg), lambda i, g: (g, 0, 0)),
            pl.BlockSpec((1, cg), lambda i, g: (0, g)),
        ],
        out_specs=pl.BlockSpec((rows, cg), lambda i, g: (i, g)),
        compiler_params=_cparams("arbitrary", "arbitrary"),
        name="pool_mix",
    )(u, z, pool_w, pool_scale.reshape(1, width).astype(F32))


MLA_QW = 2 * LANES

_ROT_SRC = np.concatenate([np.arange(16, 32), np.arange(0, 16), np.arange(48, 64), np.arange(32, 48)])
_ROT_SIGN = np.concatenate([-np.ones(16), np.ones(16), -np.ones(16), np.ones(16)]).astype(np.float32)


def _rope_tables(n_prompt, n_sample_seq, sample_len):
    half = MLA_ROPE // 4
    tok = jnp.arange(sample_len)
    row = (tok // GRID_W).astype(F32)
    col = (tok % GRID_W).astype(F32)
    inv = ROPE_THETA ** (-jnp.arange(half, dtype=F32) / half)
    a_row, a_col = row[:, None] * inv, col[:, None] * inv
    cos = jnp.concatenate([jnp.cos(a_row), jnp.cos(a_row), jnp.cos(a_col), jnp.cos(a_col)], axis=-1)
    sin = jnp.concatenate([jnp.sin(a_row), jnp.sin(a_row), jnp.sin(a_col), jnp.sin(a_col)], axis=-1)
    pad = jnp.zeros((sample_len, LANES - MLA_ROPE), F32)
    cos_s = jnp.tile(jnp.concatenate([cos, pad], axis=-1), (n_sample_seq, 1))
    sin_s = jnp.tile(jnp.concatenate([sin, pad], axis=-1), (n_sample_seq, 1))
    cos_p = jnp.concatenate([jnp.ones((n_prompt, MLA_ROPE), F32), jnp.zeros((n_prompt, LANES - MLA_ROPE), F32)], -1)
    return jnp.concatenate([cos_p, cos_s]), jnp.concatenate([jnp.zeros((n_prompt, LANES), F32), sin_s])


def _rms(x, g):
    return x * lax.rsqrt(jnp.mean(x * x, axis=-1, keepdims=True) + NORM_EPS) * g


def _mla_post_kernel(sm_ref, cos_ref, sin_ref, qn_ref, kn_ref, wa_ref, wb_ref,
                     q_ref, ckv_ref, kpe_ref, *, q_rank, kv_rank, heads_per_dot):
    cosp, sinp = cos_ref[...], sin_ref[...]
    qn = _rms(sm_ref[:, 0:q_rank], qn_ref[...]).astype(BF16)
    for h0 in range(0, MLA_HEADS, heads_per_dot):
        a = jnp.dot(qn, wa_ref[:, h0 * MLA_QW:(h0 + heads_per_dot) * MLA_QW], preferred_element_type=F32)
        b = jnp.dot(qn, wb_ref[:, h0 * LANES:(h0 + heads_per_dot) * LANES], preferred_element_type=F32)
        for j in range(heads_per_dot):
            h = h0 + j
            q_ref[:, h * MLA_QW:h * MLA_QW + LANES] = a[:, j * MLA_QW:j * MLA_QW + LANES].astype(BF16)
            pe = a[:, j * MLA_QW + LANES:(j + 1) * MLA_QW] * cosp + b[:, j * LANES:(j + 1) * LANES] * sinp
            q_ref[:, h * MLA_QW + LANES:(h + 1) * MLA_QW] = pe.astype(BF16)
    c0 = q_rank
    ckv_ref[...] = _rms(sm_ref[:, c0:c0 + kv_rank], kn_ref[...])
    k0 = c0 + kv_rank
    kpe_ref[...] = (sm_ref[:, k0:k0 + LANES] * cosp + sm_ref[:, k0 + LANES:k0 + 2 * LANES] * sinp).astype(BF16)


def _mla_post_call(small, cos_t, sin_t, q_norm, kv_norm, wq_a, wq_b, *, bm=512):
    n_tok, ws = small.shape
    q_rank, kv_rank = q_norm.shape[-1], kv_norm.shape[-1]
    row = lambda i: (i, 0)
    fix = lambda i: (0, 0)
    kern = functools.partial(_mla_post_kernel, q_rank=q_rank, kv_rank=kv_rank, heads_per_dot=4)
    return pl.pallas_call(
        kern,
        out_shape=[jax.ShapeDtypeStruct((n_tok, MLA_HEADS * MLA_QW), BF16),
                   jax.ShapeDtypeStruct((n_tok, kv_rank), F32),
                   jax.ShapeDtypeStruct((n_tok, LANES), BF16)],
        grid=(n_tok // bm,),
        in_specs=[
            pl.BlockSpec((bm, ws), row),
            pl.BlockSpec((bm, LANES), row),
            pl.BlockSpec((bm, LANES), row),
            pl.BlockSpec((1, q_rank), fix),
            pl.BlockSpec((1, kv_rank), fix),
            pl.BlockSpec(wq_a.shape, fix),
            pl.BlockSpec(wq_b.shape, fix),
        ],
        out_specs=[pl.BlockSpec((bm, MLA_HEADS * MLA_QW), row),
                   pl.BlockSpec((bm, kv_rank), row),
                   pl.BlockSpec((bm, LANES), row)],
        compiler_params=_cparams("arbitrary"),
        name="mla_q_rope",
    )(small, cos_t, sin_t, q_norm.reshape(1, q_rank).astype(F32), kv_norm.reshape(1, kv_rank).astype(F32),
      wq_a, wq_b)


def _kv_expand_kernel(c_ref, w_ref, o_ref, *, n_chunk):
    c = c_ref[...].astype(BF16)
    n = w_ref.shape[1]
    for s in range(0, n, n_chunk):
        o_ref[:, s:s + n_chunk] = jnp.dot(c, w_ref[:, s:s + n_chunk],
                                          preferred_element_type=F32).astype(o_ref.dtype)


def _kv_expand_call(ckv, wkv_b, *, bm=512):
    rows, kr = ckv.shape
    n = wkv_b.shape[1]
    return pl.pallas_call(
        functools.partial(_kv_expand_kernel, n_chunk=1024),
        out_shape=jax.ShapeDtypeStruct((rows, n), BF16),
        grid=(rows // bm,),
        in_specs=[pl.BlockSpec((bm, kr), lambda i: (i, 0)), pl.BlockSpec((kr, n), lambda i: (0, 0))],
        out_specs=pl.BlockSpec((bm, n), lambda i: (i, 0)),
        compiler_params=_cparams("arbitrary"),
        name="mla_kv_expand",
    )(ckv, wkv_b)


def _attn_kernel(q_ref, kv_ref, kpe_ref, z_ref, o_ref, kcat_scr, vext_scr, *, hg, scale):
    c2 = scale * math.log2(math.e)

    @pl.when(pl.program_id(2) == 0)
    def _():
        ones = jnp.ones((kv_ref.shape[0], LANES), BF16)
        for j in range(hg):
            kcat_scr[j, :, 0:LANES] = kv_ref[:, j * 2 * LANES:j * 2 * LANES + LANES]
            kcat_scr[j, :, LANES:2 * LANES] = kpe_ref[...]
            vext_scr[j, :, 0:LANES] = kv_ref[:, j * 2 * LANES + LANES:(j + 1) * 2 * LANES]
            vext_scr[j, :, LANES:2 * LANES] = ones

    for j in range(hg):
        q = q_ref[:, j * MLA_QW:(j + 1) * MLA_QW]
        s = lax.dot_general(q, kcat_scr[j], (((1,), (1,)), ((), ())), preferred_element_type=F32)
        e = jnp.exp2((s - jnp.max(s, axis=-1, keepdims=True)) * c2)
        pv = jnp.dot(e.astype(BF16), vext_scr[j], preferred_element_type=F32)
        zs = slice(j * MLA_V, (j + 1) * MLA_V)
        o = pv[:, 0:MLA_V] / pv[:, MLA_V:2 * MLA_V]
        o_ref[:, zs] = (o * _silu(z_ref[:, zs].astype(F32))).astype(o_ref.dtype)


def _attn_call(q, kv, kpe, z, *, q_row0, n_seq, q_len, k_len, hg, qb):
    width = MLA_HEADS * MLA_V
    nqb = q_len // qb
    qb0 = q_row0 // qb
    assert q_row0 % qb == 0 and q_len % qb == 0
    scale = float((MLA_NOPE + MLA_ROPE) ** -0.5)
    qrow = lambda b, g, i: (qb0 + b * nqb + i, g)
    return pl.pallas_call(
        functools.partial(_attn_kernel, hg=hg, scale=scale),
        out_shape=jax.ShapeDtypeStruct((n_seq * q_len, width), BF16),
        grid=(n_seq, MLA_HEADS // hg, nqb),
        in_specs=[
            pl.BlockSpec((qb, hg * MLA_QW), qrow),
            pl.BlockSpec((k_len, hg * 2 * LANES), lambda b, g, i: (b, g)),
            pl.BlockSpec((k_len, LANES), lambda b, g, i: (b, 0)),
            pl.BlockSpec((qb, hg * MLA_V), qrow),
        ],
        out_specs=pl.BlockSpec((qb, hg * MLA_V), lambda b, g, i: (b * nqb + i, g)),
        scratch_shapes=[pltpu.VMEM((hg, k_len, MLA_QW), BF16),
                        pltpu.VMEM((hg, k_len, 2 * MLA_V), BF16)],
        compiler_params=_cparams("arbitrary", "arbitrary", "arbitrary"),
        name="mla_attention",
    )(q, kv, kpe, z)


def _mla_weights(w_in, wq_b):
    q_rank = wq_b.shape[0]
    kv_rank = w_in.shape[1] - q_rank - MLA_ROPE - MLA_HEADS * MLA_V
    d = w_in.shape[0]
    c_kpe = q_rank + kv_rank
    zpad = jnp.zeros((d, LANES - MLA_ROPE), w_in.dtype)
    kpe_w = w_in[:, c_kpe:c_kpe + MLA_ROPE]
    w_small = jnp.concatenate([w_in[:, :c_kpe], kpe_w, zpad,
                               kpe_w[:, _ROT_SRC] * _ROT_SIGN, zpad], axis=1)
    w_z = w_in[:, c_kpe + MLA_ROPE:]
    hd = MLA_NOPE + MLA_ROPE
    wq3 = wq_b.reshape(q_rank, MLA_HEADS, hd)
    pe = wq3[:, :, MLA_NOPE:]
    z3 = jnp.zeros((q_rank, MLA_HEADS, LANES - MLA_ROPE), wq_b.dtype)
    wq_a = jnp.concatenate([wq3, z3], axis=-1).reshape(q_rank, MLA_HEADS * MLA_QW)
    wq_r = jnp.concatenate([pe[:, :, _ROT_SRC] * _ROT_SIGN, z3], axis=-1).reshape(q_rank, MLA_HEADS * LANES)
    return w_small.astype(BF16), w_z.astype(BF16), wq_a.astype(BF16), wq_r.astype(BF16)


def kernel(x_prompt, x_sample, state_s5_re, state_s5_im, cache_ckv, cache_kpe, c, c_ctx, norm_g, ada_w, ada_b, final_norm_g, s5_w_in, s5_lam_re, s5_lam_im, s5_log_step, s5_b_re, s5_b_im, s5_c_re, s5_c_im, s5_d, s5_glu_w, s5_glu_b, s5_w_out, pool_w_in, pool_w, pool_scale, pool_w_out, mla_w_in, mla_q_norm, mla_wq_b, mla_kv_norm, mla_wkv_b, mla_w_out):
    n_pseq, p_len, d = x_prompt.shape
    n_sseq, s_len, _ = x_sample.shape
    depth = norm_g.shape[0]
    n_prompt = n_pseq * p_len
    bm = 512
    geo = dict(n_prompt=n_prompt, sample_len=s_len, bm=bm)

    x = (x_prompt.reshape(n_prompt, d), x_sample.reshape(n_sseq * s_len, d))
    conds = jnp.concatenate([c_ctx[None, :], c, jnp.zeros((SUBLANES - 1 - n_sseq, d), F32)], axis=0)
    mods = _ada_call(conds.astype(F32), ada_w, ada_b)
    mods = mods.reshape(depth, SUBLANES, 1, 3 * d)
    s5_mats = _s5_prep_all(s5_lam_re, s5_lam_im, s5_log_step, s5_b_re, s5_b_im, s5_c_re, s5_c_im)

    new_re, new_im, new_ckv, new_kpe = [], [], [], []
    for layer in range(depth):
        kind, j = layer % N_MIXERS, layer // N_MIXERS
        last = layer == depth - 1
        ml = mods[layer]
        if kind == 0:
            width = s5_w_in.shape[2] // 2
            w = s5_w_in[j].astype(BF16)
            u3, z, xc = _inproj_call(x, ml, norm_g[layer], [(w, 0, width), (w, 1, width)], [F32, BF16],
                                     lane_blocked=(0,), s5_chunks=True, **geo)
            act, f_re, f_im = _s5_mix(u3, xc, z, j, s5_mats, s5_d[j], s5_glu_w[j].astype(BF16), s5_glu_b[j],
                                      state_s5_re[:, j], state_s5_im[:, j], n_prompt_seq=n_pseq,
                                      prompt_len=p_len, n_sample_seq=n_sseq, sample_len=s_len, bm=bm)
            new_re.append(f_re)
            new_im.append(f_im)
            w_out = s5_w_out[j]
        elif kind == 1:
            width = pool_w_in.shape[2] // 2
            w = pool_w_in[j].astype(BF16)
            u, z = _inproj_call(x, ml, norm_g[layer], [(w, 0, width), (w, 1, width)], [F32, BF16], **geo)
            act = _pool_call(u, z, pool_w[j].astype(BF16), pool_scale[j], n_prompt=n_prompt,
                             prompt_len=p_len, sample_len=s_len)
            w_out = pool_w_out[j]
        else:
            q_rank, kv_rank = mla_q_norm.shape[-1], mla_kv_norm.shape[-1]
            w_small, w_z, wq_a, wq_r = _mla_weights(mla_w_in[j], mla_wq_b[j])
            small, z = _inproj_call(x, ml, norm_g[layer], [w_small, w_z], [F32, BF16], **geo)
            cos_t, sin_t = _rope_tables(n_prompt, n_sseq, s_len)
            q, ckv_n, kpe_k = _mla_post_call(small, cos_t, sin_t, mla_q_norm[j], mla_kv_norm[j], wq_a, wq_r,
                                             bm=bm)
            wkv = mla_wkv_b[j].astype(BF16)
            past = cache_ckv.shape[2]
            k_len = past + s_len
            ckv_s = jnp.concatenate([cache_ckv[:, j].astype(F32), ckv_n[n_prompt:].reshape(n_sseq, s_len, kv_rank)],
                                    axis=1).reshape(n_sseq * k_len, kv_rank)
            kpe_cache = jnp.concatenate([cache_kpe[:, j].astype(BF16),
                                         jnp.zeros((n_sseq, past, LANES - MLA_ROPE), BF16)], axis=-1)
            kpe_s = jnp.concatenate([kpe_cache, kpe_k[n_prompt:].reshape(n_sseq, s_len, LANES)],
                                    axis=1).reshape(n_sseq * k_len, LANES)
            kv_p = _kv_expand_call(ckv_n[:n_prompt], wkv, bm=bm)
            kv_s = _kv_expand_call(ckv_s, wkv, bm=bm)
            act = (_attn_call(q, kv_p, kpe_k[:n_prompt], z, q_row0=0, n_seq=n_pseq, q_len=p_len,
                              k_len=p_len, hg=MLA_HEADS, qb=p_len),
                   _attn_call(q, kv_s, kpe_s, z, q_row0=n_prompt, n_seq=n_sseq, q_len=s_len,
                              k_len=k_len, hg=4, qb=256))
            new_ckv.append(ckv_n[:n_prompt].reshape(n_pseq, p_len, kv_rank))
            c_kpe = q_rank + kv_rank
            new_kpe.append(small[:n_prompt, c_kpe:c_kpe + MLA_ROPE].reshape(n_pseq, p_len, MLA_ROPE))
            w_out = mla_w_out[j]
        x = _outproj_call(act, x, ml, w_out.astype(BF16), final_norm_g, final_norm=last, split_out=last,
                          n_prompt=n_prompt, sample_len=s_len, bm=2 * bm)

    y_prompt = x[0].reshape(n_pseq, p_len, d)
    y_sample = x[1].reshape(n_sseq, s_len, d)
    return (y_prompt, y_sample, jnp.stack(new_re, axis=1), jnp.stack(new_im, axis=1),
            jnp.stack(new_ckv, axis=1), jnp.stack(new_kpe, axis=1))
```

```python
import functools
import math

import jax
import jax.numpy as jnp
import numpy as np
from jax import lax
from jax.experimental import pallas as pl
from jax.experimental.pallas import tpu as pltpu

S5_GROUP = 16
S5_CHUNK = 16
POOL_WINDOWS = (2, 4, 8, 16)
MLA_HEADS = 16
MLA_NOPE = 128
MLA_ROPE = 64
MLA_V = 128
GRID_W = 64
ROPE_THETA = 10000.0
NORM_EPS = 1e-6
N_MIXERS = 3

LANES = 128
SUBLANES = 8
VMEM_LIMIT_BYTES = 56 * 1024 * 1024

F32 = jnp.float32
BF16 = jnp.bfloat16
HIGHEST = lax.Precision.HIGHEST


def _cparams(*sem):
    return pltpu.CompilerParams(dimension_semantics=sem, vmem_limit_bytes=VMEM_LIMIT_BYTES)


def _sigmoid(x):
    return 0.5 + 0.5 * jnp.tanh(0.5 * x)


def _silu(x):
    return x * _sigmoid(x)


def _gelu_tanh(x):
    c = math.sqrt(2.0 / math.pi)
    hx = 0.5 * x
    return hx + hx * jnp.tanh(x * (c + (c * 0.044715) * (x * x)))


def _ada_kernel(c_ref, w_ref, b_ref, o_ref):
    a = _silu(c_ref[...])
    o_ref[...] = jnp.dot(a, w_ref[...], preferred_element_type=F32, precision=HIGHEST) + b_ref[...]


def _ada_call(conds, ada_w, ada_b):
    depth, d, d3 = ada_w.shape
    c8 = conds.shape[0]
    tn = 512
    return pl.pallas_call(
        _ada_kernel,
        out_shape=jax.ShapeDtypeStruct((depth, c8, d3), F32),
        grid=(depth, d3 // tn),
        in_specs=[
            pl.BlockSpec((c8, d), lambda l, n: (0, 0)),
            pl.BlockSpec((None, d, tn), lambda l, n: (l, 0, n)),
            pl.BlockSpec((None, 1, tn), lambda l, n: (l, 0, n)),
        ],
        out_specs=pl.BlockSpec((None, c8, tn), lambda l, n: (l, 0, n)),
        compiler_params=_cparams("arbitrary", "arbitrary"),
        name="ada_mod",
    )(conds, ada_w, ada_b.reshape(depth, 1, d3))


def _cond_of_block(i, n_prompt_blocks, blocks_per_sample):
    return jnp.where(i < n_prompt_blocks, 0, 1 + (i - n_prompt_blocks) // blocks_per_sample)


def _modulated(x, mod_ref, g_ref, d):
    ms = jnp.mean(x * x, axis=-1, keepdims=True)
    y = x * lax.rsqrt(ms + NORM_EPS) * g_ref[...]
    shift = mod_ref[:, 0:d]
    scale = mod_ref[:, d:2 * d]
    return (y * (1.0 + scale) + shift).astype(BF16)


def _inproj_kernel(*refs, d, n_chunk, chunk_rows, n_prompt_blocks, n_x):
    if n_x == 2:
        x = jnp.where(pl.program_id(0) < n_prompt_blocks, refs[0][...], refs[1][...])
    else:
        x = refs[0][...]
    mod_ref, g_ref = refs[n_x], refs[n_x + 1]
    rest = refs[n_x + 2:]
    u_scr = None
    if chunk_rows:
        rest, xc_ref, u_scr = rest[:-2], rest[-2], rest[-1]
    n_out = len(rest) // 2
    w_refs, o_refs = rest[:n_out], rest[n_out:]
    h = _modulated(x, mod_ref, g_ref, d)
    for k, (w_ref, o_ref) in enumerate(zip(w_refs, o_refs)):
        n = w_ref.shape[1]
        for c in range(0, n, n_chunk):
            e = min(c + n_chunk, n)
            r = jnp.dot(h, w_ref[:, c:e], preferred_element_type=F32)
            if len(o_ref.shape) == 3:
                for lb in range((e - c) // LANES):
                    part = r[:, lb * LANES:(lb + 1) * LANES]
                    o_ref[c // LANES + lb] = part.astype(o_ref.dtype)
                    if k == 0 and u_scr is not None:
                        u_scr[c // LANES + lb] = part
            else:
                o_ref[:, c:e] = r.astype(o_ref.dtype)
    if chunk_rows:
        _s5_to_chunks_kernel(u_scr, xc_ref, rows=chunk_rows)


def _inproj_call(x, mods_l, norm_g, weights, out_dtypes, *, n_prompt, sample_len, bm=512,
                 lane_blocked=(), s5_chunks=False):
    xs = list(x) if isinstance(x, tuple) else [x]
    n_tok = sum(a.shape[0] for a in xs)
    d = xs[0].shape[1]
    npb, bps = n_prompt // bm, sample_len // bm
    cond = functools.partial(_cond_of_block, n_prompt_blocks=npb, blocks_per_sample=bps)
    weights = [w if isinstance(w, tuple) else (w, 0, w.shape[1]) for w in weights]
    x_specs = _split_specs((bm, d), npb) if len(xs) == 2 else [pl.BlockSpec((bm, d), lambda i: (i, 0))]
    in_specs = x_specs + [
        pl.BlockSpec((None, 1, 3 * d), lambda i: (cond(i), 0, 0)),
        pl.BlockSpec((1, d), lambda i: (0, 0)),
    ] + [pl.BlockSpec((d, n), functools.partial(lambda i, blk: (0, blk), blk=blk)) for _, blk, n in weights]
    out_specs, out_shape = [], []
    for k, ((_, _, n), dt) in enumerate(zip(weights, out_dtypes)):
        if k in lane_blocked:
            out_specs.append(pl.BlockSpec((n // LANES, bm, LANES), lambda i: (0, i, 0)))
            out_shape.append(jax.ShapeDtypeStruct((n // LANES, n_tok, LANES), dt))
        else:
            out_specs.append(pl.BlockSpec((bm, n), lambda i: (i, 0)))
            out_shape.append(jax.ShapeDtypeStruct((n_tok, n), dt))
    chunk_rows = bm // S5_CHUNK if s5_chunks else 0
    scratch = []
    if s5_chunks:
        assert 0 in lane_blocked
        n_groups = weights[0][2] // S5_GROUP
        out_specs.append(pl.BlockSpec((n_groups, chunk_rows, 2 * LANES), lambda i: (0, i, 0)))
        out_shape.append(jax.ShapeDtypeStruct((n_groups, n_tok // S5_CHUNK, 2 * LANES), BF16))
        scratch = [pltpu.VMEM((weights[0][2] // LANES, bm, LANES), F32)]
    return pl.pallas_call(
        functools.partial(_inproj_kernel, d=d, n_chunk=512, chunk_rows=chunk_rows,
                          n_prompt_blocks=npb, n_x=len(xs)),
        out_shape=out_shape,
        grid=(n_tok // bm,),
        in_specs=in_specs,
        out_specs=out_specs,
        scratch_shapes=scratch,
        compiler_params=_cparams("arbitrary"),
        name="norm_mod_inproj",
    )(*xs, mods_l, norm_g.reshape(1, d), *[w for w, _, _ in weights])


def _outproj_kernel(*refs, d, final_norm, n_prompt_blocks, n_act, n_x):
    a_refs, x_refs = refs[:n_act], refs[n_act:n_act + n_x]
    mod_ref, w_ref, fg_ref = refs[n_act + n_x:n_act + n_x + 3]
    o_refs = refs[n_act + n_x + 3:]

    def finish(a_ref, x_ref, o_ref):
        y = jnp.dot(a_ref[...], w_ref[...], preferred_element_type=F32)
        gate = mod_ref[:, 2 * d:3 * d]
        xn = x_ref[...] + gate * y
        if final_norm:
            ms = jnp.mean(xn * xn, axis=-1, keepdims=True)
            xn = xn * lax.rsqrt(ms + NORM_EPS) * fg_ref[...]
        o_ref[...] = xn

    if max(n_act, n_x, len(o_refs)) == 1:
        finish(a_refs[0], x_refs[0], o_refs[0])
    else:
        is_prompt = pl.program_id(0) < n_prompt_blocks
        pl.when(is_prompt)(functools.partial(finish, a_refs[0], x_refs[0], o_refs[0]))
        pl.when(jnp.logical_not(is_prompt))(functools.partial(finish, a_refs[-1], x_refs[-1], o_refs[-1]))


def _split_specs(block, npb):
    return [pl.BlockSpec(block, lambda i: (jnp.minimum(i, npb - 1), 0)),
            pl.BlockSpec(block, lambda i: (jnp.maximum(i - npb, 0), 0))]


def _outproj_call(act, x, mods_l, w_out, final_g, *, n_prompt, sample_len, final_norm, bm=512,
                  split_out=False):
    acts = list(act) if isinstance(act, tuple) else [act]
    xs = list(x) if isinstance(x, tuple) else [x]
    n_tok = sum(a.shape[0] for a in xs)
    d, k = xs[0].shape[1], acts[0].shape[1]
    npb, bps = n_prompt // bm, sample_len // bm
    cond = functools.partial(_cond_of_block, n_prompt_blocks=npb, blocks_per_sample=bps)
    row = lambda i: (i, 0)
    act_specs = _split_specs((bm, k), npb) if len(acts) == 2 else [pl.BlockSpec((bm, k), row)]
    x_specs = _split_specs((bm, d), npb) if len(xs) == 2 else [pl.BlockSpec((bm, d), row)]
    if split_out:
        out_shape = [jax.ShapeDtypeStruct((n_prompt, d), F32), jax.ShapeDtypeStruct((n_tok - n_prompt, d), F32)]
        out_specs = _split_specs((bm, d), npb)
    else:
        out_shape = jax.ShapeDtypeStruct((n_tok, d), F32)
        out_specs = pl.BlockSpec((bm, d), row)
    return pl.pallas_call(
        functools.partial(_outproj_kernel, d=d, final_norm=final_norm, n_prompt_blocks=npb,
                          n_act=len(acts), n_x=len(xs)),
        out_shape=out_shape,
        grid=(n_tok // bm,),
        in_specs=act_specs + x_specs + [
            pl.BlockSpec((None, 1, 3 * d), lambda i: (cond(i), 0, 0)),
            pl.BlockSpec((k, d), lambda i: (0, 0)),
            pl.BlockSpec((1, d), lambda i: (0, 0)),
        ],
        out_specs=out_specs,
        compiler_params=_cparams("arbitrary"),
        name="outproj_residual",
    )(*acts, *xs, mods_l, w_out, final_g.reshape(1, d))


def _s5_time_of_lane_block():
    pos = np.arange(S5_CHUNK)
    half, blk = pos // 8, pos % 8
    g8 = np.arange(8)[:, None]
    return 8 * half[None, :] + (blk[None, :] - g8) % 8


def _s5_tables(lam_re, lam_im, log_step, b_re, b_im, c_re, c_im):
    t_chunk = S5_CHUNK
    n_groups, n_state = lam_re.shape[1], lam_re.shape[2]
    n_oct = n_groups // 8
    lam = lax.complex(lam_re.astype(F32), lam_im.astype(F32))
    step = jnp.exp(log_step.astype(F32))[..., None]
    lam_bar = jnp.exp(lam * step)
    b_bar = ((lam_bar - 1.0) / lam)[..., None] * lax.complex(b_re.astype(F32), b_im.astype(F32))
    c_mat = lax.complex(c_re.astype(F32), c_im.astype(F32))
    ks = jnp.arange(t_chunk + 1, dtype=F32)[:, None, None, None]
    pw = jnp.exp(ks * (lam * step)[None])

    zeros = jnp.zeros((t_chunk - 1, n_groups, n_state), pw.dtype)
    lag_f = jnp.concatenate([zeros, pw[:t_chunk, 0], zeros[:1]], axis=0)
    lag_b = jnp.concatenate([pw[t_chunk - 1::-1, 1], zeros, zeros[:1]], axis=0)
    plag = jnp.concatenate([lag_f, lag_b], axis=-1).transpose(1, 0, 2)
    plag = jnp.stack([plag.real, plag.imag])

    tl = _s5_time_of_lane_block()
    pw_ri = jnp.stack([pw.real, pw.imag]).reshape(2, t_chunk + 1, 2, n_oct, 8, n_state)
    m_idx = np.arange(t_chunk + 1)[None, None, :]

    def power_table(exponent, direction):
        sel = (exponent[:, :, None] == m_idx).astype(np.float32)
        tab = jnp.einsum('kxm,rmakp->rakxp', sel, pw_ri[:, :, direction], precision=HIGHEST)
        return tab.reshape(2, n_groups, t_chunk, n_state)

    def both(fwd, bwd):
        m = jnp.concatenate([fwd, bwd], axis=-1)
        return jnp.stack([m.real, m.imag])

    tin = jnp.concatenate([power_table(t_chunk - 1 - tl, 0), power_table(tl, 1)], axis=-1)
    tout = jnp.concatenate([power_table(tl + 1, 0), power_table(t_chunk - tl, 1)], axis=-1)
    bt = both(b_bar[0].transpose(0, 2, 1), b_bar[1].transpose(0, 2, 1))
    ct = both(c_mat[0], c_mat[1])
    lam_rows = both(pw[t_chunk, 0][:, None], pw[t_chunk, 1][:, None])[:, :, 0]
    return plag, tin, tout, bt, ct, lam_rows


def _s5_kmat_kernel(plag_ref, tin_ref, tout_ref, bt_ref, ct_ref, k_ref, pin_ref, pot_ref, x_scr, v_scr):
    masks = _lane_block_masks()
    n_lag = 2 * S5_CHUNK - 1
    for g8 in range(8):
        br, bi = bt_ref[0, g8], bt_ref[1, g8]
        cr, ci = ct_ref[0, g8], ct_ref[1, g8]
        def split(a):
            hi = a.astype(BF16)
            return hi, (a - hi.astype(F32)).astype(BF16)

        for m in range(n_lag):
            rows = slice(m * S5_GROUP, (m + 1) * S5_GROUP)
            pr, pi = plag_ref[0, g8, m:m + 1, :], plag_ref[1, g8, m:m + 1, :]
            for c0, part in ((0, cr * pr - ci * pi), (LANES, -(cr * pi + ci * pr))):
                x_scr[0, rows, c0:c0 + LANES], x_scr[1, rows, c0:c0 + LANES] = split(part)
        b_hi, b_lo = split(jnp.concatenate([jnp.concatenate([br, bi], axis=1)] * (LANES // S5_GROUP), axis=0))
        nt = functools.partial(lax.dot_general, dimension_numbers=(((1,), (1,)), ((), ())),
                               preferred_element_type=F32)
        v_scr[...] = nt(x_scr[0], b_hi) + nt(x_scr[0], b_lo) + nt(x_scr[1], b_hi)
        for pos in range(S5_CHUNK):
            rows = slice(pos * S5_GROUP, (pos + 1) * S5_GROUP)
            tr, ti = tin_ref[0, g8, pos:pos + 1, :], tin_ref[1, g8, pos:pos + 1, :]
            pin_ref[g8, rows, 0:LANES] = (tr * br - ti * bi).astype(BF16)
            pin_ref[g8, rows, LANES:2 * LANES] = (tr * bi + ti * br).astype(BF16)
            tr, ti = tout_ref[0, g8, pos:pos + 1, :], tout_ref[1, g8, pos:pos + 1, :]
            pot_ref[g8, rows, 0:LANES] = (tr * cr - ti * ci).astype(BF16)
            pot_ref[g8, rows, LANES:2 * LANES] = (-(tr * ci + ti * cr)).astype(BF16)
        for pos in range(S5_CHUNK):
            tau = 8 * (pos // 8) + (pos % 8 - g8) % 8
            rows = slice(pos * S5_GROUP, (pos + 1) * S5_GROUP)
            for half in range(2):
                acc = None
                for blk in range(8):
                    sigma = 8 * half + (blk - g8) % 8
                    m = S5_CHUNK - 1 - sigma + tau
                    src = v_scr[m * S5_GROUP:(m + 1) * S5_GROUP, :]
                    acc = src if acc is None else jnp.where(masks[blk], src, acc)
                k_ref[g8, rows, half * LANES:(half + 1) * LANES] = acc.astype(BF16)


def _s5_kmat_call(plag, tin, tout, bt, ct):
    n_groups = plag.shape[1]
    n_lag_rows = (2 * S5_CHUNK - 1) * S5_GROUP
    lag_spec = pl.BlockSpec((2, 8) + plag.shape[2:], lambda i: (0, i, 0, 0))
    tab_spec = pl.BlockSpec((2, 8, S5_GROUP, LANES), lambda i: (0, i, 0, 0))
    mat = jax.ShapeDtypeStruct((n_groups, 2 * LANES, 2 * LANES), BF16)
    mat_spec = pl.BlockSpec((8, 2 * LANES, 2 * LANES), lambda i: (i, 0, 0))
    return pl.pallas_call(
        _s5_kmat_kernel,
        out_shape=[mat, mat, mat],
        grid=(n_groups // 8,),
        in_specs=[lag_spec, tab_spec, tab_spec, tab_spec, tab_spec],
        out_specs=[mat_spec, mat_spec, mat_spec],
        scratch_shapes=[pltpu.VMEM((2, n_lag_rows, 2 * LANES), BF16), pltpu.VMEM((n_lag_rows, LANES), F32)],
        compiler_params=_cparams("arbitrary"),
        name="s5_kmat",
    )(plag, tin, tout, bt, ct)


def _lane_block_masks():
    blk = lax.broadcasted_iota(jnp.int32, (1, LANES), 1) // S5_GROUP
    return [blk == b for b in range(8)]


def _diagonal_merge(src):
    blk = lax.broadcasted_iota(jnp.int32, (1, LANES), 1) // S5_GROUP
    q = list(src)
    for bit in (1, 2, 4):
        take = (blk & bit) != 0
        q = [jnp.where(take, q[(x + bit) % 8], q[x]) for x in range(8)]
    return [q[(-t) % 8] for t in range(8)]


def _s5_to_chunks_kernel(u_ref, x_ref, *, rows):
    for o in range(u_ref.shape[0]):
        for r0 in range(0, rows, SUBLANES):
            for half in range(2):
                rolled = []
                for t8 in range(8):
                    v = u_ref[o, pl.ds(r0 * S5_CHUNK + 8 * half + t8, SUBLANES, stride=S5_CHUNK), :]
                    rolled.append(pltpu.roll(v, t8 * S5_GROUP, 1) if t8 else v)
                for g8, merged in enumerate(_diagonal_merge(rolled)):
                    x_ref[o * 8 + g8, r0:r0 + SUBLANES, half * LANES:(half + 1) * LANES] = merged.astype(BF16)


def _s5_tail_kernel(yc_ref, u_ref, d_ref, z_ref, w_ref, b_ref, o_ref, nat_scr, y_scr, *, rows, n_chunk):
    n_blk = u_ref.shape[0]
    tile = 2 * SUBLANES
    per = n_chunk // LANES
    for piece, r0 in enumerate(range(0, rows, tile)):
        tok = slice(r0 * S5_CHUNK, (r0 + tile) * S5_CHUNK)
        for o in range(n_blk):
            for half in range(2):
                src = [yc_ref[o * 8 + g8, r0:r0 + tile, half * LANES:(half + 1) * LANES] for g8 in range(8)]
                for t8, merged in enumerate(_diagonal_merge(src)):
                    nat = pltpu.roll(merged, (8 - t8) * S5_GROUP, 1) if t8 else merged
                    nat_scr[piece % 2, o, pl.ds(8 * half + t8, tile, stride=S5_CHUNK), :] = nat
            d_vec = d_ref[:, o * LANES:(o + 1) * LANES]
            y_scr[o, tok] = _gelu_tanh(nat_scr[piece % 2, o] + d_vec * u_ref[o, tok])
        yb = jnp.concatenate([y_scr[o, tok].astype(BF16) for o in range(n_blk)], axis=1)
        for c in range(0, n_blk, per):
            sl = slice(c * LANES, (c + per) * LANES)
            gate = _sigmoid(jnp.dot(yb, w_ref[:, sl], preferred_element_type=F32) + b_ref[:, sl])
            y = jnp.concatenate([y_scr[c + k, tok] for k in range(per)], axis=1)
            o_ref[tok, sl] = (y * gate * _silu(z_ref[tok, sl].astype(F32))).astype(o_ref.dtype)


def _s5_tail_call(yc, u3, d_skip, z, glu_w, glu_b, *, rows=32):
    n_blk, n_tok, _ = u3.shape
    n_groups, n_rows, _ = yc.shape
    width = n_blk * LANES
    bm = rows * S5_CHUNK
    tile_tok = 2 * SUBLANES * S5_CHUNK
    fix = lambda i: (0, 0)
    return pl.pallas_call(
        functools.partial(_s5_tail_kernel, rows=rows, n_chunk=min(512, width)),
        out_shape=jax.ShapeDtypeStruct((n_tok, width), BF16),
        grid=(n_rows // rows,),
        in_specs=[pl.BlockSpec((n_groups, rows, 2 * LANES), lambda i: (0, i, 0)),
                  pl.BlockSpec((n_blk, bm, LANES), lambda i: (0, i, 0)),
                  pl.BlockSpec((1, width), fix),
                  pl.BlockSpec((bm, width), lambda i: (i, 0)),
                  pl.BlockSpec((width, width), fix),
                  pl.BlockSpec((1, width), fix)],
        out_specs=pl.BlockSpec((bm, width), lambda i: (i, 0)),
        scratch_shapes=[pltpu.VMEM((2, n_blk, tile_tok, LANES), F32),
                        pltpu.VMEM((n_blk, bm, LANES), F32)],
        compiler_params=_cparams("arbitrary"),
        name="s5_tail",
    )(yc, u3, d_skip.reshape(1, width).astype(F32), z, glu_w, glu_b.reshape(1, width).astype(F32))


def _s5_chunk_kernel(x_ref, kt_ref, pin_ref, po_ref, lam_ref, h0r_ref, h0i_ref,
                     y_ref, fr_ref, fi_ref, r_scr, st_scr, yi_scr, *, segments, seq_block):
    gb = SUBLANES
    rows = x_ref.shape[1]
    lane = lax.broadcasted_iota(jnp.int32, (1, LANES), 1)
    fwd_lanes = lane < (LANES // 2)
    for g in range(gb):
        x = x_ref[g]
        yi_scr[g] = lax.dot_general(x, kt_ref[g], (((1,), (1,)), ((), ())), preferred_element_type=F32)
        r = jnp.dot(x, pin_ref[g], preferred_element_type=F32)
        of_group = pl.ds(g, rows, stride=gb)
        r_scr[0, of_group, :] = r[:, 0:LANES]
        r_scr[1, of_group, :] = r[:, LANES:2 * LANES]
    ar, ai = lam_ref[0], lam_ref[1]
    for row0, n_seq, n_chunks, from_input, to_output in segments:
        for b0 in range(0, n_seq, seq_block):
            nb = min(seq_block, n_seq - b0)

            def step(i, carry, row0=row0, n_chunks=n_chunks, b0=b0, nb=nb):
                out = []
                for k in range(nb):
                    base = row0 + (b0 + k) * n_chunks
                    at_f = pl.ds(pl.multiple_of((base + i) * gb, gb), gb)
                    at_b = pl.ds(pl.multiple_of((base + (n_chunks - 1) - i) * gb, gb), gb)
                    s_re, s_im = carry[k]
                    half = LANES // 2
                    st_scr[0, at_f, 0:half] = s_re[:, 0:half]
                    st_scr[0, at_b, half:LANES] = s_re[:, half:LANES]
                    st_scr[1, at_f, 0:half] = s_im[:, 0:half]
                    st_scr[1, at_b, half:LANES] = s_im[:, half:LANES]
                    v_re = jnp.where(fwd_lanes, r_scr[0, at_f, :], r_scr[0, at_b, :])
                    v_im = jnp.where(fwd_lanes, r_scr[1, at_f, :], r_scr[1, at_b, :])
                    out.append((ar * s_re - ai * s_im + v_re, ar * s_im + ai * s_re + v_im))
                return tuple(out)

            if from_input:
                init = tuple((h0r_ref[b0 + k], h0i_ref[b0 + k]) for k in range(nb))
            else:
                init = tuple((jnp.zeros((gb, LANES), F32),) * 2 for _ in range(nb))
            fin = lax.fori_loop(0, n_chunks, step, init, unroll=2)
            if to_output:
                for k in range(nb):
                    fr_ref[b0 + k] = fin[k][0]
                    fi_ref[b0 + k] = fin[k][1]
    for g in range(gb):
        of_group = pl.ds(g, rows, stride=gb)
        st = jnp.concatenate([st_scr[cb, of_group, :] for cb in range(2)], axis=1).astype(BF16)
        y_ref[g] = (yi_scr[g] + lax.dot_general(st, po_ref[g], (((1,), (1,)), ((), ())),
                                                preferred_element_type=F32)).astype(y_ref.dtype)


def _s5_chunk_call(xc, kt, pin, pout, lam_rows, h0_re, h0_im, *, layer, segments, n_final):
    n_groups, rows, _ = xc.shape
    gb = SUBLANES
    s_in = h0_re.shape[0]
    kern = functools.partial(_s5_chunk_kernel, segments=segments, seq_block=8)
    g3 = lambda i: (i, 0, 0)
    blk0 = layer * (n_groups // gb)
    p3 = lambda i: (i + blk0, 0, 0)
    mid = lambda i: (0, i, 0)
    return pl.pallas_call(
        kern,
        out_shape=[jax.ShapeDtypeStruct((n_groups, rows, 2 * LANES), F32),
                   jax.ShapeDtypeStruct((n_final, n_groups, LANES), F32),
                   jax.ShapeDtypeStruct((n_final, n_groups, LANES), F32)],
        grid=(n_groups // gb,),
        in_specs=[
            pl.BlockSpec((gb, rows, 2 * LANES), g3),
            pl.BlockSpec((gb, 2 * LANES, 2 * LANES), p3),
            pl.BlockSpec((gb, 2 * LANES, 2 * LANES), p3),
            pl.BlockSpec((gb, 2 * LANES, 2 * LANES), p3),
            pl.BlockSpec((2, gb, LANES), lambda i: (0, i + blk0, 0)),
            pl.BlockSpec((s_in, gb, LANES), mid),
            pl.BlockSpec((s_in, gb, LANES), mid),
        ],
        out_specs=[pl.BlockSpec((gb, rows, 2 * LANES), g3),
                   pl.BlockSpec((n_final, gb, LANES), mid),
                   pl.BlockSpec((n_final, gb, LANES), mid)],
        scratch_shapes=[pltpu.VMEM((2, rows * gb, LANES), F32),
                        pltpu.VMEM((2, rows * gb, LANES), F32),
                        pltpu.VMEM((gb, rows, 2 * LANES), F32)],
        compiler_params=_cparams("arbitrary"),
        name="s5_chunk_scan",
    )(xc, kt, pin, pout, lam_rows, h0_re, h0_im)


def _s5_prep_all(lam_re, lam_im, log_step, b_re, b_im, c_re, c_im):
    tabs = jax.vmap(_s5_tables)(lam_re, lam_im, log_step, b_re, b_im, c_re, c_im)
    plag, tin, tout, bt, ct, lam_rows = [jnp.moveaxis(t, 0, 1).reshape((2, -1) + t.shape[3:]) for t in tabs]
    kt, pin, pot = _s5_kmat_call(plag, tin, tout, bt, ct)
    return kt, pin, pot, lam_rows


def _s5_mix(u3, xc, z, layer, mats, d_skip, glu_w, glu_b, st_re, st_im, *, n_prompt_seq, prompt_len,
            n_sample_seq, sample_len, bm):
    kt, pin, pout, lam_rows = mats
    n_state = LANES // 2
    pc, sc = prompt_len // S5_CHUNK, sample_len // S5_CHUNK

    def state_rows(s):
        return jnp.concatenate([s[:, 0], s[:, 1]], axis=-1).astype(F32)

    segments = ((0, n_prompt_seq, pc, False, True), (n_prompt_seq * pc, n_sample_seq, sc, True, False))
    yc, fr, fi = _s5_chunk_call(xc, kt, pin, pout, lam_rows, state_rows(st_re), state_rows(st_im),
                                layer=layer, segments=segments, n_final=n_prompt_seq)
    act = _s5_tail_call(yc, u3, d_skip, z, glu_w, glu_b, rows=bm // S5_CHUNK)

    def unpack(f):
        return jnp.stack([f[:, :, :n_state], f[:, :, n_state:]], axis=1)

    return act, unpack(fr), unpack(fi)


def _pool_kernel(u_ref, z_ref, w_ref, s_ref, o_ref, *, n_prompt_blocks, prompt_len, sample_len):
    rows = u_ref.shape[0]
    seq_len = jnp.where(pl.program_id(0) < n_prompt_blocks, prompt_len, sample_len)
    t = lax.broadcasted_iota(jnp.int32, (rows, 1), 0) & (seq_len - 1)

    def later(x, k):
        return jnp.where(t + k < seq_len, pltpu.roll(x, rows - k, 0), 0.0)

    def earlier(x, k):
        return jnp.where(t >= k, pltpu.roll(x, k, 0), 0.0)

    def body(win):
        lo = win // 2
        u = u_ref[...]
        fwd = u
        bwd = earlier(u, 1)
        s = 1
        while s < lo:
            fwd = fwd + later(fwd, s)
            bwd = bwd + earlier(bwd, s)
            s *= 2
        cnt = jnp.minimum(t - lo + win, seq_len) - jnp.maximum(t - lo, 0)
        p = (fwd + bwd) / cnt.astype(F32) - u
        m = jnp.dot(p.astype(BF16), w_ref[...], preferred_element_type=F32) * s_ref[...]
        o_ref[...] = (m * _silu(z_ref[...].astype(F32))).astype(o_ref.dtype)

    for gi, win in enumerate(POOL_WINDOWS):
        pl.when(pl.program_id(1) == gi)(functools.partial(body, win))


def _pool_call(u, z, pool_w, pool_scale, *, n_prompt, prompt_len, sample_len, rows=2048):
    n_tok, width = u.shape
    n_groups = len(POOL_WINDOWS)
    cg = width // n_groups
    assert prompt_len & (prompt_len - 1) == 0 and sample_len & (sample_len - 1) == 0
    assert rows % prompt_len == 0 and rows % sample_len == 0 and n_prompt % rows == 0
    kern = functools.partial(_pool_kernel, n_prompt_blocks=n_prompt // rows, prompt_len=prompt_len,
                             sample_len=sample_len)
    return pl.pallas_call(
        kern,
        out_shape=jax.ShapeDtypeStruct((n_tok, width), BF16),
        grid=(n_tok // rows, n_groups),
        in_specs=[
            pl.BlockSpec((rows, cg), lambda i, g: (i, g)),
            pl.BlockSpec((rows, cg), lambda i, g: (i, g)),
            pl.BlockSpec((None, cg, cg), lambda i, g: (g, 0, 0)),
            pl.BlockSpec((1, cg), lambda i, g: (0, g)),
        ],
        out_specs=pl.BlockSpec((rows, cg), lambda i, g: (i, g)),
        compiler_params=_cparams("arbitrary", "arbitrary"),
        name="pool_mix",
    )(u, z, pool_w, pool_scale.reshape(1, width).astype(F32))


MLA_QW = 2 * LANES

_ROT_SRC = np.concatenate([np.arange(16, 32), np.arange(0, 16), np.arange(48, 64), np.arange(32, 48)])
_ROT_SIGN = np.concatenate([-np.ones(16), np.ones(16), -np.ones(16), np.ones(16)]).astype(np.float32)


def _rope_tables(n_prompt, n_sample_seq, sample_len):
    half = MLA_ROPE // 4
    tok = jnp.arange(sample_len)
    row = (tok // GRID_W).astype(F32)
    col = (tok % GRID_W).astype(F32)
    inv = ROPE_THETA ** (-jnp.arange(half, dtype=F32) / half)
    a_row, a_col = row[:, None] * inv, col[:, None] * inv
    cos = jnp.concatenate([jnp.cos(a_row), jnp.cos(a_row), jnp.cos(a_col), jnp.cos(a_col)], axis=-1)
    sin = jnp.concatenate([jnp.sin(a_row), jnp.sin(a_row), jnp.sin(a_col), jnp.sin(a_col)], axis=-1)
    pad = jnp.zeros((sample_len, LANES - MLA_ROPE), F32)
    cos_s = jnp.tile(jnp.concatenate([cos, pad], axis=-1), (n_sample_seq, 1))
    sin_s = jnp.tile(jnp.concatenate([sin, pad], axis=-1), (n_sample_seq, 1))
    cos_p = jnp.concatenate([jnp.ones((n_prompt, MLA_ROPE), F32), jnp.zeros((n_prompt, LANES - MLA_ROPE), F32)], -1)
    return jnp.concatenate([cos_p, cos_s]), jnp.concatenate([jnp.zeros((n_prompt, LANES), F32), sin_s])


def _rms(x, g):
    return x * lax.rsqrt(jnp.mean(x * x, axis=-1, keepdims=True) + NORM_EPS) * g


def _mla_post_kernel(sm_ref, cos_ref, sin_ref, qn_ref, kn_ref, wa_ref, wb_ref,
                     q_ref, ckv_ref, kpe_ref, *, q_rank, kv_rank, heads_per_dot):
    cosp, sinp = cos_ref[...], sin_ref[...]
    qn = _rms(sm_ref[:, 0:q_rank], qn_ref[...]).astype(BF16)
    for h0 in range(0, MLA_HEADS, heads_per_dot):
        a = jnp.dot(qn, wa_ref[:, h0 * MLA_QW:(h0 + heads_per_dot) * MLA_QW], preferred_element_type=F32)
        b = jnp.dot(qn, wb_ref[:, h0 * LANES:(h0 + heads_per_dot) * LANES], preferred_element_type=F32)
        for j in range(heads_per_dot):
            h = h0 + j
            q_ref[:, h * MLA_QW:h * MLA_QW + LANES] = a[:, j * MLA_QW:j * MLA_QW + LANES].astype(BF16)
            pe = a[:, j * MLA_QW + LANES:(j + 1) * MLA_QW] * cosp + b[:, j * LANES:(j + 1) * LANES] * sinp
            q_ref[:, h * MLA_QW + LANES:(h + 1) * MLA_QW] = pe.astype(BF16)
    c0 = q_rank
    ckv_ref[...] = _rms(sm_ref[:, c0:c0 + kv_rank], kn_ref[...])
    k0 = c0 + kv_rank
    kpe_ref[...] = (sm_ref[:, k0:k0 + LANES] * cosp + sm_ref[:, k0 + LANES:k0 + 2 * LANES] * sinp).astype(BF16)


def _mla_post_call(small, cos_t, sin_t, q_norm, kv_norm, wq_a, wq_b, *, bm=512):
    n_tok, ws = small.shape
    q_rank, kv_rank = q_norm.shape[-1], kv_norm.shape[-1]
    row = lambda i: (i, 0)
    fix = lambda i: (0, 0)
    kern = functools.partial(_mla_post_kernel, q_rank=q_rank, kv_rank=kv_rank, heads_per_dot=4)
    return pl.pallas_call(
        kern,
        out_shape=[jax.ShapeDtypeStruct((n_tok, MLA_HEADS * MLA_QW), BF16),
                   jax.ShapeDtypeStruct((n_tok, kv_rank), F32),
                   jax.ShapeDtypeStruct((n_tok, LANES), BF16)],
        grid=(n_tok // bm,),
        in_specs=[
            pl.BlockSpec((bm, ws), row),
            pl.BlockSpec((bm, LANES), row),
            pl.BlockSpec((bm, LANES), row),
            pl.BlockSpec((1, q_rank), fix),
            pl.BlockSpec((1, kv_rank), fix),
            pl.BlockSpec(wq_a.shape, fix),
            pl.BlockSpec(wq_b.shape, fix),
        ],
        out_specs=[pl.BlockSpec((bm, MLA_HEADS * MLA_QW), row),
                   pl.BlockSpec((bm, kv_rank), row),
                   pl.BlockSpec((bm, LANES), row)],
        compiler_params=_cparams("arbitrary"),
        name="mla_q_rope",
    )(small, cos_t, sin_t, q_norm.reshape(1, q_rank).astype(F32), kv_norm.reshape(1, kv_rank).astype(F32),
      wq_a, wq_b)


def _kv_expand_kernel(c_ref, w_ref, o_ref, *, n_chunk):
    c = c_ref[...].astype(BF16)
    n = w_ref.shape[1]
    for s in range(0, n, n_chunk):
        o_ref[:, s:s + n_chunk] = jnp.dot(c, w_ref[:, s:s + n_chunk],
                                          preferred_element_type=F32).astype(o_ref.dtype)


def _kv_expand_call(ckv, wkv_b, *, bm=512):
    rows, kr = ckv.shape
    n = wkv_b.shape[1]
    return pl.pallas_call(
        functools.partial(_kv_expand_kernel, n_chunk=1024),
        out_shape=jax.ShapeDtypeStruct((rows, n), BF16),
        grid=(rows // bm,),
        in_specs=[pl.BlockSpec((bm, kr), lambda i: (i, 0)), pl.BlockSpec((kr, n), lambda i: (0, 0))],
        out_specs=pl.BlockSpec((bm, n), lambda i: (i, 0)),
        compiler_params=_cparams("arbitrary"),
        name="mla_kv_expand",
    )(ckv, wkv_b)


def _attn_kernel(q_ref, kv_ref, kpe_ref, z_ref, o_ref, kcat_scr, vext_scr, *, hg, scale):
    c2 = scale * math.log2(math.e)

    @pl.when(pl.program_id(2) == 0)
    def _():
        ones = jnp.ones((kv_ref.shape[0], LANES), BF16)
        for j in range(hg):
            kcat_scr[j, :, 0:LANES] = kv_ref[:, j * 2 * LANES:j * 2 * LANES + LANES]
            kcat_scr[j, :, LANES:2 * LANES] = kpe_ref[...]
            vext_scr[j, :, 0:LANES] = kv_ref[:, j * 2 * LANES + LANES:(j + 1) * 2 * LANES]
            vext_scr[j, :, LANES:2 * LANES] = ones

    scores = [lax.dot_general(q_ref[:, j * MLA_QW:(j + 1) * MLA_QW], kcat_scr[j], (((1,), (1,)), ((), ())),
                              preferred_element_type=F32) for j in range(hg)]
    probs = [jnp.exp2((s - jnp.max(s, axis=-1, keepdims=True)) * c2).astype(BF16) for s in scores]
    for j in range(hg):
        pv = jnp.dot(probs[j], vext_scr[j], preferred_element_type=F32)
        zs = slice(j * MLA_V, (j + 1) * MLA_V)
        o = pv[:, 0:MLA_V] / pv[:, MLA_V:2 * MLA_V]
        o_ref[:, zs] = (o * _silu(z_ref[:, zs].astype(F32))).astype(o_ref.dtype)


def _attn_call(q, kv, kpe, z, *, q_row0, n_seq, q_len, k_len, hg, qb):
    width = MLA_HEADS * MLA_V
    nqb = q_len // qb
    qb0 = q_row0 // qb
    assert q_row0 % qb == 0 and q_len % qb == 0
    scale = float((MLA_NOPE + MLA_ROPE) ** -0.5)
    qrow = lambda b, g, i: (qb0 + b * nqb + i, g)
    return pl.pallas_call(
        functools.partial(_attn_kernel, hg=hg, scale=scale),
        out_shape=jax.ShapeDtypeStruct((n_seq * q_len, width), BF16),
        grid=(n_seq, MLA_HEADS // hg, nqb),
        in_specs=[
            pl.BlockSpec((qb, hg * MLA_QW), qrow),
            pl.BlockSpec((k_len, hg * 2 * LANES), lambda b, g, i: (b, g)),
            pl.BlockSpec((k_len, LANES), lambda b, g, i: (b, 0)),
            pl.BlockSpec((qb, hg * MLA_V), qrow),
        ],
        out_specs=pl.BlockSpec((qb, hg * MLA_V), lambda b, g, i: (b * nqb + i, g)),
        scratch_shapes=[pltpu.VMEM((hg, k_len, MLA_QW), BF16),
                        pltpu.VMEM((hg, k_len, 2 * MLA_V), BF16)],
        compiler_params=_cparams("arbitrary", "arbitrary", "arbitrary"),
        name="mla_attention",
    )(q, kv, kpe, z)


def _mla_weights(w_in, wq_b):
    q_rank = wq_b.shape[0]
    kv_rank = w_in.shape[1] - q_rank - MLA_ROPE - MLA_HEADS * MLA_V
    d = w_in.shape[0]
    c_kpe = q_rank + kv_rank
    zpad = jnp.zeros((d, LANES - MLA_ROPE), w_in.dtype)
    kpe_w = w_in[:, c_kpe:c_kpe + MLA_ROPE]
    w_small = jnp.concatenate([w_in[:, :c_kpe], kpe_w, zpad,
                               kpe_w[:, _ROT_SRC] * _ROT_SIGN, zpad], axis=1)
    w_z = w_in[:, c_kpe + MLA_ROPE:]
    hd = MLA_NOPE + MLA_ROPE
    wq3 = wq_b.reshape(q_rank, MLA_HEADS, hd)
    pe = wq3[:, :, MLA_NOPE:]
    z3 = jnp.zeros((q_rank, MLA_HEADS, LANES - MLA_ROPE), wq_b.dtype)
    wq_a = jnp.concatenate([wq3, z3], axis=-1).reshape(q_rank, MLA_HEADS * MLA_QW)
    wq_r = jnp.concatenate([pe[:, :, _ROT_SRC] * _ROT_SIGN, z3], axis=-1).reshape(q_rank, MLA_HEADS * LANES)
    return w_small.astype(BF16), w_z.astype(BF16), wq_a.astype(BF16), wq_r.astype(BF16)


def kernel(x_prompt, x_sample, state_s5_re, state_s5_im, cache_ckv, cache_kpe, c, c_ctx, norm_g, ada_w, ada_b, final_norm_g, s5_w_in, s5_lam_re, s5_lam_im, s5_log_step, s5_b_re, s5_b_im, s5_c_re, s5_c_im, s5_d, s5_glu_w, s5_glu_b, s5_w_out, pool_w_in, pool_w, pool_scale, pool_w_out, mla_w_in, mla_q_norm, mla_wq_b, mla_kv_norm, mla_wkv_b, mla_w_out):
    n_pseq, p_len, d = x_prompt.shape
    n_sseq, s_len, _ = x_sample.shape
    depth = norm_g.shape[0]
    n_prompt = n_pseq * p_len
    bm = 512
    geo = dict(n_prompt=n_prompt, sample_len=s_len, bm=bm)

    x = (x_prompt.reshape(n_prompt, d), x_sample.reshape(n_sseq * s_len, d))
    conds = jnp.concatenate([c_ctx[None, :], c, jnp.zeros((SUBLANES - 1 - n_sseq, d), F32)], axis=0)
    mods = _ada_call(conds.astype(F32), ada_w, ada_b)
    mods = mods.reshape(depth, SUBLANES, 1, 3 * d)
    s5_mats = _s5_prep_all(s5_lam_re, s5_lam_im, s5_log_step, s5_b_re, s5_b_im, s5_c_re, s5_c_im)

    new_re, new_im, new_ckv, new_kpe = [], [], [], []
    for layer in range(depth):
        kind, j = layer % N_MIXERS, layer // N_MIXERS
        last = layer == depth - 1
        ml = mods[layer]
        if kind == 0:
            width = s5_w_in.shape[2] // 2
            w = s5_w_in[j].astype(BF16)
            u3, z, xc = _inproj_call(x, ml, norm_g[layer], [(w, 0, width), (w, 1, width)], [F32, BF16],
                                     lane_blocked=(0,), s5_chunks=True, **geo)
            act, f_re, f_im = _s5_mix(u3, xc, z, j, s5_mats, s5_d[j], s5_glu_w[j].astype(BF16), s5_glu_b[j],
                                      state_s5_re[:, j], state_s5_im[:, j], n_prompt_seq=n_pseq,
                                      prompt_len=p_len, n_sample_seq=n_sseq, sample_len=s_len, bm=bm)
            new_re.append(f_re)
            new_im.append(f_im)
            w_out = s5_w_out[j]
        elif kind == 1:
            width = pool_w_in.shape[2] // 2
            w = pool_w_in[j].astype(BF16)
            u, z = _inproj_call(x, ml, norm_g[layer], [(w, 0, width), (w, 1, width)], [F32, BF16], **geo)
            act = _pool_call(u, z, pool_w[j].astype(BF16), pool_scale[j], n_prompt=n_prompt,
                             prompt_len=p_len, sample_len=s_len)
            w_out = pool_w_out[j]
        else:
            q_rank, kv_rank = mla_q_norm.shape[-1], mla_kv_norm.shape[-1]
            w_small, w_z, wq_a, wq_r = _mla_weights(mla_w_in[j], mla_wq_b[j])
            small, z = _inproj_call(x, ml, norm_g[layer], [w_small, w_z], [F32, BF16], **geo)
            cos_t, sin_t = _rope_tables(n_prompt, n_sseq, s_len)
            q, ckv_n, kpe_k = _mla_post_call(small, cos_t, sin_t, mla_q_norm[j], mla_kv_norm[j], wq_a, wq_r,
                                             bm=bm)
            wkv = mla_wkv_b[j].astype(BF16)
            past = cache_ckv.shape[2]
            k_len = past + s_len
            ckv_s = jnp.concatenate([cache_ckv[:, j].astype(F32), ckv_n[n_prompt:].reshape(n_sseq, s_len, kv_rank)],
                                    axis=1).reshape(n_sseq * k_len, kv_rank)
            kpe_cache = jnp.concatenate([cache_kpe[:, j].astype(BF16),
                                         jnp.zeros((n_sseq, past, LANES - MLA_ROPE), BF16)], axis=-1)
            kpe_s = jnp.concatenate([kpe_cache, kpe_k[n_prompt:].reshape(n_sseq, s_len, LANES)],
                                    axis=1).reshape(n_sseq * k_len, LANES)
            kv_p = _kv_expand_call(ckv_n[:n_prompt], wkv, bm=bm)
            kv_s = _kv_expand_call(ckv_s, wkv, bm=bm)
            act = (_attn_call(q, kv_p, kpe_k[:n_prompt], z, q_row0=0, n_seq=n_pseq, q_len=p_len,
                              k_len=p_len, hg=MLA_HEADS, qb=p_len),
                   _attn_call(q, kv_s, kpe_s, z, q_row0=n_prompt, n_seq=n_sseq, q_len=s_len,
                              k_len=k_len, hg=4, qb=256))
            new_ckv.append(ckv_n[:n_prompt].reshape(n_pseq, p_len, kv_rank))
            c_kpe = q_rank + kv_rank
            new_kpe.append(small[:n_prompt, c_kpe:c_kpe + MLA_ROPE].reshape(n_pseq, p_len, MLA_ROPE))
            w_out = mla_w_out[j]
        x = _outproj_call(act, x, ml, w_out.astype(BF16), final_norm_g, final_norm=last, split_out=last,
                          n_prompt=n_prompt, sample_len=s_len, bm=2 * bm)

    y_prompt = x[0].reshape(n_pseq, p_len, d)
    y_sample = x[1].reshape(n_sseq, s_len, d)
    return (y_prompt, y_sample, jnp.stack(new_re, axis=1), jnp.stack(new_im, axis=1),
            jnp.stack(new_ckv, axis=1), jnp.stack(new_kpe, axis=1))
```

```python
import functools
import math

import jax
import jax.numpy as jnp
import numpy as np
from jax import lax
from jax.experimental import pallas as pl
from jax.experimental.pallas import tpu as pltpu

S5_GROUP = 16
S5_CHUNK = 16
POOL_WINDOWS = (2, 4, 8, 16)
MLA_HEADS = 16
MLA_NOPE = 128
MLA_ROPE = 64
MLA_V = 128
GRID_W = 64
ROPE_THETA = 10000.0
NORM_EPS = 1e-6
N_MIXERS = 3

LANES = 128
SUBLANES = 8
VMEM_LIMIT_BYTES = 56 * 1024 * 1024

F32 = jnp.float32
BF16 = jnp.bfloat16
HIGHEST = lax.Precision.HIGHEST


def _cparams(*sem):
    return pltpu.CompilerParams(dimension_semantics=sem, vmem_limit_bytes=VMEM_LIMIT_BYTES)


def _sigmoid(x):
    return 0.5 + 0.5 * jnp.tanh(0.5 * x)


def _silu(x):
    return x * _sigmoid(x)


def _gelu_tanh(x):
    c = math.sqrt(2.0 / math.pi)
    hx = 0.5 * x
    return hx + hx * jnp.tanh(x * (c + (c * 0.044715) * (x * x)))


def _ada_kernel(c_ref, w_ref, b_ref, o_ref):
    a = _silu(c_ref[...])
    o_ref[...] = jnp.dot(a, w_ref[...], preferred_element_type=F32, precision=HIGHEST) + b_ref[...]


def _ada_call(conds, ada_w, ada_b):
    depth, d, d3 = ada_w.shape
    c8 = conds.shape[0]
    tn = d3 // 2
    return pl.pallas_call(
        _ada_kernel,
        out_shape=jax.ShapeDtypeStruct((depth, c8, d3), F32),
        grid=(depth, d3 // tn),
        in_specs=[
            pl.BlockSpec((c8, d), lambda l, n: (0, 0)),
            pl.BlockSpec((None, d, tn), lambda l, n: (l, 0, n)),
            pl.BlockSpec((None, 1, tn), lambda l, n: (l, 0, n)),
        ],
        out_specs=pl.BlockSpec((None, c8, tn), lambda l, n: (l, 0, n)),
        compiler_params=_cparams("arbitrary", "arbitrary"),
        name="ada_mod",
    )(conds, ada_w, ada_b.reshape(depth, 1, d3))


def _cond_of_block(i, n_prompt_blocks, blocks_per_sample):
    return jnp.where(i < n_prompt_blocks, 0, 1 + (i - n_prompt_blocks) // blocks_per_sample)


def _modulated(x, mod_ref, g_ref, d):
    ms = jnp.mean(x * x, axis=-1, keepdims=True)
    y = x * lax.rsqrt(ms + NORM_EPS) * g_ref[...]
    shift = mod_ref[:, 0:d]
    scale = mod_ref[:, d:2 * d]
    return (y * (1.0 + scale) + shift).astype(BF16)


def _inproj_kernel(*refs, d, n_chunk, chunk_rows, n_prompt_blocks, n_x):
    if n_x == 2:
        x = jnp.where(pl.program_id(0) < n_prompt_blocks, refs[0][...], refs[1][...])
    else:
        x = refs[0][...]
    mod_ref, g_ref = refs[n_x], refs[n_x + 1]
    rest = refs[n_x + 2:]
    u_scr = None
    if chunk_rows:
        rest, xc_ref, u_scr = rest[:-2], rest[-2], rest[-1]
    n_out = len(rest) // 2
    w_refs, o_refs = rest[:n_out], rest[n_out:]
    h = _modulated(x, mod_ref, g_ref, d)
    for k, (w_ref, o_ref) in enumerate(zip(w_refs, o_refs)):
        n = w_ref.shape[1]
        for c in range(0, n, n_chunk):
            e = min(c + n_chunk, n)
            r = jnp.dot(h, w_ref[:, c:e], preferred_element_type=F32)
            if len(o_ref.shape) == 3:
                for lb in range((e - c) // LANES):
                    part = r[:, lb * LANES:(lb + 1) * LANES]
                    o_ref[c // LANES + lb] = part.astype(o_ref.dtype)
                    if k == 0 and u_scr is not None:
                        u_scr[c // LANES + lb] = part
            else:
                o_ref[:, c:e] = r.astype(o_ref.dtype)
    if chunk_rows:
        _s5_to_chunks_kernel(u_scr, xc_ref, rows=chunk_rows)


def _inproj_call(x, mods_l, norm_g, weights, out_dtypes, *, n_prompt, sample_len, bm=512,
                 lane_blocked=(), s5_chunks=False):
    xs = list(x) if isinstance(x, tuple) else [x]
    n_tok = sum(a.shape[0] for a in xs)
    d = xs[0].shape[1]
    npb, bps = n_prompt // bm, sample_len // bm
    cond = functools.partial(_cond_of_block, n_prompt_blocks=npb, blocks_per_sample=bps)
    weights = [w if isinstance(w, tuple) else (w, 0, w.shape[1]) for w in weights]
    x_specs = _split_specs((bm, d), npb) if len(xs) == 2 else [pl.BlockSpec((bm, d), lambda i: (i, 0))]
    in_specs = x_specs + [
        pl.BlockSpec((None, 1, 3 * d), lambda i: (cond(i), 0, 0)),
        pl.BlockSpec((1, d), lambda i: (0, 0)),
    ] + [pl.BlockSpec((d, n), functools.partial(lambda i, blk: (0, blk), blk=blk)) for _, blk, n in weights]
    out_specs, out_shape = [], []
    for k, ((_, _, n), dt) in enumerate(zip(weights, out_dtypes)):
        if k in lane_blocked:
            out_specs.append(pl.BlockSpec((n // LANES, bm, LANES), lambda i: (0, i, 0)))
            out_shape.append(jax.ShapeDtypeStruct((n // LANES, n_tok, LANES), dt))
        else:
            out_specs.append(pl.BlockSpec((bm, n), lambda i: (i, 0)))
            out_shape.append(jax.ShapeDtypeStruct((n_tok, n), dt))
    chunk_rows = bm // S5_CHUNK if s5_chunks else 0
    scratch = []
    if s5_chunks:
        assert 0 in lane_blocked
        n_groups = weights[0][2] // S5_GROUP
        out_specs.append(pl.BlockSpec((n_groups, chunk_rows, 2 * LANES), lambda i: (0, i, 0)))
        out_shape.append(jax.ShapeDtypeStruct((n_groups, n_tok // S5_CHUNK, 2 * LANES), BF16))
        scratch = [pltpu.VMEM((weights[0][2] // LANES, bm, LANES), F32)]
    return pl.pallas_call(
        functools.partial(_inproj_kernel, d=d, n_chunk=512, chunk_rows=chunk_rows,
                          n_prompt_blocks=npb, n_x=len(xs)),
        out_shape=out_shape,
        grid=(n_tok // bm,),
        in_specs=in_specs,
        out_specs=out_specs,
        scratch_shapes=scratch,
        compiler_params=_cparams("arbitrary"),
        name="norm_mod_inproj",
    )(*xs, mods_l, norm_g.reshape(1, d), *[w for w, _, _ in weights])


def _outproj_kernel(*refs, d, final_norm, n_prompt_blocks, n_act, n_x):
    a_refs, x_refs = refs[:n_act], refs[n_act:n_act + n_x]
    mod_ref, w_ref, fg_ref = refs[n_act + n_x:n_act + n_x + 3]
    o_refs = refs[n_act + n_x + 3:]

    def finish(a_ref, x_ref, o_ref):
        y = jnp.dot(a_ref[...], w_ref[...], preferred_element_type=F32)
        gate = mod_ref[:, 2 * d:3 * d]
        xn = x_ref[...] + gate * y
        if final_norm:
            ms = jnp.mean(xn * xn, axis=-1, keepdims=True)
            xn = xn * lax.rsqrt(ms + NORM_EPS) * fg_ref[...]
        o_ref[...] = xn

    if max(n_act, n_x, len(o_refs)) == 1:
        finish(a_refs[0], x_refs[0], o_refs[0])
    else:
        is_prompt = pl.program_id(0) < n_prompt_blocks
        pl.when(is_prompt)(functools.partial(finish, a_refs[0], x_refs[0], o_refs[0]))
        pl.when(jnp.logical_not(is_prompt))(functools.partial(finish, a_refs[-1], x_refs[-1], o_refs[-1]))


def _split_specs(block, npb):
    return [pl.BlockSpec(block, lambda i: (jnp.minimum(i, npb - 1), 0)),
            pl.BlockSpec(block, lambda i: (jnp.maximum(i - npb, 0), 0))]


def _outproj_call(act, x, mods_l, w_out, final_g, *, n_prompt, sample_len, final_norm, bm=512,
                  split_out=False):
    acts = list(act) if isinstance(act, tuple) else [act]
    xs = list(x) if isinstance(x, tuple) else [x]
    n_tok = sum(a.shape[0] for a in xs)
    d, k = xs[0].shape[1], acts[0].shape[1]
    npb, bps = n_prompt // bm, sample_len // bm
    cond = functools.partial(_cond_of_block, n_prompt_blocks=npb, blocks_per_sample=bps)
    row = lambda i: (i, 0)
    act_specs = _split_specs((bm, k), npb) if len(acts) == 2 else [pl.BlockSpec((bm, k), row)]
    x_specs = _split_specs((bm, d), npb) if len(xs) == 2 else [pl.BlockSpec((bm, d), row)]
    if split_out:
        out_shape = [jax.ShapeDtypeStruct((n_prompt, d), F32), jax.ShapeDtypeStruct((n_tok - n_prompt, d), F32)]
        out_specs = _split_specs((bm, d), npb)
    else:
        out_shape = jax.ShapeDtypeStruct((n_tok, d), F32)
        out_specs = pl.BlockSpec((bm, d), row)
    return pl.pallas_call(
        functools.partial(_outproj_kernel, d=d, final_norm=final_norm, n_prompt_blocks=npb,
                          n_act=len(acts), n_x=len(xs)),
        out_shape=out_shape,
        grid=(n_tok // bm,),
        in_specs=act_specs + x_specs + [
            pl.BlockSpec((None, 1, 3 * d), lambda i: (cond(i), 0, 0)),
            pl.BlockSpec((k, d), lambda i: (0, 0)),
            pl.BlockSpec((1, d), lambda i: (0, 0)),
        ],
        out_specs=out_specs,
        compiler_params=_cparams("arbitrary"),
        name="outproj_residual",
    )(*acts, *xs, mods_l, w_out, final_g.reshape(1, d))


def _s5_time_of_lane_block():
    pos = np.arange(S5_CHUNK)
    half, blk = pos // 8, pos % 8
    g8 = np.arange(8)[:, None]
    return 8 * half[None, :] + (blk[None, :] - g8) % 8


def _s5_tables(lam_re, lam_im, log_step, b_re, b_im, c_re, c_im):
    t_chunk = S5_CHUNK
    n_groups, n_state = lam_re.shape[1], lam_re.shape[2]
    n_oct = n_groups // 8
    lam = lax.complex(lam_re.astype(F32), lam_im.astype(F32))
    step = jnp.exp(log_step.astype(F32))[..., None]
    lam_bar = jnp.exp(lam * step)
    b_bar = ((lam_bar - 1.0) / lam)[..., None] * lax.complex(b_re.astype(F32), b_im.astype(F32))
    c_mat = lax.complex(c_re.astype(F32), c_im.astype(F32))
    ks = jnp.arange(t_chunk + 1, dtype=F32)[:, None, None, None]
    pw = jnp.exp(ks * (lam * step)[None])

    zeros = jnp.zeros((t_chunk - 1, n_groups, n_state), pw.dtype)
    lag_f = jnp.concatenate([zeros, pw[:t_chunk, 0], zeros[:1]], axis=0)
    lag_b = jnp.concatenate([pw[t_chunk - 1::-1, 1], zeros, zeros[:1]], axis=0)
    plag = jnp.concatenate([lag_f, lag_b], axis=-1).transpose(1, 0, 2)
    plag = jnp.stack([plag.real, plag.imag])

    tl = _s5_time_of_lane_block()
    pw_ri = jnp.stack([pw.real, pw.imag]).reshape(2, t_chunk + 1, 2, n_oct, 8, n_state)
    m_idx = np.arange(t_chunk + 1)[None, None, :]

    def power_table(exponent, direction):
        sel = (exponent[:, :, None] == m_idx).astype(np.float32)
        tab = jnp.einsum('kxm,rmakp->rakxp', sel, pw_ri[:, :, direction], precision=HIGHEST)
        return tab.reshape(2, n_groups, t_chunk, n_state)

    def both(fwd, bwd):
        m = jnp.concatenate([fwd, bwd], axis=-1)
        return jnp.stack([m.real, m.imag])

    tin = jnp.concatenate([power_table(t_chunk - 1 - tl, 0), power_table(tl, 1)], axis=-1)
    tout = jnp.concatenate([power_table(tl + 1, 0), power_table(t_chunk - tl, 1)], axis=-1)
    bt = both(b_bar[0].transpose(0, 2, 1), b_bar[1].transpose(0, 2, 1))
    ct = both(c_mat[0], c_mat[1])
    lam_rows = both(pw[t_chunk, 0][:, None], pw[t_chunk, 1][:, None])[:, :, 0]
    return plag, tin, tout, bt, ct, lam_rows


def _s5_kmat_kernel(plag_ref, tin_ref, tout_ref, bt_ref, ct_ref, k_ref, pin_ref, pot_ref, x_scr, v_scr):
    masks = _lane_block_masks()
    n_lag = 2 * S5_CHUNK - 1
    for g8 in range(8):
        br, bi = bt_ref[0, g8], bt_ref[1, g8]
        cr, ci = ct_ref[0, g8], ct_ref[1, g8]
        def split(a):
            hi = a.astype(BF16)
            return hi, (a - hi.astype(F32)).astype(BF16)

        for m in range(n_lag):
            rows = slice(m * S5_GROUP, (m + 1) * S5_GROUP)
            pr, pi = plag_ref[0, g8, m:m + 1, :], plag_ref[1, g8, m:m + 1, :]
            for c0, part in ((0, cr * pr - ci * pi), (LANES, -(cr * pi + ci * pr))):
                x_scr[0, rows, c0:c0 + LANES], x_scr[1, rows, c0:c0 + LANES] = split(part)
        b_hi, b_lo = split(jnp.concatenate([jnp.concatenate([br, bi], axis=1)] * (LANES // S5_GROUP), axis=0))
        nt = functools.partial(lax.dot_general, dimension_numbers=(((1,), (1,)), ((), ())),
                               preferred_element_type=F32)
        v_scr[...] = nt(x_scr[0], b_hi) + nt(x_scr[0], b_lo) + nt(x_scr[1], b_hi)
        for pos in range(S5_CHUNK):
            rows = slice(pos * S5_GROUP, (pos + 1) * S5_GROUP)
            tr, ti = tin_ref[0, g8, pos:pos + 1, :], tin_ref[1, g8, pos:pos + 1, :]
            pin_ref[g8, rows, 0:LANES] = (tr * br - ti * bi).astype(BF16)
            pin_ref[g8, rows, LANES:2 * LANES] = (tr * bi + ti * br).astype(BF16)
            tr, ti = tout_ref[0, g8, pos:pos + 1, :], tout_ref[1, g8, pos:pos + 1, :]
            pot_ref[g8, rows, 0:LANES] = (tr * cr - ti * ci).astype(BF16)
            pot_ref[g8, rows, LANES:2 * LANES] = (-(tr * ci + ti * cr)).astype(BF16)
        for pos in range(S5_CHUNK):
            tau = 8 * (pos // 8) + (pos % 8 - g8) % 8
            rows = slice(pos * S5_GROUP, (pos + 1) * S5_GROUP)
            for half in range(2):
                acc = None
                for blk in range(8):
                    sigma = 8 * half + (blk - g8) % 8
                    m = S5_CHUNK - 1 - sigma + tau
                    src = v_scr[m * S5_GROUP:(m + 1) * S5_GROUP, :]
                    acc = src if acc is None else jnp.where(masks[blk], src, acc)
                k_ref[g8, rows, half * LANES:(half + 1) * LANES] = acc.astype(BF16)


def _s5_kmat_call(plag, tin, tout, bt, ct):
    n_groups = plag.shape[1]
    n_lag_rows = (2 * S5_CHUNK - 1) * S5_GROUP
    lag_spec = pl.BlockSpec((2, 8) + plag.shape[2:], lambda i: (0, i, 0, 0))
    tab_spec = pl.BlockSpec((2, 8, S5_GROUP, LANES), lambda i: (0, i, 0, 0))
    mat = jax.ShapeDtypeStruct((n_groups, 2 * LANES, 2 * LANES), BF16)
    mat_spec = pl.BlockSpec((8, 2 * LANES, 2 * LANES), lambda i: (i, 0, 0))
    return pl.pallas_call(
        _s5_kmat_kernel,
        out_shape=[mat, mat, mat],
        grid=(n_groups // 8,),
        in_specs=[lag_spec, tab_spec, tab_spec, tab_spec, tab_spec],
        out_specs=[mat_spec, mat_spec, mat_spec],
        scratch_shapes=[pltpu.VMEM((2, n_lag_rows, 2 * LANES), BF16), pltpu.VMEM((n_lag_rows, LANES), F32)],
        compiler_params=_cparams("arbitrary"),
        name="s5_kmat",
    )(plag, tin, tout, bt, ct)


def _lane_block_masks():
    blk = lax.broadcasted_iota(jnp.int32, (1, LANES), 1) // S5_GROUP
    return [blk == b for b in range(8)]


def _diagonal_merge(src):
    blk = lax.broadcasted_iota(jnp.int32, (1, LANES), 1) // S5_GROUP
    q = list(src)
    for bit in (1, 2, 4):
        take = (blk & bit) != 0
        q = [jnp.where(take, q[(x + bit) % 8], q[x]) for x in range(8)]
    return [q[(-t) % 8] for t in range(8)]


def _s5_to_chunks_kernel(u_ref, x_ref, *, rows):
    for o in range(u_ref.shape[0]):
        for r0 in range(0, rows, SUBLANES):
            for half in range(2):
                rolled = []
                for t8 in range(8):
                    v = u_ref[o, pl.ds(r0 * S5_CHUNK + 8 * half + t8, SUBLANES, stride=S5_CHUNK), :]
                    rolled.append(pltpu.roll(v, t8 * S5_GROUP, 1) if t8 else v)
                for g8, merged in enumerate(_diagonal_merge(rolled)):
                    x_ref[o * 8 + g8, r0:r0 + SUBLANES, half * LANES:(half + 1) * LANES] = merged.astype(BF16)


def _s5_tail_kernel(yc_ref, u_ref, d_ref, z_ref, w_ref, b_ref, o_ref, nat_scr, y_scr, *, rows, n_chunk):
    n_blk = u_ref.shape[0]
    tile = 2 * SUBLANES
    per = n_chunk // LANES
    for piece, r0 in enumerate(range(0, rows, tile)):
        tok = slice(r0 * S5_CHUNK, (r0 + tile) * S5_CHUNK)
        for o in range(n_blk):
            for half in range(2):
                src = [yc_ref[o * 8 + g8, r0:r0 + tile, half * LANES:(half + 1) * LANES] for g8 in range(8)]
                for t8, merged in enumerate(_diagonal_merge(src)):
                    nat = pltpu.roll(merged, (8 - t8) * S5_GROUP, 1) if t8 else merged
                    nat_scr[piece % 2, o, pl.ds(8 * half + t8, tile, stride=S5_CHUNK), :] = nat
            d_vec = d_ref[:, o * LANES:(o + 1) * LANES]
            y_scr[o, tok] = _gelu_tanh(nat_scr[piece % 2, o] + d_vec * u_ref[o, tok])
        yb = jnp.concatenate([y_scr[o, tok].astype(BF16) for o in range(n_blk)], axis=1)
        for c in range(0, n_blk, per):
            sl = slice(c * LANES, (c + per) * LANES)
            gate = _sigmoid(jnp.dot(yb, w_ref[:, sl], preferred_element_type=F32) + b_ref[:, sl])
            y = jnp.concatenate([y_scr[c + k, tok] for k in range(per)], axis=1)
            o_ref[tok, sl] = (y * gate * _silu(z_ref[tok, sl].astype(F32))).astype(o_ref.dtype)


def _s5_tail_call(yc, u3, d_skip, z, glu_w, glu_b, *, rows=32):
    n_blk, n_tok, _ = u3.shape
    n_groups, n_rows, _ = yc.shape
    width = n_blk * LANES
    bm = rows * S5_CHUNK
    tile_tok = 2 * SUBLANES * S5_CHUNK
    fix = lambda i: (0, 0)
    return pl.pallas_call(
        functools.partial(_s5_tail_kernel, rows=rows, n_chunk=min(512, width)),
        out_shape=jax.ShapeDtypeStruct((n_tok, width), BF16),
        grid=(n_rows // rows,),
        in_specs=[pl.BlockSpec((n_groups, rows, 2 * LANES), lambda i: (0, i, 0)),
                  pl.BlockSpec((n_blk, bm, LANES), lambda i: (0, i, 0)),
                  pl.BlockSpec((1, width), fix),
                  pl.BlockSpec((bm, width), lambda i: (i, 0)),
                  pl.BlockSpec((width, width), fix),
                  pl.BlockSpec((1, width), fix)],
        out_specs=pl.BlockSpec((bm, width), lambda i: (i, 0)),
        scratch_shapes=[pltpu.VMEM((2, n_blk, tile_tok, LANES), F32),
                        pltpu.VMEM((n_blk, bm, LANES), F32)],
        compiler_params=_cparams("arbitrary"),
        name="s5_tail",
    )(yc, u3, d_skip.reshape(1, width).astype(F32), z, glu_w, glu_b.reshape(1, width).astype(F32))


def _s5_chunk_kernel(x_ref, kt_ref, pin_ref, po_ref, lam_ref, h0r_ref, h0i_ref,
                     y_ref, fr_ref, fi_ref, r_scr, st_scr, yi_scr, *, segments, seq_block):
    gb = SUBLANES
    rows = x_ref.shape[1]
    lane = lax.broadcasted_iota(jnp.int32, (1, LANES), 1)
    fwd_lanes = lane < (LANES // 2)
    del rows
    seg_rows = [(row0, n_seq * n_chunks) for row0, n_seq, n_chunks, _, _ in segments]
    for row0, n_rows in seg_rows:
        for g in range(gb):
            x = x_ref[g, row0:row0 + n_rows, :]
            yi_scr[g, row0:row0 + n_rows, :] = lax.dot_general(x, kt_ref[g], (((1,), (1,)), ((), ())),
                                                               preferred_element_type=F32)
            r = jnp.dot(x, pin_ref[g], preferred_element_type=F32)
            of_group = pl.ds(row0 * gb + g, n_rows, stride=gb)
            r_scr[0, of_group, :] = r[:, 0:LANES]
            r_scr[1, of_group, :] = r[:, LANES:2 * LANES]
    ar, ai = lam_ref[0], lam_ref[1]
    for row0, n_seq, n_chunks, from_input, to_output in segments:
        for b0 in range(0, n_seq, seq_block):
            nb = min(seq_block, n_seq - b0)

            def step(i, carry, row0=row0, n_chunks=n_chunks, b0=b0, nb=nb):
                out = []
                for k in range(nb):
                    base = row0 + (b0 + k) * n_chunks
                    at_f = pl.ds((base + i) * gb, gb)
                    at_b = pl.ds((base + (n_chunks - 1) - i) * gb, gb)
                    s_re, s_im = carry[k]
                    half = LANES // 2
                    st_scr[0, at_f, 0:half] = s_re[:, 0:half]
                    st_scr[0, at_b, half:LANES] = s_re[:, half:LANES]
                    st_scr[1, at_f, 0:half] = s_im[:, 0:half]
                    st_scr[1, at_b, half:LANES] = s_im[:, half:LANES]
                    v_re = jnp.where(fwd_lanes, r_scr[0, at_f, :], r_scr[0, at_b, :])
                    v_im = jnp.where(fwd_lanes, r_scr[1, at_f, :], r_scr[1, at_b, :])
                    out.append((ar * s_re - ai * s_im + v_re, ar * s_im + ai * s_re + v_im))
                return tuple(out)

            if from_input:
                init = tuple((h0r_ref[b0 + k], h0i_ref[b0 + k]) for k in range(nb))
            else:
                init = tuple((jnp.zeros((gb, LANES), F32),) * 2 for _ in range(nb))
            fin = init
            for i in range(n_chunks):
                fin = step(i, fin)
            if to_output:
                for k in range(nb):
                    fr_ref[b0 + k] = fin[k][0]
                    fi_ref[b0 + k] = fin[k][1]
    for row0, n_rows in seg_rows:
        for g in range(gb):
            of_group = pl.ds(row0 * gb + g, n_rows, stride=gb)
            st = jnp.concatenate([st_scr[cb, of_group, :] for cb in range(2)], axis=1).astype(BF16)
            rows = slice(row0, row0 + n_rows)
            y_ref[g, rows, :] = (yi_scr[g, rows, :] + lax.dot_general(
                st, po_ref[g], (((1,), (1,)), ((), ())), preferred_element_type=F32)).astype(y_ref.dtype)


def _s5_chunk_call(xc, kt, pin, pout, lam_rows, h0_re, h0_im, *, layer, segments, n_final):
    n_groups, rows, _ = xc.shape
    gb = SUBLANES
    s_in = h0_re.shape[0]
    kern = functools.partial(_s5_chunk_kernel, segments=segments, seq_block=8)
    g3 = lambda i: (i, 0, 0)
    blk0 = layer * (n_groups // gb)
    p3 = lambda i: (i + blk0, 0, 0)
    mid = lambda i: (0, i, 0)
    return pl.pallas_call(
        kern,
        out_shape=[jax.ShapeDtypeStruct((n_groups, rows, 2 * LANES), F32),
                   jax.ShapeDtypeStruct((n_final, n_groups, LANES), F32),
                   jax.ShapeDtypeStruct((n_final, n_groups, LANES), F32)],
        grid=(n_groups // gb,),
        in_specs=[
            pl.BlockSpec((gb, rows, 2 * LANES), g3),
            pl.BlockSpec((gb, 2 * LANES, 2 * LANES), p3),
            pl.BlockSpec((gb, 2 * LANES, 2 * LANES), p3),
            pl.BlockSpec((gb, 2 * LANES, 2 * LANES), p3),
            pl.BlockSpec((2, gb, LANES), lambda i: (0, i + blk0, 0)),
            pl.BlockSpec((s_in, gb, LANES), mid),
            pl.BlockSpec((s_in, gb, LANES), mid),
        ],
        out_specs=[pl.BlockSpec((gb, rows, 2 * LANES), g3),
                   pl.BlockSpec((n_final, gb, LANES), mid),
                   pl.BlockSpec((n_final, gb, LANES), mid)],
        scratch_shapes=[pltpu.VMEM((2, rows * gb, LANES), F32),
                        pltpu.VMEM((2, rows * gb, LANES), F32),
                        pltpu.VMEM((gb, rows, 2 * LANES), F32)],
        compiler_params=_cparams("arbitrary"),
        name="s5_chunk_scan",
    )(xc, kt, pin, pout, lam_rows, h0_re, h0_im)


def _s5_prep_all(lam_re, lam_im, log_step, b_re, b_im, c_re, c_im):
    tabs = jax.vmap(_s5_tables)(lam_re, lam_im, log_step, b_re, b_im, c_re, c_im)
    plag, tin, tout, bt, ct, lam_rows = [jnp.moveaxis(t, 0, 1).reshape((2, -1) + t.shape[3:]) for t in tabs]
    kt, pin, pot = _s5_kmat_call(plag, tin, tout, bt, ct)
    return kt, pin, pot, lam_rows


def _s5_mix(u3, xc, z, layer, mats, d_skip, glu_w, glu_b, st_re, st_im, *, n_prompt_seq, prompt_len,
            n_sample_seq, sample_len, bm):
    kt, pin, pout, lam_rows = mats
    n_state = LANES // 2
    pc, sc = prompt_len // S5_CHUNK, sample_len // S5_CHUNK

    def state_rows(s):
        return jnp.concatenate([s[:, 0], s[:, 1]], axis=-1).astype(F32)

    segments = ((0, n_prompt_seq, pc, False, True), (n_prompt_seq * pc, n_sample_seq, sc, True, False))
    yc, fr, fi = _s5_chunk_call(xc, kt, pin, pout, lam_rows, state_rows(st_re), state_rows(st_im),
                                layer=layer, segments=segments, n_final=n_prompt_seq)
    act = _s5_tail_call(yc, u3, d_skip, z, glu_w, glu_b, rows=bm // S5_CHUNK)

    def unpack(f):
        return jnp.stack([f[:, :, :n_state], f[:, :, n_state:]], axis=1)

    return act, unpack(fr), unpack(fi)


def _pool_kernel(u_ref, z_ref, w_ref, s_ref, o_ref, *, n_prompt_blocks, prompt_len, sample_len):
    rows = u_ref.shape[0]
    seq_len = jnp.where(pl.program_id(0) < n_prompt_blocks, prompt_len, sample_len)
    t = lax.broadcasted_iota(jnp.int32, (rows, 1), 0) & (seq_len - 1)

    def later(x, k):
        return jnp.where(t + k < seq_len, pltpu.roll(x, rows - k, 0), 0.0)

    def earlier(x, k):
        return jnp.where(t >= k, pltpu.roll(x, k, 0), 0.0)

    def body(win):
        lo = win // 2
        u = u_ref[...]
        fwd = u
        bwd = earlier(u, 1)
        s = 1
        while s < lo:
            fwd = fwd + later(fwd, s)
            bwd = bwd + earlier(bwd, s)
            s *= 2
        cnt = jnp.minimum(t - lo + win, seq_len) - jnp.maximum(t - lo, 0)
        p = (fwd + bwd) / cnt.astype(F32) - u
        m = jnp.dot(p.astype(BF16), w_ref[...], preferred_element_type=F32) * s_ref[...]
        o_ref[...] = (m * _silu(z_ref[...].astype(F32))).astype(o_ref.dtype)

    for gi, win in enumerate(POOL_WINDOWS):
        pl.when(pl.program_id(1) == gi)(functools.partial(body, win))


def _pool_call(u, z, pool_w, pool_scale, *, n_prompt, prompt_len, sample_len, rows=2048):
    n_tok, width = u.shape
    n_groups = len(POOL_WINDOWS)
    cg = width // n_groups
    assert prompt_len & (prompt_len - 1) == 0 and sample_len & (sample_len - 1) == 0
    assert rows % prompt_len == 0 and rows % sample_len == 0 and n_prompt % rows == 0
    kern = functools.partial(_pool_kernel, n_prompt_blocks=n_prompt // rows, prompt_len=prompt_len,
                             sample_len=sample_len)
    return pl.pallas_call(
        kern,
        out_shape=jax.ShapeDtypeStruct((n_tok, width), BF16),
        grid=(n_tok // rows, n_groups),
        in_specs=[
            pl.BlockSpec((rows, cg), lambda i, g: (i, g)),
            pl.BlockSpec((rows, cg), lambda i, g: (i, g)),
            pl.BlockSpec((None, cg, cg), lambda i, g: (g, 0, 0)),
            pl.BlockSpec((1, cg), lambda i, g: (0, g)),
        ],
        out_specs=pl.BlockSpec((rows, cg), lambda i, g: (i, g)),
        compiler_params=_cparams("arbitrary", "arbitrary"),
        name="pool_mix",
    )(u, z, pool_w, pool_scale.reshape(1, width).astype(F32))


MLA_QW = 2 * LANES

_ROT_SRC = np.concatenate([np.arange(16, 32), np.arange(0, 16), np.arange(48, 64), np.arange(32, 48)])
_ROT_SIGN = np.concatenate([-np.ones(16), np.ones(16), -np.ones(16), np.ones(16)]).astype(np.float32)


def _rope_tables(n_prompt, n_sample_seq, sample_len):
    half = MLA_ROPE // 4
    tok = jnp.arange(sample_len)
    row = (tok // GRID_W).astype(F32)
    col = (tok % GRID_W).astype(F32)
    inv = ROPE_THETA ** (-jnp.arange(half, dtype=F32) / half)
    a_row, a_col = row[:, None] * inv, col[:, None] * inv
    cos = jnp.concatenate([jnp.cos(a_row), jnp.cos(a_row), jnp.cos(a_col), jnp.cos(a_col)], axis=-1)
    sin = jnp.concatenate([jnp.sin(a_row), jnp.sin(a_row), jnp.sin(a_col), jnp.sin(a_col)], axis=-1)
    pad = jnp.zeros((sample_len, LANES - MLA_ROPE), F32)
    cos_s = jnp.tile(jnp.concatenate([cos, pad], axis=-1), (n_sample_seq, 1))
    sin_s = jnp.tile(jnp.concatenate([sin, pad], axis=-1), (n_sample_seq, 1))
    cos_p = jnp.concatenate([jnp.ones((n_prompt, MLA_ROPE), F32), jnp.zeros((n_prompt, LANES - MLA_ROPE), F32)], -1)
    return jnp.concatenate([cos_p, cos_s]), jnp.concatenate([jnp.zeros((n_prompt, LANES), F32), sin_s])


def _rms(x, g):
    return x * lax.rsqrt(jnp.mean(x * x, axis=-1, keepdims=True) + NORM_EPS) * g


def _mla_post_kernel(sm_ref, cos_ref, sin_ref, qn_ref, kn_ref, wa_ref, wb_ref,
                     q_ref, ckv_ref, kpe_ref, *, q_rank, kv_rank, heads_per_dot):
    cosp, sinp = cos_ref[...], sin_ref[...]
    qn = _rms(sm_ref[:, 0:q_rank], qn_ref[...]).astype(BF16)
    for h0 in range(0, MLA_HEADS, heads_per_dot):
        a = jnp.dot(qn, wa_ref[:, h0 * MLA_QW:(h0 + heads_per_dot) * MLA_QW], preferred_element_type=F32)
        b = jnp.dot(qn, wb_ref[:, h0 * LANES:(h0 + heads_per_dot) * LANES], preferred_element_type=F32)
        for j in range(heads_per_dot):
            h = h0 + j
            q_ref[:, h * MLA_QW:h * MLA_QW + LANES] = a[:, j * MLA_QW:j * MLA_QW + LANES].astype(BF16)
            pe = a[:, j * MLA_QW + LANES:(j + 1) * MLA_QW] * cosp + b[:, j * LANES:(j + 1) * LANES] * sinp
            q_ref[:, h * MLA_QW + LANES:(h + 1) * MLA_QW] = pe.astype(BF16)
    c0 = q_rank
    ckv_ref[...] = _rms(sm_ref[:, c0:c0 + kv_rank], kn_ref[...])
    k0 = c0 + kv_rank
    kpe_ref[...] = (sm_ref[:, k0:k0 + LANES] * cosp + sm_ref[:, k0 + LANES:k0 + 2 * LANES] * sinp).astype(BF16)


def _mla_post_call(small, cos_t, sin_t, q_norm, kv_norm, wq_a, wq_b, *, bm=512):
    n_tok, ws = small.shape
    q_rank, kv_rank = q_norm.shape[-1], kv_norm.shape[-1]
    row = lambda i: (i, 0)
    fix = lambda i: (0, 0)
    kern = functools.partial(_mla_post_kernel, q_rank=q_rank, kv_rank=kv_rank, heads_per_dot=4)
    return pl.pallas_call(
        kern,
        out_shape=[jax.ShapeDtypeStruct((n_tok, MLA_HEADS * MLA_QW), BF16),
                   jax.ShapeDtypeStruct((n_tok, kv_rank), F32),
                   jax.ShapeDtypeStruct((n_tok, LANES), BF16)],
        grid=(n_tok // bm,),
        in_specs=[
            pl.BlockSpec((bm, ws), row),
            pl.BlockSpec((bm, LANES), row),
            pl.BlockSpec((bm, LANES), row),
            pl.BlockSpec((1, q_rank), fix),
            pl.BlockSpec((1, kv_rank), fix),
            pl.BlockSpec(wq_a.shape, fix),
            pl.BlockSpec(wq_b.shape, fix),
        ],
        out_specs=[pl.BlockSpec((bm, MLA_HEADS * MLA_QW), row),
                   pl.BlockSpec((bm, kv_rank), row),
                   pl.BlockSpec((bm, LANES), row)],
        compiler_params=_cparams("arbitrary"),
        name="mla_q_rope",
    )(small, cos_t, sin_t, q_norm.reshape(1, q_rank).astype(F32), kv_norm.reshape(1, kv_rank).astype(F32),
      wq_a, wq_b)


def _kv_expand_kernel(c_ref, w_ref, o_ref, *, n_chunk):
    c = c_ref[...].astype(BF16)
    n = w_ref.shape[1]
    for s in range(0, n, n_chunk):
        o_ref[:, s:s + n_chunk] = jnp.dot(c, w_ref[:, s:s + n_chunk],
                                          preferred_element_type=F32).astype(o_ref.dtype)


def _kv_expand_call(ckv, wkv_b, *, bm=512):
    rows, kr = ckv.shape
    n = wkv_b.shape[1]
    return pl.pallas_call(
        functools.partial(_kv_expand_kernel, n_chunk=1024),
        out_shape=jax.ShapeDtypeStruct((rows, n), BF16),
        grid=(rows // bm,),
        in_specs=[pl.BlockSpec((bm, kr), lambda i: (i, 0)), pl.BlockSpec((kr, n), lambda i: (0, 0))],
        out_specs=pl.BlockSpec((bm, n), lambda i: (i, 0)),
        compiler_params=_cparams("arbitrary"),
        name="mla_kv_expand",
    )(ckv, wkv_b)


def _attn_kernel(q_ref, kv_ref, kpe_ref, z_ref, o_ref, kcat_scr, vext_scr, *, hg, scale):
    c2 = scale * math.log2(math.e)

    @pl.when(pl.program_id(2) == 0)
    def _():
        ones = jnp.ones((kv_ref.shape[0], LANES), BF16)
        for j in range(hg):
            kcat_scr[j, :, 0:LANES] = kv_ref[:, j * 2 * LANES:j * 2 * LANES + LANES]
            kcat_scr[j, :, LANES:2 * LANES] = kpe_ref[...]
            vext_scr[j, :, 0:LANES] = kv_ref[:, j * 2 * LANES + LANES:(j + 1) * 2 * LANES]
            vext_scr[j, :, LANES:2 * LANES] = ones

    scores = [lax.dot_general(q_ref[:, j * MLA_QW:(j + 1) * MLA_QW], kcat_scr[j], (((1,), (1,)), ((), ())),
                              preferred_element_type=F32) for j in range(hg)]
    probs = [jnp.exp2((s - jnp.max(s, axis=-1, keepdims=True)) * c2).astype(BF16) for s in scores]
    for j in range(hg):
        pv = jnp.dot(probs[j], vext_scr[j], preferred_element_type=F32)
        zs = slice(j * MLA_V, (j + 1) * MLA_V)
        o = pv[:, 0:MLA_V] / pv[:, MLA_V:2 * MLA_V]
        o_ref[:, zs] = (o * _silu(z_ref[:, zs].astype(F32))).astype(o_ref.dtype)


def _attn_call(q, kv, kpe, z, *, q_row0, n_seq, q_len, k_len, hg, qb):
    width = MLA_HEADS * MLA_V
    nqb = q_len // qb
    qb0 = q_row0 // qb
    assert q_row0 % qb == 0 and q_len % qb == 0
    scale = float((MLA_NOPE + MLA_ROPE) ** -0.5)
    qrow = lambda b, g, i: (qb0 + b * nqb + i, g)
    return pl.pallas_call(
        functools.partial(_attn_kernel, hg=hg, scale=scale),
        out_shape=jax.ShapeDtypeStruct((n_seq * q_len, width), BF16),
        grid=(n_seq, MLA_HEADS // hg, nqb),
        in_specs=[
            pl.BlockSpec((qb, hg * MLA_QW), qrow),
            pl.BlockSpec((k_len, hg * 2 * LANES), lambda b, g, i: (b, g)),
            pl.BlockSpec((k_len, LANES), lambda b, g, i: (b, 0)),
            pl.BlockSpec((qb, hg * MLA_V), qrow),
        ],
        out_specs=pl.BlockSpec((qb, hg * MLA_V), lambda b, g, i: (b * nqb + i, g)),
        scratch_shapes=[pltpu.VMEM((hg, k_len, MLA_QW), BF16),
                        pltpu.VMEM((hg, k_len, 2 * MLA_V), BF16)],
        compiler_params=_cparams("arbitrary", "arbitrary", "arbitrary"),
        name="mla_attention",
    )(q, kv, kpe, z)


def _mla_weights(w_in, wq_b):
    q_rank = wq_b.shape[0]
    kv_rank = w_in.shape[1] - q_rank - MLA_ROPE - MLA_HEADS * MLA_V
    d = w_in.shape[0]
    c_kpe = q_rank + kv_rank
    zpad = jnp.zeros((d, LANES - MLA_ROPE), w_in.dtype)
    kpe_w = w_in[:, c_kpe:c_kpe + MLA_ROPE]
    w_small = jnp.concatenate([w_in[:, :c_kpe], kpe_w, zpad,
                               kpe_w[:, _ROT_SRC] * _ROT_SIGN, zpad], axis=1)
    w_z = w_in[:, c_kpe + MLA_ROPE:]
    hd = MLA_NOPE + MLA_ROPE
    wq3 = wq_b.reshape(q_rank, MLA_HEADS, hd)
    pe = wq3[:, :, MLA_NOPE:]
    z3 = jnp.zeros((q_rank, MLA_HEADS, LANES - MLA_ROPE), wq_b.dtype)
    wq_a = jnp.concatenate([wq3, z3], axis=-1).reshape(q_rank, MLA_HEADS * MLA_QW)
    wq_r = jnp.concatenate([pe[:, :, _ROT_SRC] * _ROT_SIGN, z3], axis=-1).reshape(q_rank, MLA_HEADS * LANES)
    return w_small.astype(BF16), w_z.astype(BF16), wq_a.astype(BF16), wq_r.astype(BF16)


def kernel(x_prompt, x_sample, state_s5_re, state_s5_im, cache_ckv, cache_kpe, c, c_ctx, norm_g, ada_w, ada_b, final_norm_g, s5_w_in, s5_lam_re, s5_lam_im, s5_log_step, s5_b_re, s5_b_im, s5_c_re, s5_c_im, s5_d, s5_glu_w, s5_glu_b, s5_w_out, pool_w_in, pool_w, pool_scale, pool_w_out, mla_w_in, mla_q_norm, mla_wq_b, mla_kv_norm, mla_wkv_b, mla_w_out):
    n_pseq, p_len, d = x_prompt.shape
    n_sseq, s_len, _ = x_sample.shape
    depth = norm_g.shape[0]
    n_prompt = n_pseq * p_len
    bm = 512
    geo = dict(n_prompt=n_prompt, sample_len=s_len, bm=bm)

    x = (x_prompt.reshape(n_prompt, d), x_sample.reshape(n_sseq * s_len, d))
    conds = jnp.concatenate([c_ctx[None, :], c, jnp.zeros((SUBLANES - 1 - n_sseq, d), F32)], axis=0)
    mods = _ada_call(conds.astype(F32), ada_w, ada_b)
    mods = mods.reshape(depth, SUBLANES, 1, 3 * d)
    s5_mats = _s5_prep_all(s5_lam_re, s5_lam_im, s5_log_step, s5_b_re, s5_b_im, s5_c_re, s5_c_im)

    new_re, new_im, new_ckv, new_kpe = [], [], [], []
    for layer in range(depth):
        kind, j = layer % N_MIXERS, layer // N_MIXERS
        last = layer == depth - 1
        ml = mods[layer]
        if kind == 0:
            width = s5_w_in.shape[2] // 2
            w = s5_w_in[j].astype(BF16)
            u3, z, xc = _inproj_call(x, ml, norm_g[layer], [(w, 0, width), (w, 1, width)], [F32, BF16],
                                     lane_blocked=(0,), s5_chunks=True, **geo)
            act, f_re, f_im = _s5_mix(u3, xc, z, j, s5_mats, s5_d[j], s5_glu_w[j].astype(BF16), s5_glu_b[j],
                                      state_s5_re[:, j], state_s5_im[:, j], n_prompt_seq=n_pseq,
                                      prompt_len=p_len, n_sample_seq=n_sseq, sample_len=s_len, bm=bm)
            new_re.append(f_re)
            new_im.append(f_im)
            w_out = s5_w_out[j]
        elif kind == 1:
            width = pool_w_in.shape[2] // 2
            w = pool_w_in[j].astype(BF16)
            u, z = _inproj_call(x, ml, norm_g[layer], [(w, 0, width), (w, 1, width)], [F32, BF16], **geo)
            act = _pool_call(u, z, pool_w[j].astype(BF16), pool_scale[j], n_prompt=n_prompt,
                             prompt_len=p_len, sample_len=s_len)
            w_out = pool_w_out[j]
        else:
            q_rank, kv_rank = mla_q_norm.shape[-1], mla_kv_norm.shape[-1]
            w_small, w_z, wq_a, wq_r = _mla_weights(mla_w_in[j], mla_wq_b[j])
            small, z = _inproj_call(x, ml, norm_g[layer], [w_small, w_z], [F32, BF16], **geo)
            cos_t, sin_t = _rope_tables(n_prompt, n_sseq, s_len)
            q, ckv_n, kpe_k = _mla_post_call(small, cos_t, sin_t, mla_q_norm[j], mla_kv_norm[j], wq_a, wq_r,
                                             bm=bm)
            wkv = mla_wkv_b[j].astype(BF16)
            past = cache_ckv.shape[2]
            k_len = past + s_len
            ckv_s = jnp.concatenate([cache_ckv[:, j].astype(F32), ckv_n[n_prompt:].reshape(n_sseq, s_len, kv_rank)],
                                    axis=1).reshape(n_sseq * k_len, kv_rank)
            kpe_cache = jnp.concatenate([cache_kpe[:, j].astype(BF16),
                                         jnp.zeros((n_sseq, past, LANES - MLA_ROPE), BF16)], axis=-1)
            kpe_s = jnp.concatenate([kpe_cache, kpe_k[n_prompt:].reshape(n_sseq, s_len, LANES)],
                                    axis=1).reshape(n_sseq * k_len, LANES)
            kv_p = _kv_expand_call(ckv_n[:n_prompt], wkv, bm=bm)
            kv_s = _kv_expand_call(ckv_s, wkv, bm=bm)
            act = (_attn_call(q, kv_p, kpe_k[:n_prompt], z, q_row0=0, n_seq=n_pseq, q_len=p_len,
                              k_len=p_len, hg=MLA_HEADS, qb=p_len),
                   _attn_call(q, kv_s, kpe_s, z, q_row0=n_prompt, n_seq=n_sseq, q_len=s_len,
                              k_len=k_len, hg=4, qb=256))
            new_ckv.append(ckv_n[:n_prompt].reshape(n_pseq, p_len, kv_rank))
            c_kpe = q_rank + kv_rank
            new_kpe.append(small[:n_prompt, c_kpe:c_kpe + MLA_ROPE].reshape(n_pseq, p_len, MLA_ROPE))
            w_out = mla_w_out[j]
        x = _outproj_call(act, x, ml, w_out.astype(BF16), final_norm_g, final_norm=last, split_out=last,
                          n_prompt=n_prompt, sample_len=s_len, bm=2 * bm)

    y_prompt = x[0].reshape(n_pseq, p_len, d)
    y_sample = x[1].reshape(n_sseq, s_len, d)
    return (y_prompt, y_sample, jnp.stack(new_re, axis=1), jnp.stack(new_im, axis=1),
            jnp.stack(new_ckv, axis=1), jnp.stack(new_kpe, axis=1))
```

```python
import functools
import math

import jax
import jax.numpy as jnp
import numpy as np
from jax import lax
from jax.experimental import pallas as pl
from jax.experimental.pallas import tpu as pltpu

S5_GROUP = 16
S5_CHUNK = 16
POOL_WINDOWS = (2, 4, 8, 16)
MLA_HEADS = 16
MLA_NOPE = 128
MLA_ROPE = 64
MLA_V = 128
GRID_W = 64
ROPE_THETA = 10000.0
NORM_EPS = 1e-6
N_MIXERS = 3

LANES = 128
SUBLANES = 8
VMEM_LIMIT_BYTES = 56 * 1024 * 1024

F32 = jnp.float32
BF16 = jnp.bfloat16
HIGHEST = lax.Precision.HIGHEST


def _cparams(*sem):
    return pltpu.CompilerParams(dimension_semantics=sem, vmem_limit_bytes=VMEM_LIMIT_BYTES)


def _sigmoid(x):
    return 0.5 + 0.5 * jnp.tanh(0.5 * x)


def _silu(x):
    return x * _sigmoid(x)


def _gelu_tanh(x):
    c = math.sqrt(2.0 / math.pi)
    hx = 0.5 * x
    return hx + hx * jnp.tanh(x * (c + (c * 0.044715) * (x * x)))


def _ada_kernel(c_ref, w_ref, b_ref, o_ref):
    a = _silu(c_ref[...])
    o_ref[...] = jnp.dot(a, w_ref[...], preferred_element_type=F32, precision=HIGHEST) + b_ref[...]


def _ada_call(conds, ada_w, ada_b):
    depth, d, d3 = ada_w.shape
    c8 = conds.shape[0]
    tn = d3 // 2
    return pl.pallas_call(
        _ada_kernel,
        out_shape=jax.ShapeDtypeStruct((depth, c8, d3), F32),
        grid=(depth, d3 // tn),
        in_specs=[
            pl.BlockSpec((c8, d), lambda l, n: (0, 0)),
            pl.BlockSpec((None, d, tn), lambda l, n: (l, 0, n)),
            pl.BlockSpec((None, 1, tn), lambda l, n: (l, 0, n)),
        ],
        out_specs=pl.BlockSpec((None, c8, tn), lambda l, n: (l, 0, n)),
        compiler_params=_cparams("arbitrary", "arbitrary"),
        name="ada_mod",
    )(conds, ada_w, ada_b.reshape(depth, 1, d3))


def _cond_of_block(i, n_prompt_blocks, blocks_per_sample):
    return jnp.where(i < n_prompt_blocks, 0, 1 + (i - n_prompt_blocks) // blocks_per_sample)


def _modulated(x, mod_ref, g_ref, d):
    ms = jnp.mean(x * x, axis=-1, keepdims=True)
    y = x * lax.rsqrt(ms + NORM_EPS) * g_ref[...]
    shift = mod_ref[:, 0:d]
    scale = mod_ref[:, d:2 * d]
    return (y * (1.0 + scale) + shift).astype(BF16)


def _inproj_kernel(*refs, d, n_chunk, chunk_rows, n_prompt_blocks, n_x):
    if n_x == 2:
        x = jnp.where(pl.program_id(0) < n_prompt_blocks, refs[0][...], refs[1][...])
    else:
        x = refs[0][...]
    mod_ref, g_ref = refs[n_x], refs[n_x + 1]
    rest = refs[n_x + 2:]
    u_scr = None
    if chunk_rows:
        rest, xc_ref, u_scr = rest[:-2], rest[-2], rest[-1]
    n_out = len(rest) // 2
    w_refs, o_refs = rest[:n_out], rest[n_out:]
    h = _modulated(x, mod_ref, g_ref, d)
    for k, (w_ref, o_ref) in enumerate(zip(w_refs, o_refs)):
        n = w_ref.shape[1]
        for c in range(0, n, n_chunk):
            e = min(c + n_chunk, n)
            r = jnp.dot(h, w_ref[:, c:e], preferred_element_type=F32)
            if len(o_ref.shape) == 3:
                for lb in range((e - c) // LANES):
                    part = r[:, lb * LANES:(lb + 1) * LANES]
                    o_ref[c // LANES + lb] = part.astype(o_ref.dtype)
                    if k == 0 and u_scr is not None:
                        u_scr[c // LANES + lb] = part
            else:
                o_ref[:, c:e] = r.astype(o_ref.dtype)
    if chunk_rows:
        _s5_to_chunks_kernel(u_scr, xc_ref, rows=chunk_rows)


def _inproj_call(x, mods_l, norm_g, weights, out_dtypes, *, n_prompt, sample_len, bm=512,
                 lane_blocked=(), s5_chunks=False):
    xs = list(x) if isinstance(x, tuple) else [x]
    n_tok = sum(a.shape[0] for a in xs)
    d = xs[0].shape[1]
    npb, bps = n_prompt // bm, sample_len // bm
    cond = functools.partial(_cond_of_block, n_prompt_blocks=npb, blocks_per_sample=bps)
    weights = [w if isinstance(w, tuple) else (w, 0, w.shape[1]) for w in weights]
    x_specs = _split_specs((bm, d), npb) if len(xs) == 2 else [pl.BlockSpec((bm, d), lambda i: (i, 0))]
    in_specs = x_specs + [
        pl.BlockSpec((None, 1, 3 * d), lambda i: (cond(i), 0, 0)),
        pl.BlockSpec((1, d), lambda i: (0, 0)),
    ] + [pl.BlockSpec((d, n), functools.partial(lambda i, blk: (0, blk), blk=blk)) for _, blk, n in weights]
    out_specs, out_shape = [], []
    for k, ((_, _, n), dt) in enumerate(zip(weights, out_dtypes)):
        if k in lane_blocked:
            out_specs.append(pl.BlockSpec((n // LANES, bm, LANES), lambda i: (0, i, 0)))
            out_shape.append(jax.ShapeDtypeStruct((n // LANES, n_tok, LANES), dt))
        else:
            out_specs.append(pl.BlockSpec((bm, n), lambda i: (i, 0)))
            out_shape.append(jax.ShapeDtypeStruct((n_tok, n), dt))
    chunk_rows = bm // S5_CHUNK if s5_chunks else 0
    scratch = []
    if s5_chunks:
        assert 0 in lane_blocked
        n_groups = weights[0][2] // S5_GROUP
        out_specs.append(pl.BlockSpec((n_groups, chunk_rows, 2 * LANES), lambda i: (0, i, 0)))
        out_shape.append(jax.ShapeDtypeStruct((n_groups, n_tok // S5_CHUNK, 2 * LANES), BF16))
        scratch = [pltpu.VMEM((weights[0][2] // LANES, bm, LANES), F32)]
    return pl.pallas_call(
        functools.partial(_inproj_kernel, d=d, n_chunk=512, chunk_rows=chunk_rows,
                          n_prompt_blocks=npb, n_x=len(xs)),
        out_shape=out_shape,
        grid=(n_tok // bm,),
        in_specs=in_specs,
        out_specs=out_specs,
        scratch_shapes=scratch,
        compiler_params=_cparams("arbitrary"),
        name="norm_mod_inproj",
    )(*xs, mods_l, norm_g.reshape(1, d), *[w for w, _, _ in weights])


def _outproj_kernel(*refs, d, final_norm, n_prompt_blocks, n_act, n_x):
    a_refs, x_refs = refs[:n_act], refs[n_act:n_act + n_x]
    mod_ref, w_ref, fg_ref = refs[n_act + n_x:n_act + n_x + 3]
    o_refs = refs[n_act + n_x + 3:]

    def finish(a_ref, x_ref, o_ref):
        y = jnp.dot(a_ref[...], w_ref[...], preferred_element_type=F32)
        gate = mod_ref[:, 2 * d:3 * d]
        xn = x_ref[...] + gate * y
        if final_norm:
            ms = jnp.mean(xn * xn, axis=-1, keepdims=True)
            xn = xn * lax.rsqrt(ms + NORM_EPS) * fg_ref[...]
        o_ref[...] = xn

    if max(n_act, n_x, len(o_refs)) == 1:
        finish(a_refs[0], x_refs[0], o_refs[0])
    else:
        is_prompt = pl.program_id(0) < n_prompt_blocks
        pl.when(is_prompt)(functools.partial(finish, a_refs[0], x_refs[0], o_refs[0]))
        pl.when(jnp.logical_not(is_prompt))(functools.partial(finish, a_refs[-1], x_refs[-1], o_refs[-1]))


def _split_specs(block, npb):
    return [pl.BlockSpec(block, lambda i: (jnp.minimum(i, npb - 1), 0)),
            pl.BlockSpec(block, lambda i: (jnp.maximum(i - npb, 0), 0))]


def _outproj_call(act, x, mods_l, w_out, final_g, *, n_prompt, sample_len, final_norm, bm=512,
                  split_out=False):
    acts = list(act) if isinstance(act, tuple) else [act]
    xs = list(x) if isinstance(x, tuple) else [x]
    n_tok = sum(a.shape[0] for a in xs)
    d, k = xs[0].shape[1], acts[0].shape[1]
    npb, bps = n_prompt // bm, sample_len // bm
    cond = functools.partial(_cond_of_block, n_prompt_blocks=npb, blocks_per_sample=bps)
    row = lambda i: (i, 0)
    act_specs = _split_specs((bm, k), npb) if len(acts) == 2 else [pl.BlockSpec((bm, k), row)]
    x_specs = _split_specs((bm, d), npb) if len(xs) == 2 else [pl.BlockSpec((bm, d), row)]
    if split_out:
        out_shape = [jax.ShapeDtypeStruct((n_prompt, d), F32), jax.ShapeDtypeStruct((n_tok - n_prompt, d), F32)]
        out_specs = _split_specs((bm, d), npb)
    else:
        out_shape = jax.ShapeDtypeStruct((n_tok, d), F32)
        out_specs = pl.BlockSpec((bm, d), row)
    return pl.pallas_call(
        functools.partial(_outproj_kernel, d=d, final_norm=final_norm, n_prompt_blocks=npb,
                          n_act=len(acts), n_x=len(xs)),
        out_shape=out_shape,
        grid=(n_tok // bm,),
        in_specs=act_specs + x_specs + [
            pl.BlockSpec((None, 1, 3 * d), lambda i: (cond(i), 0, 0)),
            pl.BlockSpec((k, d), lambda i: (0, 0)),
            pl.BlockSpec((1, d), lambda i: (0, 0)),
        ],
        out_specs=out_specs,
        compiler_params=_cparams("arbitrary"),
        name="outproj_residual",
    )(*acts, *xs, mods_l, w_out, final_g.reshape(1, d))


def _s5_time_of_lane_block():
    pos = np.arange(S5_CHUNK)
    half, blk = pos // 8, pos % 8
    g8 = np.arange(8)[:, None]
    return 8 * half[None, :] + (blk[None, :] - g8) % 8


def _s5_tables(lam_re, lam_im, log_step, b_re, b_im, c_re, c_im):
    t_chunk = S5_CHUNK
    n_groups, n_state = lam_re.shape[1], lam_re.shape[2]
    n_oct = n_groups // 8
    lam = lax.complex(lam_re.astype(F32), lam_im.astype(F32))
    step = jnp.exp(log_step.astype(F32))[..., None]
    lam_bar = jnp.exp(lam * step)
    b_bar = ((lam_bar - 1.0) / lam)[..., None] * lax.complex(b_re.astype(F32), b_im.astype(F32))
    c_mat = lax.complex(c_re.astype(F32), c_im.astype(F32))
    ks = jnp.arange(t_chunk + 1, dtype=F32)[:, None, None, None]
    pw = jnp.exp(ks * (lam * step)[None])

    zeros = jnp.zeros((t_chunk - 1, n_groups, n_state), pw.dtype)
    lag_f = jnp.concatenate([zeros, pw[:t_chunk, 0], zeros[:1]], axis=0)
    lag_b = jnp.concatenate([pw[t_chunk - 1::-1, 1], zeros, zeros[:1]], axis=0)
    plag = jnp.concatenate([lag_f, lag_b], axis=-1).transpose(1, 0, 2)
    plag = jnp.stack([plag.real, plag.imag])

    tl = _s5_time_of_lane_block()
    pw_ri = jnp.stack([pw.real, pw.imag]).reshape(2, t_chunk + 1, 2, n_oct, 8, n_state)
    m_idx = np.arange(t_chunk + 1)[None, None, :]

    def power_table(exponent, direction):
        sel = (exponent[:, :, None] == m_idx).astype(np.float32)
        tab = jnp.einsum('kxm,rmakp->rakxp', sel, pw_ri[:, :, direction], precision=HIGHEST)
        return tab.reshape(2, n_groups, t_chunk, n_state)

    def both(fwd, bwd):
        m = jnp.concatenate([fwd, bwd], axis=-1)
        return jnp.stack([m.real, m.imag])

    tin = jnp.concatenate([power_table(t_chunk - 1 - tl, 0), power_table(tl, 1)], axis=-1)
    tout = jnp.concatenate([power_table(tl + 1, 0), power_table(t_chunk - tl, 1)], axis=-1)
    bt = both(b_bar[0].transpose(0, 2, 1), b_bar[1].transpose(0, 2, 1))
    ct = both(c_mat[0], c_mat[1])
    lam_rows = both(pw[t_chunk, 0][:, None], pw[t_chunk, 1][:, None])[:, :, 0]
    return plag, tin, tout, bt, ct, lam_rows


def _s5_kmat_kernel(plag_ref, tin_ref, tout_ref, bt_ref, ct_ref, k_ref, pin_ref, pot_ref, x_scr, v_scr):
    masks = _lane_block_masks()
    n_lag = 2 * S5_CHUNK - 1
    for g8 in range(8):
        br, bi = bt_ref[0, g8], bt_ref[1, g8]
        cr, ci = ct_ref[0, g8], ct_ref[1, g8]
        def split(a):
            hi = a.astype(BF16)
            return hi, (a - hi.astype(F32)).astype(BF16)

        for m in range(n_lag):
            rows = slice(m * S5_GROUP, (m + 1) * S5_GROUP)
            pr, pi = plag_ref[0, g8, m:m + 1, :], plag_ref[1, g8, m:m + 1, :]
            for c0, part in ((0, cr * pr - ci * pi), (LANES, -(cr * pi + ci * pr))):
                x_scr[0, rows, c0:c0 + LANES], x_scr[1, rows, c0:c0 + LANES] = split(part)
        b_hi, b_lo = split(jnp.concatenate([jnp.concatenate([br, bi], axis=1)] * (LANES // S5_GROUP), axis=0))
        nt = functools.partial(lax.dot_general, dimension_numbers=(((1,), (1,)), ((), ())),
                               preferred_element_type=F32)
        v_scr[...] = nt(x_scr[0], b_hi) + nt(x_scr[0], b_lo) + nt(x_scr[1], b_hi)
        for pos in range(S5_CHUNK):
            rows = slice(pos * S5_GROUP, (pos + 1) * S5_GROUP)
            tr, ti = tin_ref[0, g8, pos:pos + 1, :], tin_ref[1, g8, pos:pos + 1, :]
            pin_ref[g8, rows, 0:LANES] = (tr * br - ti * bi).astype(BF16)
            pin_ref[g8, rows, LANES:2 * LANES] = (tr * bi + ti * br).astype(BF16)
            tr, ti = tout_ref[0, g8, pos:pos + 1, :], tout_ref[1, g8, pos:pos + 1, :]
            pot_ref[g8, rows, 0:LANES] = (tr * cr - ti * ci).astype(BF16)
            pot_ref[g8, rows, LANES:2 * LANES] = (-(tr * ci + ti * cr)).astype(BF16)
        for pos in range(S5_CHUNK):
            tau = 8 * (pos // 8) + (pos % 8 - g8) % 8
            rows = slice(pos * S5_GROUP, (pos + 1) * S5_GROUP)
            for half in range(2):
                acc = None
                for blk in range(8):
                    sigma = 8 * half + (blk - g8) % 8
                    m = S5_CHUNK - 1 - sigma + tau
                    src = v_scr[m * S5_GROUP:(m + 1) * S5_GROUP, :]
                    acc = src if acc is None else jnp.where(masks[blk], src, acc)
                k_ref[g8, rows, half * LANES:(half + 1) * LANES] = acc.astype(BF16)


def _s5_kmat_call(plag, tin, tout, bt, ct):
    n_groups = plag.shape[1]
    n_lag_rows = (2 * S5_CHUNK - 1) * S5_GROUP
    lag_spec = pl.BlockSpec((2, 8) + plag.shape[2:], lambda i: (0, i, 0, 0))
    tab_spec = pl.BlockSpec((2, 8, S5_GROUP, LANES), lambda i: (0, i, 0, 0))
    mat = jax.ShapeDtypeStruct((n_groups, 2 * LANES, 2 * LANES), BF16)
    mat_spec = pl.BlockSpec((8, 2 * LANES, 2 * LANES), lambda i: (i, 0, 0))
    return pl.pallas_call(
        _s5_kmat_kernel,
        out_shape=[mat, mat, mat],
        grid=(n_groups // 8,),
        in_specs=[lag_spec, tab_spec, tab_spec, tab_spec, tab_spec],
        out_specs=[mat_spec, mat_spec, mat_spec],
        scratch_shapes=[pltpu.VMEM((2, n_lag_rows, 2 * LANES), BF16), pltpu.VMEM((n_lag_rows, LANES), F32)],
        compiler_params=_cparams("arbitrary"),
        name="s5_kmat",
    )(plag, tin, tout, bt, ct)


def _lane_block_masks():
    blk = lax.broadcasted_iota(jnp.int32, (1, LANES), 1) // S5_GROUP
    return [blk == b for b in range(8)]


def _diagonal_merge(src):
    blk = lax.broadcasted_iota(jnp.int32, (1, LANES), 1) // S5_GROUP
    q = list(src)
    for bit in (1, 2, 4):
        take = (blk & bit) != 0
        q = [jnp.where(take, q[(x + bit) % 8], q[x]) for x in range(8)]
    return [q[(-t) % 8] for t in range(8)]


def _s5_to_chunks_kernel(u_ref, x_ref, *, rows):
    for o in range(u_ref.shape[0]):
        for r0 in range(0, rows, SUBLANES):
            for half in range(2):
                rolled = []
                for t8 in range(8):
                    v = u_ref[o, pl.ds(r0 * S5_CHUNK + 8 * half + t8, SUBLANES, stride=S5_CHUNK), :]
                    rolled.append(pltpu.roll(v, t8 * S5_GROUP, 1) if t8 else v)
                for g8, merged in enumerate(_diagonal_merge(rolled)):
                    x_ref[o * 8 + g8, r0:r0 + SUBLANES, half * LANES:(half + 1) * LANES] = merged.astype(BF16)


def _s5_tail_kernel(yc_ref, u_ref, d_ref, z_ref, w_ref, b_ref, o_ref, nat_scr, y_scr, *, rows, n_chunk):
    n_blk = u_ref.shape[0]
    tile = 2 * SUBLANES
    per = n_chunk // LANES
    for piece, r0 in enumerate(range(0, rows, tile)):
        tok = slice(r0 * S5_CHUNK, (r0 + tile) * S5_CHUNK)
        for o in range(n_blk):
            for half in range(2):
                src = [yc_ref[o * 8 + g8, r0:r0 + tile, half * LANES:(half + 1) * LANES] for g8 in range(8)]
                for t8, merged in enumerate(_diagonal_merge(src)):
                    nat = pltpu.roll(merged, (8 - t8) * S5_GROUP, 1) if t8 else merged
                    nat_scr[piece % 2, o, pl.ds(8 * half + t8, tile, stride=S5_CHUNK), :] = nat
            d_vec = d_ref[:, o * LANES:(o + 1) * LANES]
            y_scr[o, tok] = _gelu_tanh(nat_scr[piece % 2, o] + d_vec * u_ref[o, tok])
        yb = jnp.concatenate([y_scr[o, tok].astype(BF16) for o in range(n_blk)], axis=1)
        for c in range(0, n_blk, per):
            sl = slice(c * LANES, (c + per) * LANES)
            gate = _sigmoid(jnp.dot(yb, w_ref[:, sl], preferred_element_type=F32) + b_ref[:, sl])
            y = jnp.concatenate([y_scr[c + k, tok] for k in range(per)], axis=1)
            o_ref[tok, sl] = (y * gate * _silu(z_ref[tok, sl].astype(F32))).astype(o_ref.dtype)


def _s5_tail_call(yc, u3, d_skip, z, glu_w, glu_b, *, rows=32):
    n_blk, n_tok, _ = u3.shape
    n_groups, n_rows, _ = yc.shape
    width = n_blk * LANES
    bm = rows * S5_CHUNK
    tile_tok = 2 * SUBLANES * S5_CHUNK
    fix = lambda i: (0, 0)
    return pl.pallas_call(
        functools.partial(_s5_tail_kernel, rows=rows, n_chunk=min(512, width)),
        out_shape=jax.ShapeDtypeStruct((n_tok, width), BF16),
        grid=(n_rows // rows,),
        in_specs=[pl.BlockSpec((n_groups, rows, 2 * LANES), lambda i: (0, i, 0)),
                  pl.BlockSpec((n_blk, bm, LANES), lambda i: (0, i, 0)),
                  pl.BlockSpec((1, width), fix),
                  pl.BlockSpec((bm, width), lambda i: (i, 0)),
                  pl.BlockSpec((width, width), fix),
                  pl.BlockSpec((1, width), fix)],
        out_specs=pl.BlockSpec((bm, width), lambda i: (i, 0)),
        scratch_shapes=[pltpu.VMEM((2, n_blk, tile_tok, LANES), F32),
                        pltpu.VMEM((n_blk, bm, LANES), F32)],
        compiler_params=_cparams("arbitrary"),
        name="s5_tail",
    )(yc, u3, d_skip.reshape(1, width).astype(F32), z, glu_w, glu_b.reshape(1, width).astype(F32))


def _s5_chunk_kernel(x_ref, kt_ref, pin_ref, po_ref, lam_ref, h0r_ref, h0i_ref,
                     y_ref, fr_ref, fi_ref, r_scr, st_scr, yi_scr, *, segments, seq_block):
    gb = SUBLANES
    rows = x_ref.shape[1]
    lane = lax.broadcasted_iota(jnp.int32, (1, LANES), 1)
    fwd_lanes = lane < (LANES // 2)
    del rows
    seg_rows = [(row0, n_seq * n_chunks) for row0, n_seq, n_chunks, _, _ in segments]
    for row0, n_rows in seg_rows:
        for g in range(gb):
            x = x_ref[g, row0:row0 + n_rows, :]
            yi_scr[g, row0:row0 + n_rows, :] = lax.dot_general(x, kt_ref[g], (((1,), (1,)), ((), ())),
                                                               preferred_element_type=F32)
            r = jnp.dot(x, pin_ref[g], preferred_element_type=F32)
            of_group = pl.ds(row0 * gb + g, n_rows, stride=gb)
            r_scr[0, of_group, :] = r[:, 0:LANES]
            r_scr[1, of_group, :] = r[:, LANES:2 * LANES]
    ar, ai = lam_ref[0], lam_ref[1]
    for row0, n_seq, n_chunks, from_input, to_output in segments:
        for b0 in range(0, n_seq, seq_block):
            nb = min(seq_block, n_seq - b0)

            def step(i, carry, row0=row0, n_chunks=n_chunks, b0=b0, nb=nb):
                out = []
                for k in range(nb):
                    base = row0 + (b0 + k) * n_chunks
                    at_f = pl.ds((base + i) * gb, gb)
                    at_b = pl.ds((base + (n_chunks - 1) - i) * gb, gb)
                    s_re, s_im = carry[k]
                    half = LANES // 2
                    st_scr[0, at_f, 0:half] = s_re[:, 0:half]
                    st_scr[0, at_b, half:LANES] = s_re[:, half:LANES]
                    st_scr[1, at_f, 0:half] = s_im[:, 0:half]
                    st_scr[1, at_b, half:LANES] = s_im[:, half:LANES]
                    v_re = jnp.where(fwd_lanes, r_scr[0, at_f, :], r_scr[0, at_b, :])
                    v_im = jnp.where(fwd_lanes, r_scr[1, at_f, :], r_scr[1, at_b, :])
                    out.append((ar * s_re - ai * s_im + v_re, ar * s_im + ai * s_re + v_im))
                return tuple(out)

            if from_input:
                init = tuple((h0r_ref[b0 + k], h0i_ref[b0 + k]) for k in range(nb))
            else:
                init = tuple((jnp.zeros((gb, LANES), F32),) * 2 for _ in range(nb))
            fin = init
            for i in range(n_chunks):
                fin = step(i, fin)
            if to_output:
                for k in range(nb):
                    fr_ref[b0 + k] = fin[k][0]
                    fi_ref[b0 + k] = fin[k][1]
    for row0, n_rows in seg_rows:
        for g in range(gb):
            of_group = pl.ds(row0 * gb + g, n_rows, stride=gb)
            st = jnp.concatenate([st_scr[cb, of_group, :] for cb in range(2)], axis=1).astype(BF16)
            rows = slice(row0, row0 + n_rows)
            y_ref[g, rows, :] = (yi_scr[g, rows, :] + lax.dot_general(
                st, po_ref[g], (((1,), (1,)), ((), ())), preferred_element_type=F32)).astype(y_ref.dtype)


def _s5_chunk_call(xc, kt, pin, pout, lam_rows, h0_re, h0_im, *, layer, segments, n_final):
    n_groups, rows, _ = xc.shape
    gb = SUBLANES
    s_in = h0_re.shape[0]
    kern = functools.partial(_s5_chunk_kernel, segments=segments, seq_block=8)
    g3 = lambda i: (i, 0, 0)
    blk0 = layer * (n_groups // gb)
    p3 = lambda i: (i + blk0, 0, 0)
    mid = lambda i: (0, i, 0)
    return pl.pallas_call(
        kern,
        out_shape=[jax.ShapeDtypeStruct((n_groups, rows, 2 * LANES), F32),
                   jax.ShapeDtypeStruct((n_final, n_groups, LANES), F32),
                   jax.ShapeDtypeStruct((n_final, n_groups, LANES), F32)],
        grid=(n_groups // gb,),
        in_specs=[
            pl.BlockSpec((gb, rows, 2 * LANES), g3),
            pl.BlockSpec((gb, 2 * LANES, 2 * LANES), p3),
            pl.BlockSpec((gb, 2 * LANES, 2 * LANES), p3),
            pl.BlockSpec((gb, 2 * LANES, 2 * LANES), p3),
            pl.BlockSpec((2, gb, LANES), lambda i: (0, i + blk0, 0)),
            pl.BlockSpec((s_in, gb, LANES), mid),
            pl.BlockSpec((s_in, gb, LANES), mid),
        ],
        out_specs=[pl.BlockSpec((gb, rows, 2 * LANES), g3),
                   pl.BlockSpec((n_final, gb, LANES), mid),
                   pl.BlockSpec((n_final, gb, LANES), mid)],
        scratch_shapes=[pltpu.VMEM((2, rows * gb, LANES), F32),
                        pltpu.VMEM((2, rows * gb, LANES), F32),
                        pltpu.VMEM((gb, rows, 2 * LANES), F32)],
        compiler_params=_cparams("arbitrary"),
        name="s5_chunk_scan",
    )(xc, kt, pin, pout, lam_rows, h0_re, h0_im)


def _s5_prep_all(lam_re, lam_im, log_step, b_re, b_im, c_re, c_im):
    tabs = jax.vmap(_s5_tables)(lam_re, lam_im, log_step, b_re, b_im, c_re, c_im)
    plag, tin, tout, bt, ct, lam_rows = [jnp.moveaxis(t, 0, 1).reshape((2, -1) + t.shape[3:]) for t in tabs]
    kt, pin, pot = _s5_kmat_call(plag, tin, tout, bt, ct)
    return kt, pin, pot, lam_rows


def _s5_mix(u3, xc, z, layer, mats, d_skip, glu_w, glu_b, st_re, st_im, *, n_prompt_seq, prompt_len,
            n_sample_seq, sample_len, bm):
    kt, pin, pout, lam_rows = mats
    n_state = LANES // 2
    pc, sc = prompt_len // S5_CHUNK, sample_len // S5_CHUNK

    def state_rows(s):
        return jnp.concatenate([s[:, 0], s[:, 1]], axis=-1).astype(F32)

    segments = ((0, n_prompt_seq, pc, False, True), (n_prompt_seq * pc, n_sample_seq, sc, True, False))
    yc, fr, fi = _s5_chunk_call(xc, kt, pin, pout, lam_rows, state_rows(st_re), state_rows(st_im),
                                layer=layer, segments=segments, n_final=n_prompt_seq)
    act = _s5_tail_call(yc, u3, d_skip, z, glu_w, glu_b, rows=bm // S5_CHUNK)

    def unpack(f):
        return jnp.stack([f[:, :, :n_state], f[:, :, n_state:]], axis=1)

    return act, unpack(fr), unpack(fi)


def _pool_kernel(u_ref, z_ref, w_ref, s_ref, o_ref, *, n_prompt_blocks, prompt_len, sample_len):
    rows = u_ref.shape[0]
    seq_len = jnp.where(pl.program_id(0) < n_prompt_blocks, prompt_len, sample_len)
    t = lax.broadcasted_iota(jnp.int32, (rows, 1), 0) & (seq_len - 1)

    def later(x, k):
        return jnp.where(t + k < seq_len, pltpu.roll(x, rows - k, 0), 0.0)

    def earlier(x, k):
        return jnp.where(t >= k, pltpu.roll(x, k, 0), 0.0)

    def body(win):
        lo = win // 2
        u = u_ref[...]
        fwd = u
        bwd = earlier(u, 1)
        s = 1
        while s < lo:
            fwd = fwd + later(fwd, s)
            bwd = bwd + earlier(bwd, s)
            s *= 2
        cnt = jnp.minimum(t - lo + win, seq_len) - jnp.maximum(t - lo, 0)
        p = (fwd + bwd) / cnt.astype(F32) - u
        m = jnp.dot(p.astype(BF16), w_ref[...], preferred_element_type=F32) * s_ref[...]
        o_ref[...] = (m * _silu(z_ref[...].astype(F32))).astype(o_ref.dtype)

    for gi, win in enumerate(POOL_WINDOWS):
        pl.when(pl.program_id(1) == gi)(functools.partial(body, win))


def _pool_call(u, z, pool_w, pool_scale, *, n_prompt, prompt_len, sample_len, rows=2048):
    n_tok, width = u.shape
    n_groups = len(POOL_WINDOWS)
    cg = width // n_groups
    assert prompt_len & (prompt_len - 1) == 0 and sample_len & (sample_len - 1) == 0
    assert rows % prompt_len == 0 and rows % sample_len == 0 and n_prompt % rows == 0
    kern = functools.partial(_pool_kernel, n_prompt_blocks=n_prompt // rows, prompt_len=prompt_len,
                             sample_len=sample_len)
    return pl.pallas_call(
        kern,
        out_shape=jax.ShapeDtypeStruct((n_tok, width), BF16),
        grid=(n_tok // rows, n_groups),
        in_specs=[
            pl.BlockSpec((rows, cg), lambda i, g: (i, g)),
            pl.BlockSpec((rows, cg), lambda i, g: (i, g)),
            pl.BlockSpec((None, cg, cg), lambda i, g: (g, 0, 0)),
            pl.BlockSpec((1, cg), lambda i, g: (0, g)),
        ],
        out_specs=pl.BlockSpec((rows, cg), lambda i, g: (i, g)),
        compiler_params=_cparams("arbitrary", "arbitrary"),
        name="pool_mix",
    )(u, z, pool_w, pool_scale.reshape(1, width).astype(F32))


MLA_QW = 2 * LANES

_ROT_SRC = np.concatenate([np.arange(16, 32), np.arange(0, 16), np.arange(48, 64), np.arange(32, 48)])
_ROT_SIGN = np.concatenate([-np.ones(16), np.ones(16), -np.ones(16), np.ones(16)]).astype(np.float32)


def _rope_tables(n_prompt, n_sample_seq, sample_len):
    half = MLA_ROPE // 4
    tok = jnp.arange(sample_len)
    row = (tok // GRID_W).astype(F32)
    col = (tok % GRID_W).astype(F32)
    inv = ROPE_THETA ** (-jnp.arange(half, dtype=F32) / half)
    a_row, a_col = row[:, None] * inv, col[:, None] * inv
    cos = jnp.concatenate([jnp.cos(a_row), jnp.cos(a_row), jnp.cos(a_col), jnp.cos(a_col)], axis=-1)
    sin = jnp.concatenate([jnp.sin(a_row), jnp.sin(a_row), jnp.sin(a_col), jnp.sin(a_col)], axis=-1)
    pad = jnp.zeros((sample_len, LANES - MLA_ROPE), F32)
    cos_s = jnp.tile(jnp.concatenate([cos, pad], axis=-1), (n_sample_seq, 1))
    sin_s = jnp.tile(jnp.concatenate([sin, pad], axis=-1), (n_sample_seq, 1))
    cos_p = jnp.concatenate([jnp.ones((n_prompt, MLA_ROPE), F32), jnp.zeros((n_prompt, LANES - MLA_ROPE), F32)], -1)
    return jnp.concatenate([cos_p, cos_s]), jnp.concatenate([jnp.zeros((n_prompt, LANES), F32), sin_s])


def _rms(x, g):
    return x * lax.rsqrt(jnp.mean(x * x, axis=-1, keepdims=True) + NORM_EPS) * g


def _mla_post_kernel(sm_ref, cos_ref, sin_ref, qn_ref, kn_ref, wa_ref, wb_ref,
                     q_ref, ckv_ref, kpe_ref, *, q_rank, kv_rank, heads_per_dot):
    cosp, sinp = cos_ref[...], sin_ref[...]
    qn = _rms(sm_ref[:, 0:q_rank], qn_ref[...]).astype(BF16)
    for h0 in range(0, MLA_HEADS, heads_per_dot):
        a = jnp.dot(qn, wa_ref[:, h0 * MLA_QW:(h0 + heads_per_dot) * MLA_QW], preferred_element_type=F32)
        b = jnp.dot(qn, wb_ref[:, h0 * LANES:(h0 + heads_per_dot) * LANES], preferred_element_type=F32)
        for j in range(heads_per_dot):
            h = h0 + j
            q_ref[:, h * MLA_QW:h * MLA_QW + LANES] = a[:, j * MLA_QW:j * MLA_QW + LANES].astype(BF16)
            pe = a[:, j * MLA_QW + LANES:(j + 1) * MLA_QW] * cosp + b[:, j * LANES:(j + 1) * LANES] * sinp
            q_ref[:, h * MLA_QW + LANES:(h + 1) * MLA_QW] = pe.astype(BF16)
    c0 = q_rank
    ckv_ref[...] = _rms(sm_ref[:, c0:c0 + kv_rank], kn_ref[...])
    k0 = c0 + kv_rank
    kpe_ref[...] = (sm_ref[:, k0:k0 + LANES] * cosp + sm_ref[:, k0 + LANES:k0 + 2 * LANES] * sinp).astype(BF16)


def _mla_post_call(small, cos_t, sin_t, q_norm, kv_norm, wq_a, wq_b, *, bm=512):
    n_tok, ws = small.shape
    q_rank, kv_rank = q_norm.shape[-1], kv_norm.shape[-1]
    row = lambda i: (i, 0)
    fix = lambda i: (0, 0)
    kern = functools.partial(_mla_post_kernel, q_rank=q_rank, kv_rank=kv_rank, heads_per_dot=4)
    return pl.pallas_call(
        kern,
        out_shape=[jax.ShapeDtypeStruct((n_tok, MLA_HEADS * MLA_QW), BF16),
                   jax.ShapeDtypeStruct((n_tok, kv_rank), F32),
                   jax.ShapeDtypeStruct((n_tok, LANES), BF16)],
        grid=(n_tok // bm,),
        in_specs=[
            pl.BlockSpec((bm, ws), row),
            pl.BlockSpec((bm, LANES), row),
            pl.BlockSpec((bm, LANES), row),
            pl.BlockSpec((1, q_rank), fix),
            pl.BlockSpec((1, kv_rank), fix),
            pl.BlockSpec(wq_a.shape, fix),
            pl.BlockSpec(wq_b.shape, fix),
        ],
        out_specs=[pl.BlockSpec((bm, MLA_HEADS * MLA_QW), row),
                   pl.BlockSpec((bm, kv_rank), row),
                   pl.BlockSpec((bm, LANES), row)],
        compiler_params=_cparams("arbitrary"),
        name="mla_q_rope",
    )(small, cos_t, sin_t, q_norm.reshape(1, q_rank).astype(F32), kv_norm.reshape(1, kv_rank).astype(F32),
      wq_a, wq_b)


def _attn_kernel(q_ref, ckv_ref, wkv_ref, kpe_ref, z_ref, o_ref, kcat_scr, vext_scr, *, hg, scale):
    c2 = scale * math.log2(math.e)

    @pl.when(pl.program_id(2) == 0)
    def _():
        ckv = ckv_ref[...].astype(BF16)
        ones = jnp.ones((ckv.shape[0], LANES), BF16)
        for j in range(hg):
            kv = jnp.dot(ckv, wkv_ref[:, j * 2 * LANES:(j + 1) * 2 * LANES],
                         preferred_element_type=F32).astype(BF16)
            kcat_scr[j, :, 0:LANES] = kv[:, 0:LANES]
            kcat_scr[j, :, LANES:2 * LANES] = kpe_ref[...]
            vext_scr[j, :, 0:LANES] = kv[:, LANES:2 * LANES]
            vext_scr[j, :, LANES:2 * LANES] = ones

    scores = [lax.dot_general(q_ref[:, j * MLA_QW:(j + 1) * MLA_QW], kcat_scr[j], (((1,), (1,)), ((), ())),
                              preferred_element_type=F32) for j in range(hg)]
    probs = [jnp.exp2((s - jnp.max(s, axis=-1, keepdims=True)) * c2).astype(BF16) for s in scores]
    for j in range(hg):
        pv = jnp.dot(probs[j], vext_scr[j], preferred_element_type=F32)
        zs = slice(j * MLA_V, (j + 1) * MLA_V)
        o = pv[:, 0:MLA_V] / pv[:, MLA_V:2 * MLA_V]
        o_ref[:, zs] = (o * _silu(z_ref[:, zs].astype(F32))).astype(o_ref.dtype)


def _attn_call(q, ckv, wkv_b, kpe, z, *, q_row0, n_seq, q_len, k_len, hg, qb):
    width = MLA_HEADS * MLA_V
    nqb = q_len // qb
    qb0 = q_row0 // qb
    assert q_row0 % qb == 0 and q_len % qb == 0
    scale = float((MLA_NOPE + MLA_ROPE) ** -0.5)
    qrow = lambda b, g, i: (qb0 + b * nqb + i, g)
    return pl.pallas_call(
        functools.partial(_attn_kernel, hg=hg, scale=scale),
        out_shape=jax.ShapeDtypeStruct((n_seq * q_len, width), BF16),
        grid=(n_seq, MLA_HEADS // hg, nqb),
        in_specs=[
            pl.BlockSpec((qb, hg * MLA_QW), qrow),
            pl.BlockSpec((k_len, ckv.shape[1]), lambda b, g, i: (b, 0)),
            pl.BlockSpec((wkv_b.shape[0], hg * 2 * LANES), lambda b, g, i: (0, g)),
            pl.BlockSpec((k_len, LANES), lambda b, g, i: (b, 0)),
            pl.BlockSpec((qb, hg * MLA_V), qrow),
        ],
        out_specs=pl.BlockSpec((qb, hg * MLA_V), lambda b, g, i: (b * nqb + i, g)),
        scratch_shapes=[pltpu.VMEM((hg, k_len, MLA_QW), BF16),
                        pltpu.VMEM((hg, k_len, 2 * MLA_V), BF16)],
        compiler_params=_cparams("arbitrary", "arbitrary", "arbitrary"),
        name="mla_attention",
    )(q, ckv, wkv_b, kpe, z)


def _mla_weights(w_in, wq_b):
    q_rank = wq_b.shape[0]
    kv_rank = w_in.shape[1] - q_rank - MLA_ROPE - MLA_HEADS * MLA_V
    d = w_in.shape[0]
    c_kpe = q_rank + kv_rank
    zpad = jnp.zeros((d, LANES - MLA_ROPE), w_in.dtype)
    kpe_w = w_in[:, c_kpe:c_kpe + MLA_ROPE]
    w_small = jnp.concatenate([w_in[:, :c_kpe], kpe_w, zpad,
                               kpe_w[:, _ROT_SRC] * _ROT_SIGN, zpad], axis=1)
    w_z = w_in[:, c_kpe + MLA_ROPE:]
    hd = MLA_NOPE + MLA_ROPE
    wq3 = wq_b.reshape(q_rank, MLA_HEADS, hd)
    pe = wq3[:, :, MLA_NOPE:]
    z3 = jnp.zeros((q_rank, MLA_HEADS, LANES - MLA_ROPE), wq_b.dtype)
    wq_a = jnp.concatenate([wq3, z3], axis=-1).reshape(q_rank, MLA_HEADS * MLA_QW)
    wq_r = jnp.concatenate([pe[:, :, _ROT_SRC] * _ROT_SIGN, z3], axis=-1).reshape(q_rank, MLA_HEADS * LANES)
    return w_small.astype(BF16), w_z.astype(BF16), wq_a.astype(BF16), wq_r.astype(BF16)


def kernel(x_prompt, x_sample, state_s5_re, state_s5_im, cache_ckv, cache_kpe, c, c_ctx, norm_g, ada_w, ada_b, final_norm_g, s5_w_in, s5_lam_re, s5_lam_im, s5_log_step, s5_b_re, s5_b_im, s5_c_re, s5_c_im, s5_d, s5_glu_w, s5_glu_b, s5_w_out, pool_w_in, pool_w, pool_scale, pool_w_out, mla_w_in, mla_q_norm, mla_wq_b, mla_kv_norm, mla_wkv_b, mla_w_out):
    n_pseq, p_len, d = x_prompt.shape
    n_sseq, s_len, _ = x_sample.shape
    depth = norm_g.shape[0]
    n_prompt = n_pseq * p_len
    bm = 512
    geo = dict(n_prompt=n_prompt, sample_len=s_len, bm=bm)

    x = (x_prompt.reshape(n_prompt, d), x_sample.reshape(n_sseq * s_len, d))
    conds = jnp.concatenate([c_ctx[None, :], c, jnp.zeros((SUBLANES - 1 - n_sseq, d), F32)], axis=0)
    mods = _ada_call(conds.astype(F32), ada_w, ada_b)
    mods = mods.reshape(depth, SUBLANES, 1, 3 * d)
    s5_mats = _s5_prep_all(s5_lam_re, s5_lam_im, s5_log_step, s5_b_re, s5_b_im, s5_c_re, s5_c_im)

    new_re, new_im, new_ckv, new_kpe = [], [], [], []
    for layer in range(depth):
        kind, j = layer % N_MIXERS, layer // N_MIXERS
        last = layer == depth - 1
        ml = mods[layer]
        if kind == 0:
            width = s5_w_in.shape[2] // 2
            w = s5_w_in[j].astype(BF16)
            u3, z, xc = _inproj_call(x, ml, norm_g[layer], [(w, 0, width), (w, 1, width)], [F32, BF16],
                                     lane_blocked=(0,), s5_chunks=True, **geo)
            act, f_re, f_im = _s5_mix(u3, xc, z, j, s5_mats, s5_d[j], s5_glu_w[j].astype(BF16), s5_glu_b[j],
                                      state_s5_re[:, j], state_s5_im[:, j], n_prompt_seq=n_pseq,
                                      prompt_len=p_len, n_sample_seq=n_sseq, sample_len=s_len, bm=bm)
            new_re.append(f_re)
            new_im.append(f_im)
            w_out = s5_w_out[j]
        elif kind == 1:
            width = pool_w_in.shape[2] // 2
            w = pool_w_in[j].astype(BF16)
            u, z = _inproj_call(x, ml, norm_g[layer], [(w, 0, width), (w, 1, width)], [F32, BF16], **geo)
            act = _pool_call(u, z, pool_w[j].astype(BF16), pool_scale[j], n_prompt=n_prompt,
                             prompt_len=p_len, sample_len=s_len)
            w_out = pool_w_out[j]
        else:
            q_rank, kv_rank = mla_q_norm.shape[-1], mla_kv_norm.shape[-1]
            w_small, w_z, wq_a, wq_r = _mla_weights(mla_w_in[j], mla_wq_b[j])
            small, z = _inproj_call(x, ml, norm_g[layer], [w_small, w_z], [F32, BF16], **geo)
            cos_t, sin_t = _rope_tables(n_prompt, n_sseq, s_len)
            q, ckv_n, kpe_k = _mla_post_call(small, cos_t, sin_t, mla_q_norm[j], mla_kv_norm[j], wq_a, wq_r,
                                             bm=bm)
            wkv = mla_wkv_b[j].astype(BF16)
            past = cache_ckv.shape[2]
            k_len = past + s_len
            ckv_s = jnp.concatenate([cache_ckv[:, j].astype(F32), ckv_n[n_prompt:].reshape(n_sseq, s_len, kv_rank)],
                                    axis=1).reshape(n_sseq * k_len, kv_rank)
            kpe_cache = jnp.concatenate([cache_kpe[:, j].astype(BF16),
                                         jnp.zeros((n_sseq, past, LANES - MLA_ROPE), BF16)], axis=-1)
            kpe_s = jnp.concatenate([kpe_cache, kpe_k[n_prompt:].reshape(n_sseq, s_len, LANES)],
                                    axis=1).reshape(n_sseq * k_len, LANES)
            act = (_attn_call(q, ckv_n, wkv, kpe_k, z, q_row0=0, n_seq=n_pseq, q_len=p_len,
                              k_len=p_len, hg=MLA_HEADS, qb=p_len),
                   _attn_call(q, ckv_s, wkv, kpe_s, z, q_row0=n_prompt, n_seq=n_sseq, q_len=s_len,
                              k_len=k_len, hg=4, qb=256))
            new_ckv.append(ckv_n[:n_prompt].reshape(n_pseq, p_len, kv_rank))
            c_kpe = q_rank + kv_rank
            new_kpe.append(small[:n_prompt, c_kpe:c_kpe + MLA_ROPE].reshape(n_pseq, p_len, MLA_ROPE))
            w_out = mla_w_out[j]
        x = _outproj_call(act, x, ml, w_out.astype(BF16), final_norm_g, final_norm=last, split_out=last,
                          n_prompt=n_prompt, sample_len=s_len, bm=2 * bm)

    y_prompt = x[0].reshape(n_pseq, p_len, d)
    y_sample = x[1].reshape(n_sseq, s_len, d)
    return (y_prompt, y_sample, jnp.stack(new_re, axis=1), jnp.stack(new_im, axis=1),
            jnp.stack(new_ckv, axis=1), jnp.stack(new_kpe, axis=1))
```

```python
import functools
import math

import jax
import jax.numpy as jnp
import numpy as np
from jax import lax
from jax.experimental import pallas as pl
from jax.experimental.pallas import tpu as pltpu

S5_GROUP = 16
S5_CHUNK = 16
POOL_WINDOWS = (2, 4, 8, 16)
MLA_HEADS = 16
MLA_NOPE = 128
MLA_ROPE = 64
MLA_V = 128
GRID_W = 64
ROPE_THETA = 10000.0
NORM_EPS = 1e-6
N_MIXERS = 3

LANES = 128
SUBLANES = 8
VMEM_LIMIT_BYTES = 56 * 1024 * 1024

F32 = jnp.float32
BF16 = jnp.bfloat16
HIGHEST = lax.Precision.HIGHEST


def _cparams(*sem):
    return pltpu.CompilerParams(dimension_semantics=sem, vmem_limit_bytes=VMEM_LIMIT_BYTES)


def _sigmoid(x):
    return 0.5 + 0.5 * jnp.tanh(0.5 * x)


def _silu(x):
    return x * _sigmoid(x)


def _gelu_tanh(x):
    c = math.sqrt(2.0 / math.pi)
    hx = 0.5 * x
    return hx + hx * jnp.tanh(x * (c + (c * 0.044715) * (x * x)))


def _ada_kernel(c_ref, w_ref, b_ref, o_ref):
    a = _silu(c_ref[...])
    o_ref[...] = jnp.dot(a, w_ref[...], preferred_element_type=F32, precision=HIGHEST) + b_ref[...]


def _ada_call(conds, ada_w, ada_b):
    depth, d, d3 = ada_w.shape
    c8 = conds.shape[0]
    tn = d3 // 2
    return pl.pallas_call(
        _ada_kernel,
        out_shape=jax.ShapeDtypeStruct((depth, c8, d3), F32),
        grid=(depth, d3 // tn),
        in_specs=[
            pl.BlockSpec((c8, d), lambda l, n: (0, 0)),
            pl.BlockSpec((None, d, tn), lambda l, n: (l, 0, n)),
            pl.BlockSpec((None, 1, tn), lambda l, n: (l, 0, n)),
        ],
        out_specs=pl.BlockSpec((None, c8, tn), lambda l, n: (l, 0, n)),
        compiler_params=_cparams("arbitrary", "arbitrary"),
        name="ada_mod",
    )(conds, ada_w, ada_b.reshape(depth, 1, d3))


def _cond_of_block(i, n_prompt_blocks, blocks_per_sample):
    return jnp.where(i < n_prompt_blocks, 0, 1 + (i - n_prompt_blocks) // blocks_per_sample)


def _modulated(x, mod_ref, g_ref, d):
    ms = jnp.mean(x * x, axis=-1, keepdims=True)
    y = x * lax.rsqrt(ms + NORM_EPS) * g_ref[...]
    shift = mod_ref[:, 0:d]
    scale = mod_ref[:, d:2 * d]
    return (y * (1.0 + scale) + shift).astype(BF16)


def _inproj_kernel(*refs, d, n_chunk, chunk_rows, n_prompt_blocks, n_x):
    if n_x == 2:
        x = jnp.where(pl.program_id(0) < n_prompt_blocks, refs[0][...], refs[1][...])
    else:
        x = refs[0][...]
    mod_ref, g_ref = refs[n_x], refs[n_x + 1]
    rest = refs[n_x + 2:]
    u_scr = None
    if chunk_rows:
        rest, xc_ref, u_scr = rest[:-2], rest[-2], rest[-1]
    n_out = len(rest) // 2
    w_refs, o_refs = rest[:n_out], rest[n_out:]
    h = _modulated(x, mod_ref, g_ref, d)
    for k, (w_ref, o_ref) in enumerate(zip(w_refs, o_refs)):
        n = w_ref.shape[1]
        for c in range(0, n, n_chunk):
            e = min(c + n_chunk, n)
            r = jnp.dot(h, w_ref[:, c:e], preferred_element_type=F32)
            if len(o_ref.shape) == 3:
                for lb in range((e - c) // LANES):
                    part = r[:, lb * LANES:(lb + 1) * LANES]
                    o_ref[c // LANES + lb] = part.astype(o_ref.dtype)
                    if k == 0 and u_scr is not None:
                        u_scr[c // LANES + lb] = part
            else:
                o_ref[:, c:e] = r.astype(o_ref.dtype)
    if chunk_rows:
        _s5_to_chunks_kernel(u_scr, xc_ref, rows=chunk_rows)


def _inproj_call(x, mods_l, norm_g, weights, out_dtypes, *, n_prompt, sample_len, bm=512,
                 lane_blocked=(), s5_chunks=False):
    xs = list(x) if isinstance(x, tuple) else [x]
    n_tok = sum(a.shape[0] for a in xs)
    d = xs[0].shape[1]
    npb, bps = n_prompt // bm, sample_len // bm
    cond = functools.partial(_cond_of_block, n_prompt_blocks=npb, blocks_per_sample=bps)
    weights = [w if isinstance(w, tuple) else (w, 0, w.shape[1]) for w in weights]
    x_specs = _split_specs((bm, d), npb) if len(xs) == 2 else [pl.BlockSpec((bm, d), lambda i: (i, 0))]
    in_specs = x_specs + [
        pl.BlockSpec((None, 1, 3 * d), lambda i: (cond(i), 0, 0)),
        pl.BlockSpec((1, d), lambda i: (0, 0)),
    ] + [pl.BlockSpec((d, n), functools.partial(lambda i, blk: (0, blk), blk=blk)) for _, blk, n in weights]
    out_specs, out_shape = [], []
    for k, ((_, _, n), dt) in enumerate(zip(weights, out_dtypes)):
        if k in lane_blocked:
            out_specs.append(pl.BlockSpec((n // LANES, bm, LANES), lambda i: (0, i, 0)))
            out_shape.append(jax.ShapeDtypeStruct((n // LANES, n_tok, LANES), dt))
        else:
            out_specs.append(pl.BlockSpec((bm, n), lambda i: (i, 0)))
            out_shape.append(jax.ShapeDtypeStruct((n_tok, n), dt))
    chunk_rows = bm // S5_CHUNK if s5_chunks else 0
    scratch = []
    if s5_chunks:
        assert 0 in lane_blocked
        n_groups = weights[0][2] // S5_GROUP
        out_specs.append(pl.BlockSpec((n_groups, chunk_rows, 2 * LANES), lambda i: (0, i, 0)))
        out_shape.append(jax.ShapeDtypeStruct((n_groups, n_tok // S5_CHUNK, 2 * LANES), BF16))
        scratch = [pltpu.VMEM((weights[0][2] // LANES, bm, LANES), F32)]
    return pl.pallas_call(
        functools.partial(_inproj_kernel, d=d, n_chunk=512, chunk_rows=chunk_rows,
                          n_prompt_blocks=npb, n_x=len(xs)),
        out_shape=out_shape,
        grid=(n_tok // bm,),
        in_specs=in_specs,
        out_specs=out_specs,
        scratch_shapes=scratch,
        compiler_params=_cparams("arbitrary"),
        name="norm_mod_inproj",
    )(*xs, mods_l, norm_g.reshape(1, d), *[w for w, _, _ in weights])


def _outproj_kernel(*refs, d, final_norm, n_prompt_blocks, n_act, n_x):
    a_refs, x_refs = refs[:n_act], refs[n_act:n_act + n_x]
    mod_ref, w_ref, fg_ref = refs[n_act + n_x:n_act + n_x + 3]
    o_refs = refs[n_act + n_x + 3:]

    def finish(a_ref, x_ref, o_ref):
        y = jnp.dot(a_ref[...], w_ref[...], preferred_element_type=F32)
        gate = mod_ref[:, 2 * d:3 * d]
        xn = x_ref[...] + gate * y
        if final_norm:
            ms = jnp.mean(xn * xn, axis=-1, keepdims=True)
            xn = xn * lax.rsqrt(ms + NORM_EPS) * fg_ref[...]
        o_ref[...] = xn

    if max(n_act, n_x, len(o_refs)) == 1:
        finish(a_refs[0], x_refs[0], o_refs[0])
    else:
        is_prompt = pl.program_id(0) < n_prompt_blocks
        pl.when(is_prompt)(functools.partial(finish, a_refs[0], x_refs[0], o_refs[0]))
        pl.when(jnp.logical_not(is_prompt))(functools.partial(finish, a_refs[-1], x_refs[-1], o_refs[-1]))


def _split_specs(block, npb):
    return [pl.BlockSpec(block, lambda i: (jnp.minimum(i, npb - 1), 0)),
            pl.BlockSpec(block, lambda i: (jnp.maximum(i - npb, 0), 0))]


def _outproj_call(act, x, mods_l, w_out, final_g, *, n_prompt, sample_len, final_norm, bm=512,
                  split_out=False):
    acts = list(act) if isinstance(act, tuple) else [act]
    xs = list(x) if isinstance(x, tuple) else [x]
    n_tok = sum(a.shape[0] for a in xs)
    d, k = xs[0].shape[1], acts[0].shape[1]
    npb, bps = n_prompt // bm, sample_len // bm
    cond = functools.partial(_cond_of_block, n_prompt_blocks=npb, blocks_per_sample=bps)
    row = lambda i: (i, 0)
    act_specs = _split_specs((bm, k), npb) if len(acts) == 2 else [pl.BlockSpec((bm, k), row)]
    x_specs = _split_specs((bm, d), npb) if len(xs) == 2 else [pl.BlockSpec((bm, d), row)]
    if split_out:
        out_shape = [jax.ShapeDtypeStruct((n_prompt, d), F32), jax.ShapeDtypeStruct((n_tok - n_prompt, d), F32)]
        out_specs = _split_specs((bm, d), npb)
    else:
        out_shape = jax.ShapeDtypeStruct((n_tok, d), F32)
        out_specs = pl.BlockSpec((bm, d), row)
    return pl.pallas_call(
        functools.partial(_outproj_kernel, d=d, final_norm=final_norm, n_prompt_blocks=npb,
                          n_act=len(acts), n_x=len(xs)),
        out_shape=out_shape,
        grid=(n_tok // bm,),
        in_specs=act_specs + x_specs + [
            pl.BlockSpec((None, 1, 3 * d), lambda i: (cond(i), 0, 0)),
            pl.BlockSpec((k, d), lambda i: (0, 0)),
            pl.BlockSpec((1, d), lambda i: (0, 0)),
        ],
        out_specs=out_specs,
        compiler_params=_cparams("arbitrary"),
        name="outproj_residual",
    )(*acts, *xs, mods_l, w_out, final_g.reshape(1, d))


def _s5_time_of_lane_block():
    pos = np.arange(S5_CHUNK)
    half, blk = pos // 8, pos % 8
    g8 = np.arange(8)[:, None]
    return 8 * half[None, :] + (blk[None, :] - g8) % 8


def _s5_tables(lam_re, lam_im, log_step, b_re, b_im, c_re, c_im):
    t_chunk = S5_CHUNK
    n_groups, n_state = lam_re.shape[1], lam_re.shape[2]
    n_oct = n_groups // 8
    lam = lax.complex(lam_re.astype(F32), lam_im.astype(F32))
    step = jnp.exp(log_step.astype(F32))[..., None]
    lam_bar = jnp.exp(lam * step)
    b_bar = ((lam_bar - 1.0) / lam)[..., None] * lax.complex(b_re.astype(F32), b_im.astype(F32))
    c_mat = lax.complex(c_re.astype(F32), c_im.astype(F32))
    powers = [jnp.ones_like(lam_bar), lam_bar]
    for _ in range(t_chunk - 1):
        powers.append(powers[-1] * lam_bar)
    pw = jnp.stack(powers)

    zeros = jnp.zeros((t_chunk - 1, n_groups, n_state), pw.dtype)
    lag_f = jnp.concatenate([zeros, pw[:t_chunk, 0], zeros[:1]], axis=0)
    lag_b = jnp.concatenate([pw[t_chunk - 1::-1, 1], zeros, zeros[:1]], axis=0)
    plag = jnp.concatenate([lag_f, lag_b], axis=-1).transpose(1, 0, 2)
    plag = jnp.stack([plag.real, plag.imag])

    tl = _s5_time_of_lane_block()
    pw_ri = jnp.stack([pw.real, pw.imag]).reshape(2, t_chunk + 1, 2, n_oct, 8, n_state)
    m_idx = np.arange(t_chunk + 1)[None, None, :]

    def power_table(exponent, direction):
        sel = (exponent[:, :, None] == m_idx).astype(np.float32)
        tab = jnp.einsum('kxm,rmakp->rakxp', sel, pw_ri[:, :, direction], precision=HIGHEST)
        return tab.reshape(2, n_groups, t_chunk, n_state)

    def both(fwd, bwd):
        m = jnp.concatenate([fwd, bwd], axis=-1)
        return jnp.stack([m.real, m.imag])

    tin = jnp.concatenate([power_table(t_chunk - 1 - tl, 0), power_table(tl, 1)], axis=-1)
    tout = jnp.concatenate([power_table(tl + 1, 0), power_table(t_chunk - tl, 1)], axis=-1)
    bt = both(b_bar[0].transpose(0, 2, 1), b_bar[1].transpose(0, 2, 1))
    ct = both(c_mat[0], c_mat[1])
    lam_rows = both(pw[t_chunk, 0][:, None], pw[t_chunk, 1][:, None])[:, :, 0]
    return plag, tin, tout, bt, ct, lam_rows


def _s5_kmat_kernel(plag_ref, tin_ref, tout_ref, bt_ref, ct_ref, k_ref, pin_ref, pot_ref, x_scr, v_scr):
    masks = _lane_block_masks()
    n_lag = 2 * S5_CHUNK - 1
    for g8 in range(8):
        br, bi = bt_ref[0, g8], bt_ref[1, g8]
        cr, ci = ct_ref[0, g8], ct_ref[1, g8]
        def split(a):
            hi = a.astype(BF16)
            return hi, (a - hi.astype(F32)).astype(BF16)

        for m in range(n_lag):
            rows = slice(m * S5_GROUP, (m + 1) * S5_GROUP)
            pr, pi = plag_ref[0, g8, m:m + 1, :], plag_ref[1, g8, m:m + 1, :]
            for c0, part in ((0, cr * pr - ci * pi), (LANES, -(cr * pi + ci * pr))):
                x_scr[0, rows, c0:c0 + LANES], x_scr[1, rows, c0:c0 + LANES] = split(part)
        b_hi, b_lo = split(jnp.concatenate([jnp.concatenate([br, bi], axis=1)] * (LANES // S5_GROUP), axis=0))
        nt = functools.partial(lax.dot_general, dimension_numbers=(((1,), (1,)), ((), ())),
                               preferred_element_type=F32)
        v_scr[...] = nt(x_scr[0], b_hi) + nt(x_scr[0], b_lo) + nt(x_scr[1], b_hi)
        for pos in range(S5_CHUNK):
            rows = slice(pos * S5_GROUP, (pos + 1) * S5_GROUP)
            tr, ti = tin_ref[0, g8, pos:pos + 1, :], tin_ref[1, g8, pos:pos + 1, :]
            pin_ref[g8, rows, 0:LANES] = (tr * br - ti * bi).astype(BF16)
            pin_ref[g8, rows, LANES:2 * LANES] = (tr * bi + ti * br).astype(BF16)
            tr, ti = tout_ref[0, g8, pos:pos + 1, :], tout_ref[1, g8, pos:pos + 1, :]
            pot_ref[g8, rows, 0:LANES] = (tr * cr - ti * ci).astype(BF16)
            pot_ref[g8, rows, LANES:2 * LANES] = (-(tr * ci + ti * cr)).astype(BF16)
        for pos in range(S5_CHUNK):
            tau = 8 * (pos // 8) + (pos % 8 - g8) % 8
            rows = slice(pos * S5_GROUP, (pos + 1) * S5_GROUP)
            for half in range(2):
                acc = None
                for blk in range(8):
                    sigma = 8 * half + (blk - g8) % 8
                    m = S5_CHUNK - 1 - sigma + tau
                    src = v_scr[m * S5_GROUP:(m + 1) * S5_GROUP, :]
                    acc = src if acc is None else jnp.where(masks[blk], src, acc)
                k_ref[g8, rows, half * LANES:(half + 1) * LANES] = acc.astype(BF16)


def _s5_kmat_call(plag, tin, tout, bt, ct):
    n_groups = plag.shape[1]
    n_lag_rows = (2 * S5_CHUNK - 1) * S5_GROUP
    lag_spec = pl.BlockSpec((2, 8) + plag.shape[2:], lambda i: (0, i, 0, 0))
    tab_spec = pl.BlockSpec((2, 8, S5_GROUP, LANES), lambda i: (0, i, 0, 0))
    mat = jax.ShapeDtypeStruct((n_groups, 2 * LANES, 2 * LANES), BF16)
    mat_spec = pl.BlockSpec((8, 2 * LANES, 2 * LANES), lambda i: (i, 0, 0))
    return pl.pallas_call(
        _s5_kmat_kernel,
        out_shape=[mat, mat, mat],
        grid=(n_groups // 8,),
        in_specs=[lag_spec, tab_spec, tab_spec, tab_spec, tab_spec],
        out_specs=[mat_spec, mat_spec, mat_spec],
        scratch_shapes=[pltpu.VMEM((2, n_lag_rows, 2 * LANES), BF16), pltpu.VMEM((n_lag_rows, LANES), F32)],
        compiler_params=_cparams("arbitrary"),
        name="s5_kmat",
    )(plag, tin, tout, bt, ct)


def _lane_block_masks():
    blk = lax.broadcasted_iota(jnp.int32, (1, LANES), 1) // S5_GROUP
    return [blk == b for b in range(8)]


def _diagonal_merge(src):
    blk = lax.broadcasted_iota(jnp.int32, (1, LANES), 1) // S5_GROUP
    q = list(src)
    for bit in (1, 2, 4):
        take = (blk & bit) != 0
        q = [jnp.where(take, q[(x + bit) % 8], q[x]) for x in range(8)]
    return [q[(-t) % 8] for t in range(8)]


def _s5_to_chunks_kernel(u_ref, x_ref, *, rows):
    for o in range(u_ref.shape[0]):
        for r0 in range(0, rows, SUBLANES):
            for half in range(2):
                rolled = []
                for t8 in range(8):
                    v = u_ref[o, pl.ds(r0 * S5_CHUNK + 8 * half + t8, SUBLANES, stride=S5_CHUNK), :]
                    rolled.append(pltpu.roll(v, t8 * S5_GROUP, 1) if t8 else v)
                for g8, merged in enumerate(_diagonal_merge(rolled)):
                    x_ref[o * 8 + g8, r0:r0 + SUBLANES, half * LANES:(half + 1) * LANES] = merged.astype(BF16)


def _s5_tail_kernel(yc_ref, u_ref, d_ref, z_ref, w_ref, b_ref, o_ref, nat_scr, y_scr, *, rows, n_chunk):
    n_blk = u_ref.shape[0]
    tile = 2 * SUBLANES
    per = n_chunk // LANES
    for piece, r0 in enumerate(range(0, rows, tile)):
        tok = slice(r0 * S5_CHUNK, (r0 + tile) * S5_CHUNK)
        for o in range(n_blk):
            for half in range(2):
                src = [yc_ref[o * 8 + g8, r0:r0 + tile, half * LANES:(half + 1) * LANES] for g8 in range(8)]
                for t8, merged in enumerate(_diagonal_merge(src)):
                    nat = pltpu.roll(merged, (8 - t8) * S5_GROUP, 1) if t8 else merged
                    nat_scr[piece % 2, o, pl.ds(8 * half + t8, tile, stride=S5_CHUNK), :] = nat
            d_vec = d_ref[:, o * LANES:(o + 1) * LANES]
            y_scr[o, tok] = _gelu_tanh(nat_scr[piece % 2, o] + d_vec * u_ref[o, tok])
        yb = jnp.concatenate([y_scr[o, tok].astype(BF16) for o in range(n_blk)], axis=1)
        for c in range(0, n_blk, per):
            sl = slice(c * LANES, (c + per) * LANES)
            gate = _sigmoid(jnp.dot(yb, w_ref[:, sl], preferred_element_type=F32) + b_ref[:, sl])
            y = jnp.concatenate([y_scr[c + k, tok] for k in range(per)], axis=1)
            o_ref[tok, sl] = (y * gate * _silu(z_ref[tok, sl].astype(F32))).astype(o_ref.dtype)


def _s5_tail_call(yc, u3, d_skip, z, glu_w, glu_b, *, rows=32):
    n_blk, n_tok, _ = u3.shape
    n_groups, n_rows, _ = yc.shape
    width = n_blk * LANES
    bm = rows * S5_CHUNK
    tile_tok = 2 * SUBLANES * S5_CHUNK
    fix = lambda i: (0, 0)
    return pl.pallas_call(
        functools.partial(_s5_tail_kernel, rows=rows, n_chunk=min(512, width)),
        out_shape=jax.ShapeDtypeStruct((n_tok, width), BF16),
        grid=(n_rows // rows,),
        in_specs=[pl.BlockSpec((n_groups, rows, 2 * LANES), lambda i: (0, i, 0)),
                  pl.BlockSpec((n_blk, bm, LANES), lambda i: (0, i, 0)),
                  pl.BlockSpec((1, width), fix),
                  pl.BlockSpec((bm, width), lambda i: (i, 0)),
                  pl.BlockSpec((width, width), fix),
                  pl.BlockSpec((1, width), fix)],
        out_specs=pl.BlockSpec((bm, width), lambda i: (i, 0)),
        scratch_shapes=[pltpu.VMEM((2, n_blk, tile_tok, LANES), F32),
                        pltpu.VMEM((n_blk, bm, LANES), F32)],
        compiler_params=_cparams("arbitrary"),
        name="s5_tail",
    )(yc, u3, d_skip.reshape(1, width).astype(F32), z, glu_w, glu_b.reshape(1, width).astype(F32))


def _s5_chunk_kernel(x_ref, kt_ref, pin_ref, po_ref, lam_ref, h0r_ref, h0i_ref,
                     y_ref, fr_ref, fi_ref, r_scr, st_scr, yi_scr, *, segments, seq_block):
    gb = SUBLANES
    rows = x_ref.shape[1]
    lane = lax.broadcasted_iota(jnp.int32, (1, LANES), 1)
    fwd_lanes = lane < (LANES // 2)
    del rows
    seg_rows = [(row0, n_seq * n_chunks) for row0, n_seq, n_chunks, _, _ in segments]
    for row0, n_rows in seg_rows:
        for g in range(gb):
            x = x_ref[g, row0:row0 + n_rows, :]
            yi_scr[g, row0:row0 + n_rows, :] = lax.dot_general(x, kt_ref[g], (((1,), (1,)), ((), ())),
                                                               preferred_element_type=F32)
            r = jnp.dot(x, pin_ref[g], preferred_element_type=F32)
            of_group = pl.ds(row0 * gb + g, n_rows, stride=gb)
            r_scr[0, of_group, :] = r[:, 0:LANES]
            r_scr[1, of_group, :] = r[:, LANES:2 * LANES]
    ar, ai = lam_ref[0], lam_ref[1]
    for row0, n_seq, n_chunks, from_input, to_output in segments:
        for b0 in range(0, n_seq, seq_block):
            nb = min(seq_block, n_seq - b0)

            def step(i, carry, row0=row0, n_chunks=n_chunks, b0=b0, nb=nb):
                out = []
                for k in range(nb):
                    base = row0 + (b0 + k) * n_chunks
                    at_f = pl.ds((base + i) * gb, gb)
                    at_b = pl.ds((base + (n_chunks - 1) - i) * gb, gb)
                    s_re, s_im = carry[k]
                    half = LANES // 2
                    st_scr[0, at_f, 0:half] = s_re[:, 0:half]
                    st_scr[0, at_b, half:LANES] = s_re[:, half:LANES]
                    st_scr[1, at_f, 0:half] = s_im[:, 0:half]
                    st_scr[1, at_b, half:LANES] = s_im[:, half:LANES]
                    v_re = jnp.where(fwd_lanes, r_scr[0, at_f, :], r_scr[0, at_b, :])
                    v_im = jnp.where(fwd_lanes, r_scr[1, at_f, :], r_scr[1, at_b, :])
                    out.append((ar * s_re - ai * s_im + v_re, ar * s_im + ai * s_re + v_im))
                return tuple(out)

            if from_input:
                init = tuple((h0r_ref[b0 + k], h0i_ref[b0 + k]) for k in range(nb))
            else:
                init = tuple((jnp.zeros((gb, LANES), F32),) * 2 for _ in range(nb))
            fin = init
            for i in range(n_chunks):
                fin = step(i, fin)
            if to_output:
                for k in range(nb):
                    fr_ref[b0 + k] = fin[k][0]
                    fi_ref[b0 + k] = fin[k][1]
    for row0, n_rows in seg_rows:
        for g in range(gb):
            of_group = pl.ds(row0 * gb + g, n_rows, stride=gb)
            st = jnp.concatenate([st_scr[cb, of_group, :] for cb in range(2)], axis=1).astype(BF16)
            rows = slice(row0, row0 + n_rows)
            y_ref[g, rows, :] = (yi_scr[g, rows, :] + lax.dot_general(
                st, po_ref[g], (((1,), (1,)), ((), ())), preferred_element_type=F32)).astype(y_ref.dtype)


def _s5_chunk_call(xc, kt, pin, pout, lam_rows, h0_re, h0_im, *, layer, segments, n_final):
    n_groups, rows, _ = xc.shape
    gb = SUBLANES
    s_in = h0_re.shape[0]
    kern = functools.partial(_s5_chunk_kernel, segments=segments, seq_block=8)
    g3 = lambda i: (i, 0, 0)
    blk0 = layer * (n_groups // gb)
    p3 = lambda i: (i + blk0, 0, 0)
    mid = lambda i: (0, i, 0)
    return pl.pallas_call(
        kern,
        out_shape=[jax.ShapeDtypeStruct((n_groups, rows, 2 * LANES), F32),
                   jax.ShapeDtypeStruct((n_final, n_groups, LANES), F32),
                   jax.ShapeDtypeStruct((n_final, n_groups, LANES), F32)],
        grid=(n_groups // gb,),
        in_specs=[
            pl.BlockSpec((gb, rows, 2 * LANES), g3),
            pl.BlockSpec((gb, 2 * LANES, 2 * LANES), p3),
            pl.BlockSpec((gb, 2 * LANES, 2 * LANES), p3),
            pl.BlockSpec((gb, 2 * LANES, 2 * LANES), p3),
            pl.BlockSpec((2, gb, LANES), lambda i: (0, i + blk0, 0)),
            pl.BlockSpec((s_in, gb, LANES), mid),
            pl.BlockSpec((s_in, gb, LANES), mid),
        ],
        out_specs=[pl.BlockSpec((gb, rows, 2 * LANES), g3),
                   pl.BlockSpec((n_final, gb, LANES), mid),
                   pl.BlockSpec((n_final, gb, LANES), mid)],
        scratch_shapes=[pltpu.VMEM((2, rows * gb, LANES), F32),
                        pltpu.VMEM((2, rows * gb, LANES), F32),
                        pltpu.VMEM((gb, rows, 2 * LANES), F32)],
        compiler_params=_cparams("arbitrary"),
        name="s5_chunk_scan",
    )(xc, kt, pin, pout, lam_rows, h0_re, h0_im)


def _s5_prep_all(lam_re, lam_im, log_step, b_re, b_im, c_re, c_im):
    tabs = jax.vmap(_s5_tables)(lam_re, lam_im, log_step, b_re, b_im, c_re, c_im)
    plag, tin, tout, bt, ct, lam_rows = [jnp.moveaxis(t, 0, 1).reshape((2, -1) + t.shape[3:]) for t in tabs]
    kt, pin, pot = _s5_kmat_call(plag, tin, tout, bt, ct)
    return kt, pin, pot, lam_rows


def _s5_mix(u3, xc, z, layer, mats, d_skip, glu_w, glu_b, st_re, st_im, *, n_prompt_seq, prompt_len,
            n_sample_seq, sample_len, bm):
    kt, pin, pout, lam_rows = mats
    n_state = LANES // 2
    pc, sc = prompt_len // S5_CHUNK, sample_len // S5_CHUNK

    def state_rows(s):
        return jnp.concatenate([s[:, 0], s[:, 1]], axis=-1).astype(F32)

    segments = ((0, n_prompt_seq, pc, False, True), (n_prompt_seq * pc, n_sample_seq, sc, True, False))
    yc, fr, fi = _s5_chunk_call(xc, kt, pin, pout, lam_rows, state_rows(st_re), state_rows(st_im),
                                layer=layer, segments=segments, n_final=n_prompt_seq)
    act = _s5_tail_call(yc, u3, d_skip, z, glu_w, glu_b, rows=bm // S5_CHUNK)

    def unpack(f):
        return jnp.stack([f[:, :, :n_state], f[:, :, n_state:]], axis=1)

    return act, unpack(fr), unpack(fi)


POOL_TILE = 2 * LANES
POOL_EDGE = 16


def _pool_band_matrices():
    t = np.arange(POOL_TILE)[:, None]
    s = np.arange(POOL_TILE)[None, :]
    mats = []
    for win in POOL_WINDOWS:
        lo = win // 2
        inside = lambda src: ((src >= t - lo) & (src <= t + lo - 1)).astype(np.float32)
        mats.append(np.stack([inside(s), inside(s - POOL_TILE), inside(s + POOL_TILE)]))
    return np.stack(mats)


def _pool_kernel(u_ref, z_ref, w_ref, s_ref, a_ref, o_ref, sum_scr, *, n_prompt_blocks, prompt_len,
                 sample_len):
    rows = u_ref.shape[0]
    n_tiles = rows // POOL_TILE
    is_prompt = pl.program_id(0) < n_prompt_blocks
    seq_len = jnp.where(is_prompt, prompt_len, sample_len)
    lo = jnp.left_shift(1, pl.program_id(1))
    t = lax.broadcasted_iota(jnp.int32, (rows, 1), 0) & (seq_len - 1)
    u = u_ref[...]
    u_hi = u.astype(BF16)
    u_lo = (u - u_hi.astype(F32)).astype(BF16)

    def band(a, tile):
        sl = slice(tile * POOL_TILE, (tile + 1) * POOL_TILE)
        return (jnp.dot(a, u_hi[sl], preferred_element_type=F32)
                + jnp.dot(a, u_lo[sl], preferred_element_type=F32))

    for r in range(n_tiles):
        sum_scr[r * POOL_TILE:(r + 1) * POOL_TILE, :] = band(a_ref[0], r)

    @pl.when(jnp.logical_not(is_prompt))
    def _():
        for r in range(n_tiles):
            if r > 0:
                top = slice(r * POOL_TILE, r * POOL_TILE + POOL_EDGE)
                sum_scr[top, :] = sum_scr[top, :] + band(a_ref[1, 0:POOL_EDGE, :], r - 1)
            if r < n_tiles - 1:
                bot = slice((r + 1) * POOL_TILE - POOL_EDGE, (r + 1) * POOL_TILE)
                sum_scr[bot, :] = sum_scr[bot, :] + band(a_ref[2, POOL_TILE - POOL_EDGE:POOL_TILE, :], r + 1)

    cnt = jnp.minimum(t + lo, seq_len) - jnp.maximum(t - lo, 0)
    p = sum_scr[...] / cnt.astype(F32) - u
    m = jnp.dot(p.astype(BF16), w_ref[...], preferred_element_type=F32) * s_ref[...]
    o_ref[...] = (m * _silu(z_ref[...].astype(F32))).astype(o_ref.dtype)


def _pool_call(u, z, pool_w, pool_scale, *, n_prompt, prompt_len, sample_len, rows=2048):
    n_tok, width = u.shape
    n_groups = len(POOL_WINDOWS)
    cg = width // n_groups
    assert prompt_len & (prompt_len - 1) == 0 and sample_len & (sample_len - 1) == 0
    assert rows % prompt_len == 0 and n_prompt % rows == 0
    assert prompt_len == POOL_TILE and sample_len == rows and max(POOL_WINDOWS) // 2 <= POOL_EDGE
    assert POOL_WINDOWS == tuple(2 << g for g in range(n_groups))
    kern = functools.partial(_pool_kernel, n_prompt_blocks=n_prompt // rows, prompt_len=prompt_len,
                             sample_len=sample_len)
    bands = jnp.asarray(_pool_band_matrices(), BF16)
    return pl.pallas_call(
        kern,
        out_shape=jax.ShapeDtypeStruct((n_tok, width), BF16),
        grid=(n_tok // rows, n_groups),
        in_specs=[
            pl.BlockSpec((rows, cg), lambda i, g: (i, g)),
            pl.BlockSpec((rows, cg), lambda i, g: (i, g)),
            pl.BlockSpec((None, cg, cg), lambda i, g: (g, 0, 0)),
            pl.BlockSpec((1, cg), lambda i, g: (0, g)),
            pl.BlockSpec((None, 3, POOL_TILE, POOL_TILE), lambda i, g: (g, 0, 0, 0)),
        ],
        out_specs=pl.BlockSpec((rows, cg), lambda i, g: (i, g)),
        scratch_shapes=[pltpu.VMEM((rows, cg), F32)],
        compiler_params=_cparams("arbitrary", "arbitrary"),
        name="pool_mix",
    )(u, z, pool_w, pool_scale.reshape(1, width).astype(F32), bands)


MLA_QW = 2 * LANES

_ROT_SRC = np.concatenate([np.arange(16, 32), np.arange(0, 16), np.arange(48, 64), np.arange(32, 48)])
_ROT_SIGN = np.concatenate([-np.ones(16), np.ones(16), -np.ones(16), np.ones(16)]).astype(np.float32)


def _rope_tables(n_prompt, n_sample_seq, sample_len):
    half = MLA_ROPE // 4
    tok = jnp.arange(sample_len)
    row = (tok // GRID_W).astype(F32)
    col = (tok % GRID_W).astype(F32)
    inv = ROPE_THETA ** (-jnp.arange(half, dtype=F32) / half)
    a_row, a_col = row[:, None] * inv, col[:, None] * inv
    cos = jnp.concatenate([jnp.cos(a_row), jnp.cos(a_row), jnp.cos(a_col), jnp.cos(a_col)], axis=-1)
    sin = jnp.concatenate([jnp.sin(a_row), jnp.sin(a_row), jnp.sin(a_col), jnp.sin(a_col)], axis=-1)
    pad = jnp.zeros((sample_len, LANES - MLA_ROPE), F32)
    cos_s = jnp.tile(jnp.concatenate([cos, pad], axis=-1), (n_sample_seq, 1))
    sin_s = jnp.tile(jnp.concatenate([sin, pad], axis=-1), (n_sample_seq, 1))
    cos_p = jnp.concatenate([jnp.ones((n_prompt, MLA_ROPE), F32), jnp.zeros((n_prompt, LANES - MLA_ROPE), F32)], -1)
    return jnp.concatenate([cos_p, cos_s]), jnp.concatenate([jnp.zeros((n_prompt, LANES), F32), sin_s])


def _rms(x, g):
    return x * lax.rsqrt(jnp.mean(x * x, axis=-1, keepdims=True) + NORM_EPS) * g


def _mla_post_kernel(sm_ref, cos_ref, sin_ref, qn_ref, kn_ref, wa_ref, wb_ref,
                     q_ref, ckv_ref, kpe_ref, *, q_rank, kv_rank, heads_per_dot):
    cosp, sinp = cos_ref[...], sin_ref[...]
    qn = _rms(sm_ref[:, 0:q_rank], qn_ref[...]).astype(BF16)
    for h0 in range(0, MLA_HEADS, heads_per_dot):
        a = jnp.dot(qn, wa_ref[:, h0 * MLA_QW:(h0 + heads_per_dot) * MLA_QW], preferred_element_type=F32)
        b = jnp.dot(qn, wb_ref[:, h0 * LANES:(h0 + heads_per_dot) * LANES], preferred_element_type=F32)
        for j in range(heads_per_dot):
            h = h0 + j
            q_ref[:, h * MLA_QW:h * MLA_QW + LANES] = a[:, j * MLA_QW:j * MLA_QW + LANES].astype(BF16)
            pe = a[:, j * MLA_QW + LANES:(j + 1) * MLA_QW] * cosp + b[:, j * LANES:(j + 1) * LANES] * sinp
            q_ref[:, h * MLA_QW + LANES:(h + 1) * MLA_QW] = pe.astype(BF16)
    c0 = q_rank
    ckv_ref[...] = _rms(sm_ref[:, c0:c0 + kv_rank], kn_ref[...])
    k0 = c0 + kv_rank
    kpe_ref[...] = (sm_ref[:, k0:k0 + LANES] * cosp + sm_ref[:, k0 + LANES:k0 + 2 * LANES] * sinp).astype(BF16)


def _mla_post_call(small, cos_t, sin_t, q_norm, kv_norm, wq_a, wq_b, *, bm=512):
    n_tok, ws = small.shape
    q_rank, kv_rank = q_norm.shape[-1], kv_norm.shape[-1]
    row = lambda i: (i, 0)
    fix = lambda i: (0, 0)
    kern = functools.partial(_mla_post_kernel, q_rank=q_rank, kv_rank=kv_rank, heads_per_dot=4)
    return pl.pallas_call(
        kern,
        out_shape=[jax.ShapeDtypeStruct((n_tok, MLA_HEADS * MLA_QW), BF16),
                   jax.ShapeDtypeStruct((n_tok, kv_rank), F32),
                   jax.ShapeDtypeStruct((n_tok, LANES), BF16)],
        grid=(n_tok // bm,),
        in_specs=[
            pl.BlockSpec((bm, ws), row),
            pl.BlockSpec((bm, LANES), row),
            pl.BlockSpec((bm, LANES), row),
            pl.BlockSpec((1, q_rank), fix),
            pl.BlockSpec((1, kv_rank), fix),
            pl.BlockSpec(wq_a.shape, fix),
            pl.BlockSpec(wq_b.shape, fix),
        ],
        out_specs=[pl.BlockSpec((bm, MLA_HEADS * MLA_QW), row),
                   pl.BlockSpec((bm, kv_rank), row),
                   pl.BlockSpec((bm, LANES), row)],
        compiler_params=_cparams("arbitrary"),
        name="mla_q_rope",
    )(small, cos_t, sin_t, q_norm.reshape(1, q_rank).astype(F32), kv_norm.reshape(1, kv_rank).astype(F32),
      wq_a, wq_b)


def _attn_kernel(q_ref, ckv_ref, wkv_ref, kpe_ref, z_ref, o_ref, kcat_scr, vext_scr, *, hg, scale):
    c2 = scale * math.log2(math.e)

    @pl.when(pl.program_id(2) == 0)
    def _():
        ckv = ckv_ref[...].astype(BF16)
        ones = jnp.ones((ckv.shape[0], LANES), BF16)
        for j in range(hg):
            kv = jnp.dot(ckv, wkv_ref[:, j * 2 * LANES:(j + 1) * 2 * LANES],
                         preferred_element_type=F32).astype(BF16)
            kcat_scr[j, :, 0:LANES] = kv[:, 0:LANES]
            kcat_scr[j, :, LANES:2 * LANES] = kpe_ref[...]
            vext_scr[j, :, 0:LANES] = kv[:, LANES:2 * LANES]
            vext_scr[j, :, LANES:2 * LANES] = ones

    scores = [lax.dot_general(q_ref[:, j * MLA_QW:(j + 1) * MLA_QW], kcat_scr[j], (((1,), (1,)), ((), ())),
                              preferred_element_type=F32) for j in range(hg)]
    probs = [jnp.exp2((s - jnp.max(s, axis=-1, keepdims=True)) * c2).astype(BF16) for s in scores]
    for j in range(hg):
        pv = jnp.dot(probs[j], vext_scr[j], preferred_element_type=F32)
        zs = slice(j * MLA_V, (j + 1) * MLA_V)
        o = pv[:, 0:MLA_V] / pv[:, MLA_V:2 * MLA_V]
        o_ref[:, zs] = (o * _silu(z_ref[:, zs].astype(F32))).astype(o_ref.dtype)


def _attn_call(q, ckv, wkv_b, kpe, z, *, q_row0, n_seq, q_len, k_len, hg, qb):
    width = MLA_HEADS * MLA_V
    nqb = q_len // qb
    qb0 = q_row0 // qb
    assert q_row0 % qb == 0 and q_len % qb == 0
    scale = float((MLA_NOPE + MLA_ROPE) ** -0.5)
    qrow = lambda b, g, i: (qb0 + b * nqb + i, g)
    return pl.pallas_call(
        functools.partial(_attn_kernel, hg=hg, scale=scale),
        out_shape=jax.ShapeDtypeStruct((n_seq * q_len, width), BF16),
        grid=(n_seq, MLA_HEADS // hg, nqb),
        in_specs=[
            pl.BlockSpec((qb, hg * MLA_QW), qrow),
            pl.BlockSpec((k_len, ckv.shape[1]), lambda b, g, i: (b, 0)),
            pl.BlockSpec((wkv_b.shape[0], hg * 2 * LANES), lambda b, g, i: (0, g)),
            pl.BlockSpec((k_len, LANES), lambda b, g, i: (b, 0)),
            pl.BlockSpec((qb, hg * MLA_V), qrow),
        ],
        out_specs=pl.BlockSpec((qb, hg * MLA_V), lambda b, g, i: (b * nqb + i, g)),
        scratch_shapes=[pltpu.VMEM((hg, k_len, MLA_QW), BF16),
                        pltpu.VMEM((hg, k_len, 2 * MLA_V), BF16)],
        compiler_params=_cparams("arbitrary", "arbitrary", "arbitrary"),
        name="mla_attention",
    )(q, ckv, wkv_b, kpe, z)


def _mla_weights(w_in, wq_b):
    q_rank = wq_b.shape[0]
    kv_rank = w_in.shape[1] - q_rank - MLA_ROPE - MLA_HEADS * MLA_V
    d = w_in.shape[0]
    c_kpe = q_rank + kv_rank
    zpad = jnp.zeros((d, LANES - MLA_ROPE), w_in.dtype)
    kpe_w = w_in[:, c_kpe:c_kpe + MLA_ROPE]
    w_small = jnp.concatenate([w_in[:, :c_kpe], kpe_w, zpad,
                               kpe_w[:, _ROT_SRC] * _ROT_SIGN, zpad], axis=1)
    w_z = w_in[:, c_kpe + MLA_ROPE:]
    hd = MLA_NOPE + MLA_ROPE
    wq3 = wq_b.reshape(q_rank, MLA_HEADS, hd)
    pe = wq3[:, :, MLA_NOPE:]
    z3 = jnp.zeros((q_rank, MLA_HEADS, LANES - MLA_ROPE), wq_b.dtype)
    wq_a = jnp.concatenate([wq3, z3], axis=-1).reshape(q_rank, MLA_HEADS * MLA_QW)
    wq_r = jnp.concatenate([pe[:, :, _ROT_SRC] * _ROT_SIGN, z3], axis=-1).reshape(q_rank, MLA_HEADS * LANES)
    return w_small.astype(BF16), w_z.astype(BF16), wq_a.astype(BF16), wq_r.astype(BF16)


def kernel(x_prompt, x_sample, state_s5_re, state_s5_im, cache_ckv, cache_kpe, c, c_ctx, norm_g, ada_w, ada_b, final_norm_g, s5_w_in, s5_lam_re, s5_lam_im, s5_log_step, s5_b_re, s5_b_im, s5_c_re, s5_c_im, s5_d, s5_glu_w, s5_glu_b, s5_w_out, pool_w_in, pool_w, pool_scale, pool_w_out, mla_w_in, mla_q_norm, mla_wq_b, mla_kv_norm, mla_wkv_b, mla_w_out):
    n_pseq, p_len, d = x_prompt.shape
    n_sseq, s_len, _ = x_sample.shape
    depth = norm_g.shape[0]
    n_prompt = n_pseq * p_len
    bm = 512
    geo = dict(n_prompt=n_prompt, sample_len=s_len, bm=bm)

    x = (x_prompt.reshape(n_prompt, d), x_sample.reshape(n_sseq * s_len, d))
    conds = jnp.concatenate([c_ctx[None, :], c, jnp.zeros((SUBLANES - 1 - n_sseq, d), F32)], axis=0)
    mods = _ada_call(conds.astype(F32), ada_w, ada_b)
    mods = mods.reshape(depth, SUBLANES, 1, 3 * d)
    s5_mats = _s5_prep_all(s5_lam_re, s5_lam_im, s5_log_step, s5_b_re, s5_b_im, s5_c_re, s5_c_im)

    new_re, new_im, new_ckv, new_kpe = [], [], [], []
    for layer in range(depth):
        kind, j = layer % N_MIXERS, layer // N_MIXERS
        last = layer == depth - 1
        ml = mods[layer]
        if kind == 0:
            width = s5_w_in.shape[2] // 2
            w = s5_w_in[j].astype(BF16)
            u3, z, xc = _inproj_call(x, ml, norm_g[layer], [(w, 0, width), (w, 1, width)], [F32, BF16],
                                     lane_blocked=(0,), s5_chunks=True, **geo)
            act, f_re, f_im = _s5_mix(u3, xc, z, j, s5_mats, s5_d[j], s5_glu_w[j].astype(BF16), s5_glu_b[j],
                                      state_s5_re[:, j], state_s5_im[:, j], n_prompt_seq=n_pseq,
                                      prompt_len=p_len, n_sample_seq=n_sseq, sample_len=s_len, bm=bm)
            new_re.append(f_re)
            new_im.append(f_im)
            w_out = s5_w_out[j]
        elif kind == 1:
            width = pool_w_in.shape[2] // 2
            w = pool_w_in[j].astype(BF16)
            u, z = _inproj_call(x, ml, norm_g[layer], [(w, 0, width), (w, 1, width)], [F32, BF16], **geo)
            act = _pool_call(u, z, pool_w[j].astype(BF16), pool_scale[j], n_prompt=n_prompt,
                             prompt_len=p_len, sample_len=s_len)
            w_out = pool_w_out[j]
        else:
            q_rank, kv_rank = mla_q_norm.shape[-1], mla_kv_norm.shape[-1]
            w_small, w_z, wq_a, wq_r = _mla_weights(mla_w_in[j], mla_wq_b[j])
            small, z = _inproj_call(x, ml, norm_g[layer], [w_small, w_z], [F32, BF16], **geo)
            cos_t, sin_t = _rope_tables(n_prompt, n_sseq, s_len)
            q, ckv_n, kpe_k = _mla_post_call(small, cos_t, sin_t, mla_q_norm[j], mla_kv_norm[j], wq_a, wq_r,
                                             bm=bm)
            wkv = mla_wkv_b[j].astype(BF16)
            past = cache_ckv.shape[2]
            k_len = past + s_len
            ckv_s = jnp.concatenate([cache_ckv[:, j].astype(F32), ckv_n[n_prompt:].reshape(n_sseq, s_len, kv_rank)],
                                    axis=1).reshape(n_sseq * k_len, kv_rank)
            kpe_cache = jnp.concatenate([cache_kpe[:, j].astype(BF16),
                                         jnp.zeros((n_sseq, past, LANES - MLA_ROPE), BF16)], axis=-1)
            kpe_s = jnp.concatenate([kpe_cache, kpe_k[n_prompt:].reshape(n_sseq, s_len, LANES)],
                                    axis=1).reshape(n_sseq * k_len, LANES)
            act = (_attn_call(q, ckv_n, wkv, kpe_k, z, q_row0=0, n_seq=n_pseq, q_len=p_len,
                              k_len=p_len, hg=MLA_HEADS, qb=p_len),
                   _attn_call(q, ckv_s, wkv, kpe_s, z, q_row0=n_prompt, n_seq=n_sseq, q_len=s_len,
                              k_len=k_len, hg=4, qb=256))
            new_ckv.append(ckv_n[:n_prompt].reshape(n_pseq, p_len, kv_rank))
            c_kpe = q_rank + kv_rank
            new_kpe.append(small[:n_prompt, c_kpe:c_kpe + MLA_ROPE].reshape(n_pseq, p_len, MLA_ROPE))
            w_out = mla_w_out[j]
        x = _outproj_call(act, x, ml, w_out.astype(BF16), final_norm_g, final_norm=last, split_out=last,
                          n_prompt=n_prompt, sample_len=s_len, bm=2 * bm)

    y_prompt = x[0].reshape(n_pseq, p_len, d)
    y_sample = x[1].reshape(n_sseq, s_len, d)
    return (y_prompt, y_sample, jnp.stack(new_re, axis=1), jnp.stack(new_im, axis=1),
            jnp.stack(new_ckv, axis=1), jnp.stack(new_kpe, axis=1))
```

```python
import functools
import math

import jax
import jax.numpy as jnp
import numpy as np
from jax import lax
from jax.experimental import pallas as pl
from jax.experimental.pallas import tpu as pltpu

S5_GROUP = 16
S5_CHUNK = 16
POOL_WINDOWS = (2, 4, 8, 16)
MLA_HEADS = 16
MLA_NOPE = 128
MLA_ROPE = 64
MLA_V = 128
GRID_W = 64
ROPE_THETA = 10000.0
NORM_EPS = 1e-6
N_MIXERS = 3

LANES = 128
SUBLANES = 8
VMEM_LIMIT_BYTES = 56 * 1024 * 1024

F32 = jnp.float32
BF16 = jnp.bfloat16
HIGHEST = lax.Precision.HIGHEST


def _cparams(*sem):
    return pltpu.CompilerParams(dimension_semantics=sem, vmem_limit_bytes=VMEM_LIMIT_BYTES)


def _sigmoid(x):
    return 0.5 + 0.5 * jnp.tanh(0.5 * x)


def _silu(x):
    return x * _sigmoid(x)


def _gelu_tanh(x):
    c = math.sqrt(2.0 / math.pi)
    hx = 0.5 * x
    return hx + hx * jnp.tanh(x * (c + (c * 0.044715) * (x * x)))


def _ada_kernel(c_ref, w_ref, b_ref, o_ref):
    a = _silu(c_ref[...])
    o_ref[...] = jnp.dot(a, w_ref[...], preferred_element_type=F32, precision=HIGHEST) + b_ref[...]


def _ada_call(conds, ada_w, ada_b):
    depth, d, d3 = ada_w.shape
    c8 = conds.shape[0]
    tn = d3 // 2
    return pl.pallas_call(
        _ada_kernel,
        out_shape=jax.ShapeDtypeStruct((depth, c8, d3), F32),
        grid=(depth, d3 // tn),
        in_specs=[
            pl.BlockSpec((c8, d), lambda l, n: (0, 0)),
            pl.BlockSpec((None, d, tn), lambda l, n: (l, 0, n)),
            pl.BlockSpec((None, 1, tn), lambda l, n: (l, 0, n)),
        ],
        out_specs=pl.BlockSpec((None, c8, tn), lambda l, n: (l, 0, n)),
        compiler_params=_cparams("arbitrary", "arbitrary"),
        name="ada_mod",
    )(conds, ada_w, ada_b.reshape(depth, 1, d3))


def _cond_of_block(i, n_prompt_blocks, blocks_per_sample):
    return jnp.where(i < n_prompt_blocks, 0, 1 + (i - n_prompt_blocks) // blocks_per_sample)


def _modulated(x, mod_ref, g_ref, d):
    ms = jnp.mean(x * x, axis=-1, keepdims=True)
    y = x * lax.rsqrt(ms + NORM_EPS) * g_ref[...]
    shift = mod_ref[:, 0:d]
    scale = mod_ref[:, d:2 * d]
    return (y * (1.0 + scale) + shift).astype(BF16)


def _inproj_kernel(*refs, d, n_chunk, chunk_rows, n_prompt_blocks, n_x):
    if n_x == 2:
        x = jnp.where(pl.program_id(0) < n_prompt_blocks, refs[0][...], refs[1][...])
    else:
        x = refs[0][...]
    mod_ref, g_ref = refs[n_x], refs[n_x + 1]
    rest = refs[n_x + 2:]
    u_scr = None
    if chunk_rows:
        rest, xc_ref, u_scr = rest[:-2], rest[-2], rest[-1]
    n_out = len(rest) // 2
    w_refs, o_refs = rest[:n_out], rest[n_out:]
    h = _modulated(x, mod_ref, g_ref, d)
    for k, (w_ref, o_ref) in enumerate(zip(w_refs, o_refs)):
        n = w_ref.shape[1]
        for c in range(0, n, n_chunk):
            e = min(c + n_chunk, n)
            r = jnp.dot(h, w_ref[:, c:e], preferred_element_type=F32)
            if len(o_ref.shape) == 3:
                for lb in range((e - c) // LANES):
                    part = r[:, lb * LANES:(lb + 1) * LANES]
                    o_ref[c // LANES + lb] = part.astype(o_ref.dtype)
                    if k == 0 and u_scr is not None:
                        u_scr[c // LANES + lb] = part
            else:
                o_ref[:, c:e] = r.astype(o_ref.dtype)
    if chunk_rows:
        _s5_to_chunks_kernel(u_scr, xc_ref, rows=chunk_rows)


def _inproj_call(x, mods_l, norm_g, weights, out_dtypes, *, n_prompt, sample_len, bm=512,
                 lane_blocked=(), s5_chunks=False):
    xs = list(x) if isinstance(x, tuple) else [x]
    n_tok = sum(a.shape[0] for a in xs)
    d = xs[0].shape[1]
    npb, bps = n_prompt // bm, sample_len // bm
    cond = functools.partial(_cond_of_block, n_prompt_blocks=npb, blocks_per_sample=bps)
    weights = [w if isinstance(w, tuple) else (w, 0, w.shape[1]) for w in weights]
    x_specs = _split_specs((bm, d), npb) if len(xs) == 2 else [pl.BlockSpec((bm, d), lambda i: (i, 0))]
    in_specs = x_specs + [
        pl.BlockSpec((None, 1, 3 * d), lambda i: (cond(i), 0, 0)),
        pl.BlockSpec((1, d), lambda i: (0, 0)),
    ] + [pl.BlockSpec((d, n), functools.partial(lambda i, blk: (0, blk), blk=blk)) for _, blk, n in weights]
    out_specs, out_shape = [], []
    for k, ((_, _, n), dt) in enumerate(zip(weights, out_dtypes)):
        if k in lane_blocked:
            out_specs.append(pl.BlockSpec((n // LANES, bm, LANES), lambda i: (0, i, 0)))
            out_shape.append(jax.ShapeDtypeStruct((n // LANES, n_tok, LANES), dt))
        else:
            out_specs.append(pl.BlockSpec((bm, n), lambda i: (i, 0)))
            out_shape.append(jax.ShapeDtypeStruct((n_tok, n), dt))
    chunk_rows = bm // S5_CHUNK if s5_chunks else 0
    scratch = []
    if s5_chunks:
        assert 0 in lane_blocked
        n_groups = weights[0][2] // S5_GROUP
        out_specs.append(pl.BlockSpec((n_groups, chunk_rows, 2 * LANES), lambda i: (0, i, 0)))
        out_shape.append(jax.ShapeDtypeStruct((n_groups, n_tok // S5_CHUNK, 2 * LANES), BF16))
        scratch = [pltpu.VMEM((weights[0][2] // LANES, bm, LANES), F32)]
    return pl.pallas_call(
        functools.partial(_inproj_kernel, d=d, n_chunk=512, chunk_rows=chunk_rows,
                          n_prompt_blocks=npb, n_x=len(xs)),
        out_shape=out_shape,
        grid=(n_tok // bm,),
        in_specs=in_specs,
        out_specs=out_specs,
        scratch_shapes=scratch,
        compiler_params=_cparams("arbitrary"),
        name="norm_mod_inproj",
    )(*xs, mods_l, norm_g.reshape(1, d), *[w for w, _, _ in weights])


def _outproj_kernel(*refs, d, final_norm, n_prompt_blocks, n_act, n_x):
    a_refs, x_refs = refs[:n_act], refs[n_act:n_act + n_x]
    mod_ref, w_ref, fg_ref = refs[n_act + n_x:n_act + n_x + 3]
    o_refs = refs[n_act + n_x + 3:]

    def finish(a_ref, x_ref, o_ref):
        y = jnp.dot(a_ref[...], w_ref[...], preferred_element_type=F32)
        gate = mod_ref[:, 2 * d:3 * d]
        xn = x_ref[...] + gate * y
        if final_norm:
            ms = jnp.mean(xn * xn, axis=-1, keepdims=True)
            xn = xn * lax.rsqrt(ms + NORM_EPS) * fg_ref[...]
        o_ref[...] = xn

    if max(n_act, n_x, len(o_refs)) == 1:
        finish(a_refs[0], x_refs[0], o_refs[0])
    else:
        is_prompt = pl.program_id(0) < n_prompt_blocks
        pl.when(is_prompt)(functools.partial(finish, a_refs[0], x_refs[0], o_refs[0]))
        pl.when(jnp.logical_not(is_prompt))(functools.partial(finish, a_refs[-1], x_refs[-1], o_refs[-1]))


def _split_specs(block, npb):
    return [pl.BlockSpec(block, lambda i: (jnp.minimum(i, npb - 1), 0)),
            pl.BlockSpec(block, lambda i: (jnp.maximum(i - npb, 0), 0))]


def _outproj_call(act, x, mods_l, w_out, final_g, *, n_prompt, sample_len, final_norm, bm=512,
                  split_out=False):
    acts = list(act) if isinstance(act, tuple) else [act]
    xs = list(x) if isinstance(x, tuple) else [x]
    n_tok = sum(a.shape[0] for a in xs)
    d, k = xs[0].shape[1], acts[0].shape[1]
    npb, bps = n_prompt // bm, sample_len // bm
    cond = functools.partial(_cond_of_block, n_prompt_blocks=npb, blocks_per_sample=bps)
    row = lambda i: (i, 0)
    act_specs = _split_specs((bm, k), npb) if len(acts) == 2 else [pl.BlockSpec((bm, k), row)]
    x_specs = _split_specs((bm, d), npb) if len(xs) == 2 else [pl.BlockSpec((bm, d), row)]
    if split_out:
        out_shape = [jax.ShapeDtypeStruct((n_prompt, d), F32), jax.ShapeDtypeStruct((n_tok - n_prompt, d), F32)]
        out_specs = _split_specs((bm, d), npb)
    else:
        out_shape = jax.ShapeDtypeStruct((n_tok, d), F32)
        out_specs = pl.BlockSpec((bm, d), row)
    return pl.pallas_call(
        functools.partial(_outproj_kernel, d=d, final_norm=final_norm, n_prompt_blocks=npb,
                          n_act=len(acts), n_x=len(xs)),
        out_shape=out_shape,
        grid=(n_tok // bm,),
        in_specs=act_specs + x_specs + [
            pl.BlockSpec((None, 1, 3 * d), lambda i: (cond(i), 0, 0)),
            pl.BlockSpec((k, d), lambda i: (0, 0)),
            pl.BlockSpec((1, d), lambda i: (0, 0)),
        ],
        out_specs=out_specs,
        compiler_params=_cparams("arbitrary"),
        name="outproj_residual",
    )(*acts, *xs, mods_l, w_out, final_g.reshape(1, d))


def _s5_time_of_lane_block():
    pos = np.arange(S5_CHUNK)
    half, blk = pos // 8, pos % 8
    g8 = np.arange(8)[:, None]
    return 8 * half[None, :] + (blk[None, :] - g8) % 8


def _s5_tables(lam_re, lam_im, log_step, b_re, b_im, c_re, c_im):
    t_chunk = S5_CHUNK
    n_groups, n_state = lam_re.shape[1], lam_re.shape[2]
    n_oct = n_groups // 8
    lam = lax.complex(lam_re.astype(F32), lam_im.astype(F32))
    step = jnp.exp(log_step.astype(F32))[..., None]
    lam_bar = jnp.exp(lam * step)
    b_bar = ((lam_bar - 1.0) / lam)[..., None] * lax.complex(b_re.astype(F32), b_im.astype(F32))
    c_mat = lax.complex(c_re.astype(F32), c_im.astype(F32))
    powers = [jnp.ones_like(lam_bar), lam_bar]
    for _ in range(t_chunk - 1):
        powers.append(powers[-1] * lam_bar)
    pw = jnp.stack(powers)

    zeros = jnp.zeros((t_chunk - 1, n_groups, n_state), pw.dtype)
    lag_f = jnp.concatenate([zeros, pw[:t_chunk, 0], zeros[:1]], axis=0)
    lag_b = jnp.concatenate([pw[t_chunk - 1::-1, 1], zeros, zeros[:1]], axis=0)
    plag = jnp.concatenate([lag_f, lag_b], axis=-1).transpose(1, 0, 2)
    plag = jnp.stack([plag.real, plag.imag])

    tl = _s5_time_of_lane_block()
    pw_ri = jnp.stack([pw.real, pw.imag]).reshape(2, t_chunk + 1, 2, n_oct, 8, n_state)
    m_idx = np.arange(t_chunk + 1)[None, None, :]

    def power_table(exponent, direction):
        sel = (exponent[:, :, None] == m_idx).astype(np.float32)
        tab = jnp.einsum('kxm,rmakp->rakxp', sel, pw_ri[:, :, direction], precision=HIGHEST)
        return tab.reshape(2, n_groups, t_chunk, n_state)

    def both(fwd, bwd):
        m = jnp.concatenate([fwd, bwd], axis=-1)
        return jnp.stack([m.real, m.imag])

    tin = jnp.concatenate([power_table(t_chunk - 1 - tl, 0), power_table(tl, 1)], axis=-1)
    tout = jnp.concatenate([power_table(tl + 1, 0), power_table(t_chunk - tl, 1)], axis=-1)
    bt = both(b_bar[0].transpose(0, 2, 1), b_bar[1].transpose(0, 2, 1))
    ct = both(c_mat[0], c_mat[1])
    lam_rows = both(pw[t_chunk, 0][:, None], pw[t_chunk, 1][:, None])[:, :, 0]
    return plag, tin, tout, bt, ct, lam_rows


def _s5_kmat_kernel(plag_ref, tin_ref, tout_ref, bt_ref, ct_ref, k_ref, pin_ref, pot_ref, x_scr, v_scr):
    masks = _lane_block_masks()
    n_lag = 2 * S5_CHUNK - 1
    for g8 in range(8):
        br, bi = bt_ref[0, g8], bt_ref[1, g8]
        cr, ci = ct_ref[0, g8], ct_ref[1, g8]
        def split(a):
            hi = a.astype(BF16)
            return hi, (a - hi.astype(F32)).astype(BF16)

        for m in range(n_lag):
            rows = slice(m * S5_GROUP, (m + 1) * S5_GROUP)
            pr, pi = plag_ref[0, g8, m:m + 1, :], plag_ref[1, g8, m:m + 1, :]
            for c0, part in ((0, cr * pr - ci * pi), (LANES, -(cr * pi + ci * pr))):
                x_scr[0, rows, c0:c0 + LANES], x_scr[1, rows, c0:c0 + LANES] = split(part)
        b_hi, b_lo = split(jnp.concatenate([jnp.concatenate([br, bi], axis=1)] * (LANES // S5_GROUP), axis=0))
        nt = functools.partial(lax.dot_general, dimension_numbers=(((1,), (1,)), ((), ())),
                               preferred_element_type=F32)
        v_scr[...] = nt(x_scr[0], b_hi) + nt(x_scr[0], b_lo) + nt(x_scr[1], b_hi)
        for pos in range(S5_CHUNK):
            rows = slice(pos * S5_GROUP, (pos + 1) * S5_GROUP)
            tr, ti = tin_ref[0, g8, pos:pos + 1, :], tin_ref[1, g8, pos:pos + 1, :]
            pin_ref[g8, rows, 0:LANES] = (tr * br - ti * bi).astype(BF16)
            pin_ref[g8, rows, LANES:2 * LANES] = (tr * bi + ti * br).astype(BF16)
            tr, ti = tout_ref[0, g8, pos:pos + 1, :], tout_ref[1, g8, pos:pos + 1, :]
            pot_ref[g8, rows, 0:LANES] = (tr * cr - ti * ci).astype(BF16)
            pot_ref[g8, rows, LANES:2 * LANES] = (-(tr * ci + ti * cr)).astype(BF16)
        for pos in range(S5_CHUNK):
            tau = 8 * (pos // 8) + (pos % 8 - g8) % 8
            rows = slice(pos * S5_GROUP, (pos + 1) * S5_GROUP)
            for half in range(2):
                acc = None
                for blk in range(8):
                    sigma = 8 * half + (blk - g8) % 8
                    m = S5_CHUNK - 1 - sigma + tau
                    src = v_scr[m * S5_GROUP:(m + 1) * S5_GROUP, :]
                    acc = src if acc is None else jnp.where(masks[blk], src, acc)
                k_ref[g8, rows, half * LANES:(half + 1) * LANES] = acc.astype(BF16)


def _s5_kmat_call(plag, tin, tout, bt, ct):
    n_groups = plag.shape[1]
    n_lag_rows = (2 * S5_CHUNK - 1) * S5_GROUP
    lag_spec = pl.BlockSpec((2, 8) + plag.shape[2:], lambda i: (0, i, 0, 0))
    tab_spec = pl.BlockSpec((2, 8, S5_GROUP, LANES), lambda i: (0, i, 0, 0))
    mat = jax.ShapeDtypeStruct((n_groups, 2 * LANES, 2 * LANES), BF16)
    mat_spec = pl.BlockSpec((8, 2 * LANES, 2 * LANES), lambda i: (i, 0, 0))
    return pl.pallas_call(
        _s5_kmat_kernel,
        out_shape=[mat, mat, mat],
        grid=(n_groups // 8,),
        in_specs=[lag_spec, tab_spec, tab_spec, tab_spec, tab_spec],
        out_specs=[mat_spec, mat_spec, mat_spec],
        scratch_shapes=[pltpu.VMEM((2, n_lag_rows, 2 * LANES), BF16), pltpu.VMEM((n_lag_rows, LANES), F32)],
        compiler_params=_cparams("arbitrary"),
        name="s5_kmat",
    )(plag, tin, tout, bt, ct)


def _lane_block_masks():
    blk = lax.broadcasted_iota(jnp.int32, (1, LANES), 1) // S5_GROUP
    return [blk == b for b in range(8)]


def _diagonal_merge(src):
    blk = lax.broadcasted_iota(jnp.int32, (1, LANES), 1) // S5_GROUP
    q = list(src)
    for bit in (1, 2, 4):
        take = (blk & bit) != 0
        q = [jnp.where(take, q[(x + bit) % 8], q[x]) for x in range(8)]
    return [q[(-t) % 8] for t in range(8)]


def _s5_to_chunks_kernel(u_ref, x_ref, *, rows):
    for o in range(u_ref.shape[0]):
        for r0 in range(0, rows, SUBLANES):
            for half in range(2):
                rolled = []
                for t8 in range(8):
                    v = u_ref[o, pl.ds(r0 * S5_CHUNK + 8 * half + t8, SUBLANES, stride=S5_CHUNK), :]
                    rolled.append(pltpu.roll(v, t8 * S5_GROUP, 1) if t8 else v)
                for g8, merged in enumerate(_diagonal_merge(rolled)):
                    x_ref[o * 8 + g8, r0:r0 + SUBLANES, half * LANES:(half + 1) * LANES] = merged.astype(BF16)


def _s5_tail_kernel(yc_ref, u_ref, d_ref, z_ref, w_ref, b_ref, o_ref, nat_scr, y_scr, *, rows, n_chunk):
    n_blk = u_ref.shape[0]
    tile = 2 * SUBLANES
    per = n_chunk // LANES
    for piece, r0 in enumerate(range(0, rows, tile)):
        tok = slice(r0 * S5_CHUNK, (r0 + tile) * S5_CHUNK)
        for o in range(n_blk):
            for half in range(2):
                src = [yc_ref[o * 8 + g8, r0:r0 + tile, half * LANES:(half + 1) * LANES] for g8 in range(8)]
                for t8, merged in enumerate(_diagonal_merge(src)):
                    nat = pltpu.roll(merged, (8 - t8) * S5_GROUP, 1) if t8 else merged
                    nat_scr[piece % 2, o, pl.ds(8 * half + t8, tile, stride=S5_CHUNK), :] = nat
            d_vec = d_ref[:, o * LANES:(o + 1) * LANES]
            y_scr[o, tok] = _gelu_tanh(nat_scr[piece % 2, o] + d_vec * u_ref[o, tok])
        yb = jnp.concatenate([y_scr[o, tok].astype(BF16) for o in range(n_blk)], axis=1)
        for c in range(0, n_blk, per):
            sl = slice(c * LANES, (c + per) * LANES)
            gate = _sigmoid(jnp.dot(yb, w_ref[:, sl], preferred_element_type=F32) + b_ref[:, sl])
            y = jnp.concatenate([y_scr[c + k, tok] for k in range(per)], axis=1)
            o_ref[tok, sl] = (y * gate * _silu(z_ref[tok, sl].astype(F32))).astype(o_ref.dtype)


def _s5_tail_call(yc, u3, d_skip, z, glu_w, glu_b, *, rows=32):
    n_blk, n_tok, _ = u3.shape
    n_groups, n_rows, _ = yc.shape
    width = n_blk * LANES
    bm = rows * S5_CHUNK
    tile_tok = 2 * SUBLANES * S5_CHUNK
    fix = lambda i: (0, 0)
    return pl.pallas_call(
        functools.partial(_s5_tail_kernel, rows=rows, n_chunk=min(512, width)),
        out_shape=jax.ShapeDtypeStruct((n_tok, width), BF16),
        grid=(n_rows // rows,),
        in_specs=[pl.BlockSpec((n_groups, rows, 2 * LANES), lambda i: (0, i, 0)),
                  pl.BlockSpec((n_blk, bm, LANES), lambda i: (0, i, 0)),
                  pl.BlockSpec((1, width), fix),
                  pl.BlockSpec((bm, width), lambda i: (i, 0)),
                  pl.BlockSpec((width, width), fix),
                  pl.BlockSpec((1, width), fix)],
        out_specs=pl.BlockSpec((bm, width), lambda i: (i, 0)),
        scratch_shapes=[pltpu.VMEM((2, n_blk, tile_tok, LANES), F32),
                        pltpu.VMEM((n_blk, bm, LANES), F32)],
        compiler_params=_cparams("arbitrary"),
        name="s5_tail",
    )(yc, u3, d_skip.reshape(1, width).astype(F32), z, glu_w, glu_b.reshape(1, width).astype(F32))


def _s5_chunk_kernel(x_ref, kt_ref, pin_ref, po_ref, lam_ref, h0r_ref, h0i_ref,
                     y_ref, fr_ref, fi_ref, r_scr, st_scr, *, segments, seq_block):
    gb = SUBLANES
    rows = x_ref.shape[1]
    lane = lax.broadcasted_iota(jnp.int32, (1, LANES), 1)
    fwd_lanes = lane < (LANES // 2)
    del rows
    seg_rows = [(row0, n_seq * n_chunks) for row0, n_seq, n_chunks, _, _ in segments]
    for row0, n_rows in seg_rows:
        for g in range(gb):
            r = jnp.dot(x_ref[g, row0:row0 + n_rows, :], pin_ref[g], preferred_element_type=F32)
            of_group = pl.ds(row0 * gb + g, n_rows, stride=gb)
            r_scr[0, of_group, :] = r[:, 0:LANES]
            r_scr[1, of_group, :] = r[:, LANES:2 * LANES]
    ar, ai = lam_ref[0], lam_ref[1]
    for row0, n_seq, n_chunks, from_input, to_output in segments:
        for b0 in range(0, n_seq, seq_block):
            nb = min(seq_block, n_seq - b0)

            def step(i, carry, row0=row0, n_chunks=n_chunks, b0=b0, nb=nb):
                out = []
                for k in range(nb):
                    base = row0 + (b0 + k) * n_chunks
                    at_f = pl.ds((base + i) * gb, gb)
                    at_b = pl.ds((base + (n_chunks - 1) - i) * gb, gb)
                    s_re, s_im = carry[k]
                    half = LANES // 2
                    st_scr[0, at_f, 0:half] = s_re[:, 0:half]
                    st_scr[0, at_b, half:LANES] = s_re[:, half:LANES]
                    st_scr[1, at_f, 0:half] = s_im[:, 0:half]
                    st_scr[1, at_b, half:LANES] = s_im[:, half:LANES]
                    v_re = jnp.where(fwd_lanes, r_scr[0, at_f, :], r_scr[0, at_b, :])
                    v_im = jnp.where(fwd_lanes, r_scr[1, at_f, :], r_scr[1, at_b, :])
                    out.append((ar * s_re - ai * s_im + v_re, ar * s_im + ai * s_re + v_im))
                return tuple(out)

            if from_input:
                init = tuple((h0r_ref[b0 + k], h0i_ref[b0 + k]) for k in range(nb))
            else:
                init = tuple((jnp.zeros((gb, LANES), F32),) * 2 for _ in range(nb))
            fin = init
            for i in range(n_chunks):
                fin = step(i, fin)
            if to_output:
                for k in range(nb):
                    fr_ref[b0 + k] = fin[k][0]
                    fi_ref[b0 + k] = fin[k][1]
    for row0, n_rows in seg_rows:
        for g in range(gb):
            of_group = pl.ds(row0 * gb + g, n_rows, stride=gb)
            st = jnp.concatenate([st_scr[cb, of_group, :] for cb in range(2)], axis=1).astype(BF16)
            rows = slice(row0, row0 + n_rows)
            nt = functools.partial(lax.dot_general, dimension_numbers=(((1,), (1,)), ((), ())),
                                   preferred_element_type=F32)
            y_ref[g, rows, :] = (nt(x_ref[g, rows, :], kt_ref[g]) + nt(st, po_ref[g])).astype(y_ref.dtype)


def _s5_chunk_call(xc, kt, pin, pout, lam_rows, h0_re, h0_im, *, layer, segments, n_final):
    n_groups, rows, _ = xc.shape
    gb = SUBLANES
    s_in = h0_re.shape[0]
    kern = functools.partial(_s5_chunk_kernel, segments=segments, seq_block=8)
    g3 = lambda i: (i, 0, 0)
    blk0 = layer * (n_groups // gb)
    p3 = lambda i: (i + blk0, 0, 0)
    mid = lambda i: (0, i, 0)
    return pl.pallas_call(
        kern,
        out_shape=[jax.ShapeDtypeStruct((n_groups, rows, 2 * LANES), F32),
                   jax.ShapeDtypeStruct((n_final, n_groups, LANES), F32),
                   jax.ShapeDtypeStruct((n_final, n_groups, LANES), F32)],
        grid=(n_groups // gb,),
        in_specs=[
            pl.BlockSpec((gb, rows, 2 * LANES), g3),
            pl.BlockSpec((gb, 2 * LANES, 2 * LANES), p3),
            pl.BlockSpec((gb, 2 * LANES, 2 * LANES), p3),
            pl.BlockSpec((gb, 2 * LANES, 2 * LANES), p3),
            pl.BlockSpec((2, gb, LANES), lambda i: (0, i + blk0, 0)),
            pl.BlockSpec((s_in, gb, LANES), mid),
            pl.BlockSpec((s_in, gb, LANES), mid),
        ],
        out_specs=[pl.BlockSpec((gb, rows, 2 * LANES), g3),
                   pl.BlockSpec((n_final, gb, LANES), mid),
                   pl.BlockSpec((n_final, gb, LANES), mid)],
        scratch_shapes=[pltpu.VMEM((2, rows * gb, LANES), F32),
                        pltpu.VMEM((2, rows * gb, LANES), F32)],
        compiler_params=_cparams("arbitrary"),
        name="s5_chunk_scan",
    )(xc, kt, pin, pout, lam_rows, h0_re, h0_im)


def _s5_prep_all(lam_re, lam_im, log_step, b_re, b_im, c_re, c_im):
    tabs = jax.vmap(_s5_tables)(lam_re, lam_im, log_step, b_re, b_im, c_re, c_im)
    plag, tin, tout, bt, ct, lam_rows = [jnp.moveaxis(t, 0, 1).reshape((2, -1) + t.shape[3:]) for t in tabs]
    kt, pin, pot = _s5_kmat_call(plag, tin, tout, bt, ct)
    return kt, pin, pot, lam_rows


def _s5_mix(u3, xc, z, layer, mats, d_skip, glu_w, glu_b, st_re, st_im, *, n_prompt_seq, prompt_len,
            n_sample_seq, sample_len, bm):
    kt, pin, pout, lam_rows = mats
    n_state = LANES // 2
    pc, sc = prompt_len // S5_CHUNK, sample_len // S5_CHUNK

    def state_rows(s):
        return jnp.concatenate([s[:, 0], s[:, 1]], axis=-1).astype(F32)

    segments = ((0, n_prompt_seq, pc, False, True), (n_prompt_seq * pc, n_sample_seq, sc, True, False))
    yc, fr, fi = _s5_chunk_call(xc, kt, pin, pout, lam_rows, state_rows(st_re), state_rows(st_im),
                                layer=layer, segments=segments, n_final=n_prompt_seq)
    act = _s5_tail_call(yc, u3, d_skip, z, glu_w, glu_b, rows=bm // S5_CHUNK)

    def unpack(f):
        return jnp.stack([f[:, :, :n_state], f[:, :, n_state:]], axis=1)

    return act, unpack(fr), unpack(fi)


POOL_TILE = 2 * LANES
POOL_EDGE = 16


def _pool_band_matrices():
    t = np.arange(POOL_TILE)[:, None]
    s = np.arange(POOL_TILE)[None, :]
    mats = []
    for win in POOL_WINDOWS:
        lo = win // 2
        inside = lambda src: ((src >= t - lo) & (src <= t + lo - 1)).astype(np.float32)
        mats.append(np.stack([inside(s), inside(s - POOL_TILE), inside(s + POOL_TILE)]))
    return np.stack(mats)


def _pool_kernel(u_ref, z_ref, w_ref, s_ref, a_ref, o_ref, sum_scr, *, n_prompt_blocks, prompt_len,
                 sample_len):
    rows = u_ref.shape[0]
    n_tiles = rows // POOL_TILE
    is_prompt = pl.program_id(0) < n_prompt_blocks
    seq_len = jnp.where(is_prompt, prompt_len, sample_len)
    lo = jnp.left_shift(1, pl.program_id(1))
    t = lax.broadcasted_iota(jnp.int32, (rows, 1), 0) & (seq_len - 1)
    u = u_ref[...]
    u_hi = u.astype(BF16)
    u_lo = (u - u_hi.astype(F32)).astype(BF16)

    def band(a, tile):
        sl = slice(tile * POOL_TILE, (tile + 1) * POOL_TILE)
        return (jnp.dot(a, u_hi[sl], preferred_element_type=F32)
                + jnp.dot(a, u_lo[sl], preferred_element_type=F32))

    for r in range(n_tiles):
        sum_scr[r * POOL_TILE:(r + 1) * POOL_TILE, :] = band(a_ref[0], r)

    @pl.when(jnp.logical_not(is_prompt))
    def _():
        for r in range(n_tiles):
            if r > 0:
                top = slice(r * POOL_TILE, r * POOL_TILE + POOL_EDGE)
                sum_scr[top, :] = sum_scr[top, :] + band(a_ref[1, 0:POOL_EDGE, :], r - 1)
            if r < n_tiles - 1:
                bot = slice((r + 1) * POOL_TILE - POOL_EDGE, (r + 1) * POOL_TILE)
                sum_scr[bot, :] = sum_scr[bot, :] + band(a_ref[2, POOL_TILE - POOL_EDGE:POOL_TILE, :], r + 1)

    cnt = jnp.minimum(t + lo, seq_len) - jnp.maximum(t - lo, 0)
    p = sum_scr[...] / cnt.astype(F32) - u
    m = jnp.dot(p.astype(BF16), w_ref[...], preferred_element_type=F32) * s_ref[...]
    o_ref[...] = (m * _silu(z_ref[...].astype(F32))).astype(o_ref.dtype)


def _pool_call(u, z, pool_w, pool_scale, *, n_prompt, prompt_len, sample_len, rows=2048):
    n_tok, width = u.shape
    n_groups = len(POOL_WINDOWS)
    cg = width // n_groups
    assert prompt_len & (prompt_len - 1) == 0 and sample_len & (sample_len - 1) == 0
    assert rows % prompt_len == 0 and n_prompt % rows == 0
    assert prompt_len == POOL_TILE and sample_len == rows and max(POOL_WINDOWS) // 2 <= POOL_EDGE
    assert POOL_WINDOWS == tuple(2 << g for g in range(n_groups))
    kern = functools.partial(_pool_kernel, n_prompt_blocks=n_prompt // rows, prompt_len=prompt_len,
                             sample_len=sample_len)
    bands = jnp.asarray(_pool_band_matrices(), BF16)
    return pl.pallas_call(
        kern,
        out_shape=jax.ShapeDtypeStruct((n_tok, width), BF16),
        grid=(n_tok // rows, n_groups),
        in_specs=[
            pl.BlockSpec((rows, cg), lambda i, g: (i, g)),
            pl.BlockSpec((rows, cg), lambda i, g: (i, g)),
            pl.BlockSpec((None, cg, cg), lambda i, g: (g, 0, 0)),
            pl.BlockSpec((1, cg), lambda i, g: (0, g)),
            pl.BlockSpec((None, 3, POOL_TILE, POOL_TILE), lambda i, g: (g, 0, 0, 0)),
        ],
        out_specs=pl.BlockSpec((rows, cg), lambda i, g: (i, g)),
        scratch_shapes=[pltpu.VMEM((rows, cg), F32)],
        compiler_params=_cparams("arbitrary", "arbitrary"),
        name="pool_mix",
    )(u, z, pool_w, pool_scale.reshape(1, width).astype(F32), bands)


MLA_QW = 2 * LANES

_ROT_SRC = np.concatenate([np.arange(16, 32), np.arange(0, 16), np.arange(48, 64), np.arange(32, 48)])
_ROT_SIGN = np.concatenate([-np.ones(16), np.ones(16), -np.ones(16), np.ones(16)]).astype(np.float32)


def _rope_tables(n_prompt, n_sample_seq, sample_len):
    half = MLA_ROPE // 4
    tok = jnp.arange(sample_len)
    row = (tok // GRID_W).astype(F32)
    col = (tok % GRID_W).astype(F32)
    inv = ROPE_THETA ** (-jnp.arange(half, dtype=F32) / half)
    a_row, a_col = row[:, None] * inv, col[:, None] * inv
    cos = jnp.concatenate([jnp.cos(a_row), jnp.cos(a_row), jnp.cos(a_col), jnp.cos(a_col)], axis=-1)
    sin = jnp.concatenate([jnp.sin(a_row), jnp.sin(a_row), jnp.sin(a_col), jnp.sin(a_col)], axis=-1)
    pad = jnp.zeros((sample_len, LANES - MLA_ROPE), F32)
    cos_s = jnp.tile(jnp.concatenate([cos, pad], axis=-1), (n_sample_seq, 1))
    sin_s = jnp.tile(jnp.concatenate([sin, pad], axis=-1), (n_sample_seq, 1))
    cos_p = jnp.concatenate([jnp.ones((n_prompt, MLA_ROPE), F32), jnp.zeros((n_prompt, LANES - MLA_ROPE), F32)], -1)
    return jnp.concatenate([cos_p, cos_s]), jnp.concatenate([jnp.zeros((n_prompt, LANES), F32), sin_s])


def _rms(x, g):
    return x * lax.rsqrt(jnp.mean(x * x, axis=-1, keepdims=True) + NORM_EPS) * g


def _mla_post_kernel(sm_ref, cos_ref, sin_ref, qn_ref, kn_ref, wa_ref, wb_ref,
                     q_ref, ckv_ref, kpe_ref, *, q_rank, kv_rank, heads_per_dot):
    cosp, sinp = cos_ref[...], sin_ref[...]
    qn = _rms(sm_ref[:, 0:q_rank], qn_ref[...]).astype(BF16)
    for h0 in range(0, MLA_HEADS, heads_per_dot):
        a = jnp.dot(qn, wa_ref[:, h0 * MLA_QW:(h0 + heads_per_dot) * MLA_QW], preferred_element_type=F32)
        b = jnp.dot(qn, wb_ref[:, h0 * LANES:(h0 + heads_per_dot) * LANES], preferred_element_type=F32)
        for j in range(heads_per_dot):
            h = h0 + j
            q_ref[:, h * MLA_QW:h * MLA_QW + LANES] = a[:, j * MLA_QW:j * MLA_QW + LANES].astype(BF16)
            pe = a[:, j * MLA_QW + LANES:(j + 1) * MLA_QW] * cosp + b[:, j * LANES:(j + 1) * LANES] * sinp
            q_ref[:, h * MLA_QW + LANES:(h + 1) * MLA_QW] = pe.astype(BF16)
    c0 = q_rank
    ckv_ref[...] = _rms(sm_ref[:, c0:c0 + kv_rank], kn_ref[...])
    k0 = c0 + kv_rank
    kpe_ref[...] = (sm_ref[:, k0:k0 + LANES] * cosp + sm_ref[:, k0 + LANES:k0 + 2 * LANES] * sinp).astype(BF16)


def _mla_post_call(small, cos_t, sin_t, q_norm, kv_norm, wq_a, wq_b, *, bm=512):
    n_tok, ws = small.shape
    q_rank, kv_rank = q_norm.shape[-1], kv_norm.shape[-1]
    row = lambda i: (i, 0)
    fix = lambda i: (0, 0)
    kern = functools.partial(_mla_post_kernel, q_rank=q_rank, kv_rank=kv_rank, heads_per_dot=4)
    return pl.pallas_call(
        kern,
        out_shape=[jax.ShapeDtypeStruct((n_tok, MLA_HEADS * MLA_QW), BF16),
                   jax.ShapeDtypeStruct((n_tok, kv_rank), F32),
                   jax.ShapeDtypeStruct((n_tok, LANES), BF16)],
        grid=(n_tok // bm,),
        in_specs=[
            pl.BlockSpec((bm, ws), row),
            pl.BlockSpec((bm, LANES), row),
            pl.BlockSpec((bm, LANES), row),
            pl.BlockSpec((1, q_rank), fix),
            pl.BlockSpec((1, kv_rank), fix),
            pl.BlockSpec(wq_a.shape, fix),
            pl.BlockSpec(wq_b.shape, fix),
        ],
        out_specs=[pl.BlockSpec((bm, MLA_HEADS * MLA_QW), row),
                   pl.BlockSpec((bm, kv_rank), row),
                   pl.BlockSpec((bm, LANES), row)],
        compiler_params=_cparams("arbitrary"),
        name="mla_q_rope",
    )(small, cos_t, sin_t, q_norm.reshape(1, q_rank).astype(F32), kv_norm.reshape(1, kv_rank).astype(F32),
      wq_a, wq_b)


def _attn_kernel(q_ref, ckv_ref, wkv_ref, kpe_ref, z_ref, o_ref, kcat_scr, vext_scr, *, hg, scale):
    c2 = scale * math.log2(math.e)

    @pl.when(pl.program_id(2) == 0)
    def _():
        ckv = ckv_ref[...].astype(BF16)
        ones = jnp.ones((ckv.shape[0], LANES), BF16)
        for j in range(hg):
            kv = jnp.dot(ckv, wkv_ref[:, j * 2 * LANES:(j + 1) * 2 * LANES],
                         preferred_element_type=F32).astype(BF16)
            kcat_scr[j, :, 0:LANES] = kv[:, 0:LANES]
            kcat_scr[j, :, LANES:2 * LANES] = kpe_ref[...]
            vext_scr[j, :, 0:LANES] = kv[:, LANES:2 * LANES]
            vext_scr[j, :, LANES:2 * LANES] = ones

    scores = [lax.dot_general(q_ref[:, j * MLA_QW:(j + 1) * MLA_QW], kcat_scr[j], (((1,), (1,)), ((), ())),
                              preferred_element_type=F32) for j in range(hg)]
    probs = [jnp.exp2((s - jnp.max(s, axis=-1, keepdims=True)) * c2).astype(BF16) for s in scores]
    for j in range(hg):
        pv = jnp.dot(probs[j], vext_scr[j], preferred_element_type=F32)
        zs = slice(j * MLA_V, (j + 1) * MLA_V)
        o = pv[:, 0:MLA_V] / pv[:, MLA_V:2 * MLA_V]
        o_ref[:, zs] = (o * _silu(z_ref[:, zs].astype(F32))).astype(o_ref.dtype)


def _attn_call(q, ckv, wkv_b, kpe, z, *, q_row0, n_seq, q_len, k_len, hg, qb):
    width = MLA_HEADS * MLA_V
    nqb = q_len // qb
    qb0 = q_row0 // qb
    assert q_row0 % qb == 0 and q_len % qb == 0
    scale = float((MLA_NOPE + MLA_ROPE) ** -0.5)
    qrow = lambda b, g, i: (qb0 + b * nqb + i, g)
    return pl.pallas_call(
        functools.partial(_attn_kernel, hg=hg, scale=scale),
        out_shape=jax.ShapeDtypeStruct((n_seq * q_len, width), BF16),
        grid=(n_seq, MLA_HEADS // hg, nqb),
        in_specs=[
            pl.BlockSpec((qb, hg * MLA_QW), qrow),
            pl.BlockSpec((k_len, ckv.shape[1]), lambda b, g, i: (b, 0)),
            pl.BlockSpec((wkv_b.shape[0], hg * 2 * LANES), lambda b, g, i: (0, g)),
            pl.BlockSpec((k_len, LANES), lambda b, g, i: (b, 0)),
            pl.BlockSpec((qb, hg * MLA_V), qrow),
        ],
        out_specs=pl.BlockSpec((qb, hg * MLA_V), lambda b, g, i: (b * nqb + i, g)),
        scratch_shapes=[pltpu.VMEM((hg, k_len, MLA_QW), BF16),
                        pltpu.VMEM((hg, k_len, 2 * MLA_V), BF16)],
        compiler_params=_cparams("arbitrary", "arbitrary", "arbitrary"),
        name="mla_attention",
    )(q, ckv, wkv_b, kpe, z)


def _mla_weights(w_in, wq_b):
    q_rank = wq_b.shape[0]
    kv_rank = w_in.shape[1] - q_rank - MLA_ROPE - MLA_HEADS * MLA_V
    d = w_in.shape[0]
    c_kpe = q_rank + kv_rank
    zpad = jnp.zeros((d, LANES - MLA_ROPE), w_in.dtype)
    kpe_w = w_in[:, c_kpe:c_kpe + MLA_ROPE]
    w_small = jnp.concatenate([w_in[:, :c_kpe], kpe_w, zpad,
                               kpe_w[:, _ROT_SRC] * _ROT_SIGN, zpad], axis=1)
    w_z = w_in[:, c_kpe + MLA_ROPE:]
    hd = MLA_NOPE + MLA_ROPE
    wq3 = wq_b.reshape(q_rank, MLA_HEADS, hd)
    pe = wq3[:, :, MLA_NOPE:]
    z3 = jnp.zeros((q_rank, MLA_HEADS, LANES - MLA_ROPE), wq_b.dtype)
    wq_a = jnp.concatenate([wq3, z3], axis=-1).reshape(q_rank, MLA_HEADS * MLA_QW)
    wq_r = jnp.concatenate([pe[:, :, _ROT_SRC] * _ROT_SIGN, z3], axis=-1).reshape(q_rank, MLA_HEADS * LANES)
    return w_small.astype(BF16), w_z.astype(BF16), wq_a.astype(BF16), wq_r.astype(BF16)


def kernel(x_prompt, x_sample, state_s5_re, state_s5_im, cache_ckv, cache_kpe, c, c_ctx, norm_g, ada_w, ada_b, final_norm_g, s5_w_in, s5_lam_re, s5_lam_im, s5_log_step, s5_b_re, s5_b_im, s5_c_re, s5_c_im, s5_d, s5_glu_w, s5_glu_b, s5_w_out, pool_w_in, pool_w, pool_scale, pool_w_out, mla_w_in, mla_q_norm, mla_wq_b, mla_kv_norm, mla_wkv_b, mla_w_out):
    n_pseq, p_len, d = x_prompt.shape
    n_sseq, s_len, _ = x_sample.shape
    depth = norm_g.shape[0]
    n_prompt = n_pseq * p_len
    bm = 512
    geo = dict(n_prompt=n_prompt, sample_len=s_len, bm=bm)

    x = (x_prompt.reshape(n_prompt, d), x_sample.reshape(n_sseq * s_len, d))
    conds = jnp.concatenate([c_ctx[None, :], c, jnp.zeros((SUBLANES - 1 - n_sseq, d), F32)], axis=0)
    mods = _ada_call(conds.astype(F32), ada_w, ada_b)
    mods = mods.reshape(depth, SUBLANES, 1, 3 * d)
    s5_mats = _s5_prep_all(s5_lam_re, s5_lam_im, s5_log_step, s5_b_re, s5_b_im, s5_c_re, s5_c_im)

    new_re, new_im, new_ckv, new_kpe = [], [], [], []
    for layer in range(depth):
        kind, j = layer % N_MIXERS, layer // N_MIXERS
        last = layer == depth - 1
        ml = mods[layer]
        if kind == 0:
            width = s5_w_in.shape[2] // 2
            w = s5_w_in[j].astype(BF16)
            u3, z, xc = _inproj_call(x, ml, norm_g[layer], [(w, 0, width), (w, 1, width)], [F32, BF16],
                                     lane_blocked=(0,), s5_chunks=True, **geo)
            act, f_re, f_im = _s5_mix(u3, xc, z, j, s5_mats, s5_d[j], s5_glu_w[j].astype(BF16), s5_glu_b[j],
                                      state_s5_re[:, j], state_s5_im[:, j], n_prompt_seq=n_pseq,
                                      prompt_len=p_len, n_sample_seq=n_sseq, sample_len=s_len, bm=bm)
            new_re.append(f_re)
            new_im.append(f_im)
            w_out = s5_w_out[j]
        elif kind == 1:
            width = pool_w_in.shape[2] // 2
            w = pool_w_in[j].astype(BF16)
            u, z = _inproj_call(x, ml, norm_g[layer], [(w, 0, width), (w, 1, width)], [F32, BF16], **geo)
            act = _pool_call(u, z, pool_w[j].astype(BF16), pool_scale[j], n_prompt=n_prompt,
                             prompt_len=p_len, sample_len=s_len)
            w_out = pool_w_out[j]
        else:
            q_rank, kv_rank = mla_q_norm.shape[-1], mla_kv_norm.shape[-1]
            w_small, w_z, wq_a, wq_r = _mla_weights(mla_w_in[j], mla_wq_b[j])
            small, z = _inproj_call(x, ml, norm_g[layer], [w_small, w_z], [F32, BF16], **geo)
            cos_t, sin_t = _rope_tables(n_prompt, n_sseq, s_len)
            q, ckv_n, kpe_k = _mla_post_call(small, cos_t, sin_t, mla_q_norm[j], mla_kv_norm[j], wq_a, wq_r,
                                             bm=bm)
            wkv = mla_wkv_b[j].astype(BF16)
            past = cache_ckv.shape[2]
            k_len = past + s_len
            ckv_s = jnp.concatenate([cache_ckv[:, j].astype(F32), ckv_n[n_prompt:].reshape(n_sseq, s_len, kv_rank)],
                                    axis=1).reshape(n_sseq * k_len, kv_rank)
            kpe_cache = jnp.concatenate([cache_kpe[:, j].astype(BF16),
                                         jnp.zeros((n_sseq, past, LANES - MLA_ROPE), BF16)], axis=-1)
            kpe_s = jnp.concatenate([kpe_cache, kpe_k[n_prompt:].reshape(n_sseq, s_len, LANES)],
                                    axis=1).reshape(n_sseq * k_len, LANES)
            act = (_attn_call(q, ckv_n, wkv, kpe_k, z, q_row0=0, n_seq=n_pseq, q_len=p_len,
                              k_len=p_len, hg=MLA_HEADS, qb=p_len),
                   _attn_call(q, ckv_s, wkv, kpe_s, z, q_row0=n_prompt, n_seq=n_sseq, q_len=s_len,
                              k_len=k_len, hg=4, qb=256))
            new_ckv.append(ckv_n[:n_prompt].reshape(n_pseq, p_len, kv_rank))
            c_kpe = q_rank + kv_rank
            new_kpe.append(small[:n_prompt, c_kpe:c_kpe + MLA_ROPE].reshape(n_pseq, p_len, MLA_ROPE))
            w_out = mla_w_out[j]
        x = _outproj_call(act, x, ml, w_out.astype(BF16), final_norm_g, final_norm=last, split_out=last,
                          n_prompt=n_prompt, sample_len=s_len, bm=2 * bm)

    y_prompt = x[0].reshape(n_pseq, p_len, d)
    y_sample = x[1].reshape(n_sseq, s_len, d)
    return (y_prompt, y_sample, jnp.stack(new_re, axis=1), jnp.stack(new_im, axis=1),
            jnp.stack(new_ckv, axis=1), jnp.stack(new_kpe, axis=1))
```

```python
import functools
import math

import jax
import jax.numpy as jnp
import numpy as np
from jax import lax
from jax.experimental import pallas as pl
from jax.experimental.pallas import tpu as pltpu

S5_GROUP = 16
S5_CHUNK = 16
POOL_WINDOWS = (2, 4, 8, 16)
MLA_HEADS = 16
MLA_NOPE = 128
MLA_ROPE = 64
MLA_V = 128
GRID_W = 64
ROPE_THETA = 10000.0
NORM_EPS = 1e-6
N_MIXERS = 3

LANES = 128
SUBLANES = 8
VMEM_LIMIT_BYTES = 56 * 1024 * 1024

F32 = jnp.float32
BF16 = jnp.bfloat16
HIGHEST = lax.Precision.HIGHEST


def _cparams(*sem):
    return pltpu.CompilerParams(dimension_semantics=sem, vmem_limit_bytes=VMEM_LIMIT_BYTES)


def _sigmoid(x):
    return 0.5 + 0.5 * jnp.tanh(0.5 * x)


def _silu(x):
    return x * _sigmoid(x)


def _gelu_tanh(x):
    c = math.sqrt(2.0 / math.pi)
    hx = 0.5 * x
    return hx + hx * jnp.tanh(x * (c + (c * 0.044715) * (x * x)))


def _ada_kernel(c_ref, w_ref, b_ref, o_ref):
    a = _silu(c_ref[...])
    o_ref[...] = jnp.dot(a, w_ref[...], preferred_element_type=F32, precision=HIGHEST) + b_ref[...]


def _ada_call(conds, ada_w, ada_b):
    depth, d, d3 = ada_w.shape
    c8 = conds.shape[0]
    tn = d3 // 2
    return pl.pallas_call(
        _ada_kernel,
        out_shape=jax.ShapeDtypeStruct((depth, c8, d3), F32),
        grid=(depth, d3 // tn),
        in_specs=[
            pl.BlockSpec((c8, d), lambda l, n: (0, 0)),
            pl.BlockSpec((None, d, tn), lambda l, n: (l, 0, n)),
            pl.BlockSpec((None, 1, tn), lambda l, n: (l, 0, n)),
        ],
        out_specs=pl.BlockSpec((None, c8, tn), lambda l, n: (l, 0, n)),
        compiler_params=_cparams("arbitrary", "arbitrary"),
        name="ada_mod",
    )(conds, ada_w, ada_b.reshape(depth, 1, d3))


def _cond_of_block(i, n_prompt_blocks, blocks_per_sample):
    return jnp.where(i < n_prompt_blocks, 0, 1 + (i - n_prompt_blocks) // blocks_per_sample)


def _modulated(x, mod_ref, g_ref, d):
    ms = jnp.mean(x * x, axis=-1, keepdims=True)
    y = x * lax.rsqrt(ms + NORM_EPS) * g_ref[...]
    shift = mod_ref[:, 0:d]
    scale = mod_ref[:, d:2 * d]
    return (y * (1.0 + scale) + shift).astype(BF16)


def _inproj_kernel(*refs, d, n_chunk, chunk_rows, n_prompt_blocks, n_x):
    if n_x == 2:
        x = jnp.where(pl.program_id(0) < n_prompt_blocks, refs[0][...], refs[1][...])
    else:
        x = refs[0][...]
    mod_ref, g_ref = refs[n_x], refs[n_x + 1]
    rest = refs[n_x + 2:]
    u_scr = None
    if chunk_rows:
        rest, xc_ref, u_scr = rest[:-2], rest[-2], rest[-1]
    n_out = len(rest) // 2
    w_refs, o_refs = rest[:n_out], rest[n_out:]
    h = _modulated(x, mod_ref, g_ref, d)
    for k, (w_ref, o_ref) in enumerate(zip(w_refs, o_refs)):
        n = w_ref.shape[1]
        for c in range(0, n, n_chunk):
            e = min(c + n_chunk, n)
            r = jnp.dot(h, w_ref[:, c:e], preferred_element_type=F32)
            if len(o_ref.shape) == 3:
                for lb in range((e - c) // LANES):
                    part = r[:, lb * LANES:(lb + 1) * LANES]
                    o_ref[c // LANES + lb] = part.astype(o_ref.dtype)
                    if k == 0 and u_scr is not None:
                        u_scr[c // LANES + lb] = part
            else:
                o_ref[:, c:e] = r.astype(o_ref.dtype)
    if chunk_rows:
        _s5_to_chunks_kernel(u_scr, xc_ref, rows=chunk_rows)


def _inproj_call(x, mods_l, norm_g, weights, out_dtypes, *, n_prompt, sample_len, bm=512,
                 lane_blocked=(), s5_chunks=False):
    xs = list(x) if isinstance(x, tuple) else [x]
    n_tok = sum(a.shape[0] for a in xs)
    d = xs[0].shape[1]
    npb, bps = n_prompt // bm, sample_len // bm
    cond = functools.partial(_cond_of_block, n_prompt_blocks=npb, blocks_per_sample=bps)
    weights = [w if isinstance(w, tuple) else (w, 0, w.shape[1]) for w in weights]
    x_specs = _split_specs((bm, d), npb) if len(xs) == 2 else [pl.BlockSpec((bm, d), lambda i: (i, 0))]
    in_specs = x_specs + [
        pl.BlockSpec((None, 1, 3 * d), lambda i: (cond(i), 0, 0)),
        pl.BlockSpec((1, d), lambda i: (0, 0)),
    ] + [pl.BlockSpec((d, n), functools.partial(lambda i, blk: (0, blk), blk=blk)) for _, blk, n in weights]
    out_specs, out_shape = [], []
    for k, ((_, _, n), dt) in enumerate(zip(weights, out_dtypes)):
        if k in lane_blocked:
            out_specs.append(pl.BlockSpec((n // LANES, bm, LANES), lambda i: (0, i, 0)))
            out_shape.append(jax.ShapeDtypeStruct((n // LANES, n_tok, LANES), dt))
        else:
            out_specs.append(pl.BlockSpec((bm, n), lambda i: (i, 0)))
            out_shape.append(jax.ShapeDtypeStruct((n_tok, n), dt))
    chunk_rows = bm // S5_CHUNK if s5_chunks else 0
    scratch = []
    if s5_chunks:
        assert 0 in lane_blocked
        n_groups = weights[0][2] // S5_GROUP
        out_specs.append(pl.BlockSpec((n_groups, chunk_rows, 2 * LANES), lambda i: (0, i, 0)))
        out_shape.append(jax.ShapeDtypeStruct((n_groups, n_tok // S5_CHUNK, 2 * LANES), BF16))
        scratch = [pltpu.VMEM((weights[0][2] // LANES, bm, LANES), F32)]
    return pl.pallas_call(
        functools.partial(_inproj_kernel, d=d, n_chunk=512, chunk_rows=chunk_rows,
                          n_prompt_blocks=npb, n_x=len(xs)),
        out_shape=out_shape,
        grid=(n_tok // bm,),
        in_specs=in_specs,
        out_specs=out_specs,
        scratch_shapes=scratch,
        compiler_params=_cparams("arbitrary"),
        name="norm_mod_inproj",
    )(*xs, mods_l, norm_g.reshape(1, d), *[w for w, _, _ in weights])


def _outproj_kernel(*refs, d, final_norm, n_prompt_blocks, n_act, n_x):
    a_refs, x_refs = refs[:n_act], refs[n_act:n_act + n_x]
    mod_ref, w_ref, fg_ref = refs[n_act + n_x:n_act + n_x + 3]
    o_refs = refs[n_act + n_x + 3:]

    def finish(a_ref, x_ref, o_ref):
        y = jnp.dot(a_ref[...], w_ref[...], preferred_element_type=F32)
        gate = mod_ref[:, 2 * d:3 * d]
        xn = x_ref[...] + gate * y
        if final_norm:
            ms = jnp.mean(xn * xn, axis=-1, keepdims=True)
            xn = xn * lax.rsqrt(ms + NORM_EPS) * fg_ref[...]
        o_ref[...] = xn

    if max(n_act, n_x, len(o_refs)) == 1:
        finish(a_refs[0], x_refs[0], o_refs[0])
    else:
        is_prompt = pl.program_id(0) < n_prompt_blocks
        pl.when(is_prompt)(functools.partial(finish, a_refs[0], x_refs[0], o_refs[0]))
        pl.when(jnp.logical_not(is_prompt))(functools.partial(finish, a_refs[-1], x_refs[-1], o_refs[-1]))


def _split_specs(block, npb):
    return [pl.BlockSpec(block, lambda i: (jnp.minimum(i, npb - 1), 0)),
            pl.BlockSpec(block, lambda i: (jnp.maximum(i - npb, 0), 0))]


def _outproj_call(act, x, mods_l, w_out, final_g, *, n_prompt, sample_len, final_norm, bm=512,
                  split_out=False):
    acts = list(act) if isinstance(act, tuple) else [act]
    xs = list(x) if isinstance(x, tuple) else [x]
    n_tok = sum(a.shape[0] for a in xs)
    d, k = xs[0].shape[1], acts[0].shape[1]
    npb, bps = n_prompt // bm, sample_len // bm
    cond = functools.partial(_cond_of_block, n_prompt_blocks=npb, blocks_per_sample=bps)
    row = lambda i: (i, 0)
    act_specs = _split_specs((bm, k), npb) if len(acts) == 2 else [pl.BlockSpec((bm, k), row)]
    x_specs = _split_specs((bm, d), npb) if len(xs) == 2 else [pl.BlockSpec((bm, d), row)]
    if split_out:
        out_shape = [jax.ShapeDtypeStruct((n_prompt, d), F32), jax.ShapeDtypeStruct((n_tok - n_prompt, d), F32)]
        out_specs = _split_specs((bm, d), npb)
    else:
        out_shape = jax.ShapeDtypeStruct((n_tok, d), F32)
        out_specs = pl.BlockSpec((bm, d), row)
    return pl.pallas_call(
        functools.partial(_outproj_kernel, d=d, final_norm=final_norm, n_prompt_blocks=npb,
                          n_act=len(acts), n_x=len(xs)),
        out_shape=out_shape,
        grid=(n_tok // bm,),
        in_specs=act_specs + x_specs + [
            pl.BlockSpec((None, 1, 3 * d), lambda i: (cond(i), 0, 0)),
            pl.BlockSpec((k, d), lambda i: (0, 0)),
            pl.BlockSpec((1, d), lambda i: (0, 0)),
        ],
        out_specs=out_specs,
        compiler_params=_cparams("arbitrary"),
        name="outproj_residual",
    )(*acts, *xs, mods_l, w_out, final_g.reshape(1, d))


def _s5_time_of_lane_block():
    pos = np.arange(S5_CHUNK)
    half, blk = pos // 8, pos % 8
    g8 = np.arange(8)[:, None]
    return 8 * half[None, :] + (blk[None, :] - g8) % 8


def _s5_tables(lam_re, lam_im, log_step, b_re, b_im, c_re, c_im):
    t_chunk = S5_CHUNK
    n_groups, n_state = lam_re.shape[1], lam_re.shape[2]
    n_oct = n_groups // 8
    lam = lax.complex(lam_re.astype(F32), lam_im.astype(F32))
    step = jnp.exp(log_step.astype(F32))[..., None]
    lam_bar = jnp.exp(lam * step)
    b_bar = ((lam_bar - 1.0) / lam)[..., None] * lax.complex(b_re.astype(F32), b_im.astype(F32))
    c_mat = lax.complex(c_re.astype(F32), c_im.astype(F32))
    powers = [jnp.ones_like(lam_bar), lam_bar]
    for _ in range(t_chunk - 1):
        powers.append(powers[-1] * lam_bar)
    pw = jnp.stack(powers)

    zeros = jnp.zeros((t_chunk - 1, n_groups, n_state), pw.dtype)
    lag_f = jnp.concatenate([zeros, pw[:t_chunk, 0], zeros[:1]], axis=0)
    lag_b = jnp.concatenate([pw[t_chunk - 1::-1, 1], zeros, zeros[:1]], axis=0)
    plag = jnp.concatenate([lag_f, lag_b], axis=-1).transpose(1, 0, 2)
    plag = jnp.stack([plag.real, plag.imag])

    tl = _s5_time_of_lane_block()
    pw_ri = jnp.stack([pw.real, pw.imag]).reshape(2, t_chunk + 1, 2, n_oct, 8, n_state)
    m_idx = np.arange(t_chunk + 1)[None, None, :]

    def power_table(exponent, direction):
        sel = (exponent[:, :, None] == m_idx).astype(np.float32)
        tab = jnp.einsum('kxm,rmakp->rakxp', sel, pw_ri[:, :, direction], precision=HIGHEST)
        return tab.reshape(2, n_groups, t_chunk, n_state)

    def both(fwd, bwd):
        m = jnp.concatenate([fwd, bwd], axis=-1)
        return jnp.stack([m.real, m.imag])

    tin = jnp.concatenate([power_table(t_chunk - 1 - tl, 0), power_table(tl, 1)], axis=-1)
    tout = jnp.concatenate([power_table(tl + 1, 0), power_table(t_chunk - tl, 1)], axis=-1)
    bt = both(b_bar[0].transpose(0, 2, 1), b_bar[1].transpose(0, 2, 1))
    ct = both(c_mat[0], c_mat[1])
    lam_rows = both(pw[t_chunk, 0][:, None], pw[t_chunk, 1][:, None])[:, :, 0]
    return plag, tin, tout, bt, ct, lam_rows


def _s5_kmat_kernel(plag_ref, tin_ref, tout_ref, bt_ref, ct_ref, k_ref, pin_ref, pot_ref, x_scr, v_scr):
    masks = _lane_block_masks()
    n_lag = 2 * S5_CHUNK - 1
    for g8 in range(8):
        br, bi = bt_ref[0, g8], bt_ref[1, g8]
        cr, ci = ct_ref[0, g8], ct_ref[1, g8]
        def split(a):
            hi = a.astype(BF16)
            return hi, (a - hi.astype(F32)).astype(BF16)

        for m in range(n_lag):
            rows = slice(m * S5_GROUP, (m + 1) * S5_GROUP)
            pr, pi = plag_ref[0, g8, m:m + 1, :], plag_ref[1, g8, m:m + 1, :]
            for c0, part in ((0, cr * pr - ci * pi), (LANES, -(cr * pi + ci * pr))):
                x_scr[0, rows, c0:c0 + LANES], x_scr[1, rows, c0:c0 + LANES] = split(part)
        b_hi, b_lo = split(jnp.concatenate([jnp.concatenate([br, bi], axis=1)] * (LANES // S5_GROUP), axis=0))
        nt = functools.partial(lax.dot_general, dimension_numbers=(((1,), (1,)), ((), ())),
                               preferred_element_type=F32)
        v_scr[...] = nt(x_scr[0], b_hi) + nt(x_scr[0], b_lo) + nt(x_scr[1], b_hi)
        for pos in range(S5_CHUNK):
            rows = slice(pos * S5_GROUP, (pos + 1) * S5_GROUP)
            tr, ti = tin_ref[0, g8, pos:pos + 1, :], tin_ref[1, g8, pos:pos + 1, :]
            pin_ref[g8, rows, 0:LANES] = (tr * br - ti * bi).astype(BF16)
            pin_ref[g8, rows, LANES:2 * LANES] = (tr * bi + ti * br).astype(BF16)
            tr, ti = tout_ref[0, g8, pos:pos + 1, :], tout_ref[1, g8, pos:pos + 1, :]
            pot_ref[g8, rows, 0:LANES] = (tr * cr - ti * ci).astype(BF16)
            pot_ref[g8, rows, LANES:2 * LANES] = (-(tr * ci + ti * cr)).astype(BF16)
        for pos in range(S5_CHUNK):
            tau = 8 * (pos // 8) + (pos % 8 - g8) % 8
            rows = slice(pos * S5_GROUP, (pos + 1) * S5_GROUP)
            for half in range(2):
                acc = None
                for blk in range(8):
                    sigma = 8 * half + (blk - g8) % 8
                    m = S5_CHUNK - 1 - sigma + tau
                    src = v_scr[m * S5_GROUP:(m + 1) * S5_GROUP, :]
                    acc = src if acc is None else jnp.where(masks[blk], src, acc)
                k_ref[g8, rows, half * LANES:(half + 1) * LANES] = acc.astype(BF16)


def _s5_kmat_call(plag, tin, tout, bt, ct):
    n_groups = plag.shape[1]
    n_lag_rows = (2 * S5_CHUNK - 1) * S5_GROUP
    lag_spec = pl.BlockSpec((2, 8) + plag.shape[2:], lambda i: (0, i, 0, 0))
    tab_spec = pl.BlockSpec((2, 8, S5_GROUP, LANES), lambda i: (0, i, 0, 0))
    mat = jax.ShapeDtypeStruct((n_groups, 2 * LANES, 2 * LANES), BF16)
    mat_spec = pl.BlockSpec((8, 2 * LANES, 2 * LANES), lambda i: (i, 0, 0))
    return pl.pallas_call(
        _s5_kmat_kernel,
        out_shape=[mat, mat, mat],
        grid=(n_groups // 8,),
        in_specs=[lag_spec, tab_spec, tab_spec, tab_spec, tab_spec],
        out_specs=[mat_spec, mat_spec, mat_spec],
        scratch_shapes=[pltpu.VMEM((2, n_lag_rows, 2 * LANES), BF16), pltpu.VMEM((n_lag_rows, LANES), F32)],
        compiler_params=_cparams("arbitrary"),
        name="s5_kmat",
    )(plag, tin, tout, bt, ct)


def _lane_block_masks():
    blk = lax.broadcasted_iota(jnp.int32, (1, LANES), 1) // S5_GROUP
    return [blk == b for b in range(8)]


def _diagonal_merge(src):
    blk = lax.broadcasted_iota(jnp.int32, (1, LANES), 1) // S5_GROUP
    q = list(src)
    for bit in (1, 2, 4):
        take = (blk & bit) != 0
        q = [jnp.where(take, q[(x + bit) % 8], q[x]) for x in range(8)]
    return [q[(-t) % 8] for t in range(8)]


def _s5_to_chunks_kernel(u_ref, x_ref, *, rows):
    for o in range(u_ref.shape[0]):
        for r0 in range(0, rows, SUBLANES):
            for half in range(2):
                rolled = []
                for t8 in range(8):
                    v = u_ref[o, pl.ds(r0 * S5_CHUNK + 8 * half + t8, SUBLANES, stride=S5_CHUNK), :]
                    rolled.append(pltpu.roll(v, t8 * S5_GROUP, 1) if t8 else v)
                for g8, merged in enumerate(_diagonal_merge(rolled)):
                    x_ref[o * 8 + g8, r0:r0 + SUBLANES, half * LANES:(half + 1) * LANES] = merged.astype(BF16)


def _s5_tail_kernel(yc_ref, u_ref, d_ref, z_ref, w_ref, b_ref, o_ref, nat_scr, y_scr, *, rows, n_chunk):
    n_blk = u_ref.shape[0]
    tile = 2 * SUBLANES
    per = n_chunk // LANES
    for piece, r0 in enumerate(range(0, rows, tile)):
        tok = slice(r0 * S5_CHUNK, (r0 + tile) * S5_CHUNK)
        for o in range(n_blk):
            for half in range(2):
                src = [yc_ref[o * 8 + g8, r0:r0 + tile, half * LANES:(half + 1) * LANES] for g8 in range(8)]
                for t8, merged in enumerate(_diagonal_merge(src)):
                    nat = pltpu.roll(merged, (8 - t8) * S5_GROUP, 1) if t8 else merged
                    nat_scr[piece % 2, o, pl.ds(8 * half + t8, tile, stride=S5_CHUNK), :] = nat
            d_vec = d_ref[:, o * LANES:(o + 1) * LANES]
            y_scr[o, tok] = _gelu_tanh(nat_scr[piece % 2, o] + d_vec * u_ref[o, tok])
        yb = jnp.concatenate([y_scr[o, tok].astype(BF16) for o in range(n_blk)], axis=1)
        for c in range(0, n_blk, per):
            sl = slice(c * LANES, (c + per) * LANES)
            gate = _sigmoid(jnp.dot(yb, w_ref[:, sl], preferred_element_type=F32) + b_ref[:, sl])
            y = jnp.concatenate([y_scr[c + k, tok] for k in range(per)], axis=1)
            o_ref[tok, sl] = (y * gate * _silu(z_ref[tok, sl].astype(F32))).astype(o_ref.dtype)


def _s5_tail_call(yc, u3, d_skip, z, glu_w, glu_b, *, rows=32):
    n_blk, n_tok, _ = u3.shape
    n_groups, n_rows, _ = yc.shape
    width = n_blk * LANES
    bm = rows * S5_CHUNK
    tile_tok = 2 * SUBLANES * S5_CHUNK
    fix = lambda i: (0, 0)
    return pl.pallas_call(
        functools.partial(_s5_tail_kernel, rows=rows, n_chunk=min(512, width)),
        out_shape=jax.ShapeDtypeStruct((n_tok, width), BF16),
        grid=(n_rows // rows,),
        in_specs=[pl.BlockSpec((n_groups, rows, 2 * LANES), lambda i: (0, i, 0)),
                  pl.BlockSpec((n_blk, bm, LANES), lambda i: (0, i, 0)),
                  pl.BlockSpec((1, width), fix),
                  pl.BlockSpec((bm, width), lambda i: (i, 0)),
                  pl.BlockSpec((width, width), fix),
                  pl.BlockSpec((1, width), fix)],
        out_specs=pl.BlockSpec((bm, width), lambda i: (i, 0)),
        scratch_shapes=[pltpu.VMEM((2, n_blk, tile_tok, LANES), F32),
                        pltpu.VMEM((n_blk, bm, LANES), F32)],
        compiler_params=_cparams("arbitrary"),
        name="s5_tail",
    )(yc, u3, d_skip.reshape(1, width).astype(F32), z, glu_w, glu_b.reshape(1, width).astype(F32))


def _s5_chunk_kernel(x_ref, kt_ref, pin_ref, po_ref, lam_ref, h0r_ref, h0i_ref,
                     y_ref, fr_ref, fi_ref, r_scr, st_scr, *, segments, seq_block):
    gb = SUBLANES
    lane = lax.broadcasted_iota(jnp.int32, (1, LANES), 1)
    fwd_lanes = lane < (LANES // 2)
    seg_rows = [(row0, n_seq * n_chunks) for row0, n_seq, n_chunks, _, _ in segments]
    for row0, n_rows in seg_rows:
        for g in range(gb):
            r = jnp.dot(x_ref[g, row0:row0 + n_rows, :], pin_ref[g], preferred_element_type=F32)
            of_group = pl.ds(row0 * gb + g, n_rows, stride=gb)
            r_scr[0, of_group, :] = r[:, 0:LANES]
            r_scr[1, of_group, :] = r[:, LANES:2 * LANES]
    ar, ai = lam_ref[0], lam_ref[1]
    for row0, n_seq, n_chunks, from_input, to_output in segments:
        for b0 in range(0, n_seq, seq_block):
            nb = min(seq_block, n_seq - b0)

            def step(i, carry, row0=row0, n_chunks=n_chunks, b0=b0, nb=nb):
                out = []
                for k in range(nb):
                    base = row0 + (b0 + k) * n_chunks
                    at_f = pl.ds((base + i) * gb, gb)
                    at_b = pl.ds((base + (n_chunks - 1) - i) * gb, gb)
                    s_re, s_im = carry[k]
                    half = LANES // 2
                    st_scr[0, at_f, 0:half] = s_re[:, 0:half]
                    st_scr[0, at_b, half:LANES] = s_re[:, half:LANES]
                    st_scr[1, at_f, 0:half] = s_im[:, 0:half]
                    st_scr[1, at_b, half:LANES] = s_im[:, half:LANES]
                    v_re = jnp.where(fwd_lanes, r_scr[0, at_f, :], r_scr[0, at_b, :])
                    v_im = jnp.where(fwd_lanes, r_scr[1, at_f, :], r_scr[1, at_b, :])
                    out.append((ar * s_re - ai * s_im + v_re, ar * s_im + ai * s_re + v_im))
                return tuple(out)

            if from_input:
                init = tuple((h0r_ref[b0 + k], h0i_ref[b0 + k]) for k in range(nb))
            else:
                init = tuple((jnp.zeros((gb, LANES), F32),) * 2 for _ in range(nb))
            fin = init
            for i in range(n_chunks):
                fin = step(i, fin)
            if to_output:
                for k in range(nb):
                    fr_ref[b0 + k] = fin[k][0]
                    fi_ref[b0 + k] = fin[k][1]
    for row0, n_rows in seg_rows:
        for g in range(gb):
            of_group = pl.ds(row0 * gb + g, n_rows, stride=gb)
            st = jnp.concatenate([st_scr[cb, of_group, :] for cb in range(2)], axis=1).astype(BF16)
            rows = slice(row0, row0 + n_rows)
            nt = functools.partial(lax.dot_general, dimension_numbers=(((1,), (1,)), ((), ())),
                                   preferred_element_type=F32)
            y_ref[g, rows, :] = (nt(x_ref[g, rows, :], kt_ref[g]) + nt(st, po_ref[g])).astype(y_ref.dtype)


def _s5_chunk_call(xc, kt, pin, pout, lam_rows, h0_re, h0_im, *, layer, segments, n_final):
    n_groups, rows, _ = xc.shape
    gb = SUBLANES
    s_in = h0_re.shape[0]
    kern = functools.partial(_s5_chunk_kernel, segments=segments, seq_block=8)
    g3 = lambda i: (i, 0, 0)
    blk0 = layer * (n_groups // gb)
    p3 = lambda i: (i + blk0, 0, 0)
    mid = lambda i: (0, i, 0)
    return pl.pallas_call(
        kern,
        out_shape=[jax.ShapeDtypeStruct((n_groups, rows, 2 * LANES), F32),
                   jax.ShapeDtypeStruct((n_final, n_groups, LANES), F32),
                   jax.ShapeDtypeStruct((n_final, n_groups, LANES), F32)],
        grid=(n_groups // gb,),
        in_specs=[
            pl.BlockSpec((gb, rows, 2 * LANES), g3),
            pl.BlockSpec((gb, 2 * LANES, 2 * LANES), p3),
            pl.BlockSpec((gb, 2 * LANES, 2 * LANES), p3),
            pl.BlockSpec((gb, 2 * LANES, 2 * LANES), p3),
            pl.BlockSpec((2, gb, LANES), lambda i: (0, i + blk0, 0)),
            pl.BlockSpec((s_in, gb, LANES), mid),
            pl.BlockSpec((s_in, gb, LANES), mid),
        ],
        out_specs=[pl.BlockSpec((gb, rows, 2 * LANES), g3),
                   pl.BlockSpec((n_final, gb, LANES), mid),
                   pl.BlockSpec((n_final, gb, LANES), mid)],
        scratch_shapes=[pltpu.VMEM((2, rows * gb, LANES), F32),
                        pltpu.VMEM((2, rows * gb, LANES), F32)],
        compiler_params=_cparams("arbitrary"),
        name="s5_chunk_scan",
    )(xc, kt, pin, pout, lam_rows, h0_re, h0_im)


def _s5_prep_all(lam_re, lam_im, log_step, b_re, b_im, c_re, c_im):
    tabs = jax.vmap(_s5_tables)(lam_re, lam_im, log_step, b_re, b_im, c_re, c_im)
    plag, tin, tout, bt, ct, lam_rows = [jnp.moveaxis(t, 0, 1).reshape((2, -1) + t.shape[3:]) for t in tabs]
    kt, pin, pot = _s5_kmat_call(plag, tin, tout, bt, ct)
    return kt, pin, pot, lam_rows


def _s5_mix(u3, xc, z, layer, mats, d_skip, glu_w, glu_b, st_re, st_im, *, n_prompt_seq, prompt_len,
            n_sample_seq, sample_len, bm):
    kt, pin, pout, lam_rows = mats
    n_state = LANES // 2
    pc, sc = prompt_len // S5_CHUNK, sample_len // S5_CHUNK

    def state_rows(s):
        return jnp.concatenate([s[:, 0], s[:, 1]], axis=-1).astype(F32)

    segments = ((0, n_prompt_seq, pc, False, True), (n_prompt_seq * pc, n_sample_seq, sc, True, False))
    yc, fr, fi = _s5_chunk_call(xc, kt, pin, pout, lam_rows, state_rows(st_re), state_rows(st_im),
                                layer=layer, segments=segments, n_final=n_prompt_seq)
    act = _s5_tail_call(yc, u3, d_skip, z, glu_w, glu_b, rows=bm // S5_CHUNK)

    def unpack(f):
        return jnp.stack([f[:, :, :n_state], f[:, :, n_state:]], axis=1)

    return act, unpack(fr), unpack(fi)


POOL_TILE = 2 * LANES
POOL_EDGE = 16


def _pool_band_matrices():
    t = np.arange(POOL_TILE)[:, None]
    s = np.arange(POOL_TILE)[None, :]
    mats = []
    for win in POOL_WINDOWS:
        lo = win // 2
        inside = lambda src: ((src >= t - lo) & (src <= t + lo - 1)).astype(np.float32)
        mats.append(np.stack([inside(s), inside(s - POOL_TILE), inside(s + POOL_TILE)]))
    return np.stack(mats)


def _pool_kernel(u_ref, z_ref, w_ref, s_ref, a_ref, o_ref, sum_scr, *, n_prompt_blocks, prompt_len,
                 sample_len):
    rows = u_ref.shape[0]
    n_tiles = rows // POOL_TILE
    is_prompt = pl.program_id(0) < n_prompt_blocks
    seq_len = jnp.where(is_prompt, prompt_len, sample_len)
    lo = jnp.left_shift(1, pl.program_id(1))
    t = lax.broadcasted_iota(jnp.int32, (rows, 1), 0) & (seq_len - 1)
    u = u_ref[...]
    u_hi = u.astype(BF16)
    u_lo = (u - u_hi.astype(F32)).astype(BF16)

    def band(a, tile):
        sl = slice(tile * POOL_TILE, (tile + 1) * POOL_TILE)
        return (jnp.dot(a, u_hi[sl], preferred_element_type=F32)
                + jnp.dot(a, u_lo[sl], preferred_element_type=F32))

    for r in range(n_tiles):
        sum_scr[r * POOL_TILE:(r + 1) * POOL_TILE, :] = band(a_ref[0], r)

    @pl.when(jnp.logical_not(is_prompt))
    def _():
        for r in range(n_tiles):
            if r > 0:
                top = slice(r * POOL_TILE, r * POOL_TILE + POOL_EDGE)
                sum_scr[top, :] = sum_scr[top, :] + band(a_ref[1, 0:POOL_EDGE, :], r - 1)
            if r < n_tiles - 1:
                bot = slice((r + 1) * POOL_TILE - POOL_EDGE, (r + 1) * POOL_TILE)
                sum_scr[bot, :] = sum_scr[bot, :] + band(a_ref[2, POOL_TILE - POOL_EDGE:POOL_TILE, :], r + 1)

    cnt = jnp.minimum(t + lo, seq_len) - jnp.maximum(t - lo, 0)
    p = sum_scr[...] / cnt.astype(F32) - u
    m = jnp.dot(p.astype(BF16), w_ref[...], preferred_element_type=F32) * s_ref[...]
    o_ref[...] = (m * _silu(z_ref[...].astype(F32))).astype(o_ref.dtype)


def _pool_call(u, z, pool_w, pool_scale, *, n_prompt, prompt_len, sample_len, rows=2048):
    n_tok, width = u.shape
    n_groups = len(POOL_WINDOWS)
    cg = width // n_groups
    assert prompt_len & (prompt_len - 1) == 0 and sample_len & (sample_len - 1) == 0
    assert rows % prompt_len == 0 and n_prompt % rows == 0
    assert prompt_len == POOL_TILE and sample_len == rows and max(POOL_WINDOWS) // 2 <= POOL_EDGE
    assert POOL_WINDOWS == tuple(2 << g for g in range(n_groups))
    kern = functools.partial(_pool_kernel, n_prompt_blocks=n_prompt // rows, prompt_len=prompt_len,
                             sample_len=sample_len)
    bands = jnp.asarray(_pool_band_matrices(), BF16)
    return pl.pallas_call(
        kern,
        out_shape=jax.ShapeDtypeStruct((n_tok, width), BF16),
        grid=(n_tok // rows, n_groups),
        in_specs=[
            pl.BlockSpec((rows, cg), lambda i, g: (i, g)),
            pl.BlockSpec((rows, cg), lambda i, g: (i, g)),
            pl.BlockSpec((None, cg, cg), lambda i, g: (g, 0, 0)),
            pl.BlockSpec((1, cg), lambda i, g: (0, g)),
            pl.BlockSpec((None, 3, POOL_TILE, POOL_TILE), lambda i, g: (g, 0, 0, 0)),
        ],
        out_specs=pl.BlockSpec((rows, cg), lambda i, g: (i, g)),
        scratch_shapes=[pltpu.VMEM((rows, cg), F32)],
        compiler_params=_cparams("arbitrary", "arbitrary"),
        name="pool_mix",
    )(u, z, pool_w, pool_scale.reshape(1, width).astype(F32), bands)


MLA_QW = 2 * LANES

_ROT_SRC = np.concatenate([np.arange(16, 32), np.arange(0, 16), np.arange(48, 64), np.arange(32, 48)])
_ROT_SIGN = np.concatenate([-np.ones(16), np.ones(16), -np.ones(16), np.ones(16)]).astype(np.float32)


def _rope_tables(sample_len):
    half = MLA_ROPE // 4
    tok = jnp.arange(sample_len)
    row = (tok // GRID_W).astype(F32)
    col = (tok % GRID_W).astype(F32)
    inv = ROPE_THETA ** (-jnp.arange(half, dtype=F32) / half)
    a_row, a_col = row[:, None] * inv, col[:, None] * inv
    cos = jnp.concatenate([jnp.cos(a_row), jnp.cos(a_row), jnp.cos(a_col), jnp.cos(a_col)], axis=-1)
    sin = jnp.concatenate([jnp.sin(a_row), jnp.sin(a_row), jnp.sin(a_col), jnp.sin(a_col)], axis=-1)
    pad = jnp.zeros((sample_len, LANES - MLA_ROPE), F32)
    return jnp.concatenate([cos, pad], axis=-1), jnp.concatenate([sin, pad], axis=-1)


def _rms(x, g):
    return x * lax.rsqrt(jnp.mean(x * x, axis=-1, keepdims=True) + NORM_EPS) * g


def _mla_post_kernel(sm_ref, cos_ref, sin_ref, qn_ref, kn_ref, wa_ref, wb_ref,
                     q_ref, ckv_ref, kpe_ref, *, q_rank, kv_rank, heads_per_dot, n_prompt_blocks):
    is_prompt = pl.program_id(0) < n_prompt_blocks
    rope_lanes = (lax.broadcasted_iota(jnp.int32, (1, LANES), 1) < MLA_ROPE).astype(F32)
    cosp = jnp.where(is_prompt, rope_lanes, cos_ref[...])
    sinp = jnp.where(is_prompt, 0.0, sin_ref[...])
    qn = _rms(sm_ref[:, 0:q_rank], qn_ref[...]).astype(BF16)
    for h0 in range(0, MLA_HEADS, heads_per_dot):
        a = jnp.dot(qn, wa_ref[:, h0 * MLA_QW:(h0 + heads_per_dot) * MLA_QW], preferred_element_type=F32)
        b = jnp.dot(qn, wb_ref[:, h0 * LANES:(h0 + heads_per_dot) * LANES], preferred_element_type=F32)
        for j in range(heads_per_dot):
            h = h0 + j
            q_ref[:, h * MLA_QW:h * MLA_QW + LANES] = a[:, j * MLA_QW:j * MLA_QW + LANES].astype(BF16)
            pe = a[:, j * MLA_QW + LANES:(j + 1) * MLA_QW] * cosp + b[:, j * LANES:(j + 1) * LANES] * sinp
            q_ref[:, h * MLA_QW + LANES:(h + 1) * MLA_QW] = pe.astype(BF16)
    c0 = q_rank
    ckv_ref[...] = _rms(sm_ref[:, c0:c0 + kv_rank], kn_ref[...])
    k0 = c0 + kv_rank
    kpe_ref[...] = (sm_ref[:, k0:k0 + LANES] * cosp + sm_ref[:, k0 + LANES:k0 + 2 * LANES] * sinp).astype(BF16)


def _mla_post_call(small, cos_t, sin_t, q_norm, kv_norm, wq_a, wq_b, *, n_prompt, bm=512):
    n_tok, ws = small.shape
    q_rank, kv_rank = q_norm.shape[-1], kv_norm.shape[-1]
    npb, bps = n_prompt // bm, cos_t.shape[0] // bm
    row = lambda i: (i, 0)
    fix = lambda i: (0, 0)
    pos = lambda i: (jnp.where(i < npb, 0, (i - npb) % bps), 0)
    kern = functools.partial(_mla_post_kernel, q_rank=q_rank, kv_rank=kv_rank, heads_per_dot=4,
                             n_prompt_blocks=npb)
    return pl.pallas_call(
        kern,
        out_shape=[jax.ShapeDtypeStruct((n_tok, MLA_HEADS * MLA_QW), BF16),
                   jax.ShapeDtypeStruct((n_tok, kv_rank), F32),
                   jax.ShapeDtypeStruct((n_tok, LANES), BF16)],
        grid=(n_tok // bm,),
        in_specs=[
            pl.BlockSpec((bm, ws), row),
            pl.BlockSpec((bm, LANES), pos),
            pl.BlockSpec((bm, LANES), pos),
            pl.BlockSpec((1, q_rank), fix),
            pl.BlockSpec((1, kv_rank), fix),
            pl.BlockSpec(wq_a.shape, fix),
            pl.BlockSpec(wq_b.shape, fix),
        ],
        out_specs=[pl.BlockSpec((bm, MLA_HEADS * MLA_QW), row),
                   pl.BlockSpec((bm, kv_rank), row),
                   pl.BlockSpec((bm, LANES), row)],
        compiler_params=_cparams("arbitrary"),
        name="mla_q_rope",
    )(small, cos_t, sin_t, q_norm.reshape(1, q_rank).astype(F32), kv_norm.reshape(1, kv_rank).astype(F32),
      wq_a, wq_b)


def _attn_kernel(q_ref, ckv_ref, wkv_ref, kpe_ref, z_ref, o_ref, kcat_scr, vext_scr, *, hg, scale):
    c2 = scale * math.log2(math.e)

    @pl.when(pl.program_id(2) == 0)
    def _():
        ckv = ckv_ref[...].astype(BF16)
        ones = jnp.ones((ckv.shape[0], LANES), BF16)
        for j in range(hg):
            kv = jnp.dot(ckv, wkv_ref[:, j * 2 * LANES:(j + 1) * 2 * LANES],
                         preferred_element_type=F32).astype(BF16)
            kcat_scr[j, :, 0:LANES] = kv[:, 0:LANES]
            kcat_scr[j, :, LANES:2 * LANES] = kpe_ref[...]
            vext_scr[j, :, 0:LANES] = kv[:, LANES:2 * LANES]
            vext_scr[j, :, LANES:2 * LANES] = ones

    scores = [lax.dot_general(q_ref[:, j * MLA_QW:(j + 1) * MLA_QW], kcat_scr[j], (((1,), (1,)), ((), ())),
                              preferred_element_type=F32) for j in range(hg)]
    probs = [jnp.exp2((s - jnp.max(s, axis=-1, keepdims=True)) * c2).astype(BF16) for s in scores]
    for j in range(hg):
        pv = jnp.dot(probs[j], vext_scr[j], preferred_element_type=F32)
        zs = slice(j * MLA_V, (j + 1) * MLA_V)
        o = pv[:, 0:MLA_V] / pv[:, MLA_V:2 * MLA_V]
        o_ref[:, zs] = (o * _silu(z_ref[:, zs].astype(F32))).astype(o_ref.dtype)


def _attn_call(q, ckv, wkv_b, kpe, z, *, q_row0, n_seq, q_len, k_len, hg, qb):
    width = MLA_HEADS * MLA_V
    nqb = q_len // qb
    qb0 = q_row0 // qb
    assert q_row0 % qb == 0 and q_len % qb == 0
    scale = float((MLA_NOPE + MLA_ROPE) ** -0.5)
    qrow = lambda b, g, i: (qb0 + b * nqb + i, g)
    return pl.pallas_call(
        functools.partial(_attn_kernel, hg=hg, scale=scale),
        out_shape=jax.ShapeDtypeStruct((n_seq * q_len, width), BF16),
        grid=(n_seq, MLA_HEADS // hg, nqb),
        in_specs=[
            pl.BlockSpec((qb, hg * MLA_QW), qrow),
            pl.BlockSpec((k_len, ckv.shape[1]), lambda b, g, i: (b, 0)),
            pl.BlockSpec((wkv_b.shape[0], hg * 2 * LANES), lambda b, g, i: (0, g)),
            pl.BlockSpec((k_len, LANES), lambda b, g, i: (b, 0)),
            pl.BlockSpec((qb, hg * MLA_V), qrow),
        ],
        out_specs=pl.BlockSpec((qb, hg * MLA_V), lambda b, g, i: (b * nqb + i, g)),
        scratch_shapes=[pltpu.VMEM((hg, k_len, MLA_QW), BF16),
                        pltpu.VMEM((hg, k_len, 2 * MLA_V), BF16)],
        compiler_params=_cparams("arbitrary", "arbitrary", "arbitrary"),
        name="mla_attention",
    )(q, ckv, wkv_b, kpe, z)


def _mla_weights(w_in, wq_b):
    q_rank = wq_b.shape[0]
    kv_rank = w_in.shape[1] - q_rank - MLA_ROPE - MLA_HEADS * MLA_V
    d = w_in.shape[0]
    c_kpe = q_rank + kv_rank
    zpad = jnp.zeros((d, LANES - MLA_ROPE), w_in.dtype)
    kpe_w = w_in[:, c_kpe:c_kpe + MLA_ROPE]
    w_small = jnp.concatenate([w_in[:, :c_kpe], kpe_w, zpad,
                               kpe_w[:, _ROT_SRC] * _ROT_SIGN, zpad], axis=1)
    w_z = w_in[:, c_kpe + MLA_ROPE:]
    hd = MLA_NOPE + MLA_ROPE
    wq3 = wq_b.reshape(q_rank, MLA_HEADS, hd)
    pe = wq3[:, :, MLA_NOPE:]
    z3 = jnp.zeros((q_rank, MLA_HEADS, LANES - MLA_ROPE), wq_b.dtype)
    wq_a = jnp.concatenate([wq3, z3], axis=-1).reshape(q_rank, MLA_HEADS * MLA_QW)
    wq_r = jnp.concatenate([pe[:, :, _ROT_SRC] * _ROT_SIGN, z3], axis=-1).reshape(q_rank, MLA_HEADS * LANES)
    return w_small.astype(BF16), w_z.astype(BF16), wq_a.astype(BF16), wq_r.astype(BF16)


def kernel(x_prompt, x_sample, state_s5_re, state_s5_im, cache_ckv, cache_kpe, c, c_ctx, norm_g, ada_w, ada_b, final_norm_g, s5_w_in, s5_lam_re, s5_lam_im, s5_log_step, s5_b_re, s5_b_im, s5_c_re, s5_c_im, s5_d, s5_glu_w, s5_glu_b, s5_w_out, pool_w_in, pool_w, pool_scale, pool_w_out, mla_w_in, mla_q_norm, mla_wq_b, mla_kv_norm, mla_wkv_b, mla_w_out):
    n_pseq, p_len, d = x_prompt.shape
    n_sseq, s_len, _ = x_sample.shape
    depth = norm_g.shape[0]
    n_prompt = n_pseq * p_len
    bm = 512
    geo = dict(n_prompt=n_prompt, sample_len=s_len, bm=bm)

    x = (x_prompt.reshape(n_prompt, d), x_sample.reshape(n_sseq * s_len, d))
    conds = jnp.concatenate([c_ctx[None, :], c, jnp.zeros((SUBLANES - 1 - n_sseq, d), F32)], axis=0)
    mods = _ada_call(conds.astype(F32), ada_w, ada_b)
    mods = mods.reshape(depth, SUBLANES, 1, 3 * d)
    s5_mats = _s5_prep_all(s5_lam_re, s5_lam_im, s5_log_step, s5_b_re, s5_b_im, s5_c_re, s5_c_im)

    new_re, new_im, new_ckv, new_kpe = [], [], [], []
    for layer in range(depth):
        kind, j = layer % N_MIXERS, layer // N_MIXERS
        last = layer == depth - 1
        ml = mods[layer]
        if kind == 0:
            width = s5_w_in.shape[2] // 2
            w = s5_w_in[j].astype(BF16)
            u3, z, xc = _inproj_call(x, ml, norm_g[layer], [(w, 0, width), (w, 1, width)], [F32, BF16],
                                     lane_blocked=(0,), s5_chunks=True, **geo)
            act, f_re, f_im = _s5_mix(u3, xc, z, j, s5_mats, s5_d[j], s5_glu_w[j].astype(BF16), s5_glu_b[j],
                                      state_s5_re[:, j], state_s5_im[:, j], n_prompt_seq=n_pseq,
                                      prompt_len=p_len, n_sample_seq=n_sseq, sample_len=s_len, bm=bm)
            new_re.append(f_re)
            new_im.append(f_im)
            w_out = s5_w_out[j]
        elif kind == 1:
            width = pool_w_in.shape[2] // 2
            w = pool_w_in[j].astype(BF16)
            u, z = _inproj_call(x, ml, norm_g[layer], [(w, 0, width), (w, 1, width)], [F32, BF16], **geo)
            act = _pool_call(u, z, pool_w[j].astype(BF16), pool_scale[j], n_prompt=n_prompt,
                             prompt_len=p_len, sample_len=s_len)
            w_out = pool_w_out[j]
        else:
            q_rank, kv_rank = mla_q_norm.shape[-1], mla_kv_norm.shape[-1]
            w_small, w_z, wq_a, wq_r = _mla_weights(mla_w_in[j], mla_wq_b[j])
            small, z = _inproj_call(x, ml, norm_g[layer], [w_small, w_z], [F32, BF16], **geo)
            cos_t, sin_t = _rope_tables(s_len)
            q, ckv_n, kpe_k = _mla_post_call(small, cos_t, sin_t, mla_q_norm[j], mla_kv_norm[j], wq_a, wq_r,
                                             n_prompt=n_prompt, bm=bm)
            wkv = mla_wkv_b[j].astype(BF16)
            past = cache_ckv.shape[2]
            k_len = past + s_len
            ckv_s = jnp.concatenate([cache_ckv[:, j].astype(F32), ckv_n[n_prompt:].reshape(n_sseq, s_len, kv_rank)],
                                    axis=1).reshape(n_sseq * k_len, kv_rank)
            kpe_cache = jnp.concatenate([cache_kpe[:, j].astype(BF16),
                                         jnp.zeros((n_sseq, past, LANES - MLA_ROPE), BF16)], axis=-1)
            kpe_s = jnp.concatenate([kpe_cache, kpe_k[n_prompt:].reshape(n_sseq, s_len, LANES)],
                                    axis=1).reshape(n_sseq * k_len, LANES)
            act = (_attn_call(q, ckv_n, wkv, kpe_k, z, q_row0=0, n_seq=n_pseq, q_len=p_len,
                              k_len=p_len, hg=MLA_HEADS, qb=p_len),
                   _attn_call(q, ckv_s, wkv, kpe_s, z, q_row0=n_prompt, n_seq=n_sseq, q_len=s_len,
                              k_len=k_len, hg=4, qb=256))
            new_ckv.append(ckv_n[:n_prompt].reshape(n_pseq, p_len, kv_rank))
            c_kpe = q_rank + kv_rank
            new_kpe.append(small[:n_prompt, c_kpe:c_kpe + MLA_ROPE].reshape(n_pseq, p_len, MLA_ROPE))
            w_out = mla_w_out[j]
        x = _outproj_call(act, x, ml, w_out.astype(BF16), final_norm_g, final_norm=last, split_out=last,
                          n_prompt=n_prompt, sample_len=s_len, bm=2 * bm)

    y_prompt = x[0].reshape(n_pseq, p_len, d)
    y_sample = x[1].reshape(n_sseq, s_len, d)
    return (y_prompt, y_sample, jnp.stack(new_re, axis=1), jnp.stack(new_im, axis=1),
            jnp.stack(new_ckv, axis=1), jnp.stack(new_kpe, axis=1))
```

```python
import functools
import math

import jax
import jax.numpy as jnp
import numpy as np
from jax import lax
from jax.experimental import pallas as pl
from jax.experimental.pallas import tpu as pltpu

S5_GROUP = 16
S5_CHUNK = 16
POOL_WINDOWS = (2, 4, 8, 16)
MLA_HEADS = 16
MLA_NOPE = 128
MLA_ROPE = 64
MLA_V = 128
GRID_W = 64
ROPE_THETA = 10000.0
NORM_EPS = 1e-6
N_MIXERS = 3

LANES = 128
SUBLANES = 8
VMEM_LIMIT_BYTES = 56 * 1024 * 1024

F32 = jnp.float32
BF16 = jnp.bfloat16
HIGHEST = lax.Precision.HIGHEST


def _cparams(*sem):
    return pltpu.CompilerParams(dimension_semantics=sem, vmem_limit_bytes=VMEM_LIMIT_BYTES)


def _sigmoid(x):
    return 0.5 + 0.5 * jnp.tanh(0.5 * x)


def _silu(x):
    h = 0.5 * x
    return h + h * jnp.tanh(h)


def _gelu_tanh(x):
    c = math.sqrt(2.0 / math.pi)
    hx = 0.5 * x
    return hx + hx * jnp.tanh(x * (c + (c * 0.044715) * (x * x)))


def _ada_kernel(c_ref, w_ref, b_ref, o_ref):
    a = _silu(c_ref[...])
    o_ref[...] = jnp.dot(a, w_ref[...], preferred_element_type=F32, precision=HIGHEST) + b_ref[...]


def _ada_call(conds, ada_w, ada_b):
    depth, d, d3 = ada_w.shape
    c8 = conds.shape[0]
    tn = d3 // 2
    return pl.pallas_call(
        _ada_kernel,
        out_shape=jax.ShapeDtypeStruct((depth, c8, d3), F32),
        grid=(depth, d3 // tn),
        in_specs=[
            pl.BlockSpec((c8, d), lambda l, n: (0, 0)),
            pl.BlockSpec((None, d, tn), lambda l, n: (l, 0, n)),
            pl.BlockSpec((None, 1, tn), lambda l, n: (l, 0, n)),
        ],
        out_specs=pl.BlockSpec((None, c8, tn), lambda l, n: (l, 0, n)),
        compiler_params=_cparams("arbitrary", "arbitrary"),
        name="ada_mod",
    )(conds, ada_w, ada_b.reshape(depth, 1, d3))


def _cond_of_block(i, n_prompt_blocks, blocks_per_sample):
    return jnp.where(i < n_prompt_blocks, 0, 1 + (i - n_prompt_blocks) // blocks_per_sample)


def _modulated(x, mod_ref, g_ref, d):
    ms = jnp.mean(x * x, axis=-1, keepdims=True)
    y = x * lax.rsqrt(ms + NORM_EPS) * g_ref[...]
    shift = mod_ref[:, 0:d]
    scale = mod_ref[:, d:2 * d]
    return (y * (1.0 + scale) + shift).astype(BF16)


def _inproj_kernel(*refs, d, n_chunk, chunk_rows, n_prompt_blocks, n_x):
    if n_x == 2:
        x = jnp.where(pl.program_id(0) < n_prompt_blocks, refs[0][...], refs[1][...])
    else:
        x = refs[0][...]
    mod_ref, g_ref = refs[n_x], refs[n_x + 1]
    rest = refs[n_x + 2:]
    u_scr = None
    if chunk_rows:
        rest, xc_ref, u_scr = rest[:-2], rest[-2], rest[-1]
    n_out = len(rest) // 2
    w_refs, o_refs = rest[:n_out], rest[n_out:]
    h = _modulated(x, mod_ref, g_ref, d)
    for k, (w_ref, o_ref) in enumerate(zip(w_refs, o_refs)):
        n = w_ref.shape[1]
        for c in range(0, n, n_chunk):
            e = min(c + n_chunk, n)
            r = jnp.dot(h, w_ref[:, c:e], preferred_element_type=F32)
            if len(o_ref.shape) == 3:
                for lb in range((e - c) // LANES):
                    part = r[:, lb * LANES:(lb + 1) * LANES]
                    o_ref[c // LANES + lb] = part.astype(o_ref.dtype)
                    if k == 0 and u_scr is not None:
                        u_scr[c // LANES + lb] = part
            else:
                o_ref[:, c:e] = r.astype(o_ref.dtype)
    if chunk_rows:
        _s5_to_chunks_kernel(u_scr, xc_ref, rows=chunk_rows)


def _inproj_call(x, mods_l, norm_g, weights, out_dtypes, *, n_prompt, sample_len, bm=512,
                 lane_blocked=(), s5_chunks=False):
    xs = list(x) if isinstance(x, tuple) else [x]
    n_tok = sum(a.shape[0] for a in xs)
    d = xs[0].shape[1]
    npb, bps = n_prompt // bm, sample_len // bm
    cond = functools.partial(_cond_of_block, n_prompt_blocks=npb, blocks_per_sample=bps)
    weights = [w if isinstance(w, tuple) else (w, 0, w.shape[1]) for w in weights]
    x_specs = _split_specs((bm, d), npb) if len(xs) == 2 else [pl.BlockSpec((bm, d), lambda i: (i, 0))]
    in_specs = x_specs + [
        pl.BlockSpec((None, 1, 3 * d), lambda i: (cond(i), 0, 0)),
        pl.BlockSpec((1, d), lambda i: (0, 0)),
    ] + [pl.BlockSpec((d, n), functools.partial(lambda i, blk: (0, blk), blk=blk)) for _, blk, n in weights]
    out_specs, out_shape = [], []
    for k, ((_, _, n), dt) in enumerate(zip(weights, out_dtypes)):
        if k in lane_blocked:
            out_specs.append(pl.BlockSpec((n // LANES, bm, LANES), lambda i: (0, i, 0)))
            out_shape.append(jax.ShapeDtypeStruct((n // LANES, n_tok, LANES), dt))
        else:
            out_specs.append(pl.BlockSpec((bm, n), lambda i: (i, 0)))
            out_shape.append(jax.ShapeDtypeStruct((n_tok, n), dt))
    chunk_rows = bm // S5_CHUNK if s5_chunks else 0
    scratch = []
    if s5_chunks:
        assert 0 in lane_blocked
        n_groups = weights[0][2] // S5_GROUP
        out_specs.append(pl.BlockSpec((n_groups, chunk_rows, 2 * LANES), lambda i: (0, i, 0)))
        out_shape.append(jax.ShapeDtypeStruct((n_groups, n_tok // S5_CHUNK, 2 * LANES), BF16))
        scratch = [pltpu.VMEM((weights[0][2] // LANES, bm, LANES), F32)]
    return pl.pallas_call(
        functools.partial(_inproj_kernel, d=d, n_chunk=512, chunk_rows=chunk_rows,
                          n_prompt_blocks=npb, n_x=len(xs)),
        out_shape=out_shape,
        grid=(n_tok // bm,),
        in_specs=in_specs,
        out_specs=out_specs,
        scratch_shapes=scratch,
        compiler_params=_cparams("arbitrary"),
        name="norm_mod_inproj",
    )(*xs, mods_l, norm_g.reshape(1, d), *[w for w, _, _ in weights])


def _outproj_kernel(*refs, d, final_norm, n_prompt_blocks, n_act, n_x):
    a_refs, x_refs = refs[:n_act], refs[n_act:n_act + n_x]
    mod_ref, w_ref, fg_ref = refs[n_act + n_x:n_act + n_x + 3]
    o_refs = refs[n_act + n_x + 3:]

    def finish(a_ref, x_ref, o_ref):
        y = jnp.dot(a_ref[...], w_ref[...], preferred_element_type=F32)
        gate = mod_ref[:, 2 * d:3 * d]
        xn = x_ref[...] + gate * y
        if final_norm:
            ms = jnp.mean(xn * xn, axis=-1, keepdims=True)
            xn = xn * lax.rsqrt(ms + NORM_EPS) * fg_ref[...]
        o_ref[...] = xn

    if max(n_act, n_x, len(o_refs)) == 1:
        finish(a_refs[0], x_refs[0], o_refs[0])
    else:
        is_prompt = pl.program_id(0) < n_prompt_blocks
        pl.when(is_prompt)(functools.partial(finish, a_refs[0], x_refs[0], o_refs[0]))
        pl.when(jnp.logical_not(is_prompt))(functools.partial(finish, a_refs[-1], x_refs[-1], o_refs[-1]))


def _split_specs(block, npb):
    return [pl.BlockSpec(block, lambda i: (jnp.minimum(i, npb - 1), 0)),
            pl.BlockSpec(block, lambda i: (jnp.maximum(i - npb, 0), 0))]


def _outproj_call(act, x, mods_l, w_out, final_g, *, n_prompt, sample_len, final_norm, bm=512,
                  split_out=False):
    acts = list(act) if isinstance(act, tuple) else [act]
    xs = list(x) if isinstance(x, tuple) else [x]
    n_tok = sum(a.shape[0] for a in xs)
    d, k = xs[0].shape[1], acts[0].shape[1]
    npb, bps = n_prompt // bm, sample_len // bm
    cond = functools.partial(_cond_of_block, n_prompt_blocks=npb, blocks_per_sample=bps)
    row = lambda i: (i, 0)
    act_specs = _split_specs((bm, k), npb) if len(acts) == 2 else [pl.BlockSpec((bm, k), row)]
    x_specs = _split_specs((bm, d), npb) if len(xs) == 2 else [pl.BlockSpec((bm, d), row)]
    if split_out:
        out_shape = [jax.ShapeDtypeStruct((n_prompt, d), F32), jax.ShapeDtypeStruct((n_tok - n_prompt, d), F32)]
        out_specs = _split_specs((bm, d), npb)
    else:
        out_shape = jax.ShapeDtypeStruct((n_tok, d), F32)
        out_specs = pl.BlockSpec((bm, d), row)
    return pl.pallas_call(
        functools.partial(_outproj_kernel, d=d, final_norm=final_norm, n_prompt_blocks=npb,
                          n_act=len(acts), n_x=len(xs)),
        out_shape=out_shape,
        grid=(n_tok // bm,),
        in_specs=act_specs + x_specs + [
            pl.BlockSpec((None, 1, 3 * d), lambda i: (cond(i), 0, 0)),
            pl.BlockSpec((k, d), lambda i: (0, 0)),
            pl.BlockSpec((1, d), lambda i: (0, 0)),
        ],
        out_specs=out_specs,
        compiler_params=_cparams("arbitrary"),
        name="outproj_residual",
    )(*acts, *xs, mods_l, w_out, final_g.reshape(1, d))


def _s5_time_of_lane_block():
    pos = np.arange(S5_CHUNK)
    half, blk = pos // 8, pos % 8
    g8 = np.arange(8)[:, None]
    return 8 * half[None, :] + (blk[None, :] - g8) % 8


def _s5_tables(lam_re, lam_im, log_step, b_re, b_im, c_re, c_im):
    t_chunk = S5_CHUNK
    n_groups, n_state = lam_re.shape[1], lam_re.shape[2]
    n_oct = n_groups // 8
    lam = lax.complex(lam_re.astype(F32), lam_im.astype(F32))
    step = jnp.exp(log_step.astype(F32))[..., None]
    lam_bar = jnp.exp(lam * step)
    b_bar = ((lam_bar - 1.0) / lam)[..., None] * lax.complex(b_re.astype(F32), b_im.astype(F32))
    c_mat = lax.complex(c_re.astype(F32), c_im.astype(F32))
    powers = [jnp.ones_like(lam_bar), lam_bar]
    for _ in range(t_chunk - 1):
        powers.append(powers[-1] * lam_bar)
    pw = jnp.stack(powers)

    zeros = jnp.zeros((t_chunk - 1, n_groups, n_state), pw.dtype)
    lag_f = jnp.concatenate([zeros, pw[:t_chunk, 0], zeros[:1]], axis=0)
    lag_b = jnp.concatenate([pw[t_chunk - 1::-1, 1], zeros, zeros[:1]], axis=0)
    plag = jnp.concatenate([lag_f, lag_b], axis=-1).transpose(1, 0, 2)
    plag = jnp.stack([plag.real, plag.imag])

    tl = _s5_time_of_lane_block()
    pw_ri = jnp.stack([pw.real, pw.imag]).reshape(2, t_chunk + 1, 2, n_oct, 8, n_state)
    m_idx = np.arange(t_chunk + 1)[None, None, :]

    def power_table(exponent, direction):
        sel = (exponent[:, :, None] == m_idx).astype(np.float32)
        tab = jnp.einsum('kxm,rmakp->rakxp', sel, pw_ri[:, :, direction], precision=HIGHEST)
        return tab.reshape(2, n_groups, t_chunk, n_state)

    def both(fwd, bwd):
        m = jnp.concatenate([fwd, bwd], axis=-1)
        return jnp.stack([m.real, m.imag])

    tin = jnp.concatenate([power_table(t_chunk - 1 - tl, 0), power_table(tl, 1)], axis=-1)
    tout = jnp.concatenate([power_table(tl + 1, 0), power_table(t_chunk - tl, 1)], axis=-1)
    bt = both(b_bar[0].transpose(0, 2, 1), b_bar[1].transpose(0, 2, 1))
    ct = both(c_mat[0], c_mat[1])
    lam_rows = both(pw[t_chunk, 0][:, None], pw[t_chunk, 1][:, None])[:, :, 0]
    return plag, tin, tout, bt, ct, lam_rows


def _s5_kmat_kernel(plag_ref, tin_ref, tout_ref, bt_ref, ct_ref, k_ref, pin_ref, pot_ref, x_scr, v_scr):
    masks = _lane_block_masks()
    n_lag = 2 * S5_CHUNK - 1
    for g8 in range(8):
        br, bi = bt_ref[0, g8], bt_ref[1, g8]
        cr, ci = ct_ref[0, g8], ct_ref[1, g8]
        def split(a):
            hi = a.astype(BF16)
            return hi, (a - hi.astype(F32)).astype(BF16)

        for m in range(n_lag):
            rows = slice(m * S5_GROUP, (m + 1) * S5_GROUP)
            pr, pi = plag_ref[0, g8, m:m + 1, :], plag_ref[1, g8, m:m + 1, :]
            for c0, part in ((0, cr * pr - ci * pi), (LANES, -(cr * pi + ci * pr))):
                x_scr[0, rows, c0:c0 + LANES], x_scr[1, rows, c0:c0 + LANES] = split(part)
        b_hi, b_lo = split(jnp.concatenate([jnp.concatenate([br, bi], axis=1)] * (LANES // S5_GROUP), axis=0))
        nt = functools.partial(lax.dot_general, dimension_numbers=(((1,), (1,)), ((), ())),
                               preferred_element_type=F32)
        v_scr[...] = nt(x_scr[0], b_hi) + nt(x_scr[0], b_lo) + nt(x_scr[1], b_hi)
        for pos in range(S5_CHUNK):
            rows = slice(pos * S5_GROUP, (pos + 1) * S5_GROUP)
            tr, ti = tin_ref[0, g8, pos:pos + 1, :], tin_ref[1, g8, pos:pos + 1, :]
            pin_ref[g8, rows, 0:LANES] = (tr * br - ti * bi).astype(BF16)
            pin_ref[g8, rows, LANES:2 * LANES] = (tr * bi + ti * br).astype(BF16)
            tr, ti = tout_ref[0, g8, pos:pos + 1, :], tout_ref[1, g8, pos:pos + 1, :]
            pot_ref[g8, rows, 0:LANES] = (tr * cr - ti * ci).astype(BF16)
            pot_ref[g8, rows, LANES:2 * LANES] = (-(tr * ci + ti * cr)).astype(BF16)
        for pos in range(S5_CHUNK):
            tau = 8 * (pos // 8) + (pos % 8 - g8) % 8
            rows = slice(pos * S5_GROUP, (pos + 1) * S5_GROUP)
            for half in range(2):
                acc = None
                for blk in range(8):
                    sigma = 8 * half + (blk - g8) % 8
                    m = S5_CHUNK - 1 - sigma + tau
                    src = v_scr[m * S5_GROUP:(m + 1) * S5_GROUP, :]
                    acc = src if acc is None else jnp.where(masks[blk], src, acc)
                k_ref[g8, rows, half * LANES:(half + 1) * LANES] = acc.astype(BF16)


def _s5_kmat_call(plag, tin, tout, bt, ct):
    n_groups = plag.shape[1]
    n_lag_rows = (2 * S5_CHUNK - 1) * S5_GROUP
    lag_spec = pl.BlockSpec((2, 8) + plag.shape[2:], lambda i: (0, i, 0, 0))
    tab_spec = pl.BlockSpec((2, 8, S5_GROUP, LANES), lambda i: (0, i, 0, 0))
    mat = jax.ShapeDtypeStruct((n_groups, 2 * LANES, 2 * LANES), BF16)
    mat_spec = pl.BlockSpec((8, 2 * LANES, 2 * LANES), lambda i: (i, 0, 0))
    return pl.pallas_call(
        _s5_kmat_kernel,
        out_shape=[mat, mat, mat],
        grid=(n_groups // 8,),
        in_specs=[lag_spec, tab_spec, tab_spec, tab_spec, tab_spec],
        out_specs=[mat_spec, mat_spec, mat_spec],
        scratch_shapes=[pltpu.VMEM((2, n_lag_rows, 2 * LANES), BF16), pltpu.VMEM((n_lag_rows, LANES), F32)],
        compiler_params=_cparams("arbitrary"),
        name="s5_kmat",
    )(plag, tin, tout, bt, ct)


def _lane_block_masks():
    blk = lax.broadcasted_iota(jnp.int32, (1, LANES), 1) // S5_GROUP
    return [blk == b for b in range(8)]


def _diagonal_merge(src):
    blk = lax.broadcasted_iota(jnp.int32, (1, LANES), 1) // S5_GROUP
    q = list(src)
    for bit in (1, 2, 4):
        take = (blk & bit) != 0
        q = [jnp.where(take, q[(x + bit) % 8], q[x]) for x in range(8)]
    return [q[(-t) % 8] for t in range(8)]


def _s5_to_chunks_kernel(u_ref, x_ref, *, rows):
    for o in range(u_ref.shape[0]):
        for r0 in range(0, rows, SUBLANES):
            for half in range(2):
                rolled = []
                for t8 in range(8):
                    v = u_ref[o, pl.ds(r0 * S5_CHUNK + 8 * half + t8, SUBLANES, stride=S5_CHUNK), :]
                    rolled.append(pltpu.roll(v, t8 * S5_GROUP, 1) if t8 else v)
                for g8, merged in enumerate(_diagonal_merge(rolled)):
                    x_ref[o * 8 + g8, r0:r0 + SUBLANES, half * LANES:(half + 1) * LANES] = merged.astype(BF16)


def _s5_tail_kernel(yc_ref, u_ref, d_ref, z_ref, w_ref, b_ref, o_ref, nat_scr, y_scr, *, rows, n_chunk):
    n_blk = u_ref.shape[0]
    tile = 2 * SUBLANES
    per = n_chunk // LANES
    for piece, r0 in enumerate(range(0, rows, tile)):
        tok = slice(r0 * S5_CHUNK, (r0 + tile) * S5_CHUNK)
        for o in range(n_blk):
            for half in range(2):
                src = [yc_ref[o * 8 + g8, r0:r0 + tile, half * LANES:(half + 1) * LANES] for g8 in range(8)]
                for t8, merged in enumerate(_diagonal_merge(src)):
                    nat = pltpu.roll(merged, (8 - t8) * S5_GROUP, 1) if t8 else merged
                    nat_scr[piece % 2, o, pl.ds(8 * half + t8, tile, stride=S5_CHUNK), :] = nat
            d_vec = d_ref[:, o * LANES:(o + 1) * LANES]
            y_scr[o, tok] = _gelu_tanh(nat_scr[piece % 2, o] + d_vec * u_ref[o, tok])
        yb = jnp.concatenate([y_scr[o, tok].astype(BF16) for o in range(n_blk)], axis=1)
        for c in range(0, n_blk, per):
            sl = slice(c * LANES, (c + per) * LANES)
            gate = 0.5 + 0.5 * jnp.tanh(jnp.dot(yb, w_ref[:, sl], preferred_element_type=F32) + b_ref[:, sl])
            y = jnp.concatenate([y_scr[c + k, tok] for k in range(per)], axis=1)
            o_ref[tok, sl] = (y * gate * _silu(z_ref[tok, sl].astype(F32))).astype(o_ref.dtype)


def _s5_tail_call(yc, u3, d_skip, z, glu_w_half, glu_b, *, rows=32):
    glu_w, glu_b = glu_w_half, 0.5 * glu_b
    n_blk, n_tok, _ = u3.shape
    n_groups, n_rows, _ = yc.shape
    width = n_blk * LANES
    bm = rows * S5_CHUNK
    tile_tok = 2 * SUBLANES * S5_CHUNK
    fix = lambda i: (0, 0)
    return pl.pallas_call(
        functools.partial(_s5_tail_kernel, rows=rows, n_chunk=min(512, width)),
        out_shape=jax.ShapeDtypeStruct((n_tok, width), BF16),
        grid=(n_rows // rows,),
        in_specs=[pl.BlockSpec((n_groups, rows, 2 * LANES), lambda i: (0, i, 0)),
                  pl.BlockSpec((n_blk, bm, LANES), lambda i: (0, i, 0)),
                  pl.BlockSpec((1, width), fix),
                  pl.BlockSpec((bm, width), lambda i: (i, 0)),
                  pl.BlockSpec((width, width), fix),
                  pl.BlockSpec((1, width), fix)],
        out_specs=pl.BlockSpec((bm, width), lambda i: (i, 0)),
        scratch_shapes=[pltpu.VMEM((2, n_blk, tile_tok, LANES), F32),
                        pltpu.VMEM((n_blk, bm, LANES), F32)],
        compiler_params=_cparams("arbitrary"),
        name="s5_tail",
    )(yc, u3, d_skip.reshape(1, width).astype(F32), z, glu_w, glu_b.reshape(1, width).astype(F32))


def _s5_chunk_kernel(x_ref, kt_ref, pin_ref, po_ref, lam_ref, h0r_ref, h0i_ref,
                     y_ref, fr_ref, fi_ref, r_scr, st_scr, *, segments, seq_block):
    gb = SUBLANES
    lane = lax.broadcasted_iota(jnp.int32, (1, LANES), 1)
    fwd_lanes = lane < (LANES // 2)
    seg_rows = [(row0, n_seq * n_chunks) for row0, n_seq, n_chunks, _, _ in segments]
    for row0, n_rows in seg_rows:
        for g in range(gb):
            r = jnp.dot(x_ref[g, row0:row0 + n_rows, :], pin_ref[g], preferred_element_type=F32)
            of_group = pl.ds(row0 * gb + g, n_rows, stride=gb)
            r_scr[0, of_group, :] = r[:, 0:LANES]
            r_scr[1, of_group, :] = r[:, LANES:2 * LANES]
    ar, ai = lam_ref[0], lam_ref[1]
    for row0, n_seq, n_chunks, from_input, to_output in segments:
        for b0 in range(0, n_seq, seq_block):
            nb = min(seq_block, n_seq - b0)

            def step(i, carry, row0=row0, n_chunks=n_chunks, b0=b0, nb=nb):
                out = []
                for k in range(nb):
                    base = row0 + (b0 + k) * n_chunks
                    at_f = pl.ds((base + i) * gb, gb)
                    at_b = pl.ds((base + (n_chunks - 1) - i) * gb, gb)
                    s_re, s_im = carry[k]
                    half = LANES // 2
                    st_scr[0, at_f, 0:half] = s_re[:, 0:half]
                    st_scr[0, at_b, half:LANES] = s_re[:, half:LANES]
                    st_scr[1, at_f, 0:half] = s_im[:, 0:half]
                    st_scr[1, at_b, half:LANES] = s_im[:, half:LANES]
                    v_re = jnp.where(fwd_lanes, r_scr[0, at_f, :], r_scr[0, at_b, :])
                    v_im = jnp.where(fwd_lanes, r_scr[1, at_f, :], r_scr[1, at_b, :])
                    out.append((ar * s_re - ai * s_im + v_re, ar * s_im + ai * s_re + v_im))
                return tuple(out)

            if from_input:
                init = tuple((h0r_ref[b0 + k], h0i_ref[b0 + k]) for k in range(nb))
            else:
                init = tuple((jnp.zeros((gb, LANES), F32),) * 2 for _ in range(nb))
            fin = init
            for i in range(n_chunks):
                fin = step(i, fin)
            if to_output:
                for k in range(nb):
                    fr_ref[b0 + k] = fin[k][0]
                    fi_ref[b0 + k] = fin[k][1]
    for row0, n_rows in seg_rows:
        for g in range(gb):
            of_group = pl.ds(row0 * gb + g, n_rows, stride=gb)
            st = jnp.concatenate([st_scr[cb, of_group, :] for cb in range(2)], axis=1).astype(BF16)
            rows = slice(row0, row0 + n_rows)
            nt = functools.partial(lax.dot_general, dimension_numbers=(((1,), (1,)), ((), ())),
                                   preferred_element_type=F32)
            y_ref[g, rows, :] = (nt(x_ref[g, rows, :], kt_ref[g]) + nt(st, po_ref[g])).astype(y_ref.dtype)


def _s5_chunk_call(xc, kt, pin, pout, lam_rows, h0_re, h0_im, *, layer, segments, n_final):
    n_groups, rows, _ = xc.shape
    gb = SUBLANES
    s_in = h0_re.shape[0]
    kern = functools.partial(_s5_chunk_kernel, segments=segments, seq_block=8)
    g3 = lambda i: (i, 0, 0)
    blk0 = layer * (n_groups // gb)
    p3 = lambda i: (i + blk0, 0, 0)
    mid = lambda i: (0, i, 0)
    return pl.pallas_call(
        kern,
        out_shape=[jax.ShapeDtypeStruct((n_groups, rows, 2 * LANES), F32),
                   jax.ShapeDtypeStruct((n_final, n_groups, LANES), F32),
                   jax.ShapeDtypeStruct((n_final, n_groups, LANES), F32)],
        grid=(n_groups // gb,),
        in_specs=[
            pl.BlockSpec((gb, rows, 2 * LANES), g3),
            pl.BlockSpec((gb, 2 * LANES, 2 * LANES), p3),
            pl.BlockSpec((gb, 2 * LANES, 2 * LANES), p3),
            pl.BlockSpec((gb, 2 * LANES, 2 * LANES), p3),
            pl.BlockSpec((2, gb, LANES), lambda i: (0, i + blk0, 0)),
            pl.BlockSpec((s_in, gb, LANES), mid),
            pl.BlockSpec((s_in, gb, LANES), mid),
        ],
        out_specs=[pl.BlockSpec((gb, rows, 2 * LANES), g3),
                   pl.BlockSpec((n_final, gb, LANES), mid),
                   pl.BlockSpec((n_final, gb, LANES), mid)],
        scratch_shapes=[pltpu.VMEM((2, rows * gb, LANES), F32),
                        pltpu.VMEM((2, rows * gb, LANES), F32)],
        compiler_params=_cparams("arbitrary"),
        name="s5_chunk_scan",
    )(xc, kt, pin, pout, lam_rows, h0_re, h0_im)


def _s5_prep_all(lam_re, lam_im, log_step, b_re, b_im, c_re, c_im):
    tabs = jax.vmap(_s5_tables)(lam_re, lam_im, log_step, b_re, b_im, c_re, c_im)
    plag, tin, tout, bt, ct, lam_rows = [jnp.moveaxis(t, 0, 1).reshape((2, -1) + t.shape[3:]) for t in tabs]
    kt, pin, pot = _s5_kmat_call(plag, tin, tout, bt, ct)
    return kt, pin, pot, lam_rows


def _s5_mix(u3, xc, z, layer, mats, d_skip, glu_w, glu_b, st_re, st_im, *, n_prompt_seq, prompt_len,
            n_sample_seq, sample_len, bm):
    kt, pin, pout, lam_rows = mats
    n_state = LANES // 2
    pc, sc = prompt_len // S5_CHUNK, sample_len // S5_CHUNK

    def state_rows(s):
        return jnp.concatenate([s[:, 0], s[:, 1]], axis=-1).astype(F32)

    segments = ((0, n_prompt_seq, pc, False, True), (n_prompt_seq * pc, n_sample_seq, sc, True, False))
    yc, fr, fi = _s5_chunk_call(xc, kt, pin, pout, lam_rows, state_rows(st_re), state_rows(st_im),
                                layer=layer, segments=segments, n_final=n_prompt_seq)
    act = _s5_tail_call(yc, u3, d_skip, z, glu_w, glu_b, rows=bm // S5_CHUNK)

    def unpack(f):
        return jnp.stack([f[:, :, :n_state], f[:, :, n_state:]], axis=1)

    return act, unpack(fr), unpack(fi)


POOL_TILE = 2 * LANES
POOL_EDGE = 16


def _pool_band_matrices():
    t = np.arange(POOL_TILE)[:, None]
    s = np.arange(POOL_TILE)[None, :]
    mats = []
    for win in POOL_WINDOWS:
        lo = win // 2
        inside = lambda src: ((src >= t - lo) & (src <= t + lo - 1)).astype(np.float32)
        mats.append(np.stack([inside(s), inside(s - POOL_TILE), inside(s + POOL_TILE)]))
    return np.stack(mats)


def _pool_kernel(u_ref, z_ref, w_ref, s_ref, a_ref, o_ref, sum_scr, *, n_prompt_blocks, prompt_len,
                 sample_len):
    rows = u_ref.shape[0]
    n_tiles = rows // POOL_TILE
    is_prompt = pl.program_id(0) < n_prompt_blocks
    seq_len = jnp.where(is_prompt, prompt_len, sample_len)
    lo = jnp.left_shift(1, pl.program_id(1))
    t = lax.broadcasted_iota(jnp.int32, (rows, 1), 0) & (seq_len - 1)
    u = u_ref[...]
    u_hi = u.astype(BF16)
    u_lo = (u - u_hi.astype(F32)).astype(BF16)

    def band(a, tile):
        sl = slice(tile * POOL_TILE, (tile + 1) * POOL_TILE)
        return (jnp.dot(a, u_hi[sl], preferred_element_type=F32)
                + jnp.dot(a, u_lo[sl], preferred_element_type=F32))

    for r in range(n_tiles):
        sum_scr[r * POOL_TILE:(r + 1) * POOL_TILE, :] = band(a_ref[0], r)

    @pl.when(jnp.logical_not(is_prompt))
    def _():
        for r in range(n_tiles):
            if r > 0:
                top = slice(r * POOL_TILE, r * POOL_TILE + POOL_EDGE)
                sum_scr[top, :] = sum_scr[top, :] + band(a_ref[1, 0:POOL_EDGE, :], r - 1)
            if r < n_tiles - 1:
                bot = slice((r + 1) * POOL_TILE - POOL_EDGE, (r + 1) * POOL_TILE)
                sum_scr[bot, :] = sum_scr[bot, :] + band(a_ref[2, POOL_TILE - POOL_EDGE:POOL_TILE, :], r + 1)

    cnt = jnp.minimum(t + lo, seq_len) - jnp.maximum(t - lo, 0)
    p = sum_scr[...] / cnt.astype(F32) - u
    m = jnp.dot(p.astype(BF16), w_ref[...], preferred_element_type=F32) * s_ref[...]
    o_ref[...] = (m * _silu(z_ref[...].astype(F32))).astype(o_ref.dtype)


def _pool_call(u, z, pool_w, pool_scale, *, n_prompt, prompt_len, sample_len, rows=2048):
    n_tok, width = u.shape
    n_groups = len(POOL_WINDOWS)
    cg = width // n_groups
    assert prompt_len & (prompt_len - 1) == 0 and sample_len & (sample_len - 1) == 0
    assert rows % prompt_len == 0 and n_prompt % rows == 0
    assert prompt_len == POOL_TILE and sample_len == rows and max(POOL_WINDOWS) // 2 <= POOL_EDGE
    assert POOL_WINDOWS == tuple(2 << g for g in range(n_groups))
    kern = functools.partial(_pool_kernel, n_prompt_blocks=n_prompt // rows, prompt_len=prompt_len,
                             sample_len=sample_len)
    bands = jnp.asarray(_pool_band_matrices(), BF16)
    return pl.pallas_call(
        kern,
        out_shape=jax.ShapeDtypeStruct((n_tok, width), BF16),
        grid=(n_tok // rows, n_groups),
        in_specs=[
            pl.BlockSpec((rows, cg), lambda i, g: (i, g)),
            pl.BlockSpec((rows, cg), lambda i, g: (i, g)),
            pl.BlockSpec((None, cg, cg), lambda i, g: (g, 0, 0)),
            pl.BlockSpec((1, cg), lambda i, g: (0, g)),
            pl.BlockSpec((None, 3, POOL_TILE, POOL_TILE), lambda i, g: (g, 0, 0, 0)),
        ],
        out_specs=pl.BlockSpec((rows, cg), lambda i, g: (i, g)),
        scratch_shapes=[pltpu.VMEM((rows, cg), F32)],
        compiler_params=_cparams("arbitrary", "arbitrary"),
        name="pool_mix",
    )(u, z, pool_w, pool_scale.reshape(1, width).astype(F32), bands)


MLA_QW = 2 * LANES

_ROT_SRC = np.concatenate([np.arange(16, 32), np.arange(0, 16), np.arange(48, 64), np.arange(32, 48)])
_ROT_SIGN = np.concatenate([-np.ones(16), np.ones(16), -np.ones(16), np.ones(16)]).astype(np.float32)


def _rope_tables(sample_len):
    half = MLA_ROPE // 4
    tok = jnp.arange(sample_len)
    row = (tok // GRID_W).astype(F32)
    col = (tok % GRID_W).astype(F32)
    inv = ROPE_THETA ** (-jnp.arange(half, dtype=F32) / half)
    a_row, a_col = row[:, None] * inv, col[:, None] * inv
    cos = jnp.concatenate([jnp.cos(a_row), jnp.cos(a_row), jnp.cos(a_col), jnp.cos(a_col)], axis=-1)
    sin = jnp.concatenate([jnp.sin(a_row), jnp.sin(a_row), jnp.sin(a_col), jnp.sin(a_col)], axis=-1)
    pad = jnp.zeros((sample_len, LANES - MLA_ROPE), F32)
    return jnp.concatenate([cos, pad], axis=-1), jnp.concatenate([sin, pad], axis=-1)


def _rms(x, g):
    return x * lax.rsqrt(jnp.mean(x * x, axis=-1, keepdims=True) + NORM_EPS) * g


def _mla_post_kernel(sm_ref, cos_ref, sin_ref, qn_ref, kn_ref, wa_ref, wb_ref,
                     q_ref, ckv_ref, kpe_ref, *, q_rank, kv_rank, heads_per_dot, n_prompt_blocks):
    is_prompt = pl.program_id(0) < n_prompt_blocks
    rope_lanes = (lax.broadcasted_iota(jnp.int32, (1, LANES), 1) < MLA_ROPE).astype(F32)
    cosp = jnp.where(is_prompt, rope_lanes, cos_ref[...])
    sinp = jnp.where(is_prompt, 0.0, sin_ref[...])
    qn = _rms(sm_ref[:, 0:q_rank], qn_ref[...]).astype(BF16)
    for h0 in range(0, MLA_HEADS, heads_per_dot):
        a = jnp.dot(qn, wa_ref[:, h0 * MLA_QW:(h0 + heads_per_dot) * MLA_QW], preferred_element_type=F32)
        b = jnp.dot(qn, wb_ref[:, h0 * LANES:(h0 + heads_per_dot) * LANES], preferred_element_type=F32)
        for j in range(heads_per_dot):
            h = h0 + j
            q_ref[:, h * MLA_QW:h * MLA_QW + LANES] = a[:, j * MLA_QW:j * MLA_QW + LANES].astype(BF16)
            pe = a[:, j * MLA_QW + LANES:(j + 1) * MLA_QW] * cosp + b[:, j * LANES:(j + 1) * LANES] * sinp
            q_ref[:, h * MLA_QW + LANES:(h + 1) * MLA_QW] = pe.astype(BF16)
    c0 = q_rank
    ckv_ref[...] = _rms(sm_ref[:, c0:c0 + kv_rank], kn_ref[...])
    k0 = c0 + kv_rank
    kpe_ref[...] = (sm_ref[:, k0:k0 + LANES] * cosp + sm_ref[:, k0 + LANES:k0 + 2 * LANES] * sinp).astype(BF16)


def _mla_post_call(small, cos_t, sin_t, q_norm, kv_norm, wq_a, wq_b, *, n_prompt, bm=512):
    n_tok, ws = small.shape
    q_rank, kv_rank = q_norm.shape[-1], kv_norm.shape[-1]
    npb, bps = n_prompt // bm, cos_t.shape[0] // bm
    row = lambda i: (i, 0)
    fix = lambda i: (0, 0)
    pos = lambda i: (jnp.where(i < npb, 0, (i - npb) % bps), 0)
    kern = functools.partial(_mla_post_kernel, q_rank=q_rank, kv_rank=kv_rank, heads_per_dot=4,
                             n_prompt_blocks=npb)
    return pl.pallas_call(
        kern,
        out_shape=[jax.ShapeDtypeStruct((n_tok, MLA_HEADS * MLA_QW), BF16),
                   jax.ShapeDtypeStruct((n_tok, kv_rank), F32),
                   jax.ShapeDtypeStruct((n_tok, LANES), BF16)],
        grid=(n_tok // bm,),
        in_specs=[
            pl.BlockSpec((bm, ws), row),
            pl.BlockSpec((bm, LANES), pos),
            pl.BlockSpec((bm, LANES), pos),
            pl.BlockSpec((1, q_rank), fix),
            pl.BlockSpec((1, kv_rank), fix),
            pl.BlockSpec(wq_a.shape, fix),
            pl.BlockSpec(wq_b.shape, fix),
        ],
        out_specs=[pl.BlockSpec((bm, MLA_HEADS * MLA_QW), row),
                   pl.BlockSpec((bm, kv_rank), row),
                   pl.BlockSpec((bm, LANES), row)],
        compiler_params=_cparams("arbitrary"),
        name="mla_q_rope",
    )(small, cos_t, sin_t, q_norm.reshape(1, q_rank).astype(F32), kv_norm.reshape(1, kv_rank).astype(F32),
      wq_a, wq_b)


def _attn_kernel(q_ref, ckv_ref, wkv_ref, kpe_ref, z_ref, o_ref, kcat_scr, vext_scr, *, hg, scale):
    c2 = scale * math.log2(math.e)

    @pl.when(pl.program_id(2) == 0)
    def _():
        ckv = ckv_ref[...].astype(BF16)
        ones = jnp.ones((ckv.shape[0], LANES), BF16)
        for j in range(hg):
            kv = jnp.dot(ckv, wkv_ref[:, j * 2 * LANES:(j + 1) * 2 * LANES],
                         preferred_element_type=F32).astype(BF16)
            kcat_scr[j, :, 0:LANES] = kv[:, 0:LANES]
            kcat_scr[j, :, LANES:2 * LANES] = kpe_ref[...]
            vext_scr[j, :, 0:LANES] = kv[:, LANES:2 * LANES]
            vext_scr[j, :, LANES:2 * LANES] = ones

    scores = [lax.dot_general(q_ref[:, j * MLA_QW:(j + 1) * MLA_QW], kcat_scr[j], (((1,), (1,)), ((), ())),
                              preferred_element_type=F32) for j in range(hg)]
    probs = [jnp.exp2((s - jnp.max(s, axis=-1, keepdims=True)) * c2).astype(BF16) for s in scores]
    for j in range(hg):
        pv = jnp.dot(probs[j], vext_scr[j], preferred_element_type=F32)
        zs = slice(j * MLA_V, (j + 1) * MLA_V)
        o = pv[:, 0:MLA_V] / pv[:, MLA_V:2 * MLA_V]
        o_ref[:, zs] = (o * _silu(z_ref[:, zs].astype(F32))).astype(o_ref.dtype)


def _attn_call(q, ckv, wkv_b, kpe, z, *, q_row0, n_seq, q_len, k_len, hg, qb):
    width = MLA_HEADS * MLA_V
    nqb = q_len // qb
    qb0 = q_row0 // qb
    assert q_row0 % qb == 0 and q_len % qb == 0
    scale = float((MLA_NOPE + MLA_ROPE) ** -0.5)
    qrow = lambda b, g, i: (qb0 + b * nqb + i, g)
    return pl.pallas_call(
        functools.partial(_attn_kernel, hg=hg, scale=scale),
        out_shape=jax.ShapeDtypeStruct((n_seq * q_len, width), BF16),
        grid=(n_seq, MLA_HEADS // hg, nqb),
        in_specs=[
            pl.BlockSpec((qb, hg * MLA_QW), qrow),
            pl.BlockSpec((k_len, ckv.shape[1]), lambda b, g, i: (b, 0)),
            pl.BlockSpec((wkv_b.shape[0], hg * 2 * LANES), lambda b, g, i: (0, g)),
            pl.BlockSpec((k_len, LANES), lambda b, g, i: (b, 0)),
            pl.BlockSpec((qb, hg * MLA_V), qrow),
        ],
        out_specs=pl.BlockSpec((qb, hg * MLA_V), lambda b, g, i: (b * nqb + i, g)),
        scratch_shapes=[pltpu.VMEM((hg, k_len, MLA_QW), BF16),
                        pltpu.VMEM((hg, k_len, 2 * MLA_V), BF16)],
        compiler_params=_cparams("arbitrary", "arbitrary", "arbitrary"),
        name="mla_attention",
    )(q, ckv, wkv_b, kpe, z)


def _mla_weights(w_in, wq_b):
    q_rank = wq_b.shape[0]
    kv_rank = w_in.shape[1] - q_rank - MLA_ROPE - MLA_HEADS * MLA_V
    d = w_in.shape[0]
    c_kpe = q_rank + kv_rank
    zpad = jnp.zeros((d, LANES - MLA_ROPE), w_in.dtype)
    kpe_w = w_in[:, c_kpe:c_kpe + MLA_ROPE]
    w_small = jnp.concatenate([w_in[:, :c_kpe], kpe_w, zpad,
                               kpe_w[:, _ROT_SRC] * _ROT_SIGN, zpad], axis=1)
    w_z = w_in[:, c_kpe + MLA_ROPE:]
    hd = MLA_NOPE + MLA_ROPE
    wq3 = wq_b.reshape(q_rank, MLA_HEADS, hd)
    pe = wq3[:, :, MLA_NOPE:]
    z3 = jnp.zeros((q_rank, MLA_HEADS, LANES - MLA_ROPE), wq_b.dtype)
    wq_a = jnp.concatenate([wq3, z3], axis=-1).reshape(q_rank, MLA_HEADS * MLA_QW)
    wq_r = jnp.concatenate([pe[:, :, _ROT_SRC] * _ROT_SIGN, z3], axis=-1).reshape(q_rank, MLA_HEADS * LANES)
    return w_small.astype(BF16), w_z.astype(BF16), wq_a.astype(BF16), wq_r.astype(BF16)


def kernel(x_prompt, x_sample, state_s5_re, state_s5_im, cache_ckv, cache_kpe, c, c_ctx, norm_g, ada_w, ada_b, final_norm_g, s5_w_in, s5_lam_re, s5_lam_im, s5_log_step, s5_b_re, s5_b_im, s5_c_re, s5_c_im, s5_d, s5_glu_w, s5_glu_b, s5_w_out, pool_w_in, pool_w, pool_scale, pool_w_out, mla_w_in, mla_q_norm, mla_wq_b, mla_kv_norm, mla_wkv_b, mla_w_out):
    n_pseq, p_len, d = x_prompt.shape
    n_sseq, s_len, _ = x_sample.shape
    depth = norm_g.shape[0]
    n_prompt = n_pseq * p_len
    bm = 512
    geo = dict(n_prompt=n_prompt, sample_len=s_len, bm=bm)

    x = (x_prompt.reshape(n_prompt, d), x_sample.reshape(n_sseq * s_len, d))
    conds = jnp.concatenate([c_ctx[None, :], c, jnp.zeros((SUBLANES - 1 - n_sseq, d), F32)], axis=0)
    mods = _ada_call(conds.astype(F32), ada_w, ada_b)
    mods = mods.reshape(depth, SUBLANES, 1, 3 * d)
    s5_mats = _s5_prep_all(s5_lam_re, s5_lam_im, s5_log_step, s5_b_re, s5_b_im, s5_c_re, s5_c_im)

    new_re, new_im, new_ckv, new_kpe = [], [], [], []
    for layer in range(depth):
        kind, j = layer % N_MIXERS, layer // N_MIXERS
        last = layer == depth - 1
        ml = mods[layer]
        if kind == 0:
            width = s5_w_in.shape[2] // 2
            w = s5_w_in[j].astype(BF16)
            u3, z, xc = _inproj_call(x, ml, norm_g[layer], [(w, 0, width), (w, 1, width)], [F32, BF16],
                                     lane_blocked=(0,), s5_chunks=True, **geo)
            act, f_re, f_im = _s5_mix(u3, xc, z, j, s5_mats, s5_d[j], (0.5 * s5_glu_w[j]).astype(BF16), s5_glu_b[j],
                                      state_s5_re[:, j], state_s5_im[:, j], n_prompt_seq=n_pseq,
                                      prompt_len=p_len, n_sample_seq=n_sseq, sample_len=s_len, bm=bm)
            new_re.append(f_re)
            new_im.append(f_im)
            w_out = s5_w_out[j]
        elif kind == 1:
            width = pool_w_in.shape[2] // 2
            w = pool_w_in[j].astype(BF16)
            u, z = _inproj_call(x, ml, norm_g[layer], [(w, 0, width), (w, 1, width)], [F32, BF16], **geo)
            act = _pool_call(u, z, pool_w[j].astype(BF16), pool_scale[j], n_prompt=n_prompt,
                             prompt_len=p_len, sample_len=s_len)
            w_out = pool_w_out[j]
        else:
            q_rank, kv_rank = mla_q_norm.shape[-1], mla_kv_norm.shape[-1]
            w_small, w_z, wq_a, wq_r = _mla_weights(mla_w_in[j], mla_wq_b[j])
            small, z = _inproj_call(x, ml, norm_g[layer], [w_small, w_z], [F32, BF16], **geo)
            cos_t, sin_t = _rope_tables(s_len)
            q, ckv_n, kpe_k = _mla_post_call(small, cos_t, sin_t, mla_q_norm[j], mla_kv_norm[j], wq_a, wq_r,
                                             n_prompt=n_prompt, bm=bm)
            wkv = mla_wkv_b[j].astype(BF16)
            past = cache_ckv.shape[2]
            k_len = past + s_len
            ckv_s = jnp.concatenate([cache_ckv[:, j].astype(F32), ckv_n[n_prompt:].reshape(n_sseq, s_len, kv_rank)],
                                    axis=1).reshape(n_sseq * k_len, kv_rank)
            kpe_cache = jnp.concatenate([cache_kpe[:, j].astype(BF16),
                                         jnp.zeros((n_sseq, past, LANES - MLA_ROPE), BF16)], axis=-1)
            kpe_s = jnp.concatenate([kpe_cache, kpe_k[n_prompt:].reshape(n_sseq, s_len, LANES)],
                                    axis=1).reshape(n_sseq * k_len, LANES)
            act = (_attn_call(q, ckv_n, wkv, kpe_k, z, q_row0=0, n_seq=n_pseq, q_len=p_len,
                              k_len=p_len, hg=MLA_HEADS, qb=p_len),
                   _attn_call(q, ckv_s, wkv, kpe_s, z, q_row0=n_prompt, n_seq=n_sseq, q_len=s_len,
                              k_len=k_len, hg=4, qb=256))
            new_ckv.append(ckv_n[:n_prompt].reshape(n_pseq, p_len, kv_rank))
            c_kpe = q_rank + kv_rank
            new_kpe.append(small[:n_prompt, c_kpe:c_kpe + MLA_ROPE].reshape(n_pseq, p_len, MLA_ROPE))
            w_out = mla_w_out[j]
        x = _outproj_call(act, x, ml, w_out.astype(BF16), final_norm_g, final_norm=last, split_out=last,
                          n_prompt=n_prompt, sample_len=s_len, bm=2 * bm)

    y_prompt = x[0].reshape(n_pseq, p_len, d)
    y_sample = x[1].reshape(n_sseq, s_len, d)
    return (y_prompt, y_sample, jnp.stack(new_re, axis=1), jnp.stack(new_im, axis=1),
            jnp.stack(new_ckv, axis=1), jnp.stack(new_kpe, axis=1))
```
